```python
import math
import jax
import jax.numpy as jnp
from jax import lax
import numpy as np

D_MODEL = 1024
BATCH = 4
SEQ = 4096
DEPTH = 4

N_MIXERS = 4
NORM_EPS = 1e-6

SSM_EXPAND = 2
SSM_D_INNER = SSM_EXPAND * D_MODEL
SSM_HEAD_DIM = 64
SSM_N_HEADS = SSM_D_INNER // SSM_HEAD_DIM
SSM_N_GROUPS = 8
SSM_HEADS_PER_GROUP = SSM_N_HEADS // SSM_N_GROUPS
SSM_D_STATE = 128
SSM_CONV = 4
SSM_CHUNK = 128
SSM_CONV_DIM = SSM_D_INNER + 2 * SSM_N_GROUPS * SSM_D_STATE
SSM_IN_DIM = SSM_D_INNER + SSM_CONV_DIM + SSM_N_HEADS

RWKV_HEAD_DIM = 64
RWKV_N_HEADS = D_MODEL // RWKV_HEAD_DIM
RWKV_DECAY_LORA = 64
RWKV_AAA_LORA = 64
RWKV_GATE_LORA = 160
RWKV_LN_EPS = 64e-5

RET_N_HEADS = 4
RET_QK_DIM = D_MODEL // RET_N_HEADS
RET_V_DIM = 2 * D_MODEL // RET_N_HEADS
RET_CHUNK = 128
RET_IN_DIM = 2 * D_MODEL + 4 * D_MODEL

ATT_GROUPS = ((128, 1), (512, 4), (2048, 16))
ATT_HEADS_PER_GROUP = 8
ATT_HEAD_DIM = 128
ATT_BLOCK = 128
ATT_N_HEADS = len(ATT_GROUPS) * ATT_HEADS_PER_GROUP
ATT_IN_DIM = 3 * ATT_N_HEADS * ATT_HEAD_DIM
ATT_OUT_DIM = ATT_HEADS_PER_GROUP * ATT_HEAD_DIM

MOE_GROUPS = 4
MOE_EXPERTS_PER_GROUP = 8
MOE_EXPERTS = MOE_GROUPS * MOE_EXPERTS_PER_GROUP
MOE_TOP_K = 2
MOE_D_FF = 512
MOE_BLOCK = 128

kernel_name = "hybrid_ssd_rwkv7_retnet_dilated_hmoe"


def rms_norm(x, w):
    xf = x.astype(jnp.float32)
    y = xf * lax.rsqrt(jnp.mean(xf * xf, axis=-1, keepdims=True) + NORM_EPS)
    return (y * w.astype(jnp.float32)).astype(x.dtype)


def modulate(h, shift, scale):
    return h * (1 + scale[:, None, :]) + shift[:, None, :]


def alibi_slopes(n):
    return jnp.exp2(-8.0 * jnp.arange(1, n + 1, dtype=jnp.float32) / n)


def mamba2_ssd(h, in_w, conv_w, conv_b, dt_bias, a_log, d_skip, norm_w, out_w):
    B, T, _ = h.shape
    G, HG, P, N, Q = SSM_N_GROUPS, SSM_HEADS_PER_GROUP, SSM_HEAD_DIM, SSM_D_STATE, SSM_CHUNK
    nc = T // Q
    f32 = jnp.float32
    z, xbc, dt = jnp.split(h @ in_w, [SSM_D_INNER, SSM_D_INNER + SSM_CONV_DIM], axis=-1)
    xbc = lax.conv_general_dilated(xbc, conv_w[:, None, :], window_strides=(1,),
                                   padding=[(SSM_CONV - 1, 0)],
                                   dimension_numbers=("NWC", "WIO", "NWC"),
                                   feature_group_count=SSM_CONV_DIM)
    xbc = jax.nn.silu(xbc + conv_b)
    xs, bm, cm = jnp.split(xbc, [SSM_D_INNER, SSM_D_INNER + G * N], axis=-1)
    xs = xs.astype(f32).reshape(B, nc, Q, G, HG, P)
    bm = bm.astype(f32).reshape(B, nc, Q, G, N)
    cm = cm.astype(f32).reshape(B, nc, Q, G, N)
    dt = jax.nn.softplus(dt.astype(f32) + dt_bias.astype(f32)).reshape(B, nc, Q, G, HG)
    a = -jnp.exp(a_log.astype(f32)).reshape(G, HG)
    acum = jnp.cumsum(dt * a, axis=2)
    xdt = xs * dt[..., None]
    causal = jnp.tril(jnp.ones((Q, Q), bool))[:, :, None, None]
    seg = acum[:, :, :, None] - acum[:, :, None, :]
    lmat = jnp.exp(jnp.where(causal, seg, -jnp.inf))
    cb = jnp.einsum("bcign,bcjgn->bcijg", cm, bm)
    y_diag = jnp.einsum("bcijg,bcijgh,bcjghp->bcighp", cb, lmat, xdt)
    decay_end = jnp.exp(acum[:, :, -1:] - acum)
    states = jnp.einsum("bcjgn,bcjgh,bcjghp->bcghpn", bm, decay_end, xdt)
    chunk_decay = jnp.exp(acum[:, :, -1])

    def step(state, inp):
        st, cdec, c_c, eacum = inp
        y = jnp.einsum("bign,bghpn,bigh->bighp", c_c, state, eacum)
        return state * cdec[..., None, None] + st, y

    init = jnp.zeros((B, G, HG, P, N), f32)
    _, y_off = lax.scan(step, init, (jnp.moveaxis(states, 1, 0), jnp.moveaxis(chunk_decay, 1, 0),
                                     jnp.moveaxis(cm, 1, 0), jnp.moveaxis(jnp.exp(acum), 1, 0)))
    y = y_diag + jnp.moveaxis(y_off, 0, 1) + xs * d_skip.astype(f32).reshape(G, HG, 1)
    gsz = SSM_D_INNER // G
    y = y.reshape(B, T, G, gsz) * jax.nn.silu(z.astype(f32)).reshape(B, T, G, gsz)
    y = y * lax.rsqrt(jnp.mean(y * y, axis=-1, keepdims=True) + NORM_EPS)
    y = y.reshape(B, T, SSM_D_INNER) * norm_w.astype(f32)
    return y.astype(h.dtype) @ out_w


def rwkv7_time_mix(h, mix, rkv_w, w0, w1, w2, a0, a1, a2, g1, g2, k_k, k_a, r_k, ln_w, ln_b, out_w):
    B, T, D = h.shape
    H, N = RWKV_N_HEADS, RWKV_HEAD_DIM
    f32 = jnp.float32
    xx = jnp.pad(h, ((0, 0), (1, 0), (0, 0)))[:, :-1] - h
    xr, xw, xk, xv, xa, xg = [h + xx * mix[j] for j in range(6)]
    r, k, v = jnp.einsum("jbtd,jde->jbte", jnp.stack([xr, xk, xv]), rkv_w)
    w_log = -jax.nn.softplus(-(w0 + jnp.tanh(xw @ w1) @ w2)) - 0.5
    decay = jnp.exp(-jnp.exp(w_log.astype(f32)))
    a = jax.nn.sigmoid(a0 + (xa @ a1) @ a2)
    g = jax.nn.sigmoid(xg @ g1) @ g2
    kk = (k * k_k).astype(f32).reshape(B, T, H, N)
    kk = kk / jnp.maximum(jnp.sqrt(jnp.sum(kk * kk, -1, keepdims=True)), 1e-12)
    k = k * (1 + (a - 1) * k_a)
    r4 = r.astype(f32).reshape(B, T, H, N)
    k4 = k.astype(f32).reshape(B, T, H, N)
    v4 = v.astype(f32).reshape(B, T, H, N)
    a4 = a.astype(f32).reshape(B, T, H, N)
    w4 = decay.reshape(B, T, H, N)
    a_in = -kk
    b_in = kk * a4

    def step(S, inp):
        r_t, w_t, k_t, v_t, a_t, b_t = inp
        sa = jnp.einsum("bhvk,bhk->bhv", S, a_t)
        S = S * w_t[:, :, None, :] + sa[..., None] * b_t[:, :, None, :] + v_t[..., None] * k_t[:, :, None, :]
        return S, jnp.einsum("bhvk,bhk->bhv", S, r_t)

    tm = lambda t: jnp.moveaxis(t, 1, 0)
    _, o = lax.scan(step, jnp.zeros((B, H, N, N), f32),
                    (tm(r4), tm(w4), tm(k4), tm(v4), tm(a_in), tm(b_in)))
    o = jnp.moveaxis(o, 0, 1)
    mu = jnp.mean(o, -1, keepdims=True)
    var = jnp.mean((o - mu) ** 2, -1, keepdims=True)
    o = ((o - mu) * lax.rsqrt(var + RWKV_LN_EPS)).reshape(B, T, D) * ln_w.astype(f32) + ln_b.astype(f32)
    bonus = jnp.sum(r4 * k4 * r_k.astype(f32), -1, keepdims=True) * v4
    o = o + bonus.reshape(B, T, D)
    return (o.astype(h.dtype) * g) @ out_w


def multiscale_retention(h, in_w, out_w):
    B, T, D = h.shape
    H, DK, DV, Q = RET_N_HEADS, RET_QK_DIM, RET_V_DIM, RET_CHUNK
    nc = T // Q
    f32 = jnp.float32
    q, k, v, g = jnp.split(h @ in_w, [D, 2 * D, 4 * D], axis=-1)
    q = q.astype(f32).reshape(B, nc, Q, H, DK)
    k = (k.astype(f32) * DK ** -0.5).reshape(B, nc, Q, H, DK)
    v = v.astype(f32).reshape(B, nc, Q, H, DV)
    log_gamma = jnp.log(1 - jnp.exp2(-5.0 - jnp.arange(H, dtype=f32)))
    idx = jnp.arange(Q)
    rel = idx[:, None] - idx[None, :]
    inner_decay = jnp.where(rel >= 0, jnp.exp(jnp.maximum(rel, 0).astype(f32) * log_gamma[:, None, None]), 0.0)
    scores = jnp.einsum("bcihd,bcjhd->bchij", q, k) * inner_decay
    inner = jnp.einsum("bchij,bcjhe->bcihe", scores, v)
    q_decay = jnp.exp((idx + 1).astype(f32)[:, None] * log_gamma)
    k_decay = jnp.exp((Q - 1 - idx).astype(f32)[:, None] * log_gamma)
    chunk_decay = jnp.exp(Q * log_gamma)

    def step(R, inp):
        qc, kc, vc = inp
        cross = jnp.einsum("bihd,bhde->bihe", qc * q_decay[:, :, None], R)
        R = R * chunk_decay[:, None, None] + jnp.einsum("bjhd,bjhe->bhde", kc * k_decay[:, :, None], vc)
        return R, cross

    _, cross = lax.scan(step, jnp.zeros((B, H, DK, DV), f32),
                        (jnp.moveaxis(q, 1, 0), jnp.moveaxis(k, 1, 0), jnp.moveaxis(v, 1, 0)))
    o = (inner + jnp.moveaxis(cross, 0, 1)).reshape(B, T, H, DV)
    o = o * lax.rsqrt(jnp.mean(o * o, -1, keepdims=True) + NORM_EPS)
    o = jax.nn.silu(g.astype(f32)) * o.reshape(B, T, H * DV)
    return o.astype(h.dtype) @ out_w


def dilated_group_attention(q, k, v, window, dil, slopes):
    B, T, H, E = q.shape
    W = ATT_BLOCK
    span = window // dil
    Ls = T // dil
    nb = -(-Ls // W)
    Lp = nb * W
    f32 = jnp.float32

    def to_blocks(a):
        a = a.reshape(B, Ls, dil, H, E).transpose(0, 2, 1, 3, 4)
        a = jnp.pad(a, ((0, 0), (0, 0), (0, Lp - Ls), (0, 0), (0, 0)))
        return a.astype(f32).reshape(B, dil, nb, W, H, E)

    def with_prev(a):
        prev = jnp.pad(a, ((0, 0), (0, 0), (1, 0), (0, 0), (0, 0), (0, 0)))[:, :, :-1]
        return jnp.concatenate([prev, a], axis=3)

    qb = to_blocks(q)
    kw = with_prev(to_blocks(k))
    vw = with_prev(to_blocks(v))
    s = jnp.einsum("brnqhe,brnkhe->brnhqk", qb, kw) * E ** -0.5
    delta = W + jnp.arange(W)[:, None] - jnp.arange(2 * W)[None, :]
    band = (delta >= 0) & (delta <= span)
    key_exists = (jnp.arange(nb)[:, None] > 0) | (jnp.arange(2 * W)[None, :] >= W)
    mask = band[None, :, :] & key_exists[:, None, :]
    bias = -slopes[:, None, None] * (dil * delta).astype(f32)
    s = jnp.where(mask[:, None], s + bias, -jnp.inf)
    m = jnp.max(s, axis=-1, keepdims=True)
    e = jnp.exp(s - m)
    den = jnp.sum(e, axis=-1, keepdims=True)
    o = jnp.einsum("brnhqk,brnkhe->brnqhe", e / den, vw)
    lse = jnp.moveaxis((m + jnp.log(den))[..., 0], 3, 4)
    o = o.reshape(B, dil, Lp, H, E)[:, :, :Ls].transpose(0, 2, 1, 3, 4).reshape(B, T, H, E)
    lse = lse.reshape(B, dil, Lp, H)[:, :, :Ls].transpose(0, 2, 1, 3).reshape(B, T, H)
    return o, lse


def dilated_attention(h, in_w, out_w):
    B, T, _ = h.shape
    G, HG, E = len(ATT_GROUPS), ATT_HEADS_PER_GROUP, ATT_HEAD_DIM
    qkv = (h @ in_w).reshape(B, T, 3, G, HG, E)
    slopes = alibi_slopes(G * HG).reshape(G, HG)
    outs, lses = [], []
    for gi, (window, dil) in enumerate(ATT_GROUPS):
        o, lse = dilated_group_attention(qkv[:, :, 0, gi], qkv[:, :, 1, gi], qkv[:, :, 2, gi],
                                         window, dil, slopes[gi])
        outs.append(o)
        lses.append(lse)
    wts = jax.nn.softmax(jnp.stack(lses), axis=0)
    o = jnp.sum(wts[..., None] * jnp.stack(outs), axis=0).reshape(B, T, HG * E)
    return o.astype(h.dtype) @ out_w


def hierarchical_moe(h, group_w, group_b, expert_w, expert_b, w1, w3, w2):
    B, T, D = h.shape
    N = B * T
    E, EPG, K, BLK = MOE_EXPERTS, MOE_EXPERTS_PER_GROUP, MOE_TOP_K, MOE_BLOCK
    f32 = jnp.float32
    xt = h.reshape(N, D)
    g_logits = (xt @ group_w + group_b).astype(f32)
    g_prob = jax.nn.softmax(g_logits, axis=-1)
    g_idx = jnp.argmax(g_logits, axis=-1)
    g_top = jnp.take_along_axis(g_prob, g_idx[:, None], axis=-1)
    e_logits = (xt @ expert_w + expert_b).astype(f32).reshape(N, MOE_GROUPS, EPG)
    e_in_group = jnp.take_along_axis(e_logits, g_idx[:, None, None], axis=1)[:, 0]
    top_vals, top_idx = lax.top_k(e_in_group, K)
    gate = g_top * jax.nn.softmax(top_vals, axis=-1)
    flat_e = (g_idx[:, None] * EPG + top_idx).reshape(-1)
    flat_gate = gate.reshape(-1)
    flat_tok = jnp.repeat(jnp.arange(N), K)
    S = N * K
    order = jnp.argsort(flat_e)
    se = flat_e[order]
    counts = jnp.bincount(flat_e, length=E)
    starts = jnp.cumsum(counts) - counts
    padded = (counts + BLK - 1) // BLK * BLK
    pends = jnp.cumsum(padded)
    pstarts = pends - padded
    dest = pstarts[se] + jnp.arange(S) - starts[se]
    R = -(-S // BLK) * BLK + E * BLK
    row_tok = jnp.full((R,), N, jnp.int32).at[dest].set(flat_tok[order].astype(jnp.int32))
    row_gate = jnp.zeros((R,), f32).at[dest].set(flat_gate[order])
    nblk = R // BLK
    blk_e = jnp.minimum(jnp.searchsorted(pends, jnp.arange(nblk) * BLK, side="right"), E - 1)
    xpad = jnp.concatenate([xt, jnp.zeros((1, D), xt.dtype)], axis=0)
    xb = xpad[row_tok].reshape(nblk, BLK, D)

    def expert_block(args):
        xblk, e = args
        return (jax.nn.silu(xblk @ w1[e]) * (xblk @ w3[e])) @ w2[e]

    yb = lax.map(expert_block, (xb, blk_e)).reshape(R, D)
    y = yb * row_gate[:, None].astype(yb.dtype)
    out = jnp.zeros((N + 1, D), y.dtype).at[row_tok].add(y)[:N]
    return out.reshape(B, T, D)


def setup_inputs(seed: int = 0) -> dict:
    key = jax.random.key(seed)
    ks = iter(jax.random.split(key, 64))
    D = D_MODEL

    def nrm(shape, scale):
        return jax.random.normal(next(ks), shape, jnp.float32) * scale

    def unif(shape, lo, hi):
        return jax.random.uniform(next(ks), shape, jnp.float32, lo, hi)

    n_a, n_b, n_c, n_d = [len(range(m, DEPTH, N_MIXERS)) for m in range(N_MIXERS)]
    dt0 = jnp.exp(unif((n_a, SSM_N_HEADS), math.log(1e-3), math.log(1e-1)))
    return {
        "x": nrm((BATCH, SEQ, D), 1.0),
        "c": nrm((BATCH, D), 1.0),
        "ada_w": nrm((DEPTH, D, 6 * D), 0.5 * D ** -0.5),
        "ada_b": nrm((DEPTH, 6 * D), 0.02),
        "norm_mix_w": 1.0 + nrm((DEPTH, D), 0.02),
        "norm_ffn_w": 1.0 + nrm((DEPTH, D), 0.02),
        "ssm_in_w": nrm((n_a, D, SSM_IN_DIM), D ** -0.5),
        "ssm_conv_w": nrm((n_a, SSM_CONV, SSM_CONV_DIM), SSM_CONV ** -0.5),
        "ssm_conv_b": nrm((n_a, SSM_CONV_DIM), 0.02),
        "ssm_dt_bias": dt0 + jnp.log(-jnp.expm1(-dt0)),
        "ssm_a_log": jnp.log(unif((n_a, SSM_N_HEADS), 1.0, 16.0)),
        "ssm_d": 1.0 + nrm((n_a, SSM_N_HEADS), 0.1),
        "ssm_norm_w": 1.0 + nrm((n_a, SSM_D_INNER), 0.02),
        "ssm_out_w": nrm((n_a, SSM_D_INNER, D), SSM_D_INNER ** -0.5),
        "rwkv_mix": unif((n_b, 6, D), 0.0, 1.0),
        "rwkv_rkv_w": nrm((n_b, 3, D, D), D ** -0.5),
        "rwkv_w0": unif((n_b, D), -6.0, -1.0),
        "rwkv_w1": nrm((n_b, D, RWKV_DECAY_LORA), D ** -0.5),
        "rwkv_w2": nrm((n_b, RWKV_DECAY_LORA, D), 0.1 * RWKV_DECAY_LORA ** -0.5),
        "rwkv_a0": nrm((n_b, D), 0.1),
        "rwkv_a1": nrm((n_b, D, RWKV_AAA_LORA), D ** -0.5),
        "rwkv_a2": nrm((n_b, RWKV_AAA_LORA, D), 0.1 * RWKV_AAA_LORA ** -0.5),
        "rwkv_g1": nrm((n_b, D, RWKV_GATE_LORA), D ** -0.5),
        "rwkv_g2": nrm((n_b, RWKV_GATE_LORA, D), RWKV_GATE_LORA ** -0.5),
        "rwkv_k_k": 0.85 + nrm((n_b, D), 0.05),
        "rwkv_k_a": 1.0 + nrm((n_b, D), 0.05),
        "rwkv_r_k": nrm((n_b, RWKV_N_HEADS, RWKV_HEAD_DIM), 0.1),
        "rwkv_ln_w": 1.0 + nrm((n_b, D), 0.02),
        "rwkv_ln_b": nrm((n_b, D), 0.02),
        "rwkv_out_w": nrm((n_b, D, D), D ** -0.5),
        "ret_in_w": nrm((n_c, D, RET_IN_DIM), D ** -0.5),
        "ret_out_w": nrm((n_c, 2 * D, D), (2 * D) ** -0.5),
        "att_in_w": nrm((n_d, D, ATT_IN_DIM), D ** -0.5),
        "att_out_w": nrm((n_d, ATT_OUT_DIM, D), ATT_OUT_DIM ** -0.5),
        "moe_group_w": nrm((DEPTH, D, MOE_GROUPS), D ** -0.5),
        "moe_group_b": nrm((DEPTH, MOE_GROUPS), 0.01),
        "moe_expert_w": nrm((DEPTH, D, MOE_EXPERTS), D ** -0.5),
        "moe_expert_b": nrm((DEPTH, MOE_EXPERTS), 0.01),
        "moe_w1": nrm((DEPTH, MOE_EXPERTS, D, MOE_D_FF), D ** -0.5),
        "moe_w3": nrm((DEPTH, MOE_EXPERTS, D, MOE_D_FF), D ** -0.5),
        "moe_w2": nrm((DEPTH, MOE_EXPERTS, MOE_D_FF, D), MOE_D_FF ** -0.5),
        "final_norm_w": 1.0 + nrm((D,), 0.02),
    }


def reference(x, c, ada_w, ada_b, norm_mix_w, norm_ffn_w,
              ssm_in_w, ssm_conv_w, ssm_conv_b, ssm_dt_bias, ssm_a_log, ssm_d, ssm_norm_w, ssm_out_w,
              rwkv_mix, rwkv_rkv_w, rwkv_w0, rwkv_w1, rwkv_w2, rwkv_a0, rwkv_a1, rwkv_a2,
              rwkv_g1, rwkv_g2, rwkv_k_k, rwkv_k_a, rwkv_r_k, rwkv_ln_w, rwkv_ln_b, rwkv_out_w,
              ret_in_w, ret_out_w, att_in_w, att_out_w,
              moe_group_w, moe_group_b, moe_expert_w, moe_expert_b, moe_w1, moe_w3, moe_w2,
              final_norm_w):
    cs = jax.nn.silu(c)
    for i in range(DEPTH):
        mod = cs @ ada_w[i] + ada_b[i]
        sh1, sc1, gt1, sh2, sc2, gt2 = jnp.split(mod, 6, axis=-1)
        hm = modulate(rms_norm(x, norm_mix_w[i]), sh1, sc1)
        kind, j = i % N_MIXERS, i // N_MIXERS
        if kind == 0:
            y = mamba2_ssd(hm, ssm_in_w[j], ssm_conv_w[j], ssm_conv_b[j], ssm_dt_bias[j],
                           ssm_a_log[j], ssm_d[j], ssm_norm_w[j], ssm_out_w[j])
        elif kind == 1:
            y = rwkv7_time_mix(hm, rwkv_mix[j], rwkv_rkv_w[j], rwkv_w0[j], rwkv_w1[j], rwkv_w2[j],
                               rwkv_a0[j], rwkv_a1[j], rwkv_a2[j], rwkv_g1[j], rwkv_g2[j],
                               rwkv_k_k[j], rwkv_k_a[j], rwkv_r_k[j], rwkv_ln_w[j], rwkv_ln_b[j],
                               rwkv_out_w[j])
        elif kind == 2:
            y = multiscale_retention(hm, ret_in_w[j], ret_out_w[j])
        else:
            y = dilated_attention(hm, att_in_w[j], att_out_w[j])
        x = x + gt1[:, None, :] * y
        hf = modulate(rms_norm(x, norm_ffn_w[i]), sh2, sc2)
        x = x + gt2[:, None, :] * hierarchical_moe(hf, moe_group_w[i], moe_group_b[i], moe_expert_w[i],
                                                    moe_expert_b[i], moe_w1[i], moe_w3[i], moe_w2[i])
    return rms_norm(x, final_norm_w)
```

```python
import functools
import math

import jax
import jax.numpy as jnp
from jax import lax
from jax.experimental import pallas as pl
from jax.experimental.pallas import tpu as pltpu

F32 = jnp.float32
BF16 = jnp.bfloat16
HI = lax.Precision.HIGHEST

NORM_EPS = 1e-6
LANES = 128
VMEM_LIMIT = 56 * 1024 * 1024

SSM_HEAD_DIM = 64
SSM_N_GROUPS = 8
SSM_HEADS_PER_GROUP = 4
SSM_D_STATE = 128
SSM_CONV = 4
SSM_CHUNK = 128
RWKV_HEAD_DIM = 64
RWKV_LN_EPS = 64e-5
RWKV_CHUNK = 64
RWKV_SUB = 16
RET_N_HEADS = 4
RET_CHUNK = 128
ATT_GROUPS = ((128, 1), (512, 4), (2048, 16))
ATT_HEADS_PER_GROUP = 8
ATT_HEAD_DIM = 128
ATT_BLOCK = 128
MOE_GROUPS = 4
MOE_EXPERTS_PER_GROUP = 8
MOE_EXPERTS = 32
MOE_ROWS = 256


def _cparams(*sem):
    return pltpu.CompilerParams(dimension_semantics=sem, vmem_limit_bytes=VMEM_LIMIT)


def _sigmoid(x):
    return 1.0 / (1.0 + jnp.exp(-x))


def _silu(x):
    return x * _sigmoid(x)


def _softplus(x):
    return jnp.maximum(x, 0.0) + jnp.log(1.0 + jnp.exp(-jnp.abs(x)))


def _dot(a, b, precision=None):
    return jnp.dot(a, b, preferred_element_type=F32, precision=precision)


def _dot_nt(a, b, precision=None):
    return lax.dot_general(a, b, (((1,), (1,)), ((), ())), preferred_element_type=F32,
                           precision=precision)


def _dot_tn(a, b, precision=None):
    return lax.dot_general(a, b, (((0,), (0,)), ((), ())), preferred_element_type=F32,
                           precision=precision)


def _iota2(shape, axis):
    return lax.broadcasted_iota(jnp.int32, shape, axis)


def _ada_kernel(c_ref, w_ref, b_ref, o_ref):
    cs = _silu(c_ref[...])
    o_ref[0] = _dot(cs, w_ref[0], HI) + b_ref[0]


def ada_modulation(c, ada_w, ada_b):
    depth, d, n = ada_w.shape
    bsz = c.shape[0]
    rows = -(-bsz // 8) * 8
    cp = jnp.pad(c, ((0, rows - bsz), (0, 0)))
    tn = 1536
    out = pl.pallas_call(
        _ada_kernel,
        grid=(depth, n // tn),
        in_specs=[pl.BlockSpec((rows, d), lambda l, j: (0, 0)),
                  pl.BlockSpec((1, d, tn), lambda l, j: (l, 0, j)),
                  pl.BlockSpec((1, 1, tn), lambda l, j: (l, 0, j))],
        out_specs=pl.BlockSpec((1, rows, tn), lambda l, j: (l, 0, j)),
        out_shape=jax.ShapeDtypeStruct((depth, rows, n), F32),
        compiler_params=_cparams("parallel", "parallel"),
        name="ada_modulation",
    )(cp, ada_w, ada_b.reshape(depth, 1, n))
    return out[:, :bsz]


def _norm_mod(x, nw, sh, sc):
    y = x * lax.rsqrt(jnp.mean(x * x, axis=-1, keepdims=True) + NORM_EPS) * nw
    return y * (1.0 + sc) + sh


def _nmm_kernel(x_ref, nw_ref, sh_ref, sc_ref, w_ref, o_ref, h_ref, *, precision):
    @pl.when(pl.program_id(2) == 0)
    def _():
        h_ref[...] = _norm_mod(x_ref[0], nw_ref[...], sh_ref[0], sc_ref[0]).astype(h_ref.dtype)

    o_ref[0] = _dot(h_ref[...], w_ref[...], precision).astype(o_ref.dtype)


def norm_mod_matmul(x, nw, sh, sc, w, *, tm, tn, out_dtype=BF16, precise=False):
    bsz, t, d = x.shape
    n = w.shape[1]
    tm = min(tm, t)
    kern = functools.partial(_nmm_kernel, precision=HI if precise else None)
    return pl.pallas_call(
        kern,
        grid=(bsz, t // tm, n // tn),
        in_specs=[pl.BlockSpec((1, tm, d), lambda b, i, j: (b, i, 0)),
                  pl.BlockSpec((1, d), lambda b, i, j: (0, 0)),
                  pl.BlockSpec((1, 1, d), lambda b, i, j: (b, 0, 0)),
                  pl.BlockSpec((1, 1, d), lambda b, i, j: (b, 0, 0)),
                  pl.BlockSpec((d, tn), lambda b, i, j: (0, j))],
        out_specs=pl.BlockSpec((1, tm, tn), lambda b, i, j: (b, i, j)),
        out_shape=jax.ShapeDtypeStruct((bsz, t, n), out_dtype),
        scratch_shapes=[pltpu.VMEM((tm, d), F32 if precise else BF16)],
        compiler_params=_cparams("parallel", "parallel", "arbitrary"),
        name="norm_mod_matmul",
    )(x, nw.reshape(1, d), sh, sc, w)


def _mgr_kernel(*refs, has_mul):
    if has_mul:
        y_ref, g_ref, w_ref, x_ref, gate_ref, o_ref = refs
        y = (y_ref[0].astype(F32) * g_ref[0].astype(F32)).astype(BF16)
    else:
        y_ref, w_ref, x_ref, gate_ref, o_ref = refs
        y = y_ref[0].astype(BF16)
    o_ref[0] = x_ref[0] + gate_ref[0] * _dot(y, w_ref[...])


def matmul_gated_residual(y, w, x, gate, mul=None, *, tm=512):
    bsz, t, k = y.shape
    d = w.shape[1]
    tm = min(tm, t)
    ins = [y] + ([mul] if mul is not None else []) + [w, x, gate]
    row = lambda width: pl.BlockSpec((1, tm, width), lambda b, i: (b, i, 0))
    specs = [row(k)] + ([row(k)] if mul is not None else []) + [
        pl.BlockSpec((k, d), lambda b, i: (0, 0)), row(d),
        pl.BlockSpec((1, 1, d), lambda b, i: (b, 0, 0))]
    return pl.pallas_call(
        functools.partial(_mgr_kernel, has_mul=mul is not None),
        grid=(bsz, t // tm),
        in_specs=specs,
        out_specs=row(d),
        out_shape=jax.ShapeDtypeStruct((bsz, t, d), F32),
        compiler_params=_cparams("parallel", "parallel"),
        name="matmul_gated_residual",
    )(*ins)


def _ssd_kernel(zx_ref, dt_ref, cw_ref, cb_ref, dtb_ref, alog_ref, dsk_ref, nw_ref, o_ref,
                carry_ref, state_ref):
    q = SSM_CHUNK
    g_n, hg, p, n = SSM_N_GROUPS, SSM_HEADS_PER_GROUP, SSM_HEAD_DIM, SSM_D_STATE
    d_inner = g_n * hg * p
    nh = g_n * hg
    cdim = d_inner + 2 * g_n * n

    @pl.when(pl.program_id(1) == 0)
    def _():
        carry_ref[...] = jnp.zeros_like(carry_ref)
        state_ref[...] = jnp.zeros_like(state_ref)

    cur = zx_ref[0, :, d_inner:].astype(F32)
    ext = jnp.concatenate([carry_ref[...], cur], axis=0)
    acc = jnp.zeros((q, cdim), F32) + cb_ref[...]
    for j in range(SSM_CONV):
        off = 8 - (SSM_CONV - 1) + j
        acc = acc + ext[off:off + q] * cw_ref[j:j + 1, :]
    carry_ref[...] = cur[q - 8:]
    xbc = _silu(acc)

    dt = _softplus(dt_ref[0][:, :nh] + dtb_ref[...])
    a = -jnp.exp(alog_ref[...])
    da = dt * a
    tril = (_iota2((q, q), 0) >= _iota2((q, q), 1))
    acum = _dot(tril.astype(F32), da, HI)
    acum_t = acum.T
    a_end = acum[q - 1:q, :]
    e_acum = jnp.exp(acum)
    decay_end = jnp.exp(a_end - acum)
    chunk_decay = jnp.exp(a_end)

    for g in range(g_n):
        xs_g = xbc[:, g * hg * p:(g + 1) * hg * p]
        bm = xbc[:, d_inner + g * n:d_inner + (g + 1) * n]
        cm = xbc[:, d_inner + g_n * n + g * n:d_inner + g_n * n + (g + 1) * n]
        bm_b = bm.astype(BF16)
        cm_b = cm.astype(BF16)
        cb = _dot_nt(cm_b, bm_b)
        st_old = state_ref[g]
        y_off = _dot_nt(cm_b, st_old.astype(BF16))
        ys = []
        for hh in range(hg):
            h = g * hg + hh
            xs_h = xs_g[:, hh * p:(hh + 1) * p]
            xdt = xs_h * dt[:, h:h + 1]
            seg = acum[:, h:h + 1] - acum_t[h:h + 1, :]
            lmat = jnp.exp(jnp.where(tril, seg, -jnp.inf))
            y_h = _dot((cb * lmat).astype(BF16), xdt.astype(BF16))
            y_h = y_h + y_off[:, hh * p:(hh + 1) * p] * e_acum[:, h:h + 1]
            y_h = y_h + xs_h * dsk_ref[:, h:h + 1]
            ys.append(y_h)
            st_new = _dot_tn((xdt * decay_end[:, h:h + 1]).astype(BF16), bm_b)
            state_ref[g, hh * p:(hh + 1) * p, :] = (
                st_old[hh * p:(hh + 1) * p, :] * chunk_decay[:, h:h + 1] + st_new)
        y = jnp.concatenate(ys, axis=1)
        z = zx_ref[0, :, g * hg * p:(g + 1) * hg * p].astype(F32)
        y = y * _silu(z)
        y = y * lax.rsqrt(jnp.mean(y * y, axis=-1, keepdims=True) + NORM_EPS)
        o_ref[0, :, g * hg * p:(g + 1) * hg * p] = (
            y * nw_ref[:, g * hg * p:(g + 1) * hg * p]).astype(o_ref.dtype)


def ssd_core(zx, dt_raw, conv_w, conv_b, dt_bias, a_log, d_skip, norm_w):
    bsz, t, width = zx.shape
    d_inner = SSM_N_GROUPS * SSM_HEADS_PER_GROUP * SSM_HEAD_DIM
    cdim = width - d_inner
    nh = dt_bias.shape[0]
    q = SSM_CHUNK
    full = lambda shape: pl.BlockSpec(shape, lambda b, c: (0,) * len(shape))
    return pl.pallas_call(
        _ssd_kernel,
        grid=(bsz, t // q),
        in_specs=[pl.BlockSpec((1, q, width), lambda b, c: (b, c, 0)),
                  pl.BlockSpec((1, q, LANES), lambda b, c: (b, c, 0)),
                  full((SSM_CONV, cdim)), full((1, cdim)), full((1, nh)), full((1, nh)),
                  full((1, nh)), full((1, d_inner))],
        out_specs=pl.BlockSpec((1, q, d_inner), lambda b, c: (b, c, 0)),
        out_shape=jax.ShapeDtypeStruct((bsz, t, d_inner), BF16),
        scratch_shapes=[pltpu.VMEM((8, cdim), F32),
                        pltpu.VMEM((SSM_N_GROUPS, SSM_HEADS_PER_GROUP * SSM_HEAD_DIM, SSM_D_STATE),
                                   F32)],
        compiler_params=_cparams("parallel", "arbitrary"),
        name="ssd_core",
    )(zx, dt_raw, conv_w, conv_b.reshape(1, cdim), dt_bias.reshape(1, nh),
      a_log.reshape(1, nh), d_skip.reshape(1, nh), norm_w.reshape(1, d_inner))


def mamba2_layer(x, nw, sh, sc, gate, in_w, conv_w, conv_b, dt_bias, a_log, d_skip, norm_w, out_w):
    d_inner = SSM_N_GROUPS * SSM_HEADS_PER_GROUP * SSM_HEAD_DIM
    cdim = conv_w.shape[1]
    nh = dt_bias.shape[0]
    w_main = in_w[:, :d_inner + cdim].astype(BF16)
    w_dt = jnp.pad(in_w[:, d_inner + cdim:], ((0, 0), (0, LANES - nh)))
    zx = norm_mod_matmul(x, nw, sh, sc, w_main, tm=1024, tn=1024)
    dt_raw = norm_mod_matmul(x, nw, sh, sc, w_dt, tm=1024, tn=LANES, out_dtype=F32, precise=True)
    y = ssd_core(zx, dt_raw, conv_w, conv_b, dt_bias, a_log, d_skip, norm_w)
    return matmul_gated_residual(y, out_w.astype(BF16), x, gate)


def _head_indicator(d, n):
    return jnp.where(_iota2((d, LANES), 0) // n == _iota2((d, LANES), 1), 1.0, 0.0)


def _rwkv_prep_kernel(x_ref, xp_ref, nw_ref, sh_ref, sc_ref, mix_ref, wr_ref, wk_ref, wv_ref,
                      w0_ref, w1_ref, w2_ref, a0_ref, a1_ref, a2_ref, g1_ref, g2_ref, kk_ref, ka_ref,
                      r_o, lw_o, k_o, v_o, kn_o, b_o, g_o):
    tm, d = x_ref.shape[1], x_ref.shape[2]
    nw, sh, sc = nw_ref[...], sh_ref[0], sc_ref[0]
    hm = _norm_mod(x_ref[0], nw, sh, sc)
    prev = _norm_mod(xp_ref[0], nw, sh, sc)[7:8, :]
    prev = jnp.where(pl.program_id(1) > 0, prev, 0.0)
    shifted = jnp.where(_iota2((tm, d), 0) == 0, prev, pltpu.roll(hm, 1, axis=0))
    xx = shifted - hm
    mixed = lambda j: (hm + xx * mix_ref[j:j + 1, :]).astype(BF16)
    xr, xw, xk, xv, xa, xg = [mixed(j) for j in range(6)]
    r = _dot(xr, wr_ref[...])
    k = _dot(xk, wk_ref[...])
    v = _dot(xv, wv_ref[...])
    lora = lambda u, w: _dot(u.astype(BF16), w[...])
    w_log = -_softplus(-(w0_ref[...] + lora(jnp.tanh(lora(xw, w1_ref)), w2_ref))) - 0.5
    a = _sigmoid(a0_ref[...] + lora(lora(xa, a1_ref), a2_ref))
    g = lora(_sigmoid(lora(xg, g1_ref)), g2_ref)
    kk = k * kk_ref[...]
    ind = _head_indicator(d, RWKV_HEAD_DIM)
    nrm = jnp.maximum(jnp.sqrt(_dot(kk * kk, ind, HI)), 1e-12)
    kn = kk * _dot_nt(1.0 / nrm, ind, HI)
    r_o[0] = r
    lw_o[0] = -jnp.exp(w_log)
    k_o[0] = k * (1.0 + (a - 1.0) * ka_ref[...])
    v_o[0] = v
    kn_o[0] = kn
    b_o[0] = kn * a
    g_o[0] = g.astype(g_o.dtype)


def rwkv_prep(x, nw, sh, sc, mix, rkv_w, w0, w1, w2, a0, a1, a2, g1, g2, k_k, k_a, *, tm=512):
    bsz, t, d = x.shape
    tm = min(tm, t)
    full = lambda arr: pl.BlockSpec(arr.shape, lambda b, i: (0,) * arr.ndim)
    row = pl.BlockSpec((1, tm, d), lambda b, i: (b, i, 0))
    vec = pl.BlockSpec((1, 1, d), lambda b, i: (b, 0, 0))
    bf = lambda w: w.astype(BF16)
    params = [mix, bf(rkv_w[0]), bf(rkv_w[1]), bf(rkv_w[2]), w0.reshape(1, d), bf(w1), bf(w2),
              a0.reshape(1, d), bf(a1), bf(a2), bf(g1), bf(g2), k_k.reshape(1, d), k_a.reshape(1, d)]
    prev_spec = pl.BlockSpec((1, 8, d), lambda b, i: (b, jnp.maximum(i * (tm // 8) - 1, 0), 0))
    return pl.pallas_call(
        _rwkv_prep_kernel,
        grid=(bsz, t // tm),
        in_specs=[row, prev_spec, pl.BlockSpec((1, d), lambda b, i: (0, 0)), vec, vec]
        + [full(p) for p in params],
        out_specs=[row] * 7,
        out_shape=[jax.ShapeDtypeStruct((bsz, t, d), F32)] * 6
        + [jax.ShapeDtypeStruct((bsz, t, d), BF16)],
        compiler_params=_cparams("parallel", "parallel"),
        name="rwkv_prep",
    )(x, x, nw.reshape(1, d), sh, sc, *params)


def _unit_lower_inverse(m_strict, length, sub):
    ri = _iota2((length, length), 0)
    ci = _iota2((length, length), 1)
    eye = jnp.where(ri == ci, 1.0, 0.0)
    mm = lambda a, b: _dot(a, b, HI)
    md = jnp.where(ri // sub == ci // sub, m_strict, 0.0)
    off = m_strict - md
    dinv = eye + md
    pw = md
    for _ in range(int(math.log2(sub)) - 1):
        pw = mm(pw, pw)
        dinv = dinv + mm(dinv, pw)
    pm = mm(dinv, off)
    acc = eye + pm
    pw = pm
    for _ in range(int(math.log2(length // sub)) - 1):
        pw = mm(pw, pw)
        acc = acc + mm(acc, pw)
    return mm(acc, dinv)


def _rwkv_core_kernel(r_ref, lw_ref, k_ref, v_ref, kn_ref, b_ref, rk_ref, lnw_ref, lnb_ref, o_ref,
                      state_ref):
    length, d = r_ref.shape[1], r_ref.shape[2]
    n = RWKV_HEAD_DIM

    @pl.when(pl.program_id(1) == 0)
    def _():
        state_ref[...] = jnp.zeros_like(state_ref)

    ri = _iota2((length, length), 0)
    ci = _iota2((length, length), 1)
    strict = ri > ci
    incl = ri >= ci
    eye_n = _iota2((n, n), 0) == _iota2((n, n), 1)

    r, lw, k, v, kn, b = r_ref[0], lw_ref[0], k_ref[0], v_ref[0], kn_ref[0], b_ref[0]
    cum = _dot(jnp.where(incl, 1.0, 0.0), lw, HI)
    c_end = cum[length - 1:length, :]
    e_neg = jnp.exp(-cum)
    e_rem = jnp.exp(c_end - cum)
    a_t = -kn * jnp.exp(cum - lw)
    b_t = b * e_neg
    k_t = k * e_neg
    r_t = r * jnp.exp(cum)
    b_h = b * e_rem
    k_h = k * e_rem
    w_end = jnp.exp(c_end)
    bonus_w = r * k * rk_ref[...]

    for h in range(d // n):
        hs = slice(h * n, (h + 1) * n)
        a_c, b_c, k_c, r_c, v_c = a_t[:, hs], b_t[:, hs], k_t[:, hs], r_t[:, hs], v[:, hs]
        m_ab = jnp.where(strict, _dot_nt(a_c, b_c, HI), 0.0)
        m_ak = jnp.where(strict, _dot_nt(a_c, k_c, HI), 0.0)
        t_inv = _unit_lower_inverse(m_ab, length, RWKV_SUB)
        s0 = state_ref[h]
        u = _dot(t_inv, _dot(a_c, s0, HI) + _dot(m_ak, v_c, HI), HI)
        w_rb = jnp.where(incl, _dot_nt(r_c, b_c, HI), 0.0)
        w_rk = jnp.where(incl, _dot_nt(r_c, k_c, HI), 0.0)
        o = _dot(r_c, s0, HI) + _dot(w_rb, u, HI) + _dot(w_rk, v_c, HI)
        w_col = jnp.sum(jnp.where(eye_n, w_end[:, hs], 0.0), axis=1, keepdims=True)
        state_ref[h] = w_col * s0 + _dot_tn(b_h[:, hs], u, HI) + _dot_tn(k_h[:, hs], v_c, HI)
        mu = jnp.mean(o, axis=-1, keepdims=True)
        var = jnp.mean((o - mu) ** 2, axis=-1, keepdims=True)
        y = (o - mu) * lax.rsqrt(var + RWKV_LN_EPS) * lnw_ref[:, hs] + lnb_ref[:, hs]
        y = y + jnp.sum(bonus_w[:, hs], axis=-1, keepdims=True) * v_c
        o_ref[0, :, hs] = y.astype(o_ref.dtype)


def rwkv_core(r, lw, k, v, kn, b, r_k, ln_w, ln_b):
    bsz, t, d = r.shape
    length = RWKV_CHUNK
    n = RWKV_HEAD_DIM
    row = pl.BlockSpec((1, length, d), lambda bb, c: (bb, c, 0))
    vec = pl.BlockSpec((1, d), lambda bb, c: (0, 0))
    return pl.pallas_call(
        _rwkv_core_kernel,
        grid=(bsz, t // length),
        in_specs=[row] * 6 + [vec] * 3,
        out_specs=row,
        out_shape=jax.ShapeDtypeStruct((bsz, t, d), BF16),
        scratch_shapes=[pltpu.VMEM((d // n, n, n), F32)],
        compiler_params=_cparams("parallel", "arbitrary"),
        name="rwkv_core",
    )(r, lw, k, v, kn, b, r_k.reshape(1, d), ln_w.reshape(1, d), ln_b.reshape(1, d))


def rwkv7_layer(x, nw, sh, sc, gate, mix, rkv_w, w0, w1, w2, a0, a1, a2, g1, g2, k_k, k_a, r_k,
                ln_w, ln_b, out_w):
    r, lw, k, v, kn, b, g = rwkv_prep(x, nw, sh, sc, mix, rkv_w, w0, w1, w2, a0, a1, a2, g1, g2,
                                      k_k, k_a)
    y = rwkv_core(r, lw, k, v, kn, b, r_k, ln_w, ln_b)
    return matmul_gated_residual(y, out_w.astype(BF16), x, gate, mul=g)


def _ret_kernel(qkvg_ref, idec_ref, qdec_ref, kdec_ref, cdec_ref, o_ref, state_ref, *, d_model):
    nh = RET_N_HEADS
    dk = d_model // nh
    dv = 2 * d_model // nh
    k_scale = dk ** -0.5

    @pl.when(pl.program_id(1) == 0)
    def _():
        state_ref[...] = jnp.zeros_like(state_ref)

    for h in range(nh):
        q_b = qkvg_ref[0, :, h * dk:(h + 1) * dk]
        k_f = qkvg_ref[0, :, d_model + h * dk:d_model + (h + 1) * dk].astype(F32) * k_scale
        v_b = qkvg_ref[0, :, 2 * d_model + h * dv:2 * d_model + (h + 1) * dv]
        g_f = qkvg_ref[0, :, 4 * d_model + h * dv:4 * d_model + (h + 1) * dv].astype(F32)
        scores = _dot_nt(q_b, k_f.astype(BF16)) * idec_ref[h]
        inner = _dot(scores.astype(BF16), v_b)
        r_old = state_ref[h]
        qd = (q_b.astype(F32) * qdec_ref[:, h:h + 1]).astype(BF16)
        cross = _dot(qd, r_old.astype(BF16))
        kd = (k_f * kdec_ref[:, h:h + 1]).astype(BF16)
        state_ref[h] = r_old * cdec_ref[:, h:h + 1] + _dot_tn(kd, v_b)
        o = inner + cross
        o = o * lax.rsqrt(jnp.mean(o * o, axis=-1, keepdims=True) + NORM_EPS)
        o_ref[0, :, h * dv:(h + 1) * dv] = (_silu(g_f) * o).astype(o_ref.dtype)


def retention_core(qkvg, d_model):
    bsz, t, width = qkvg.shape
    nh, q = RET_N_HEADS, RET_CHUNK
    dk, dv = d_model // nh, 2 * d_model // nh
    log_gamma = jnp.log(1 - jnp.exp2(-5.0 - jnp.arange(nh, dtype=F32)))
    idx = jnp.arange(q)
    rel = idx[:, None] - idx[None, :]
    inner_decay = jnp.where(rel >= 0,
                            jnp.exp(jnp.maximum(rel, 0).astype(F32) * log_gamma[:, None, None]), 0.0)
    q_decay = jnp.exp((idx + 1).astype(F32)[:, None] * log_gamma)
    k_decay = jnp.exp((q - 1 - idx).astype(F32)[:, None] * log_gamma)
    chunk_decay = jnp.exp(q * log_gamma).reshape(1, nh)
    full = lambda shape: pl.BlockSpec(shape, lambda b, c: (0,) * len(shape))
    return pl.pallas_call(
        functools.partial(_ret_kernel, d_model=d_model),
        grid=(bsz, t // q),
        in_specs=[pl.BlockSpec((1, q, width), lambda b, c: (b, c, 0)),
                  full((nh, q, q)), full((q, nh)), full((q, nh)), full((1, nh))],
        out_specs=pl.BlockSpec((1, q, nh * dv), lambda b, c: (b, c, 0)),
        out_shape=jax.ShapeDtypeStruct((bsz, t, nh * dv), BF16),
        scratch_shapes=[pltpu.VMEM((nh, dk, dv), F32)],
        compiler_params=_cparams("parallel", "arbitrary"),
        name="retention_core",
    )(qkvg, inner_decay, q_decay, k_decay, chunk_decay)


def retention_layer(x, nw, sh, sc, gate, in_w, out_w):
    d = x.shape[-1]
    qkvg = norm_mod_matmul(x, nw, sh, sc, in_w.astype(BF16), tm=1024, tn=1024)
    y = retention_core(qkvg, d)
    return matmul_gated_residual(y, out_w.astype(BF16), x, gate)


def _att_kernel(q_ref, kp_ref, kc_ref, vp_ref, vc_ref, sl_ref, o_ref, lse_ref, *, dil, span):
    w = ATT_BLOCK
    e = ATT_HEAD_DIM
    nblk = pl.program_id(2)
    qi = _iota2((w, 2 * w), 0)
    ci = _iota2((w, 2 * w), 1)
    delta = w + qi - ci
    mask = (delta >= 0) & (delta <= span) & ((nblk > 0) | (ci >= w))
    dist = (dil * delta).astype(F32)
    for h in range(ATT_HEADS_PER_GROUP):
        cols = slice(h * e, (h + 1) * e)
        q_b = q_ref[0, :, cols]
        kk = jnp.concatenate([kp_ref[0, :, cols], kc_ref[0, :, cols]], axis=0)
        vv = jnp.concatenate([vp_ref[0, :, cols], vc_ref[0, :, cols]], axis=0)
        s = _dot_nt(q_b, kk) * (e ** -0.5)
        s = jnp.where(mask, s - sl_ref[:, h:h + 1] * dist, -jnp.inf)
        m = jnp.max(s, axis=-1, keepdims=True)
        p = jnp.exp(s - m)
        den = jnp.sum(p, axis=-1, keepdims=True)
        o = _dot((p / den).astype(BF16), vv)
        o_ref[0, :, cols] = o.astype(o_ref.dtype)
        lse_ref[0, :, cols] = jnp.broadcast_to(m + jnp.log(den), (w, e))


def dilated_group_core(qkv, gi, window, dil, slopes):
    bsz, t, width = qkv.shape
    w = ATT_BLOCK
    gw = ATT_HEADS_PER_GROUP * ATT_HEAD_DIM
    ng = len(ATT_GROUPS)
    per_res = width // gw
    ls = t // dil
    nb = ls // w
    qkv_s = qkv.reshape(bsz, ls, dil * width)
    spec = lambda part, prev: pl.BlockSpec(
        (1, w, gw),
        (lambda b, r, n: (b, jnp.maximum(n - 1, 0), r * per_res + part * ng + gi)) if prev
        else (lambda b, r, n: (b, n, r * per_res + part * ng + gi)))
    out_spec = pl.BlockSpec((1, w, gw), lambda b, r, n: (b, n, r))
    o, lse = pl.pallas_call(
        functools.partial(_att_kernel, dil=dil, span=window // dil),
        grid=(bsz, dil, nb),
        in_specs=[spec(0, False), spec(1, True), spec(1, False), spec(2, True), spec(2, False),
                  pl.BlockSpec((1, ATT_HEADS_PER_GROUP), lambda b, r, n: (0, 0))],
        out_specs=[out_spec, out_spec],
        out_shape=[jax.ShapeDtypeStruct((bsz, ls, dil * gw), BF16),
                   jax.ShapeDtypeStruct((bsz, ls, dil * gw), F32)],
        compiler_params=_cparams("parallel", "parallel", "arbitrary"),
        name="dilated_attention_g%d" % gi,
    )(qkv_s, qkv_s, qkv_s, qkv_s, qkv_s, slopes.reshape(1, ATT_HEADS_PER_GROUP))
    return o.reshape(bsz, t, gw), lse.reshape(bsz, t, gw)


def _att_out_kernel(o0_ref, o1_ref, o2_ref, l0_ref, l1_ref, l2_ref, w_ref, x_ref, gate_ref, o_ref):
    l0, l1, l2 = l0_ref[0], l1_ref[0], l2_ref[0]
    m = jnp.maximum(jnp.maximum(l0, l1), l2)
    w0, w1, w2 = jnp.exp(l0 - m), jnp.exp(l1 - m), jnp.exp(l2 - m)
    den = w0 + w1 + w2
    y = (w0 * o0_ref[0].astype(F32) + w1 * o1_ref[0].astype(F32) + w2 * o2_ref[0].astype(F32)) / den
    o_ref[0] = x_ref[0] + gate_ref[0] * _dot(y.astype(BF16), w_ref[...])


def attention_combine_out(outs, lses, w, x, gate, *, tm=512):
    bsz, t, k = outs[0].shape
    d = w.shape[1]
    tm = min(tm, t)
    row = lambda width: pl.BlockSpec((1, tm, width), lambda b, i: (b, i, 0))
    return pl.pallas_call(
        _att_out_kernel,
        grid=(bsz, t // tm),
        in_specs=[row(k)] * 6 + [pl.BlockSpec((k, d), lambda b, i: (0, 0)), row(d),
                                 pl.BlockSpec((1, 1, d), lambda b, i: (b, 0, 0))],
        out_specs=row(d),
        out_shape=jax.ShapeDtypeStruct((bsz, t, d), F32),
        compiler_params=_cparams("parallel", "parallel"),
        name="attention_combine_out",
    )(*outs, *lses, w, x, gate)


def attention_layer(x, nw, sh, sc, gate, in_w, out_w):
    qkv = norm_mod_matmul(x, nw, sh, sc, in_w.astype(BF16), tm=1024, tn=1024)
    n_heads = len(ATT_GROUPS) * ATT_HEADS_PER_GROUP
    slopes = jnp.exp2(-8.0 * jnp.arange(1, n_heads + 1, dtype=F32) / n_heads)
    slopes = slopes.reshape(len(ATT_GROUPS), ATT_HEADS_PER_GROUP)
    outs, lses = [], []
    for gi, (window, dil) in enumerate(ATT_GROUPS):
        o, lse = dilated_group_core(qkv, gi, window, dil, slopes[gi])
        outs.append(o)
        lses.append(lse)
    return attention_combine_out(outs, lses, out_w.astype(BF16), x, gate)


ROUTER_GROUP_LANE0 = 0
ROUTER_EXPERT_LANE0 = MOE_GROUPS


def _router_kernel(x_ref, nw_ref, sh_ref, sc_ref, wr_ref, br_ref, hf_ref, meta_ref, cnt_ref,
                   carry_ref):
    tm = x_ref.shape[1]

    @pl.when(pl.program_id(1) == 0)
    def _():
        carry_ref[...] = jnp.zeros_like(carry_ref)

    hf = _norm_mod(x_ref[0], nw_ref[...], sh_ref[0], sc_ref[0])
    hf_ref[0] = hf
    logits = _dot(hf, wr_ref[...], HI) + br_ref[...]
    lane = _iota2((tm, LANES), 1)
    neg = -jnp.inf
    first = lambda hit: jnp.min(jnp.where(hit, lane, LANES), axis=-1, keepdims=True)

    gl = jnp.where(lane < MOE_GROUPS, logits, neg)
    gmax = jnp.max(gl, axis=-1, keepdims=True)
    gidx = first(gl == gmax)
    g_top = 1.0 / jnp.sum(jnp.exp(gl - gmax), axis=-1, keepdims=True)

    lo = ROUTER_EXPERT_LANE0 + gidx * MOE_EXPERTS_PER_GROUP
    el = jnp.where((lane >= lo) & (lane < lo + MOE_EXPERTS_PER_GROUP), logits, neg)
    m1 = jnp.max(el, axis=-1, keepdims=True)
    i1 = first(el == m1)
    el2 = jnp.where(lane == i1, neg, el)
    m2 = jnp.max(el2, axis=-1, keepdims=True)
    i2 = first(el2 == m2)
    ex = jnp.exp(m2 - m1)
    gate1 = g_top / (1.0 + ex)
    gate2 = g_top * ex / (1.0 + ex)

    hit1 = lane == i1
    hit2 = lane == i2
    onehot = jnp.where(hit1 | hit2, 1.0, 0.0).astype(BF16)
    strict = jnp.where(_iota2((tm, tm), 0) > _iota2((tm, tm), 1), 1.0, 0.0).astype(BF16)
    before = _dot(strict, onehot) + carry_ref[0:1, :]
    rank1 = jnp.sum(jnp.where(hit1, before, 0.0), axis=-1, keepdims=True)
    rank2 = jnp.sum(jnp.where(hit2, before, 0.0), axis=-1, keepdims=True)
    carry_ref[...] = carry_ref[...] + _dot(jnp.ones((8, tm), BF16), onehot)
    cnt_ref[0] = carry_ref[...]

    e1 = (i1 - ROUTER_EXPERT_LANE0).astype(F32)
    e2 = (i2 - ROUTER_EXPERT_LANE0).astype(F32)
    meta = jnp.zeros((tm, LANES), F32)
    for j, val in enumerate((e1, e2, rank1, rank2, gate1, gate2)):
        meta = jnp.where(lane == j, val, meta)
    meta_ref[0] = meta


def moe_router(x, nw, sh, sc, group_w, group_b, expert_w, expert_b, *, tm=512):
    bsz, t, d = x.shape
    tm = min(tm, t)
    pad = LANES - MOE_GROUPS - MOE_EXPERTS
    wr = jnp.pad(jnp.concatenate([group_w, expert_w], axis=1), ((0, 0), (0, pad)))
    br = jnp.pad(jnp.concatenate([group_b, expert_b]), (0, pad)).reshape(1, LANES)
    return pl.pallas_call(
        _router_kernel,
        grid=(bsz, t // tm),
        in_specs=[pl.BlockSpec((1, tm, d), lambda b, i: (b, i, 0)),
                  pl.BlockSpec((1, d), lambda b, i: (0, 0)),
                  pl.BlockSpec((1, 1, d), lambda b, i: (b, 0, 0)),
                  pl.BlockSpec((1, 1, d), lambda b, i: (b, 0, 0)),
                  pl.BlockSpec((d, LANES), lambda b, i: (0, 0)),
                  pl.BlockSpec((1, LANES), lambda b, i: (0, 0))],
        out_specs=[pl.BlockSpec((1, tm, d), lambda b, i: (b, i, 0)),
                   pl.BlockSpec((1, tm, LANES), lambda b, i: (b, i, 0)),
                   pl.BlockSpec((1, 8, LANES), lambda b, i: (b, 0, 0))],
        out_shape=[jax.ShapeDtypeStruct((bsz, t, d), F32),
                   jax.ShapeDtypeStruct((bsz, t, LANES), F32),
                   jax.ShapeDtypeStruct((bsz, 8, LANES), F32)],
        scratch_shapes=[pltpu.VMEM((8, LANES), F32)],
        compiler_params=_cparams("parallel", "arbitrary"),
        name="moe_router",
    )(x, nw.reshape(1, d), sh, sc, wr, br)


def _expert_kernel(e1_ref, e2_ref, r1_ref, r2_ref, g1_ref, g2_ref, cnt_ref,
                   hf_hbm, x_hbm, gt_ref, w1_ref, w3_ref, w2_ref, out_hbm,
                   hf_v, acc_v, xb_v, yb_v, wb1, wb3, wb2, row_tok, row_gate, pstart, sem):
    b = pl.program_id(0)
    e = pl.program_id(1)
    n_e = pl.num_programs(1)
    t = hf_v.shape[0]
    rows = xb_v.shape[0]

    @pl.when(e == 0)
    def _():
        cp_h = pltpu.make_async_copy(hf_hbm.at[b], hf_v, sem.at[0])
        cp_x = pltpu.make_async_copy(x_hbm.at[b], acc_v, sem.at[1])
        cp_h.start()
        cp_x.start()
        xb_v[...] = jnp.zeros_like(xb_v)

        def start_body(i, s):
            pstart[i] = s
            return s + cnt_ref[b * n_e + i]

        lax.fori_loop(0, n_e, start_body, 0)

        def tok_body(tok, carry):
            d1 = pstart[e1_ref[tok]] + r1_ref[tok]
            row_tok[d1] = tok
            row_gate[d1] = g1_ref[tok]
            d2 = pstart[e2_ref[tok]] + r2_ref[tok]
            row_tok[d2] = tok
            row_gate[d2] = g2_ref[tok]
            return carry

        lax.fori_loop(0, t, tok_body, 0)
        cp_h.wait()
        cp_x.wait()

    wb1[...] = w1_ref[0].astype(BF16)
    wb3[...] = w3_ref[0].astype(BF16)
    wb2[...] = w2_ref[0].astype(BF16)
    cnt = cnt_ref[b * n_e + e]
    start = pstart[e]

    def block_body(j, carry):
        base = start + j * rows
        nrows = jnp.minimum(rows, cnt - j * rows)

        def gather(i, c):
            tok = row_tok[base + i]
            xb_v[pl.ds(i, 1), :] = hf_v[pl.ds(tok, 1), :]
            return c

        lax.fori_loop(0, nrows, gather, 0)
        xb = xb_v[...].astype(BF16)
        h1 = _dot(xb, wb1[...])
        h3 = _dot(xb, wb3[...])
        act = (_silu(h1) * h3).astype(BF16)
        yb_v[...] = _dot(act, wb2[...]) * gt_ref[0]

        def scatter(i, c):
            tok = row_tok[base + i]
            acc_v[pl.ds(tok, 1), :] = acc_v[pl.ds(tok, 1), :] + row_gate[base + i] * yb_v[pl.ds(i, 1), :]
            return c

        lax.fori_loop(0, nrows, scatter, 0)
        return carry

    lax.fori_loop(0, (cnt + rows - 1) // rows, block_body, 0)

    @pl.when(e == n_e - 1)
    def _():
        cp_o = pltpu.make_async_copy(acc_v, out_hbm.at[b], sem.at[2])
        cp_o.start()
        cp_o.wait()


def moe_experts(hf, x, gt, e1, e2, r1, r2, g1, g2, counts, w1, w3, w2):
    bsz, t, d = x.shape
    n_e, _, f = w1.shape
    smem_tok = pl.BlockSpec((t,), lambda b, e: (b,), memory_space=pltpu.SMEM)
    return pl.pallas_call(
        _expert_kernel,
        grid=(bsz, n_e),
        in_specs=[smem_tok] * 6 + [
            pl.BlockSpec(memory_space=pltpu.SMEM),
            pl.BlockSpec(memory_space=pl.ANY),
            pl.BlockSpec(memory_space=pl.ANY),
            pl.BlockSpec((1, 1, d), lambda b, e: (b, 0, 0)),
            pl.BlockSpec((1, d, f), lambda b, e: (e, 0, 0)),
            pl.BlockSpec((1, d, f), lambda b, e: (e, 0, 0)),
            pl.BlockSpec((1, f, d), lambda b, e: (e, 0, 0))],
        out_specs=pl.BlockSpec(memory_space=pl.ANY),
        out_shape=jax.ShapeDtypeStruct((bsz, t, d), F32),
        scratch_shapes=[pltpu.VMEM((t, d), F32), pltpu.VMEM((t, d), F32),
                        pltpu.VMEM((MOE_ROWS, d), F32), pltpu.VMEM((MOE_ROWS, d), F32),
                        pltpu.VMEM((d, f), BF16), pltpu.VMEM((d, f), BF16), pltpu.VMEM((f, d), BF16),
                        pltpu.SMEM((2 * t,), jnp.int32), pltpu.SMEM((2 * t,), F32),
                        pltpu.SMEM((n_e,), jnp.int32), pltpu.SemaphoreType.DMA((3,))],
        compiler_params=_cparams("arbitrary", "arbitrary"),
        name="moe_experts",
    )(e1, e2, r1, r2, g1, g2, counts, hf, x, gt, w1, w3, w2)


def moe_layer(x, nw, sh, sc, gate, group_w, group_b, expert_w, expert_b, w1, w3, w2):
    bsz, t, d = x.shape
    hf, meta, cnt = moe_router(x, nw, sh, sc, group_w, group_b, expert_w, expert_b)
    ints = meta[:, :, :4].astype(jnp.int32).reshape(bsz * t, 4)
    gates = meta[:, :, 4:6].reshape(bsz * t, 2)
    counts = cnt[:, 0, ROUTER_EXPERT_LANE0:ROUTER_EXPERT_LANE0 + MOE_EXPERTS].astype(jnp.int32)
    return moe_experts(hf, x, gate, ints[:, 0], ints[:, 1], ints[:, 2], ints[:, 3],
                       gates[:, 0], gates[:, 1], counts.reshape(-1), w1, w3, w2)


def _final_norm_kernel(x_ref, w_ref, o_ref):
    x = x_ref[0]
    o_ref[0] = x * lax.rsqrt(jnp.mean(x * x, axis=-1, keepdims=True) + NORM_EPS) * w_ref[...]


def final_rms_norm(x, w, *, tm=1024):
    bsz, t, d = x.shape
    tm = min(tm, t)
    row = pl.BlockSpec((1, tm, d), lambda b, i: (b, i, 0))
    return pl.pallas_call(
        _final_norm_kernel,
        grid=(bsz, t // tm),
        in_specs=[row, pl.BlockSpec((1, d), lambda b, i: (0, 0))],
        out_specs=row,
        out_shape=jax.ShapeDtypeStruct((bsz, t, d), F32),
        compiler_params=_cparams("parallel", "parallel"),
        name="final_rms_norm",
    )(x, w.reshape(1, d))


def kernel(x, c, ada_w, ada_b, norm_mix_w, norm_ffn_w, ssm_in_w, ssm_conv_w, ssm_conv_b, ssm_dt_bias, ssm_a_log, ssm_d, ssm_norm_w, ssm_out_w, rwkv_mix, rwkv_rkv_w, rwkv_w0, rwkv_w1, rwkv_w2, rwkv_a0, rwkv_a1, rwkv_a2, rwkv_g1, rwkv_g2, rwkv_k_k, rwkv_k_a, rwkv_r_k, rwkv_ln_w, rwkv_ln_b, rwkv_out_w, ret_in_w, ret_out_w, att_in_w, att_out_w, moe_group_w, moe_group_b, moe_expert_w, moe_expert_b, moe_w1, moe_w3, moe_w2, final_norm_w):
    depth = ada_w.shape[0]
    d = x.shape[-1]
    mod = ada_modulation(c, ada_w, ada_b)
    for i in range(depth):
        sh1, sc1, gt1, sh2, sc2, gt2 = [mod[i][:, None, j * d:(j + 1) * d] for j in range(6)]
        kind, j = i % 4, i // 4
        pre = (x, norm_mix_w[i], sh1, sc1, gt1)
        if kind == 0:
            x = mamba2_layer(*pre, ssm_in_w[j], ssm_conv_w[j], ssm_conv_b[j], ssm_dt_bias[j],
                             ssm_a_log[j], ssm_d[j], ssm_norm_w[j], ssm_out_w[j])
        elif kind == 1:
            x = rwkv7_layer(*pre, rwkv_mix[j], rwkv_rkv_w[j], rwkv_w0[j], rwkv_w1[j], rwkv_w2[j],
                            rwkv_a0[j], rwkv_a1[j], rwkv_a2[j], rwkv_g1[j], rwkv_g2[j], rwkv_k_k[j],
                            rwkv_k_a[j], rwkv_r_k[j], rwkv_ln_w[j], rwkv_ln_b[j], rwkv_out_w[j])
        elif kind == 2:
            x = retention_layer(*pre, ret_in_w[j], ret_out_w[j])
        else:
            x = attention_layer(*pre, att_in_w[j], att_out_w[j])
        x = moe_layer(x, norm_ffn_w[i], sh2, sc2, gt2, moe_group_w[i], moe_group_b[i],
                      moe_expert_w[i], moe_expert_b[i], moe_w1[i], moe_w3[i], moe_w2[i])
    return final_rms_norm(x, final_norm_w)
```

```python
import functools
import math

import jax
import jax.numpy as jnp
from jax import lax
from jax.experimental import pallas as pl
from jax.experimental.pallas import tpu as pltpu

F32 = jnp.float32
BF16 = jnp.bfloat16
HI = lax.Precision.HIGHEST

NORM_EPS = 1e-6
LANES = 128
VMEM_LIMIT = 56 * 1024 * 1024

SSM_HEAD_DIM = 64
SSM_N_GROUPS = 8
SSM_HEADS_PER_GROUP = 4
SSM_D_STATE = 128
SSM_CONV = 4
SSM_CHUNK = 128
RWKV_HEAD_DIM = 64
RWKV_LN_EPS = 64e-5
RWKV_CHUNK = 64
RWKV_SUB = 16
RWKV_INV_PRECISION = None
RWKV_APPLY_PRECISION = None
RET_N_HEADS = 4
RET_CHUNK = 128
ATT_GROUPS = ((128, 1), (512, 4), (2048, 16))
ATT_HEADS_PER_GROUP = 8
ATT_HEAD_DIM = 128
ATT_BLOCK = 128
MOE_GROUPS = 4
MOE_EXPERTS_PER_GROUP = 8
MOE_EXPERTS = 32
MOE_ROWS = 256
MOE_ROW_GROUP = 8


def _cparams(*sem):
    return pltpu.CompilerParams(dimension_semantics=sem, vmem_limit_bytes=VMEM_LIMIT)


def _sigmoid(x):
    return 1.0 / (1.0 + jnp.exp(-x))


def _silu(x):
    return x * _sigmoid(x)


def _softplus(x):
    return jnp.maximum(x, 0.0) + jnp.log(1.0 + jnp.exp(-jnp.abs(x)))


def _dot(a, b, precision=None):
    return jnp.dot(a, b, preferred_element_type=F32, precision=precision)


def _dot_nt(a, b, precision=None):
    return lax.dot_general(a, b, (((1,), (1,)), ((), ())), preferred_element_type=F32,
                           precision=precision)


def _dot_tn(a, b, precision=None):
    return lax.dot_general(a, b, (((0,), (0,)), ((), ())), preferred_element_type=F32,
                           precision=precision)


def _iota2(shape, axis):
    return lax.broadcasted_iota(jnp.int32, shape, axis)


def _ada_kernel(c_ref, w_ref, b_ref, o_ref):
    cs = _silu(c_ref[...])
    o_ref[0] = _dot(cs, w_ref[0], HI) + b_ref[0]


def ada_modulation(c, ada_w, ada_b):
    depth, d, n = ada_w.shape
    bsz = c.shape[0]
    rows = -(-bsz // 8) * 8
    cp = jnp.pad(c, ((0, rows - bsz), (0, 0)))
    tn = 1536
    out = pl.pallas_call(
        _ada_kernel,
        grid=(depth, n // tn),
        in_specs=[pl.BlockSpec((rows, d), lambda l, j: (0, 0)),
                  pl.BlockSpec((1, d, tn), lambda l, j: (l, 0, j)),
                  pl.BlockSpec((1, 1, tn), lambda l, j: (l, 0, j))],
        out_specs=pl.BlockSpec((1, rows, tn), lambda l, j: (l, 0, j)),
        out_shape=jax.ShapeDtypeStruct((depth, rows, n), F32),
        compiler_params=_cparams("parallel", "parallel"),
        name="ada_modulation",
    )(cp, ada_w, ada_b.reshape(depth, 1, n))
    return out[:, :bsz]


def _norm_mod(x, nw, sh, sc):
    y = x * lax.rsqrt(jnp.mean(x * x, axis=-1, keepdims=True) + NORM_EPS) * nw
    return y * (1.0 + sc) + sh


def _nmm_kernel(x_ref, nw_ref, sh_ref, sc_ref, w_ref, o_ref, h_ref, *rest, precision, dil):
    @pl.when(pl.program_id(2) == 0)
    def _():
        h_ref[...] = _norm_mod(x_ref[0], nw_ref[...], sh_ref[0], sc_ref[0]).astype(h_ref.dtype)

    res = _dot(h_ref[...], w_ref[...], precision)
    if dil is None:
        o_ref[0] = res.astype(o_ref.dtype)
    else:
        acc_ref, = rest
        rows = acc_ref.shape[1] // dil
        for c in range(acc_ref.shape[0]):
            acc_ref[c] = res[:, c * LANES:(c + 1) * LANES]
        for r in range(dil):
            for c in range(acc_ref.shape[0]):
                o_ref[0, r, :, c * LANES:(c + 1) * LANES] = (
                    acc_ref[c, pl.ds(r, rows, stride=dil), :].astype(o_ref.dtype))


def norm_mod_matmul(x, nw, sh, sc, w, *, tm, tn, out_dtype=BF16, precise=False, dil=None):
    bsz, t, d = x.shape
    n = w.shape[1]
    tm = min(tm, t)
    kern = functools.partial(_nmm_kernel, precision=HI if precise else None, dil=dil)
    scratch = [pltpu.VMEM((tm, d), F32 if precise else BF16)]
    if dil is None:
        out_spec = pl.BlockSpec((1, tm, tn), lambda b, i, j: (b, i, j))
        out_shape = jax.ShapeDtypeStruct((bsz, t, n), out_dtype)
    else:
        out_spec = pl.BlockSpec((1, dil, tm // dil, tn), lambda b, i, j: (b, 0, i, j))
        out_shape = jax.ShapeDtypeStruct((bsz, dil, t // dil, n), out_dtype)
        scratch.append(pltpu.VMEM((tn // LANES, tm, LANES), F32))
    return pl.pallas_call(
        kern,
        grid=(bsz, t // tm, n // tn),
        in_specs=[pl.BlockSpec((1, tm, d), lambda b, i, j: (b, i, 0)),
                  pl.BlockSpec((1, d), lambda b, i, j: (0, 0)),
                  pl.BlockSpec((1, 1, d), lambda b, i, j: (b, 0, 0)),
                  pl.BlockSpec((1, 1, d), lambda b, i, j: (b, 0, 0)),
                  pl.BlockSpec((d, tn), lambda b, i, j: (0, j))],
        out_specs=out_spec,
        out_shape=out_shape,
        scratch_shapes=scratch,
        compiler_params=_cparams("parallel", "parallel", "arbitrary"),
        name="norm_mod_matmul",
    )(x, nw.reshape(1, d), sh, sc, w)


def _mgr_kernel(*refs, has_mul):
    if has_mul:
        y_ref, g_ref, w_ref, x_ref, gate_ref, o_ref = refs
        y = (y_ref[0].astype(F32) * g_ref[0].astype(F32)).astype(BF16)
    else:
        y_ref, w_ref, x_ref, gate_ref, o_ref = refs
        y = y_ref[0].astype(BF16)
    o_ref[0] = x_ref[0] + gate_ref[0] * _dot(y, w_ref[...])


def matmul_gated_residual(y, w, x, gate, mul=None, *, tm=512):
    bsz, t, k = y.shape
    d = w.shape[1]
    tm = min(tm, t)
    ins = [y] + ([mul] if mul is not None else []) + [w, x, gate]
    row = lambda width: pl.BlockSpec((1, tm, width), lambda b, i: (b, i, 0))
    specs = [row(k)] + ([row(k)] if mul is not None else []) + [
        pl.BlockSpec((k, d), lambda b, i: (0, 0)), row(d),
        pl.BlockSpec((1, 1, d), lambda b, i: (b, 0, 0))]
    return pl.pallas_call(
        functools.partial(_mgr_kernel, has_mul=mul is not None),
        grid=(bsz, t // tm),
        in_specs=specs,
        out_specs=row(d),
        out_shape=jax.ShapeDtypeStruct((bsz, t, d), F32),
        compiler_params=_cparams("parallel", "parallel"),
        name="matmul_gated_residual",
    )(*ins)


def _ssd_kernel(zx_ref, dt_ref, cw_ref, cb_ref, dtb_ref, alog_ref, dsk_ref, nw_ref, o_ref,
                carry_ref, state_ref):
    q = SSM_CHUNK
    g_n, hg, p, n = SSM_N_GROUPS, SSM_HEADS_PER_GROUP, SSM_HEAD_DIM, SSM_D_STATE
    d_inner = g_n * hg * p
    nh = g_n * hg
    cdim = d_inner + 2 * g_n * n

    @pl.when(pl.program_id(1) == 0)
    def _():
        carry_ref[...] = jnp.zeros_like(carry_ref)
        state_ref[...] = jnp.zeros_like(state_ref)

    cur = zx_ref[0, :, d_inner:].astype(F32)
    ext = jnp.concatenate([carry_ref[...], cur], axis=0)
    acc = jnp.zeros((q, cdim), F32) + cb_ref[...]
    for j in range(SSM_CONV):
        off = 8 - (SSM_CONV - 1) + j
        acc = acc + ext[off:off + q] * cw_ref[j:j + 1, :]
    carry_ref[...] = cur[q - 8:]
    xbc = _silu(acc)

    dt = _softplus(dt_ref[0][:, :nh] + dtb_ref[...])
    a = -jnp.exp(alog_ref[...])
    da = dt * a
    tril = (_iota2((q, q), 0) >= _iota2((q, q), 1))
    acum = _dot(tril.astype(F32), da, HI)
    acum_t = acum.T
    a_end = acum[q - 1:q, :]
    e_acum = jnp.exp(acum)
    decay_end = jnp.exp(a_end - acum)
    chunk_decay = jnp.exp(a_end)

    for g in range(g_n):
        xs_g = xbc[:, g * hg * p:(g + 1) * hg * p]
        bm = xbc[:, d_inner + g * n:d_inner + (g + 1) * n]
        cm = xbc[:, d_inner + g_n * n + g * n:d_inner + g_n * n + (g + 1) * n]
        bm_b = bm.astype(BF16)
        cm_b = cm.astype(BF16)
        cb = _dot_nt(cm_b, bm_b)
        st_old = state_ref[g]
        y_off = _dot_nt(cm_b, st_old.astype(BF16))
        ys = []
        for hh in range(hg):
            h = g * hg + hh
            xs_h = xs_g[:, hh * p:(hh + 1) * p]
            xdt = xs_h * dt[:, h:h + 1]
            seg = acum[:, h:h + 1] - acum_t[h:h + 1, :]
            lmat = jnp.exp(jnp.where(tril, seg, -jnp.inf))
            y_h = _dot((cb * lmat).astype(BF16), xdt.astype(BF16))
            y_h = y_h + y_off[:, hh * p:(hh + 1) * p] * e_acum[:, h:h + 1]
            y_h = y_h + xs_h * dsk_ref[:, h:h + 1]
            ys.append(y_h)
            st_new = _dot_tn((xdt * decay_end[:, h:h + 1]).astype(BF16), bm_b)
            state_ref[g, hh * p:(hh + 1) * p, :] = (
                st_old[hh * p:(hh + 1) * p, :] * chunk_decay[:, h:h + 1] + st_new)
        y = jnp.concatenate(ys, axis=1)
        z = zx_ref[0, :, g * hg * p:(g + 1) * hg * p].astype(F32)
        y = y * _silu(z)
        y = y * lax.rsqrt(jnp.mean(y * y, axis=-1, keepdims=True) + NORM_EPS)
        o_ref[0, :, g * hg * p:(g + 1) * hg * p] = (
            y * nw_ref[:, g * hg * p:(g + 1) * hg * p]).astype(o_ref.dtype)


def ssd_core(zx, dt_raw, conv_w, conv_b, dt_bias, a_log, d_skip, norm_w):
    bsz, t, width = zx.shape
    d_inner = SSM_N_GROUPS * SSM_HEADS_PER_GROUP * SSM_HEAD_DIM
    cdim = width - d_inner
    nh = dt_bias.shape[0]
    q = SSM_CHUNK
    full = lambda shape: pl.BlockSpec(shape, lambda b, c: (0,) * len(shape))
    return pl.pallas_call(
        _ssd_kernel,
        grid=(bsz, t // q),
        in_specs=[pl.BlockSpec((1, q, width), lambda b, c: (b, c, 0)),
                  pl.BlockSpec((1, q, LANES), lambda b, c: (b, c, 0)),
                  full((SSM_CONV, cdim)), full((1, cdim)), full((1, nh)), full((1, nh)),
                  full((1, nh)), full((1, d_inner))],
        out_specs=pl.BlockSpec((1, q, d_inner), lambda b, c: (b, c, 0)),
        out_shape=jax.ShapeDtypeStruct((bsz, t, d_inner), BF16),
        scratch_shapes=[pltpu.VMEM((8, cdim), F32),
                        pltpu.VMEM((SSM_N_GROUPS, SSM_HEADS_PER_GROUP * SSM_HEAD_DIM, SSM_D_STATE),
                                   F32)],
        compiler_params=_cparams("parallel", "arbitrary"),
        name="ssd_core",
    )(zx, dt_raw, conv_w, conv_b.reshape(1, cdim), dt_bias.reshape(1, nh),
      a_log.reshape(1, nh), d_skip.reshape(1, nh), norm_w.reshape(1, d_inner))


def mamba2_layer(x, nw, sh, sc, gate, in_w, conv_w, conv_b, dt_bias, a_log, d_skip, norm_w, out_w):
    d_inner = SSM_N_GROUPS * SSM_HEADS_PER_GROUP * SSM_HEAD_DIM
    cdim = conv_w.shape[1]
    nh = dt_bias.shape[0]
    w_main = in_w[:, :d_inner + cdim].astype(BF16)
    w_dt = jnp.pad(in_w[:, d_inner + cdim:], ((0, 0), (0, LANES - nh)))
    zx = norm_mod_matmul(x, nw, sh, sc, w_main, tm=1024, tn=1024)
    dt_raw = norm_mod_matmul(x, nw, sh, sc, w_dt, tm=1024, tn=LANES, out_dtype=F32, precise=True)
    y = ssd_core(zx, dt_raw, conv_w, conv_b, dt_bias, a_log, d_skip, norm_w)
    return matmul_gated_residual(y, out_w.astype(BF16), x, gate)


def _head_indicator(d, n):
    return jnp.where(_iota2((d, LANES), 0) // n == _iota2((d, LANES), 1), 1.0, 0.0)


def _rwkv_prep_kernel(x_ref, xp_ref, nw_ref, sh_ref, sc_ref, mix_ref, wr_ref, wk_ref, wv_ref,
                      w0_ref, w1_ref, w2_ref, a0_ref, a1_ref, a2_ref, g1_ref, g2_ref, kk_ref, ka_ref,
                      r_o, lw_o, k_o, v_o, kn_o, b_o, g_o):
    tm, d = x_ref.shape[1], x_ref.shape[2]
    nw, sh, sc = nw_ref[...], sh_ref[0], sc_ref[0]
    hm = _norm_mod(x_ref[0], nw, sh, sc)
    prev = _norm_mod(xp_ref[0], nw, sh, sc)[7:8, :]
    prev = jnp.where(pl.program_id(1) > 0, prev, 0.0)
    shifted = jnp.where(_iota2((tm, d), 0) == 0, prev, pltpu.roll(hm, 1, axis=0))
    xx = shifted - hm
    mixed = lambda j: (hm + xx * mix_ref[j:j + 1, :]).astype(BF16)
    xr, xw, xk, xv, xa, xg = [mixed(j) for j in range(6)]
    r = _dot(xr, wr_ref[...])
    k = _dot(xk, wk_ref[...])
    v = _dot(xv, wv_ref[...])
    lora = lambda u, w: _dot(u.astype(BF16), w[...])
    w_log = -_softplus(-(w0_ref[...] + lora(jnp.tanh(lora(xw, w1_ref)), w2_ref))) - 0.5
    a = _sigmoid(a0_ref[...] + lora(lora(xa, a1_ref), a2_ref))
    g = lora(_sigmoid(lora(xg, g1_ref)), g2_ref)
    kk = k * kk_ref[...]
    ind = _head_indicator(d, RWKV_HEAD_DIM)
    nrm = jnp.maximum(jnp.sqrt(_dot(kk * kk, ind, HI)), 1e-12)
    kn = kk * _dot_nt(1.0 / nrm, ind, HI)
    r_o[0] = r
    lw_o[0] = -jnp.exp(w_log)
    k_o[0] = k * (1.0 + (a - 1.0) * ka_ref[...])
    v_o[0] = v
    kn_o[0] = kn
    b_o[0] = kn * a
    g_o[0] = g.astype(g_o.dtype)


def rwkv_prep(x, nw, sh, sc, mix, rkv_w, w0, w1, w2, a0, a1, a2, g1, g2, k_k, k_a, *, tm=512):
    bsz, t, d = x.shape
    tm = min(tm, t)
    full = lambda arr: pl.BlockSpec(arr.shape, lambda b, i: (0,) * arr.ndim)
    row = pl.BlockSpec((1, tm, d), lambda b, i: (b, i, 0))
    vec = pl.BlockSpec((1, 1, d), lambda b, i: (b, 0, 0))
    bf = lambda w: w.astype(BF16)
    params = [mix, bf(rkv_w[0]), bf(rkv_w[1]), bf(rkv_w[2]), w0.reshape(1, d), bf(w1), bf(w2),
              a0.reshape(1, d), bf(a1), bf(a2), bf(g1), bf(g2), k_k.reshape(1, d), k_a.reshape(1, d)]
    prev_spec = pl.BlockSpec((1, 8, d), lambda b, i: (b, jnp.maximum(i * (tm // 8) - 1, 0), 0))
    return pl.pallas_call(
        _rwkv_prep_kernel,
        grid=(bsz, t // tm),
        in_specs=[row, prev_spec, pl.BlockSpec((1, d), lambda b, i: (0, 0)), vec, vec]
        + [full(p) for p in params],
        out_specs=[row] * 7,
        out_shape=[jax.ShapeDtypeStruct((bsz, t, d), F32)] * 6
        + [jax.ShapeDtypeStruct((bsz, t, d), BF16)],
        compiler_params=_cparams("parallel", "parallel"),
        name="rwkv_prep",
    )(x, x, nw.reshape(1, d), sh, sc, *params)


def _mm(a, b, precision):
    if precision is None:
        return _dot(a.astype(BF16), b.astype(BF16))
    return _dot(a, b, precision)


def _mm_nt(a, b, precision):
    if precision is None:
        return _dot_nt(a.astype(BF16), b.astype(BF16))
    return _dot_nt(a, b, precision)


def _mm_tn(a, b, precision):
    if precision is None:
        return _dot_tn(a.astype(BF16), b.astype(BF16))
    return _dot_tn(a, b, precision)


def _unit_lower_inverse(m_list, length, sub, precision):
    ri = _iota2((length, length), 0)
    ci = _iota2((length, length), 1)
    eye = jnp.where(ri == ci, 1.0, 0.0)
    same_block = ri // sub == ci // sub
    mm = lambda a, b: _mm(a, b, precision)
    md = [jnp.where(same_block, m, 0.0) for m in m_list]
    off = [m - d for m, d in zip(m_list, md)]
    dinv = [eye + d for d in md]
    pw = md
    for _ in range(int(math.log2(sub)) - 1):
        pw = [mm(p, p) for p in pw]
        dinv = [x + mm(x, p) for x, p in zip(dinv, pw)]
    pm = [mm(x, o) for x, o in zip(dinv, off)]
    acc = [eye + p for p in pm]
    pw = pm
    for _ in range(int(math.log2(length // sub)) - 1):
        pw = [mm(p, p) for p in pw]
        acc = [x + mm(x, p) for x, p in zip(acc, pw)]
    return [mm(x, d) for x, d in zip(acc, dinv)]


def _rwkv_core_kernel(r_ref, lw_ref, k_ref, v_ref, kn_ref, b_ref, rk_ref, lnw_ref, lnb_ref, o_ref,
                      state_ref, *, inv_precision, apply_precision):
    length, d = r_ref.shape[1], r_ref.shape[2]
    n = RWKV_HEAD_DIM
    heads = range(d // n)
    mm = lambda a, b: _mm(a, b, apply_precision)

    @pl.when(pl.program_id(1) == 0)
    def _():
        state_ref[...] = jnp.zeros_like(state_ref)

    ri = _iota2((2 * length, 2 * length), 0)
    ci = _iota2((2 * length, 2 * length), 1) % length
    keep = ((ri < length) & (ri > ci)) | ((ri >= length) & (ri - length >= ci))
    eye_n = _iota2((n, n), 0) == _iota2((n, n), 1)

    r, lw, k, v, kn, b = r_ref[0], lw_ref[0], k_ref[0], v_ref[0], kn_ref[0], b_ref[0]
    tri = jnp.where(_iota2((length, length), 0) >= _iota2((length, length), 1), 1.0, 0.0)
    cum = _dot(tri, lw, HI)
    c_end = cum[length - 1:length, :]
    e_neg = jnp.exp(-cum)
    e_rem = jnp.exp(c_end - cum)
    ar = jnp.concatenate([-kn * jnp.exp(cum - lw), r * jnp.exp(cum)], axis=0)
    bk = jnp.concatenate([b * e_neg, k * e_neg], axis=0)
    bk_end = jnp.concatenate([b * e_rem, k * e_rem], axis=0)
    w_end = jnp.exp(c_end)
    bonus_w = r * k * rk_ref[...]
    hs = [slice(h * n, (h + 1) * n) for h in heads]
    zero = jnp.zeros((length, n), F32)

    s0 = [state_ref[h] for h in heads]
    p = [jnp.where(keep, _mm_nt(ar[:, c], bk[:, c], inv_precision), 0.0) for c in hs]
    t_inv = _unit_lower_inverse([x[:length, :length] for x in p], length, RWKV_SUB, inv_precision)
    q = [mm(ar[:, c], s) for c, s in zip(hs, s0)]
    y = [x[:length] + mm(pp[:length], jnp.concatenate([zero, v[:, c]], axis=0))
         for x, pp, c in zip(q, p, hs)]
    u = [mm(ti, yy) for ti, yy in zip(t_inv, y)]
    uv = [jnp.concatenate([uu, v[:, c]], axis=0) for uu, c in zip(u, hs)]
    o = [x[length:] + mm(pp[length:], w) for x, pp, w in zip(q, p, uv)]
    for h in heads:
        c = hs[h]
        w_col = jnp.sum(jnp.where(eye_n, w_end[:, c], 0.0), axis=1, keepdims=True)
        state_ref[h] = w_col * s0[h] + _mm_tn(bk_end[:, c], uv[h], apply_precision)
        mu = jnp.mean(o[h], axis=-1, keepdims=True)
        var = jnp.mean((o[h] - mu) ** 2, axis=-1, keepdims=True)
        yh = (o[h] - mu) * lax.rsqrt(var + RWKV_LN_EPS) * lnw_ref[:, c] + lnb_ref[:, c]
        yh = yh + jnp.sum(bonus_w[:, c], axis=-1, keepdims=True) * v[:, c]
        o_ref[0, :, c] = yh.astype(o_ref.dtype)


def rwkv_core(r, lw, k, v, kn, b, r_k, ln_w, ln_b):
    bsz, t, d = r.shape
    length = RWKV_CHUNK
    n = RWKV_HEAD_DIM
    row = pl.BlockSpec((1, length, d), lambda bb, c: (bb, c, 0))
    vec = pl.BlockSpec((1, d), lambda bb, c: (0, 0))
    return pl.pallas_call(
        functools.partial(_rwkv_core_kernel, inv_precision=RWKV_INV_PRECISION,
                          apply_precision=RWKV_APPLY_PRECISION),
        grid=(bsz, t // length),
        in_specs=[row] * 6 + [vec] * 3,
        out_specs=row,
        out_shape=jax.ShapeDtypeStruct((bsz, t, d), BF16),
        scratch_shapes=[pltpu.VMEM((d // n, n, n), F32)],
        compiler_params=_cparams("parallel", "arbitrary"),
        name="rwkv_core",
    )(r, lw, k, v, kn, b, r_k.reshape(1, d), ln_w.reshape(1, d), ln_b.reshape(1, d))


def rwkv7_layer(x, nw, sh, sc, gate, mix, rkv_w, w0, w1, w2, a0, a1, a2, g1, g2, k_k, k_a, r_k,
                ln_w, ln_b, out_w):
    r, lw, k, v, kn, b, g = rwkv_prep(x, nw, sh, sc, mix, rkv_w, w0, w1, w2, a0, a1, a2, g1, g2,
                                      k_k, k_a)
    y = rwkv_core(r, lw, k, v, kn, b, r_k, ln_w, ln_b)
    return matmul_gated_residual(y, out_w.astype(BF16), x, gate, mul=g)


def _ret_kernel(qkvg_ref, idec_ref, qdec_ref, kdec_ref, cdec_ref, o_ref, state_ref, *, d_model):
    nh = RET_N_HEADS
    dk = d_model // nh
    dv = 2 * d_model // nh
    k_scale = dk ** -0.5

    @pl.when(pl.program_id(1) == 0)
    def _():
        state_ref[...] = jnp.zeros_like(state_ref)

    for h in range(nh):
        q_b = qkvg_ref[0, :, h * dk:(h + 1) * dk]
        k_f = qkvg_ref[0, :, d_model + h * dk:d_model + (h + 1) * dk].astype(F32) * k_scale
        v_b = qkvg_ref[0, :, 2 * d_model + h * dv:2 * d_model + (h + 1) * dv]
        g_f = qkvg_ref[0, :, 4 * d_model + h * dv:4 * d_model + (h + 1) * dv].astype(F32)
        scores = _dot_nt(q_b, k_f.astype(BF16)) * idec_ref[h]
        inner = _dot(scores.astype(BF16), v_b)
        r_old = state_ref[h]
        qd = (q_b.astype(F32) * qdec_ref[:, h:h + 1]).astype(BF16)
        cross = _dot(qd, r_old.astype(BF16))
        kd = (k_f * kdec_ref[:, h:h + 1]).astype(BF16)
        state_ref[h] = r_old * cdec_ref[:, h:h + 1] + _dot_tn(kd, v_b)
        o = inner + cross
        o = o * lax.rsqrt(jnp.mean(o * o, axis=-1, keepdims=True) + NORM_EPS)
        o_ref[0, :, h * dv:(h + 1) * dv] = (_silu(g_f) * o).astype(o_ref.dtype)


def retention_core(qkvg, d_model):
    bsz, t, width = qkvg.shape
    nh, q = RET_N_HEADS, RET_CHUNK
    dk, dv = d_model // nh, 2 * d_model // nh
    log_gamma = jnp.log(1 - jnp.exp2(-5.0 - jnp.arange(nh, dtype=F32)))
    idx = jnp.arange(q)
    rel = idx[:, None] - idx[None, :]
    inner_decay = jnp.where(rel >= 0,
                            jnp.exp(jnp.maximum(rel, 0).astype(F32) * log_gamma[:, None, None]), 0.0)
    q_decay = jnp.exp((idx + 1).astype(F32)[:, None] * log_gamma)
    k_decay = jnp.exp((q - 1 - idx).astype(F32)[:, None] * log_gamma)
    chunk_decay = jnp.exp(q * log_gamma).reshape(1, nh)
    full = lambda shape: pl.BlockSpec(shape, lambda b, c: (0,) * len(shape))
    return pl.pallas_call(
        functools.partial(_ret_kernel, d_model=d_model),
        grid=(bsz, t // q),
        in_specs=[pl.BlockSpec((1, q, width), lambda b, c: (b, c, 0)),
                  full((nh, q, q)), full((q, nh)), full((q, nh)), full((1, nh))],
        out_specs=pl.BlockSpec((1, q, nh * dv), lambda b, c: (b, c, 0)),
        out_shape=jax.ShapeDtypeStruct((bsz, t, nh * dv), BF16),
        scratch_shapes=[pltpu.VMEM((nh, dk, dv), F32)],
        compiler_params=_cparams("parallel", "arbitrary"),
        name="retention_core",
    )(qkvg, inner_decay, q_decay, k_decay, chunk_decay)


def retention_layer(x, nw, sh, sc, gate, in_w, out_w):
    d = x.shape[-1]
    qkvg = norm_mod_matmul(x, nw, sh, sc, in_w.astype(BF16), tm=1024, tn=1024)
    y = retention_core(qkvg, d)
    return matmul_gated_residual(y, out_w.astype(BF16), x, gate)


def _att_kernel(q_ref, kp_ref, kc_ref, vp_ref, vc_ref, sl_ref, o_ref, lse_ref, *, dil, span):
    w = ATT_BLOCK
    e = ATT_HEAD_DIM
    nblk = pl.program_id(2)
    qi = _iota2((w, 2 * w), 0)
    ci = _iota2((w, 2 * w), 1)
    delta = w + qi - ci
    mask = (delta >= 0) & (delta <= span) & ((nblk > 0) | (ci >= w))
    dist = (dil * delta).astype(F32)
    for h in range(ATT_HEADS_PER_GROUP):
        cols = slice(h * e, (h + 1) * e)
        q_b = q_ref[0, 0, :, cols]
        kk = jnp.concatenate([kp_ref[0, 0, :, cols], kc_ref[0, 0, :, cols]], axis=0)
        vv = jnp.concatenate([vp_ref[0, 0, :, cols], vc_ref[0, 0, :, cols]], axis=0)
        s = _dot_nt(q_b, kk) * (e ** -0.5)
        s = jnp.where(mask, s - sl_ref[:, h:h + 1] * dist, -jnp.inf)
        m = jnp.max(s, axis=-1, keepdims=True)
        p = jnp.exp(s - m)
        den = jnp.sum(p, axis=-1, keepdims=True)
        o = _dot((p / den).astype(BF16), vv)
        o_ref[0, 0, :, cols] = o.astype(o_ref.dtype)
        lse_ref[0, 0, :, cols] = jnp.broadcast_to(m + jnp.log(den), (w, e))


def dilated_group_core(qkv, gi, window, dil, slopes):
    bsz, _, ls, _ = qkv.shape
    w = ATT_BLOCK
    gw = ATT_HEADS_PER_GROUP * ATT_HEAD_DIM
    spec = lambda part, prev: pl.BlockSpec(
        (1, 1, w, gw),
        (lambda b, r, n: (b, r, jnp.maximum(n - 1, 0), part)) if prev
        else (lambda b, r, n: (b, r, n, part)))
    out_spec = pl.BlockSpec((1, 1, w, gw), lambda b, r, n: (b, r, n, 0))
    return pl.pallas_call(
        functools.partial(_att_kernel, dil=dil, span=window // dil),
        grid=(bsz, dil, ls // w),
        in_specs=[spec(0, False), spec(1, True), spec(1, False), spec(2, True), spec(2, False),
                  pl.BlockSpec((1, ATT_HEADS_PER_GROUP), lambda b, r, n: (0, 0))],
        out_specs=[out_spec, out_spec],
        out_shape=[jax.ShapeDtypeStruct((bsz, dil, ls, gw), BF16),
                   jax.ShapeDtypeStruct((bsz, dil, ls, gw), F32)],
        compiler_params=_cparams("parallel", "parallel", "arbitrary"),
        name="dilated_attention_g%d" % gi,
    )(qkv, qkv, qkv, qkv, qkv, slopes.reshape(1, ATT_HEADS_PER_GROUP))


def _att_out_kernel(o0_ref, o1_ref, o2_ref, l0_ref, l1_ref, l2_ref, w_ref, x_ref, gate_ref, o_ref,
                    *scratch):
    tm = x_ref.shape[1]

    def token_major(ref, scr):
        dil = ref.shape[1]
        if dil == 1:
            return ref[0, 0].astype(F32)
        for r in range(dil):
            blk = ref[0, r].astype(F32)
            for c in range(scr.shape[0]):
                scr[c, pl.ds(r, tm // dil, stride=dil), :] = blk[:, c * LANES:(c + 1) * LANES]
        return jnp.concatenate([scr[c] for c in range(scr.shape[0])], axis=1)

    o1, l1, o2, l2 = [token_major(ref, scr) for ref, scr in
                      zip((o1_ref, l1_ref, o2_ref, l2_ref), scratch)]
    o0, l0 = token_major(o0_ref, None), token_major(l0_ref, None)
    m = jnp.maximum(jnp.maximum(l0, l1), l2)
    w0, w1, w2 = jnp.exp(l0 - m), jnp.exp(l1 - m), jnp.exp(l2 - m)
    y = (w0 * o0 + w1 * o1 + w2 * o2) / (w0 + w1 + w2)
    o_ref[0] = x_ref[0] + gate_ref[0] * _dot(y.astype(BF16), w_ref[...])


def attention_combine_out(outs, lses, w, x, gate, *, tm=512):
    bsz, t, d = x.shape
    k = w.shape[0]
    tm = min(tm, t)
    row = pl.BlockSpec((1, tm, d), lambda b, i: (b, i, 0))
    res = lambda a: pl.BlockSpec((1, a.shape[1], tm // a.shape[1], k), lambda b, i: (b, 0, i, 0))
    return pl.pallas_call(
        _att_out_kernel,
        grid=(bsz, t // tm),
        in_specs=[res(a) for a in outs] + [res(a) for a in lses] + [
            pl.BlockSpec((k, d), lambda b, i: (0, 0)), row,
            pl.BlockSpec((1, 1, d), lambda b, i: (b, 0, 0))],
        out_specs=row,
        out_shape=jax.ShapeDtypeStruct((bsz, t, d), F32),
        scratch_shapes=[pltpu.VMEM((k // LANES, tm, LANES), F32)] * 4,
        compiler_params=_cparams("parallel", "parallel"),
        name="attention_combine_out",
    )(*outs, *lses, w, x, gate)


def attention_layer(x, nw, sh, sc, gate, in_w, out_w):
    n_groups = len(ATT_GROUPS)
    gw = ATT_HEADS_PER_GROUP * ATT_HEAD_DIM
    n_heads = n_groups * ATT_HEADS_PER_GROUP
    slopes = jnp.exp2(-8.0 * jnp.arange(1, n_heads + 1, dtype=F32) / n_heads)
    slopes = slopes.reshape(n_groups, ATT_HEADS_PER_GROUP)
    w_in = in_w.astype(BF16).reshape(in_w.shape[0], 3, n_groups, gw)
    outs, lses = [], []
    for gi, (window, dil) in enumerate(ATT_GROUPS):
        w_g = w_in[:, :, gi, :].reshape(in_w.shape[0], 3 * gw)
        qkv = norm_mod_matmul(x, nw, sh, sc, w_g, tm=1024, tn=gw, dil=dil)
        o, lse = dilated_group_core(qkv, gi, window, dil, slopes[gi])
        outs.append(o)
        lses.append(lse)
    return attention_combine_out(outs, lses, out_w.astype(BF16), x, gate)


ROUTER_GROUP_LANE0 = 0
ROUTER_EXPERT_LANE0 = MOE_GROUPS


def _router_kernel(x_ref, nw_ref, sh_ref, sc_ref, wr_ref, br_ref, hf_ref, meta_ref, cnt_ref,
                   carry_ref):
    tm = x_ref.shape[1]

    @pl.when(pl.program_id(1) == 0)
    def _():
        carry_ref[...] = jnp.zeros_like(carry_ref)

    hf = _norm_mod(x_ref[0], nw_ref[...], sh_ref[0], sc_ref[0])
    hf_ref[0] = hf
    logits = _dot(hf, wr_ref[...], HI) + br_ref[...]
    lane = _iota2((tm, LANES), 1)
    neg = -jnp.inf
    first = lambda hit: jnp.min(jnp.where(hit, lane, LANES), axis=-1, keepdims=True)

    gl = jnp.where(lane < MOE_GROUPS, logits, neg)
    gmax = jnp.max(gl, axis=-1, keepdims=True)
    gidx = first(gl == gmax)
    g_top = 1.0 / jnp.sum(jnp.exp(gl - gmax), axis=-1, keepdims=True)

    lo = ROUTER_EXPERT_LANE0 + gidx * MOE_EXPERTS_PER_GROUP
    el = jnp.where((lane >= lo) & (lane < lo + MOE_EXPERTS_PER_GROUP), logits, neg)
    m1 = jnp.max(el, axis=-1, keepdims=True)
    i1 = first(el == m1)
    el2 = jnp.where(lane == i1, neg, el)
    m2 = jnp.max(el2, axis=-1, keepdims=True)
    i2 = first(el2 == m2)
    ex = jnp.exp(m2 - m1)
    gate1 = g_top / (1.0 + ex)
    gate2 = g_top * ex / (1.0 + ex)

    hit1 = lane == i1
    hit2 = lane == i2
    onehot = jnp.where(hit1 | hit2, 1.0, 0.0).astype(BF16)
    strict = jnp.where(_iota2((tm, tm), 0) > _iota2((tm, tm), 1), 1.0, 0.0).astype(BF16)
    before = _dot(strict, onehot) + carry_ref[0:1, :]
    rank1 = jnp.sum(jnp.where(hit1, before, 0.0), axis=-1, keepdims=True)
    rank2 = jnp.sum(jnp.where(hit2, before, 0.0), axis=-1, keepdims=True)
    carry_ref[...] = carry_ref[...] + _dot(jnp.ones((8, tm), BF16), onehot)
    cnt_ref[0] = carry_ref[...]

    e1 = (i1 - ROUTER_EXPERT_LANE0).astype(F32)
    e2 = (i2 - ROUTER_EXPERT_LANE0).astype(F32)
    meta = jnp.zeros((tm, LANES), F32)
    for j, val in enumerate((e1, e2, rank1, rank2, gate1, gate2)):
        meta = jnp.where(lane == j, val, meta)
    meta_ref[0] = meta


def moe_router(x, nw, sh, sc, group_w, group_b, expert_w, expert_b, *, tm=512):
    bsz, t, d = x.shape
    tm = min(tm, t)
    pad = LANES - MOE_GROUPS - MOE_EXPERTS
    wr = jnp.pad(jnp.concatenate([group_w, expert_w], axis=1), ((0, 0), (0, pad)))
    br = jnp.pad(jnp.concatenate([group_b, expert_b]), (0, pad)).reshape(1, LANES)
    return pl.pallas_call(
        _router_kernel,
        grid=(bsz, t // tm),
        in_specs=[pl.BlockSpec((1, tm, d), lambda b, i: (b, i, 0)),
                  pl.BlockSpec((1, d), lambda b, i: (0, 0)),
                  pl.BlockSpec((1, 1, d), lambda b, i: (b, 0, 0)),
                  pl.BlockSpec((1, 1, d), lambda b, i: (b, 0, 0)),
                  pl.BlockSpec((d, LANES), lambda b, i: (0, 0)),
                  pl.BlockSpec((1, LANES), lambda b, i: (0, 0))],
        out_specs=[pl.BlockSpec((1, tm, d), lambda b, i: (b, i, 0)),
                   pl.BlockSpec((1, tm, LANES), lambda b, i: (b, i, 0)),
                   pl.BlockSpec((1, 8, LANES), lambda b, i: (b, 0, 0))],
        out_shape=[jax.ShapeDtypeStruct((bsz, t, d), F32),
                   jax.ShapeDtypeStruct((bsz, t, LANES), F32),
                   jax.ShapeDtypeStruct((bsz, 8, LANES), F32)],
        scratch_shapes=[pltpu.VMEM((8, LANES), F32)],
        compiler_params=_cparams("parallel", "arbitrary"),
        name="moe_router",
    )(x, nw.reshape(1, d), sh, sc, wr, br)


def _expert_kernel(e1_ref, e2_ref, r1_ref, r2_ref, g1_ref, g2_ref, cnt_ref,
                   hf_hbm, x_hbm, gt_ref, w1_ref, w3_ref, w2_ref, out_hbm,
                   hf_v, acc_v, xb_v, yb_v, wb1, wb3, wb2, row_tok, row_gate, pstart, sem):
    b = pl.program_id(0)
    e = pl.program_id(1)
    n_e = pl.num_programs(1)
    t = hf_v.shape[0]
    rows = xb_v.shape[0]
    grp = MOE_ROW_GROUP
    acc_rows = acc_v.at[pl.ds(0, t)]

    @pl.when(e == 0)
    def _():
        cp_h = pltpu.make_async_copy(hf_hbm.at[b], hf_v, sem.at[0])
        cp_x = pltpu.make_async_copy(x_hbm.at[b], acc_rows, sem.at[1])
        cp_h.start()
        cp_x.start()
        xb_v[...] = jnp.zeros_like(xb_v)
        acc_v[pl.ds(t, grp), :] = jnp.zeros((grp, acc_v.shape[1]), F32)
        for i in range(grp):
            row_tok[2 * t + i] = 0
            row_gate[2 * t + i] = 0.0

        def start_body(i, s):
            pstart[i] = s
            return s + cnt_ref[b * n_e + i]

        lax.fori_loop(0, n_e, start_body, 0)

        def tok_body(g, carry):
            for u in range(4):
                tok = g * 4 + u
                d1 = pstart[e1_ref[tok]] + r1_ref[tok]
                row_tok[d1] = tok
                row_gate[d1] = g1_ref[tok]
                d2 = pstart[e2_ref[tok]] + r2_ref[tok]
                row_tok[d2] = tok
                row_gate[d2] = g2_ref[tok]
            return carry

        lax.fori_loop(0, t // 4, tok_body, 0)
        cp_h.wait()
        cp_x.wait()

    wb1[...] = w1_ref[0].astype(BF16)
    wb3[...] = w3_ref[0].astype(BF16)
    wb2[...] = w2_ref[0].astype(BF16)
    cnt = cnt_ref[b * n_e + e]
    start = pstart[e]

    def block_body(j, carry):
        base = start + j * rows
        nrows = jnp.minimum(rows, cnt - j * rows)
        ngroups = (nrows + grp - 1) // grp

        def gather(g, c):
            for u in range(grp):
                i = g * grp + u
                xb_v[pl.ds(i, 1), :] = hf_v[pl.ds(row_tok[base + i], 1), :]
            return c

        lax.fori_loop(0, ngroups, gather, 0)
        xb = xb_v[...].astype(BF16)
        h1 = _dot(xb, wb1[...])
        h3 = _dot(xb, wb3[...])
        act = (_silu(h1) * h3).astype(BF16)
        yb_v[...] = _dot(act, wb2[...]) * gt_ref[0]

        def scatter(g, c):
            toks = []
            vals = []
            for u in range(grp):
                i = g * grp + u
                tok = jnp.where(i < nrows, row_tok[base + i], t)
                toks.append(tok)
                vals.append(acc_v[pl.ds(tok, 1), :] + row_gate[base + i] * yb_v[pl.ds(i, 1), :])
            for u in range(grp):
                acc_v[pl.ds(toks[u], 1), :] = vals[u]
            return c

        lax.fori_loop(0, ngroups, scatter, 0)
        return carry

    lax.fori_loop(0, (cnt + rows - 1) // rows, block_body, 0)

    @pl.when(e == n_e - 1)
    def _():
        cp_o = pltpu.make_async_copy(acc_rows, out_hbm.at[b], sem.at[2])
        cp_o.start()
        cp_o.wait()


def moe_experts(hf, x, gt, e1, e2, r1, r2, g1, g2, counts, w1, w3, w2):
    bsz, t, d = x.shape
    n_e, _, f = w1.shape
    smem_tok = pl.BlockSpec((t,), lambda b, e: (b,), memory_space=pltpu.SMEM)
    return pl.pallas_call(
        _expert_kernel,
        grid=(bsz, n_e),
        in_specs=[smem_tok] * 6 + [
            pl.BlockSpec(memory_space=pltpu.SMEM),
            pl.BlockSpec(memory_space=pl.ANY),
            pl.BlockSpec(memory_space=pl.ANY),
            pl.BlockSpec((1, 1, d), lambda b, e: (b, 0, 0)),
            pl.BlockSpec((1, d, f), lambda b, e: (e, 0, 0)),
            pl.BlockSpec((1, d, f), lambda b, e: (e, 0, 0)),
            pl.BlockSpec((1, f, d), lambda b, e: (e, 0, 0))],
        out_specs=pl.BlockSpec(memory_space=pl.ANY),
        out_shape=jax.ShapeDtypeStruct((bsz, t, d), F32),
        scratch_shapes=[pltpu.VMEM((t, d), F32), pltpu.VMEM((t + MOE_ROW_GROUP, d), F32),
                        pltpu.VMEM((MOE_ROWS, d), F32), pltpu.VMEM((MOE_ROWS, d), F32),
                        pltpu.VMEM((d, f), BF16), pltpu.VMEM((d, f), BF16), pltpu.VMEM((f, d), BF16),
                        pltpu.SMEM((2 * t + MOE_ROW_GROUP,), jnp.int32),
                        pltpu.SMEM((2 * t + MOE_ROW_GROUP,), F32),
                        pltpu.SMEM((n_e,), jnp.int32), pltpu.SemaphoreType.DMA((3,))],
        compiler_params=_cparams("arbitrary", "arbitrary"),
        name="moe_experts",
    )(e1, e2, r1, r2, g1, g2, counts, hf, x, gt, w1, w3, w2)


def moe_layer(x, nw, sh, sc, gate, group_w, group_b, expert_w, expert_b, w1, w3, w2):
    bsz, t, d = x.shape
    hf, meta, cnt = moe_router(x, nw, sh, sc, group_w, group_b, expert_w, expert_b)
    ints = meta[:, :, :4].astype(jnp.int32).reshape(bsz * t, 4)
    gates = meta[:, :, 4:6].reshape(bsz * t, 2)
    counts = cnt[:, 0, ROUTER_EXPERT_LANE0:ROUTER_EXPERT_LANE0 + MOE_EXPERTS].astype(jnp.int32)
    return moe_experts(hf, x, gate, ints[:, 0], ints[:, 1], ints[:, 2], ints[:, 3],
                       gates[:, 0], gates[:, 1], counts.reshape(-1), w1, w3, w2)


def _final_norm_kernel(x_ref, w_ref, o_ref):
    x = x_ref[0]
    o_ref[0] = x * lax.rsqrt(jnp.mean(x * x, axis=-1, keepdims=True) + NORM_EPS) * w_ref[...]


def final_rms_norm(x, w, *, tm=1024):
    bsz, t, d = x.shape
    tm = min(tm, t)
    row = pl.BlockSpec((1, tm, d), lambda b, i: (b, i, 0))
    return pl.pallas_call(
        _final_norm_kernel,
        grid=(bsz, t // tm),
        in_specs=[row, pl.BlockSpec((1, d), lambda b, i: (0, 0))],
        out_specs=row,
        out_shape=jax.ShapeDtypeStruct((bsz, t, d), F32),
        compiler_params=_cparams("parallel", "parallel"),
        name="final_rms_norm",
    )(x, w.reshape(1, d))


def kernel(x, c, ada_w, ada_b, norm_mix_w, norm_ffn_w, ssm_in_w, ssm_conv_w, ssm_conv_b, ssm_dt_bias, ssm_a_log, ssm_d, ssm_norm_w, ssm_out_w, rwkv_mix, rwkv_rkv_w, rwkv_w0, rwkv_w1, rwkv_w2, rwkv_a0, rwkv_a1, rwkv_a2, rwkv_g1, rwkv_g2, rwkv_k_k, rwkv_k_a, rwkv_r_k, rwkv_ln_w, rwkv_ln_b, rwkv_out_w, ret_in_w, ret_out_w, att_in_w, att_out_w, moe_group_w, moe_group_b, moe_expert_w, moe_expert_b, moe_w1, moe_w3, moe_w2, final_norm_w):
    depth = ada_w.shape[0]
    d = x.shape[-1]
    mod = ada_modulation(c, ada_w, ada_b)
    for i in range(depth):
        sh1, sc1, gt1, sh2, sc2, gt2 = [mod[i][:, None, j * d:(j + 1) * d] for j in range(6)]
        kind, j = i % 4, i // 4
        pre = (x, norm_mix_w[i], sh1, sc1, gt1)
        if kind == 0:
            x = mamba2_layer(*pre, ssm_in_w[j], ssm_conv_w[j], ssm_conv_b[j], ssm_dt_bias[j],
                             ssm_a_log[j], ssm_d[j], ssm_norm_w[j], ssm_out_w[j])
        elif kind == 1:
            x = rwkv7_layer(*pre, rwkv_mix[j], rwkv_rkv_w[j], rwkv_w0[j], rwkv_w1[j], rwkv_w2[j],
                            rwkv_a0[j], rwkv_a1[j], rwkv_a2[j], rwkv_g1[j], rwkv_g2[j], rwkv_k_k[j],
                            rwkv_k_a[j], rwkv_r_k[j], rwkv_ln_w[j], rwkv_ln_b[j], rwkv_out_w[j])
        elif kind == 2:
            x = retention_layer(*pre, ret_in_w[j], ret_out_w[j])
        else:
            x = attention_layer(*pre, att_in_w[j], att_out_w[j])
        x = moe_layer(x, norm_ffn_w[i], sh2, sc2, gt2, moe_group_w[i], moe_group_b[i],
                      moe_expert_w[i], moe_expert_b[i], moe_w1[i], moe_w3[i], moe_w2[i])
    return final_rms_norm(x, final_norm_w)
```

```python
import functools
import math

import jax
import jax.numpy as jnp
from jax import lax
from jax.experimental import pallas as pl
from jax.experimental.pallas import tpu as pltpu

F32 = jnp.float32
BF16 = jnp.bfloat16
HI = lax.Precision.HIGHEST

NORM_EPS = 1e-6
LANES = 128
VMEM_LIMIT = 56 * 1024 * 1024

SSM_HEAD_DIM = 64
SSM_N_GROUPS = 8
SSM_HEADS_PER_GROUP = 4
SSM_D_STATE = 128
SSM_CONV = 4
SSM_CHUNK = 128
RWKV_HEAD_DIM = 64
RWKV_LN_EPS = 64e-5
RWKV_CHUNK = 64
RWKV_SUB = 16
RWKV_INV_PRECISION = None
RWKV_APPLY_PRECISION = None
RET_N_HEADS = 4
RET_CHUNK = 128
ATT_GROUPS = ((128, 1), (512, 4), (2048, 16))
ATT_HEADS_PER_GROUP = 8
ATT_HEAD_DIM = 128
ATT_BLOCK = 128
MOE_GROUPS = 4
MOE_EXPERTS_PER_GROUP = 8
MOE_EXPERTS = 32
MOE_ROWS = 256
MOE_ROW_GROUP = 8


def _cparams(*sem):
    return pltpu.CompilerParams(dimension_semantics=sem, vmem_limit_bytes=VMEM_LIMIT)


def _sigmoid(x):
    return 1.0 / (1.0 + jnp.exp(-x))


def _silu(x):
    return x * _sigmoid(x)


def _softplus(x):
    return jnp.maximum(x, 0.0) + jnp.log(1.0 + jnp.exp(-jnp.abs(x)))


def _dot(a, b, precision=None):
    return jnp.dot(a, b, preferred_element_type=F32, precision=precision)


def _dot_nt(a, b, precision=None):
    return lax.dot_general(a, b, (((1,), (1,)), ((), ())), preferred_element_type=F32,
                           precision=precision)


def _dot_tn(a, b, precision=None):
    return lax.dot_general(a, b, (((0,), (0,)), ((), ())), preferred_element_type=F32,
                           precision=precision)


def _split_bf16(a):
    hi = a.astype(BF16)
    return hi, (a - hi.astype(F32)).astype(BF16)


def _dot_split(a, m, transpose_rhs=False, lhs_exact=False):
    hi, lo = _split_bf16(a)
    mb = m.astype(BF16)
    if lhs_exact:
        return _dot(mb, hi) + _dot(mb, lo)
    dot = _dot_nt if transpose_rhs else _dot
    return dot(hi, mb) + dot(lo, mb)


def _dot_3pass(a, b):
    a_hi, a_lo = _split_bf16(a)
    b_hi, b_lo = _split_bf16(b)
    return _dot(a_hi, b_hi) + (_dot(a_lo, b_hi) + _dot(a_hi, b_lo))


def _iota2(shape, axis):
    return lax.broadcasted_iota(jnp.int32, shape, axis)


def _ada_kernel(c_ref, w_ref, b_ref, o_ref):
    cs = _silu(c_ref[...])
    o_ref[0] = _dot(cs, w_ref[0], HI) + b_ref[0]


def ada_modulation(c, ada_w, ada_b):
    depth, d, n = ada_w.shape
    bsz = c.shape[0]
    rows = -(-bsz // 8) * 8
    cp = jnp.pad(c, ((0, rows - bsz), (0, 0)))
    tn = 1536
    out = pl.pallas_call(
        _ada_kernel,
        grid=(depth, n // tn),
        in_specs=[pl.BlockSpec((rows, d), lambda l, j: (0, 0)),
                  pl.BlockSpec((1, d, tn), lambda l, j: (l, 0, j)),
                  pl.BlockSpec((1, 1, tn), lambda l, j: (l, 0, j))],
        out_specs=pl.BlockSpec((1, rows, tn), lambda l, j: (l, 0, j)),
        out_shape=jax.ShapeDtypeStruct((depth, rows, n), F32),
        compiler_params=_cparams("parallel", "parallel"),
        name="ada_modulation",
    )(cp, ada_w, ada_b.reshape(depth, 1, n))
    return out[:, :bsz]


def _norm_mod(x, nw, sh, sc):
    y = x * lax.rsqrt(jnp.mean(x * x, axis=-1, keepdims=True) + NORM_EPS) * nw
    return y * (1.0 + sc) + sh


def _nmm_kernel(*refs, dil, has_side):
    if has_side:
        x_ref, nw_ref, sh_ref, sc_ref, w_ref, ws_ref, o_ref, side_ref, h_ref, *rest = refs
    else:
        x_ref, nw_ref, sh_ref, sc_ref, w_ref, o_ref, h_ref, *rest = refs

    @pl.when(pl.program_id(2) == 0)
    def _():
        h = _norm_mod(x_ref[0], nw_ref[...], sh_ref[0], sc_ref[0]).astype(h_ref.dtype)
        h_ref[...] = h
        if has_side:
            w_hi, w_lo = _split_bf16(ws_ref[...])
            side_ref[0] = _dot(h, w_hi) + _dot(h, w_lo)

    res = _dot(h_ref[...], w_ref[...])
    if dil is None:
        o_ref[0] = res.astype(o_ref.dtype)
    else:
        acc_ref, = rest
        rows = acc_ref.shape[1] // dil
        for c in range(acc_ref.shape[0]):
            acc_ref[c] = res[:, c * LANES:(c + 1) * LANES]
        for r in range(dil):
            for c in range(acc_ref.shape[0]):
                o_ref[0, r, :, c * LANES:(c + 1) * LANES] = (
                    acc_ref[c, pl.ds(r, rows, stride=dil), :].astype(o_ref.dtype))


def norm_mod_matmul(x, nw, sh, sc, w, *, tm, tn, dil=None, side_w=None):
    bsz, t, d = x.shape
    n = w.shape[1]
    tm = min(tm, t)
    scratch = [pltpu.VMEM((tm, d), BF16)]
    in_specs = [pl.BlockSpec((1, tm, d), lambda b, i, j: (b, i, 0)),
                pl.BlockSpec((1, d), lambda b, i, j: (0, 0)),
                pl.BlockSpec((1, 1, d), lambda b, i, j: (b, 0, 0)),
                pl.BlockSpec((1, 1, d), lambda b, i, j: (b, 0, 0)),
                pl.BlockSpec((d, tn), lambda b, i, j: (0, j))]
    ins = [x, nw.reshape(1, d), sh, sc, w]
    if dil is None:
        out_specs = [pl.BlockSpec((1, tm, tn), lambda b, i, j: (b, i, j))]
        out_shape = [jax.ShapeDtypeStruct((bsz, t, n), BF16)]
    else:
        out_specs = [pl.BlockSpec((1, dil, tm // dil, tn), lambda b, i, j: (b, 0, i, j))]
        out_shape = [jax.ShapeDtypeStruct((bsz, dil, t // dil, n), BF16)]
        scratch.append(pltpu.VMEM((tn // LANES, tm, LANES), F32))
    if side_w is not None:
        in_specs.append(pl.BlockSpec((d, LANES), lambda b, i, j: (0, 0)))
        ins.append(side_w)
        out_specs.append(pl.BlockSpec((1, tm, LANES), lambda b, i, j: (b, i, 0)))
        out_shape.append(jax.ShapeDtypeStruct((bsz, t, LANES), F32))
    outs = pl.pallas_call(
        functools.partial(_nmm_kernel, dil=dil, has_side=side_w is not None),
        grid=(bsz, t // tm, n // tn),
        in_specs=in_specs,
        out_specs=out_specs,
        out_shape=out_shape,
        scratch_shapes=scratch,
        compiler_params=_cparams("parallel", "parallel", "arbitrary"),
        name="norm_mod_matmul",
    )(*ins)
    return outs if side_w is not None else outs[0]


def _mgr_kernel(*refs, has_mul):
    if has_mul:
        y_ref, g_ref, w_ref, x_ref, gate_ref, o_ref = refs
        y = (y_ref[0].astype(F32) * g_ref[0].astype(F32)).astype(BF16)
    else:
        y_ref, w_ref, x_ref, gate_ref, o_ref = refs
        y = y_ref[0].astype(BF16)
    o_ref[0] = x_ref[0] + gate_ref[0] * _dot(y, w_ref[...])


def matmul_gated_residual(y, w, x, gate, mul=None, *, tm=512):
    bsz, t, k = y.shape
    d = w.shape[1]
    tm = min(tm, t)
    ins = [y] + ([mul] if mul is not None else []) + [w, x, gate]
    row = lambda width: pl.BlockSpec((1, tm, width), lambda b, i: (b, i, 0))
    specs = [row(k)] + ([row(k)] if mul is not None else []) + [
        pl.BlockSpec((k, d), lambda b, i: (0, 0)), row(d),
        pl.BlockSpec((1, 1, d), lambda b, i: (b, 0, 0))]
    return pl.pallas_call(
        functools.partial(_mgr_kernel, has_mul=mul is not None),
        grid=(bsz, t // tm),
        in_specs=specs,
        out_specs=row(d),
        out_shape=jax.ShapeDtypeStruct((bsz, t, d), F32),
        compiler_params=_cparams("parallel", "parallel"),
        name="matmul_gated_residual",
    )(*ins)


def _ssd_kernel(zx_ref, dt_ref, cw_ref, cb_ref, dtb_ref, alog_ref, dsk_ref, nw_ref, o_ref,
                prev_ref, state_ref):
    q = SSM_CHUNK
    g_n, hg, p, n = SSM_N_GROUPS, SSM_HEADS_PER_GROUP, SSM_HEAD_DIM, SSM_D_STATE
    d_inner = g_n * hg * p
    nh = g_n * hg
    gw = hg * p

    @pl.when(pl.program_id(1) == 0)
    def _():
        prev_ref[...] = jnp.zeros_like(prev_ref)
        state_ref[...] = jnp.zeros_like(state_ref)

    cur = zx_ref[0, :, d_inner:]
    ext = jnp.concatenate([prev_ref[...], cur], axis=0)
    prev_ref[...] = cur
    src = _iota2((q, 2 * q), 1) - q - _iota2((q, 2 * q), 0)
    acc = cb_ref[...] + cur.astype(F32) * cw_ref[SSM_CONV - 1:SSM_CONV, :]
    for s in range(1, SSM_CONV):
        shift = jnp.where(src == -s, 1.0, 0.0).astype(BF16)
        acc = acc + _dot(shift, ext) * cw_ref[SSM_CONV - 1 - s:SSM_CONV - s, :]
    xbc = _silu(acc)

    dt = _softplus(dt_ref[0][:, :nh] + dtb_ref[...])
    da = dt * -jnp.exp(alog_ref[...])
    tril = (_iota2((q, q), 0) >= _iota2((q, q), 1))
    acum = _dot_split(da, tril.astype(F32), lhs_exact=True)
    a_end = acum[q - 1:q, :]
    chunk_decay = jnp.exp(a_end)
    acum_t = acum.T
    dt_t = dt.T
    wdec_t = (dt * jnp.exp(a_end - acum)).T

    for g in range(g_n):
        xs_g = xbc[:, g * gw:(g + 1) * gw]
        bm = xbc[:, d_inner + g * n:d_inner + (g + 1) * n]
        cm = xbc[:, d_inner + g_n * n + g * n:d_inner + g_n * n + (g + 1) * n]
        cb = _dot_nt(cm.astype(BF16), bm.astype(BF16))
        bm_t = bm.T
        xs_b = xs_g.astype(BF16)
        ys = []
        for hh in range(hg):
            h = g * hg + hh
            xs_h = xs_b[:, hh * p:(hh + 1) * p]
            st_old = state_ref[h]
            a_col = jnp.broadcast_to(acum[:, h:h + 1], (q, q))
            lmat = jnp.exp(jnp.where(tril, a_col - acum_t[h:h + 1, :], -jnp.inf))
            lhs = jnp.concatenate([cb * lmat * dt_t[h:h + 1, :], cm * jnp.exp(a_col)], axis=1)
            rhs = jnp.concatenate([xs_h, st_old.astype(BF16)], axis=0)
            ys.append(_dot(lhs.astype(BF16), rhs))
            st_new = _dot((bm_t * wdec_t[h:h + 1, :]).astype(BF16), xs_h)
            state_ref[h] = st_old * chunk_decay[:, h:h + 1] + st_new
        y = jnp.concatenate(ys, axis=1) + xs_g * dsk_ref[:, g * gw:(g + 1) * gw]
        z = zx_ref[0, :, g * gw:(g + 1) * gw].astype(F32)
        y = y * _silu(z)
        y = y * lax.rsqrt(jnp.mean(y * y, axis=-1, keepdims=True) + NORM_EPS)
        o_ref[0, :, g * gw:(g + 1) * gw] = (y * nw_ref[:, g * gw:(g + 1) * gw]).astype(o_ref.dtype)


def ssd_core(zx, dt_raw, conv_w, conv_b, dt_bias, a_log, d_skip, norm_w):
    bsz, t, width = zx.shape
    d_inner = SSM_N_GROUPS * SSM_HEADS_PER_GROUP * SSM_HEAD_DIM
    cdim = width - d_inner
    nh = dt_bias.shape[0]
    q = SSM_CHUNK
    full = lambda shape: pl.BlockSpec(shape, lambda b, c: (0,) * len(shape))
    return pl.pallas_call(
        _ssd_kernel,
        grid=(bsz, t // q),
        in_specs=[pl.BlockSpec((1, q, width), lambda b, c: (b, c, 0)),
                  pl.BlockSpec((1, q, LANES), lambda b, c: (b, c, 0)),
                  full((SSM_CONV, cdim)), full((1, cdim)), full((1, nh)), full((1, nh)),
                  full((1, d_inner)), full((1, d_inner))],
        out_specs=pl.BlockSpec((1, q, d_inner), lambda b, c: (b, c, 0)),
        out_shape=jax.ShapeDtypeStruct((bsz, t, d_inner), BF16),
        scratch_shapes=[pltpu.VMEM((q, cdim), BF16),
                        pltpu.VMEM((nh, SSM_D_STATE, SSM_HEAD_DIM), F32)],
        compiler_params=_cparams("parallel", "arbitrary"),
        name="ssd_core",
    )(zx, dt_raw, conv_w, conv_b.reshape(1, cdim), dt_bias.reshape(1, nh),
      a_log.reshape(1, nh), jnp.repeat(d_skip, SSM_HEAD_DIM).reshape(1, d_inner),
      norm_w.reshape(1, d_inner))


def mamba2_layer(x, nw, sh, sc, gate, in_w, conv_w, conv_b, dt_bias, a_log, d_skip, norm_w, out_w):
    d_inner = SSM_N_GROUPS * SSM_HEADS_PER_GROUP * SSM_HEAD_DIM
    cdim = conv_w.shape[1]
    nh = dt_bias.shape[0]
    w_main = in_w[:, :d_inner + cdim].astype(BF16)
    w_dt = jnp.pad(in_w[:, d_inner + cdim:], ((0, 0), (0, LANES - nh)))
    zx, dt_raw = norm_mod_matmul(x, nw, sh, sc, w_main, tm=1024, tn=1024, side_w=w_dt)
    y = ssd_core(zx, dt_raw, conv_w, conv_b, dt_bias, a_log, d_skip, norm_w)
    return matmul_gated_residual(y, out_w.astype(BF16), x, gate)


def _head_indicator(d, n):
    return jnp.where(_iota2((d, LANES), 0) // n == _iota2((d, LANES), 1), 1.0, 0.0)


def _rwkv_prep_kernel(x_ref, xp_ref, nw_ref, sh_ref, sc_ref, mix_ref, wr_ref, wk_ref, wv_ref,
                      w0_ref, w1_ref, w2_ref, a0_ref, a1_ref, a2_ref, g1_ref, g2_ref, kk_ref, ka_ref,
                      r_o, lw_o, k_o, v_o, kn_o, b_o, g_o):
    tm, d = x_ref.shape[1], x_ref.shape[2]
    nw, sh, sc = nw_ref[...], sh_ref[0], sc_ref[0]
    hm = _norm_mod(x_ref[0], nw, sh, sc)
    prev = _norm_mod(xp_ref[0], nw, sh, sc)[7:8, :]
    prev = jnp.where(pl.program_id(1) > 0, prev, 0.0)
    shifted = jnp.where(_iota2((tm, d), 0) == 0, prev, pltpu.roll(hm, 1, axis=0))
    xx = shifted - hm
    mixed = lambda j: (hm + xx * mix_ref[j:j + 1, :]).astype(BF16)
    xr, xw, xk, xv, xa, xg = [mixed(j) for j in range(6)]
    r = _dot(xr, wr_ref[...])
    k = _dot(xk, wk_ref[...])
    v = _dot(xv, wv_ref[...])
    lora = lambda u, w: _dot(u.astype(BF16), w[...])
    w_log = -_softplus(-(w0_ref[...] + lora(jnp.tanh(lora(xw, w1_ref)), w2_ref))) - 0.5
    a = _sigmoid(a0_ref[...] + lora(lora(xa, a1_ref), a2_ref))
    g = lora(_sigmoid(lora(xg, g1_ref)), g2_ref)
    kk = k * kk_ref[...]
    ind = _head_indicator(d, RWKV_HEAD_DIM)
    nrm = jnp.maximum(jnp.sqrt(_dot_split(kk * kk, ind)), 1e-12)
    kn = kk * _dot_split(1.0 / nrm, ind, transpose_rhs=True)
    r_o[0] = r.astype(r_o.dtype)
    lw_o[0] = -jnp.exp(w_log)
    k_o[0] = (k * (1.0 + (a - 1.0) * ka_ref[...])).astype(k_o.dtype)
    v_o[0] = v.astype(v_o.dtype)
    kn_o[0] = kn.astype(kn_o.dtype)
    b_o[0] = (kn * a).astype(b_o.dtype)
    g_o[0] = g.astype(g_o.dtype)


def rwkv_prep(x, nw, sh, sc, mix, rkv_w, w0, w1, w2, a0, a1, a2, g1, g2, k_k, k_a, *, tm=512):
    bsz, t, d = x.shape
    tm = min(tm, t)
    full = lambda arr: pl.BlockSpec(arr.shape, lambda b, i: (0,) * arr.ndim)
    row = pl.BlockSpec((1, tm, d), lambda b, i: (b, i, 0))
    vec = pl.BlockSpec((1, 1, d), lambda b, i: (b, 0, 0))
    bf = lambda w: w.astype(BF16)
    params = [mix, bf(rkv_w[0]), bf(rkv_w[1]), bf(rkv_w[2]), w0.reshape(1, d), bf(w1), bf(w2),
              a0.reshape(1, d), bf(a1), bf(a2), bf(g1), bf(g2), k_k.reshape(1, d), k_a.reshape(1, d)]
    prev_spec = pl.BlockSpec((1, 8, d), lambda b, i: (b, jnp.maximum(i * (tm // 8) - 1, 0), 0))
    return pl.pallas_call(
        _rwkv_prep_kernel,
        grid=(bsz, t // tm),
        in_specs=[row, prev_spec, pl.BlockSpec((1, d), lambda b, i: (0, 0)), vec, vec]
        + [full(p) for p in params],
        out_specs=[row] * 7,
        out_shape=[jax.ShapeDtypeStruct((bsz, t, d), F32 if i == 1 else BF16) for i in range(7)],
        compiler_params=_cparams("parallel", "parallel"),
        name="rwkv_prep",
    )(x, x, nw.reshape(1, d), sh, sc, *params)


def _mm(a, b, precision):
    if precision is None:
        return _dot(a.astype(BF16), b.astype(BF16))
    return _dot(a, b, precision)


def _mm_nt(a, b, precision):
    if precision is None:
        return _dot_nt(a.astype(BF16), b.astype(BF16))
    return _dot_nt(a, b, precision)


def _mm_tn(a, b, precision):
    if precision is None:
        return _dot_tn(a.astype(BF16), b.astype(BF16))
    return _dot_tn(a, b, precision)


def _unit_lower_inverse(m_list, length, sub, precision):
    ri = _iota2((length, length), 0)
    ci = _iota2((length, length), 1)
    eye = jnp.where(ri == ci, 1.0, 0.0)
    same_block = ri // sub == ci // sub
    mm = lambda a, b: _mm(a, b, precision)
    md = [jnp.where(same_block, m, 0.0) for m in m_list]
    off = [m - d for m, d in zip(m_list, md)]
    dinv = [eye + d for d in md]
    pw = md
    for _ in range(int(math.log2(sub)) - 1):
        pw = [mm(p, p) for p in pw]
        dinv = [x + mm(x, p) for x, p in zip(dinv, pw)]
    pm = [mm(x, o) for x, o in zip(dinv, off)]
    acc = [eye + p for p in pm]
    pw = pm
    for _ in range(int(math.log2(length // sub)) - 1):
        pw = [mm(p, p) for p in pw]
        acc = [x + mm(x, p) for x, p in zip(acc, pw)]
    return [mm(x, d) for x, d in zip(acc, dinv)]


def _rwkv_core_kernel(r_ref, lw_ref, k_ref, v_ref, kn_ref, b_ref, rk_ref, lnw_ref, lnb_ref, o_ref,
                      state_ref, *, inv_precision, apply_precision):
    length, d = r_ref.shape[1], r_ref.shape[2]
    n = RWKV_HEAD_DIM
    heads = range(d // n)
    mm = lambda a, b: _mm(a, b, apply_precision)

    @pl.when(pl.program_id(1) == 0)
    def _():
        state_ref[...] = jnp.zeros_like(state_ref)

    ri = _iota2((2 * length, 2 * length), 0)
    ci = _iota2((2 * length, 2 * length), 1) % length
    keep = ((ri < length) & (ri > ci)) | ((ri >= length) & (ri - length >= ci))
    eye_n = _iota2((n, n), 0) == _iota2((n, n), 1)

    lw = lw_ref[0]
    r, k, v, kn, b = [ref[0].astype(F32) for ref in (r_ref, k_ref, v_ref, kn_ref, b_ref)]
    tri = jnp.where(_iota2((length, length), 0) >= _iota2((length, length), 1), 1.0, 0.0)
    cum = _dot_split(lw, tri, lhs_exact=True)
    c_end = cum[length - 1:length, :]
    e_neg = jnp.exp(-cum)
    e_rem = jnp.exp(c_end - cum)
    ar = jnp.concatenate([-kn * jnp.exp(cum - lw), r * jnp.exp(cum)], axis=0)
    bk = jnp.concatenate([b * e_neg, k * e_neg], axis=0)
    bk_end = jnp.concatenate([b * e_rem, k * e_rem], axis=0)
    w_end = jnp.exp(c_end)
    bonus_w = r * k * rk_ref[...]
    hs = [slice(h * n, (h + 1) * n) for h in heads]
    zero = jnp.zeros((length, n), F32)

    s0 = [state_ref[h] for h in heads]
    p = [jnp.where(keep, _mm_nt(ar[:, c], bk[:, c], inv_precision), 0.0) for c in hs]
    t_inv = _unit_lower_inverse([x[:length, :length] for x in p], length, RWKV_SUB, inv_precision)
    q = [mm(ar[:, c], s) for c, s in zip(hs, s0)]
    y = [x[:length] + mm(pp[:length], jnp.concatenate([zero, v[:, c]], axis=0))
         for x, pp, c in zip(q, p, hs)]
    u = [mm(ti, yy) for ti, yy in zip(t_inv, y)]
    uv = [jnp.concatenate([uu, v[:, c]], axis=0) for uu, c in zip(u, hs)]
    o = [x[length:] + mm(pp[length:], w) for x, pp, w in zip(q, p, uv)]
    for h in heads:
        c = hs[h]
        w_col = jnp.sum(jnp.where(eye_n, w_end[:, c], 0.0), axis=1, keepdims=True)
        state_ref[h] = w_col * s0[h] + _mm_tn(bk_end[:, c], uv[h], apply_precision)
        mu = jnp.mean(o[h], axis=-1, keepdims=True)
        var = jnp.mean((o[h] - mu) ** 2, axis=-1, keepdims=True)
        yh = (o[h] - mu) * lax.rsqrt(var + RWKV_LN_EPS) * lnw_ref[:, c] + lnb_ref[:, c]
        yh = yh + jnp.sum(bonus_w[:, c], axis=-1, keepdims=True) * v[:, c]
        o_ref[0, :, c] = yh.astype(o_ref.dtype)


def rwkv_core(r, lw, k, v, kn, b, r_k, ln_w, ln_b):
    bsz, t, d = r.shape
    length = RWKV_CHUNK
    n = RWKV_HEAD_DIM
    row = pl.BlockSpec((1, length, d), lambda bb, c: (bb, c, 0))
    vec = pl.BlockSpec((1, d), lambda bb, c: (0, 0))
    return pl.pallas_call(
        functools.partial(_rwkv_core_kernel, inv_precision=RWKV_INV_PRECISION,
                          apply_precision=RWKV_APPLY_PRECISION),
        grid=(bsz, t // length),
        in_specs=[row] * 6 + [vec] * 3,
        out_specs=row,
        out_shape=jax.ShapeDtypeStruct((bsz, t, d), BF16),
        scratch_shapes=[pltpu.VMEM((d // n, n, n), F32)],
        compiler_params=_cparams("parallel", "arbitrary"),
        name="rwkv_core",
    )(r, lw, k, v, kn, b, r_k.reshape(1, d), ln_w.reshape(1, d), ln_b.reshape(1, d))


def rwkv7_layer(x, nw, sh, sc, gate, mix, rkv_w, w0, w1, w2, a0, a1, a2, g1, g2, k_k, k_a, r_k,
                ln_w, ln_b, out_w):
    r, lw, k, v, kn, b, g = rwkv_prep(x, nw, sh, sc, mix, rkv_w, w0, w1, w2, a0, a1, a2, g1, g2,
                                      k_k, k_a)
    y = rwkv_core(r, lw, k, v, kn, b, r_k, ln_w, ln_b)
    return matmul_gated_residual(y, out_w.astype(BF16), x, gate, mul=g)


def _ret_kernel(qkvg_ref, idec_ref, qdec_ref, kdec_ref, cdec_ref, o_ref, state_ref, *, d_model):
    nh = RET_N_HEADS
    dk = d_model // nh
    dv = 2 * d_model // nh
    k_scale = dk ** -0.5

    @pl.when(pl.program_id(1) == 0)
    def _():
        state_ref[...] = jnp.zeros_like(state_ref)

    for h in range(nh):
        q_b = qkvg_ref[0, :, h * dk:(h + 1) * dk]
        k_f = qkvg_ref[0, :, d_model + h * dk:d_model + (h + 1) * dk].astype(F32) * k_scale
        v_b = qkvg_ref[0, :, 2 * d_model + h * dv:2 * d_model + (h + 1) * dv]
        g_f = qkvg_ref[0, :, 4 * d_model + h * dv:4 * d_model + (h + 1) * dv].astype(F32)
        scores = _dot_nt(q_b, k_f.astype(BF16)) * idec_ref[h]
        inner = _dot(scores.astype(BF16), v_b)
        r_old = state_ref[h]
        qd = (q_b.astype(F32) * qdec_ref[:, h:h + 1]).astype(BF16)
        cross = _dot(qd, r_old.astype(BF16))
        kd = (k_f * kdec_ref[:, h:h + 1]).astype(BF16)
        state_ref[h] = r_old * cdec_ref[:, h:h + 1] + _dot_tn(kd, v_b)
        o = inner + cross
        o = o * lax.rsqrt(jnp.mean(o * o, axis=-1, keepdims=True) + NORM_EPS)
        o_ref[0, :, h * dv:(h + 1) * dv] = (_silu(g_f) * o).astype(o_ref.dtype)


def retention_core(qkvg, d_model):
    bsz, t, width = qkvg.shape
    nh, q = RET_N_HEADS, RET_CHUNK
    dk, dv = d_model // nh, 2 * d_model // nh
    log_gamma = jnp.log(1 - jnp.exp2(-5.0 - jnp.arange(nh, dtype=F32)))
    idx = jnp.arange(q)
    rel = idx[:, None] - idx[None, :]
    inner_decay = jnp.where(rel >= 0,
                            jnp.exp(jnp.maximum(rel, 0).astype(F32) * log_gamma[:, None, None]), 0.0)
    q_decay = jnp.exp((idx + 1).astype(F32)[:, None] * log_gamma)
    k_decay = jnp.exp((q - 1 - idx).astype(F32)[:, None] * log_gamma)
    chunk_decay = jnp.exp(q * log_gamma).reshape(1, nh)
    full = lambda shape: pl.BlockSpec(shape, lambda b, c: (0,) * len(shape))
    return pl.pallas_call(
        functools.partial(_ret_kernel, d_model=d_model),
        grid=(bsz, t // q),
        in_specs=[pl.BlockSpec((1, q, width), lambda b, c: (b, c, 0)),
                  full((nh, q, q)), full((q, nh)), full((q, nh)), full((1, nh))],
        out_specs=pl.BlockSpec((1, q, nh * dv), lambda b, c: (b, c, 0)),
        out_shape=jax.ShapeDtypeStruct((bsz, t, nh * dv), BF16),
        scratch_shapes=[pltpu.VMEM((nh, dk, dv), F32)],
        compiler_params=_cparams("parallel", "arbitrary"),
        name="retention_core",
    )(qkvg, inner_decay, q_decay, k_decay, chunk_decay)


def retention_layer(x, nw, sh, sc, gate, in_w, out_w):
    d = x.shape[-1]
    qkvg = norm_mod_matmul(x, nw, sh, sc, in_w.astype(BF16), tm=1024, tn=1024)
    y = retention_core(qkvg, d)
    return matmul_gated_residual(y, out_w.astype(BF16), x, gate)


def _att_kernel(q_ref, kp_ref, kc_ref, vp_ref, vc_ref, sl_ref, o_ref, lse_ref, *, dil, span):
    w = ATT_BLOCK
    e = ATT_HEAD_DIM
    nblk = pl.program_id(2)
    qi = _iota2((w, 2 * w), 0)
    ci = _iota2((w, 2 * w), 1)
    delta = w + qi - ci
    mask = (delta >= 0) & (delta <= span) & ((nblk > 0) | (ci >= w))
    dist = (dil * delta).astype(F32)
    for h in range(ATT_HEADS_PER_GROUP):
        cols = slice(h * e, (h + 1) * e)
        q_b = q_ref[0, 0, :, cols]
        kk = jnp.concatenate([kp_ref[0, 0, :, cols], kc_ref[0, 0, :, cols]], axis=0)
        vv = jnp.concatenate([vp_ref[0, 0, :, cols], vc_ref[0, 0, :, cols]], axis=0)
        s = _dot_nt(q_b, kk) * (e ** -0.5)
        s = jnp.where(mask, s - sl_ref[:, h:h + 1] * dist, -jnp.inf)
        m = jnp.max(s, axis=-1, keepdims=True)
        p = jnp.exp(s - m)
        den = jnp.sum(p, axis=-1, keepdims=True)
        o = _dot((p / den).astype(BF16), vv)
        o_ref[0, 0, :, cols] = o.astype(o_ref.dtype)
        lse_ref[0, 0, :, cols] = jnp.broadcast_to(m + jnp.log(den), (w, e))


def dilated_group_core(qkv, gi, window, dil, slopes):
    bsz, _, ls, _ = qkv.shape
    w = ATT_BLOCK
    gw = ATT_HEADS_PER_GROUP * ATT_HEAD_DIM
    spec = lambda part, prev: pl.BlockSpec(
        (1, 1, w, gw),
        (lambda b, r, n: (b, r, jnp.maximum(n - 1, 0), part)) if prev
        else (lambda b, r, n: (b, r, n, part)))
    out_spec = pl.BlockSpec((1, 1, w, gw), lambda b, r, n: (b, r, n, 0))
    return pl.pallas_call(
        functools.partial(_att_kernel, dil=dil, span=window // dil),
        grid=(bsz, dil, ls // w),
        in_specs=[spec(0, False), spec(1, True), spec(1, False), spec(2, True), spec(2, False),
                  pl.BlockSpec((1, ATT_HEADS_PER_GROUP), lambda b, r, n: (0, 0))],
        out_specs=[out_spec, out_spec],
        out_shape=[jax.ShapeDtypeStruct((bsz, dil, ls, gw), BF16),
                   jax.ShapeDtypeStruct((bsz, dil, ls, gw), F32)],
        compiler_params=_cparams("parallel", "parallel", "arbitrary"),
        name="dilated_attention_g%d" % gi,
    )(qkv, qkv, qkv, qkv, qkv, slopes.reshape(1, ATT_HEADS_PER_GROUP))


def _att_out_kernel(o0_ref, o1_ref, o2_ref, l0_ref, l1_ref, l2_ref, w_ref, x_ref, gate_ref, o_ref,
                    *scratch):
    tm = x_ref.shape[1]

    def token_major(ref, scr):
        dil = ref.shape[1]
        if dil == 1:
            return ref[0, 0].astype(F32)
        for r in range(dil):
            blk = ref[0, r].astype(F32)
            for c in range(scr.shape[0]):
                scr[c, pl.ds(r, tm // dil, stride=dil), :] = blk[:, c * LANES:(c + 1) * LANES]
        return jnp.concatenate([scr[c] for c in range(scr.shape[0])], axis=1)

    o1, l1, o2, l2 = [token_major(ref, scr) for ref, scr in
                      zip((o1_ref, l1_ref, o2_ref, l2_ref), scratch)]
    o0, l0 = token_major(o0_ref, None), token_major(l0_ref, None)
    m = jnp.maximum(jnp.maximum(l0, l1), l2)
    w0, w1, w2 = jnp.exp(l0 - m), jnp.exp(l1 - m), jnp.exp(l2 - m)
    y = (w0 * o0 + w1 * o1 + w2 * o2) / (w0 + w1 + w2)
    o_ref[0] = x_ref[0] + gate_ref[0] * _dot(y.astype(BF16), w_ref[...])


def attention_combine_out(outs, lses, w, x, gate, *, tm=512):
    bsz, t, d = x.shape
    k = w.shape[0]
    tm = min(tm, t)
    row = pl.BlockSpec((1, tm, d), lambda b, i: (b, i, 0))
    res = lambda a: pl.BlockSpec((1, a.shape[1], tm // a.shape[1], k), lambda b, i: (b, 0, i, 0))
    return pl.pallas_call(
        _att_out_kernel,
        grid=(bsz, t // tm),
        in_specs=[res(a) for a in outs] + [res(a) for a in lses] + [
            pl.BlockSpec((k, d), lambda b, i: (0, 0)), row,
            pl.BlockSpec((1, 1, d), lambda b, i: (b, 0, 0))],
        out_specs=row,
        out_shape=jax.ShapeDtypeStruct((bsz, t, d), F32),
        scratch_shapes=[pltpu.VMEM((k // LANES, tm, LANES), F32)] * 4,
        compiler_params=_cparams("parallel", "parallel"),
        name="attention_combine_out",
    )(*outs, *lses, w, x, gate)


def attention_layer(x, nw, sh, sc, gate, in_w, out_w):
    n_groups = len(ATT_GROUPS)
    gw = ATT_HEADS_PER_GROUP * ATT_HEAD_DIM
    n_heads = n_groups * ATT_HEADS_PER_GROUP
    slopes = jnp.exp2(-8.0 * jnp.arange(1, n_heads + 1, dtype=F32) / n_heads)
    slopes = slopes.reshape(n_groups, ATT_HEADS_PER_GROUP)
    w_in = in_w.astype(BF16).reshape(in_w.shape[0], 3, n_groups, gw)
    outs, lses = [], []
    for gi, (window, dil) in enumerate(ATT_GROUPS):
        w_g = w_in[:, :, gi, :].reshape(in_w.shape[0], 3 * gw)
        qkv = norm_mod_matmul(x, nw, sh, sc, w_g, tm=1024, tn=gw, dil=dil)
        o, lse = dilated_group_core(qkv, gi, window, dil, slopes[gi])
        outs.append(o)
        lses.append(lse)
    return attention_combine_out(outs, lses, out_w.astype(BF16), x, gate)


ROUTER_GROUP_LANE0 = 0
ROUTER_EXPERT_LANE0 = MOE_GROUPS


def _router_kernel(x_ref, nw_ref, sh_ref, sc_ref, wr_ref, br_ref, hf_ref, meta_ref, cnt_ref,
                   carry_ref):
    tm = x_ref.shape[1]

    @pl.when(pl.program_id(1) == 0)
    def _():
        carry_ref[...] = jnp.zeros_like(carry_ref)

    hf = _norm_mod(x_ref[0], nw_ref[...], sh_ref[0], sc_ref[0])
    hf_ref[0] = hf
    logits = _dot_3pass(hf, wr_ref[...]) + br_ref[...]
    lane = _iota2((tm, LANES), 1)
    neg = -jnp.inf
    first = lambda hit: jnp.min(jnp.where(hit, lane, LANES), axis=-1, keepdims=True)

    gl = jnp.where(lane < MOE_GROUPS, logits, neg)
    gmax = jnp.max(gl, axis=-1, keepdims=True)
    gidx = first(gl == gmax)
    g_top = 1.0 / jnp.sum(jnp.exp(gl - gmax), axis=-1, keepdims=True)

    lo = ROUTER_EXPERT_LANE0 + gidx * MOE_EXPERTS_PER_GROUP
    el = jnp.where((lane >= lo) & (lane < lo + MOE_EXPERTS_PER_GROUP), logits, neg)
    m1 = jnp.max(el, axis=-1, keepdims=True)
    i1 = first(el == m1)
    el2 = jnp.where(lane == i1, neg, el)
    m2 = jnp.max(el2, axis=-1, keepdims=True)
    i2 = first(el2 == m2)
    ex = jnp.exp(m2 - m1)
    gate1 = g_top / (1.0 + ex)
    gate2 = g_top * ex / (1.0 + ex)

    hit1 = lane == i1
    hit2 = lane == i2
    onehot = jnp.where(hit1 | hit2, 1.0, 0.0).astype(BF16)
    strict = jnp.where(_iota2((tm, tm), 0) > _iota2((tm, tm), 1), 1.0, 0.0).astype(BF16)
    before = _dot(strict, onehot) + carry_ref[0:1, :]
    rank1 = jnp.sum(jnp.where(hit1, before, 0.0), axis=-1, keepdims=True)
    rank2 = jnp.sum(jnp.where(hit2, before, 0.0), axis=-1, keepdims=True)
    carry_ref[...] = carry_ref[...] + _dot(jnp.ones((8, tm), BF16), onehot)
    cnt_ref[0] = carry_ref[...]

    e1 = (i1 - ROUTER_EXPERT_LANE0).astype(F32)
    e2 = (i2 - ROUTER_EXPERT_LANE0).astype(F32)
    meta = jnp.zeros((tm, LANES), F32)
    for j, val in enumerate((e1, e2, rank1, rank2, gate1, gate2)):
        meta = jnp.where(lane == j, val, meta)
    meta_ref[0] = meta


def moe_router(x, nw, sh, sc, group_w, group_b, expert_w, expert_b, *, tm=512):
    bsz, t, d = x.shape
    tm = min(tm, t)
    pad = LANES - MOE_GROUPS - MOE_EXPERTS
    wr = jnp.pad(jnp.concatenate([group_w, expert_w], axis=1), ((0, 0), (0, pad)))
    br = jnp.pad(jnp.concatenate([group_b, expert_b]), (0, pad)).reshape(1, LANES)
    return pl.pallas_call(
        _router_kernel,
        grid=(bsz, t // tm),
        in_specs=[pl.BlockSpec((1, tm, d), lambda b, i: (b, i, 0)),
                  pl.BlockSpec((1, d), lambda b, i: (0, 0)),
                  pl.BlockSpec((1, 1, d), lambda b, i: (b, 0, 0)),
                  pl.BlockSpec((1, 1, d), lambda b, i: (b, 0, 0)),
                  pl.BlockSpec((d, LANES), lambda b, i: (0, 0)),
                  pl.BlockSpec((1, LANES), lambda b, i: (0, 0))],
        out_specs=[pl.BlockSpec((1, tm, d), lambda b, i: (b, i, 0)),
                   pl.BlockSpec((1, tm, LANES), lambda b, i: (b, i, 0)),
                   pl.BlockSpec((1, 8, LANES), lambda b, i: (b, 0, 0))],
        out_shape=[jax.ShapeDtypeStruct((bsz, t, d), F32),
                   jax.ShapeDtypeStruct((bsz, t, LANES), F32),
                   jax.ShapeDtypeStruct((bsz, 8, LANES), F32)],
        scratch_shapes=[pltpu.VMEM((8, LANES), F32)],
        compiler_params=_cparams("parallel", "arbitrary"),
        name="moe_router",
    )(x, nw.reshape(1, d), sh, sc, wr, br)


def _expert_kernel(e1_ref, e2_ref, r1_ref, r2_ref, g1_ref, g2_ref, cnt_ref,
                   hf_hbm, x_hbm, gt_ref, w1_ref, w3_ref, w2_ref, out_hbm,
                   hf_v, acc_v, xb_v, yb_v, wb1, wb3, wb2, row_tok, row_gate, pstart, sem):
    b = pl.program_id(0)
    e = pl.program_id(1)
    n_e = pl.num_programs(1)
    t = hf_v.shape[0]
    rows = xb_v.shape[0]
    grp = MOE_ROW_GROUP
    acc_rows = acc_v.at[pl.ds(0, t)]

    @pl.when(e == 0)
    def _():
        cp_h = pltpu.make_async_copy(hf_hbm.at[b], hf_v, sem.at[0])
        cp_x = pltpu.make_async_copy(x_hbm.at[b], acc_rows, sem.at[1])
        cp_h.start()
        cp_x.start()
        xb_v[...] = jnp.zeros_like(xb_v)
        acc_v[pl.ds(t, grp), :] = jnp.zeros((grp, acc_v.shape[1]), F32)
        for i in range(grp):
            row_tok[2 * t + i] = 0
            row_gate[2 * t + i] = 0.0

        def start_body(i, s):
            pstart[i] = s
            return s + cnt_ref[b * n_e + i]

        lax.fori_loop(0, n_e, start_body, 0)

        def tok_body(g, carry):
            for u in range(4):
                tok = g * 4 + u
                d1 = pstart[e1_ref[tok]] + r1_ref[tok]
                row_tok[d1] = tok
                row_gate[d1] = g1_ref[tok]
                d2 = pstart[e2_ref[tok]] + r2_ref[tok]
                row_tok[d2] = tok
                row_gate[d2] = g2_ref[tok]
            return carry

        lax.fori_loop(0, t // 4, tok_body, 0)
        cp_h.wait()
        cp_x.wait()

    wb1[...] = w1_ref[0, 0].astype(BF16)
    wb3[...] = w3_ref[0, 0].astype(BF16)
    wb2[...] = w2_ref[0, 0].astype(BF16)
    cnt = cnt_ref[b * n_e + e]
    start = pstart[e]

    def block_body(j, carry):
        base = start + j * rows
        nrows = jnp.minimum(rows, cnt - j * rows)
        ngroups = (nrows + grp - 1) // grp

        def gather(g, c):
            for u in range(grp):
                i = g * grp + u
                xb_v[pl.ds(i, 1), :] = hf_v[pl.ds(row_tok[base + i], 1), :]
            return c

        lax.fori_loop(0, ngroups, gather, 0)
        xb = xb_v[...].astype(BF16)
        h1 = _dot(xb, wb1[...])
        h3 = _dot(xb, wb3[...])
        act = (_silu(h1) * h3).astype(BF16)
        yb_v[...] = _dot(act, wb2[...]) * gt_ref[0]

        def scatter(g, c):
            toks = []
            vals = []
            for u in range(grp):
                i = g * grp + u
                tok = jnp.where(i < nrows, row_tok[base + i], t)
                toks.append(tok)
                vals.append(acc_v[pl.ds(tok, 1), :] + row_gate[base + i] * yb_v[pl.ds(i, 1), :])
            for u in range(grp):
                acc_v[pl.ds(toks[u], 1), :] = vals[u]
            return c

        lax.fori_loop(0, ngroups, scatter, 0)
        return carry

    lax.fori_loop(0, (cnt + rows - 1) // rows, block_body, 0)

    @pl.when(e == n_e - 1)
    def _():
        cp_o = pltpu.make_async_copy(acc_rows, out_hbm.at[b], sem.at[2])
        cp_o.start()
        cp_o.wait()


def moe_experts(hf, x, gt, e1, e2, r1, r2, g1, g2, counts, layer, w1, w3, w2):
    bsz, t, d = x.shape
    _, n_e, _, f = w1.shape
    smem_tok = pl.BlockSpec((t,), lambda b, e: (b,), memory_space=pltpu.SMEM)
    return pl.pallas_call(
        _expert_kernel,
        grid=(bsz, n_e),
        in_specs=[smem_tok] * 6 + [
            pl.BlockSpec(memory_space=pltpu.SMEM),
            pl.BlockSpec(memory_space=pl.ANY),
            pl.BlockSpec(memory_space=pl.ANY),
            pl.BlockSpec((1, 1, d), lambda b, e: (b, 0, 0)),
            pl.BlockSpec((1, 1, d, f), lambda b, e: (layer, e, 0, 0)),
            pl.BlockSpec((1, 1, d, f), lambda b, e: (layer, e, 0, 0)),
            pl.BlockSpec((1, 1, f, d), lambda b, e: (layer, e, 0, 0))],
        out_specs=pl.BlockSpec(memory_space=pl.ANY),
        out_shape=jax.ShapeDtypeStruct((bsz, t, d), F32),
        scratch_shapes=[pltpu.VMEM((t, d), F32), pltpu.VMEM((t + MOE_ROW_GROUP, d), F32),
                        pltpu.VMEM((MOE_ROWS, d), F32), pltpu.VMEM((MOE_ROWS, d), F32),
                        pltpu.VMEM((d, f), BF16), pltpu.VMEM((d, f), BF16), pltpu.VMEM((f, d), BF16),
                        pltpu.SMEM((2 * t + MOE_ROW_GROUP,), jnp.int32),
                        pltpu.SMEM((2 * t + MOE_ROW_GROUP,), F32),
                        pltpu.SMEM((n_e,), jnp.int32), pltpu.SemaphoreType.DMA((3,))],
        compiler_params=_cparams("arbitrary", "arbitrary"),
        name="moe_experts",
    )(e1, e2, r1, r2, g1, g2, counts, hf, x, gt, w1, w3, w2)


def moe_layer(x, nw, sh, sc, gate, group_w, group_b, expert_w, expert_b, layer, w1, w3, w2):
    bsz, t, d = x.shape
    hf, meta, cnt = moe_router(x, nw, sh, sc, group_w, group_b, expert_w, expert_b)
    ints = meta[:, :, :4].astype(jnp.int32).reshape(bsz * t, 4)
    gates = meta[:, :, 4:6].reshape(bsz * t, 2)
    counts = cnt[:, 0, ROUTER_EXPERT_LANE0:ROUTER_EXPERT_LANE0 + MOE_EXPERTS].astype(jnp.int32)
    return moe_experts(hf, x, gate, ints[:, 0], ints[:, 1], ints[:, 2], ints[:, 3],
                       gates[:, 0], gates[:, 1], counts.reshape(-1), layer, w1, w3, w2)


def _final_norm_kernel(x_ref, w_ref, o_ref):
    x = x_ref[0]
    o_ref[0] = x * lax.rsqrt(jnp.mean(x * x, axis=-1, keepdims=True) + NORM_EPS) * w_ref[...]


def final_rms_norm(x, w, *, tm=1024):
    bsz, t, d = x.shape
    tm = min(tm, t)
    row = pl.BlockSpec((1, tm, d), lambda b, i: (b, i, 0))
    return pl.pallas_call(
        _final_norm_kernel,
        grid=(bsz, t // tm),
        in_specs=[row, pl.BlockSpec((1, d), lambda b, i: (0, 0))],
        out_specs=row,
        out_shape=jax.ShapeDtypeStruct((bsz, t, d), F32),
        compiler_params=_cparams("parallel", "parallel"),
        name="final_rms_norm",
    )(x, w.reshape(1, d))


def kernel(x, c, ada_w, ada_b, norm_mix_w, norm_ffn_w, ssm_in_w, ssm_conv_w, ssm_conv_b, ssm_dt_bias, ssm_a_log, ssm_d, ssm_norm_w, ssm_out_w, rwkv_mix, rwkv_rkv_w, rwkv_w0, rwkv_w1, rwkv_w2, rwkv_a0, rwkv_a1, rwkv_a2, rwkv_g1, rwkv_g2, rwkv_k_k, rwkv_k_a, rwkv_r_k, rwkv_ln_w, rwkv_ln_b, rwkv_out_w, ret_in_w, ret_out_w, att_in_w, att_out_w, moe_group_w, moe_group_b, moe_expert_w, moe_expert_b, moe_w1, moe_w3, moe_w2, final_norm_w):
    depth = ada_w.shape[0]
    d = x.shape[-1]
    mod = ada_modulation(c, ada_w, ada_b)
    for i in range(depth):
        sh1, sc1, gt1, sh2, sc2, gt2 = [mod[i][:, None, j * d:(j + 1) * d] for j in range(6)]
        kind, j = i % 4, i // 4
        pre = (x, norm_mix_w[i], sh1, sc1, gt1)
        if kind == 0:
            x = mamba2_layer(*pre, ssm_in_w[j], ssm_conv_w[j], ssm_conv_b[j], ssm_dt_bias[j],
                             ssm_a_log[j], ssm_d[j], ssm_norm_w[j], ssm_out_w[j])
        elif kind == 1:
            x = rwkv7_layer(*pre, rwkv_mix[j], rwkv_rkv_w[j], rwkv_w0[j], rwkv_w1[j], rwkv_w2[j],
                            rwkv_a0[j], rwkv_a1[j], rwkv_a2[j], rwkv_g1[j], rwkv_g2[j], rwkv_k_k[j],
                            rwkv_k_a[j], rwkv_r_k[j], rwkv_ln_w[j], rwkv_ln_b[j], rwkv_out_w[j])
        elif kind == 2:
            x = retention_layer(*pre, ret_in_w[j], ret_out_w[j])
        else:
            x = attention_layer(*pre, att_in_w[j], att_out_w[j])
        x = moe_layer(x, norm_ffn_w[i], sh2, sc2, gt2, moe_group_w[i], moe_group_b[i],
                      moe_expert_w[i], moe_expert_b[i], i, moe_w1, moe_w3, moe_w2)
    return final_rms_norm(x, final_norm_w)
```

```python
import functools
import math

import jax
import jax.numpy as jnp
from jax import lax
from jax.experimental import pallas as pl
from jax.experimental.pallas import tpu as pltpu

F32 = jnp.float32
BF16 = jnp.bfloat16
HI = lax.Precision.HIGHEST

NORM_EPS = 1e-6
LANES = 128
VMEM_LIMIT = 56 * 1024 * 1024
PROJ_TM = 2048
PROJ_TN = 1024

SSM_HEAD_DIM = 64
SSM_N_GROUPS = 8
SSM_HEADS_PER_GROUP = 4
SSM_D_STATE = 128
SSM_CONV = 4
SSM_CHUNK = 128
RWKV_HEAD_DIM = 64
RWKV_LN_EPS = 64e-5
RWKV_CHUNK = 64
RWKV_SUB = 16
RWKV_INV_PRECISION = None
RWKV_APPLY_PRECISION = None
RET_N_HEADS = 4
RET_CHUNK = 128
ATT_GROUPS = ((128, 1), (512, 4), (2048, 16))
ATT_HEADS_PER_GROUP = 8
ATT_HEAD_DIM = 128
ATT_BLOCK = 128
MOE_GROUPS = 4
MOE_EXPERTS_PER_GROUP = 8
MOE_EXPERTS = 32
MOE_ROWS = 256
MOE_ROW_GROUP = 8


def _cparams(*sem):
    return pltpu.CompilerParams(dimension_semantics=sem, vmem_limit_bytes=VMEM_LIMIT)


def _sigmoid(x):
    return 1.0 / (1.0 + jnp.exp(-x))


def _silu(x):
    return x * _sigmoid(x)


def _softplus(x):
    return jnp.maximum(x, 0.0) + jnp.log(1.0 + jnp.exp(-jnp.abs(x)))


def _dot(a, b, precision=None):
    return jnp.dot(a, b, preferred_element_type=F32, precision=precision)


def _dot_nt(a, b, precision=None):
    return lax.dot_general(a, b, (((1,), (1,)), ((), ())), preferred_element_type=F32,
                           precision=precision)


def _dot_tn(a, b, precision=None):
    return lax.dot_general(a, b, (((0,), (0,)), ((), ())), preferred_element_type=F32,
                           precision=precision)


def _split_bf16(a):
    hi = a.astype(BF16)
    return hi, (a - hi.astype(F32)).astype(BF16)


def _dot_split(a, m, transpose_rhs=False, lhs_exact=False):
    hi, lo = _split_bf16(a)
    mb = m.astype(BF16)
    if lhs_exact:
        return _dot(mb, hi) + _dot(mb, lo)
    dot = _dot_nt if transpose_rhs else _dot
    return dot(hi, mb) + dot(lo, mb)


def _dot_3pass(a, b):
    a_hi, a_lo = _split_bf16(a)
    b_hi, b_lo = _split_bf16(b)
    return _dot(a_hi, b_hi) + (_dot(a_lo, b_hi) + _dot(a_hi, b_lo))


def _iota2(shape, axis):
    return lax.broadcasted_iota(jnp.int32, shape, axis)


def _ada_kernel(c_ref, w_ref, b_ref, o_ref):
    cs = _silu(c_ref[...])
    o_ref[0] = _dot(cs, w_ref[0], HI) + b_ref[0]


def ada_modulation(c, ada_w, ada_b):
    depth, d, n = ada_w.shape
    bsz = c.shape[0]
    rows = -(-bsz // 8) * 8
    cp = jnp.pad(c, ((0, rows - bsz), (0, 0)))
    tn = 1536
    out = pl.pallas_call(
        _ada_kernel,
        grid=(depth, n // tn),
        in_specs=[pl.BlockSpec((rows, d), lambda l, j: (0, 0)),
                  pl.BlockSpec((1, d, tn), lambda l, j: (l, 0, j)),
                  pl.BlockSpec((1, 1, tn), lambda l, j: (l, 0, j))],
        out_specs=pl.BlockSpec((1, rows, tn), lambda l, j: (l, 0, j)),
        out_shape=jax.ShapeDtypeStruct((depth, rows, n), F32),
        compiler_params=_cparams("parallel", "parallel"),
        name="ada_modulation",
    )(cp, ada_w, ada_b.reshape(depth, 1, n))
    return out[:, :bsz]


def _norm_mod(x, nw, sh, sc):
    y = x * lax.rsqrt(jnp.mean(x * x, axis=-1, keepdims=True) + NORM_EPS) * nw
    return y * (1.0 + sc) + sh


def _nmm_kernel(*refs, dil, has_side):
    if has_side:
        x_ref, nw_ref, sh_ref, sc_ref, w_ref, ws_ref, o_ref, side_ref, h_ref, *rest = refs
    else:
        x_ref, nw_ref, sh_ref, sc_ref, w_ref, o_ref, h_ref, *rest = refs

    @pl.when(pl.program_id(2) == 0)
    def _():
        h = _norm_mod(x_ref[0], nw_ref[...], sh_ref[0], sc_ref[0]).astype(h_ref.dtype)
        h_ref[...] = h
        if has_side:
            w_hi, w_lo = _split_bf16(ws_ref[...])
            side_ref[0] = _dot(h, w_hi) + _dot(h, w_lo)

    res = _dot(h_ref[...], w_ref[...])
    if dil is None:
        o_ref[0] = res.astype(o_ref.dtype)
    else:
        acc_ref, = rest
        rows = acc_ref.shape[1] // dil
        for c in range(acc_ref.shape[0]):
            acc_ref[c] = res[:, c * LANES:(c + 1) * LANES]
        for r in range(dil):
            for c in range(acc_ref.shape[0]):
                o_ref[0, r, :, c * LANES:(c + 1) * LANES] = (
                    acc_ref[c, pl.ds(r, rows, stride=dil), :].astype(o_ref.dtype))


def norm_mod_matmul(x, nw, sh, sc, w, *, tm, tn, dil=None, side_w=None):
    bsz, t, d = x.shape
    n = w.shape[1]
    tm = min(tm, t)
    scratch = [pltpu.VMEM((tm, d), BF16)]
    in_specs = [pl.BlockSpec((1, tm, d), lambda b, i, j: (b, i, 0)),
                pl.BlockSpec((1, d), lambda b, i, j: (0, 0)),
                pl.BlockSpec((1, 1, d), lambda b, i, j: (b, 0, 0)),
                pl.BlockSpec((1, 1, d), lambda b, i, j: (b, 0, 0)),
                pl.BlockSpec((d, tn), lambda b, i, j: (0, j))]
    ins = [x, nw.reshape(1, d), sh, sc, w]
    if dil is None:
        out_specs = [pl.BlockSpec((1, tm, tn), lambda b, i, j: (b, i, j))]
        out_shape = [jax.ShapeDtypeStruct((bsz, t, n), BF16)]
    else:
        out_specs = [pl.BlockSpec((1, dil, tm // dil, tn), lambda b, i, j: (b, 0, i, j))]
        out_shape = [jax.ShapeDtypeStruct((bsz, dil, t // dil, n), BF16)]
        scratch.append(pltpu.VMEM((tn // LANES, tm, LANES), F32))
    if side_w is not None:
        in_specs.append(pl.BlockSpec((d, LANES), lambda b, i, j: (0, 0)))
        ins.append(side_w)
        out_specs.append(pl.BlockSpec((1, tm, LANES), lambda b, i, j: (b, i, 0)))
        out_shape.append(jax.ShapeDtypeStruct((bsz, t, LANES), F32))
    outs = pl.pallas_call(
        functools.partial(_nmm_kernel, dil=dil, has_side=side_w is not None),
        grid=(bsz, t // tm, n // tn),
        in_specs=in_specs,
        out_specs=out_specs,
        out_shape=out_shape,
        scratch_shapes=scratch,
        compiler_params=_cparams("parallel", "parallel", "arbitrary"),
        name="norm_mod_matmul",
    )(*ins)
    return outs if side_w is not None else outs[0]


def _mgr_kernel(*refs, has_mul):
    if has_mul:
        y_ref, g_ref, w_ref, x_ref, gate_ref, o_ref = refs
        y = (y_ref[0].astype(F32) * g_ref[0].astype(F32)).astype(BF16)
    else:
        y_ref, w_ref, x_ref, gate_ref, o_ref = refs
        y = y_ref[0].astype(BF16)
    o_ref[0] = x_ref[0] + gate_ref[0] * _dot(y, w_ref[...])


def matmul_gated_residual(y, w, x, gate, mul=None, *, tm=512):
    bsz, t, k = y.shape
    d = w.shape[1]
    tm = min(tm, t)
    ins = [y] + ([mul] if mul is not None else []) + [w, x, gate]
    row = lambda width: pl.BlockSpec((1, tm, width), lambda b, i: (b, i, 0))
    specs = [row(k)] + ([row(k)] if mul is not None else []) + [
        pl.BlockSpec((k, d), lambda b, i: (0, 0)), row(d),
        pl.BlockSpec((1, 1, d), lambda b, i: (b, 0, 0))]
    return pl.pallas_call(
        functools.partial(_mgr_kernel, has_mul=mul is not None),
        grid=(bsz, t // tm),
        in_specs=specs,
        out_specs=row(d),
        out_shape=jax.ShapeDtypeStruct((bsz, t, d), F32),
        compiler_params=_cparams("parallel", "parallel"),
        name="matmul_gated_residual",
    )(*ins)


def _ssd_kernel(zx_ref, dt_ref, cw_ref, cb_ref, dtb_ref, alog_ref, dsk_ref, nw_ref, o_ref,
                prev_ref, state_ref):
    q = SSM_CHUNK
    g_n, hg, p, n = SSM_N_GROUPS, SSM_HEADS_PER_GROUP, SSM_HEAD_DIM, SSM_D_STATE
    d_inner = g_n * hg * p
    nh = g_n * hg
    gw = hg * p

    @pl.when(pl.program_id(1) == 0)
    def _():
        prev_ref[...] = jnp.zeros_like(prev_ref)
        state_ref[...] = jnp.zeros_like(state_ref)

    cur = zx_ref[0, :, d_inner:]
    ext = jnp.concatenate([prev_ref[...], cur], axis=0)
    prev_ref[...] = cur
    src = _iota2((q, 2 * q), 1) - q - _iota2((q, 2 * q), 0)
    acc = cb_ref[...] + cur.astype(F32) * cw_ref[SSM_CONV - 1:SSM_CONV, :]
    for s in range(1, SSM_CONV):
        shift = jnp.where(src == -s, 1.0, 0.0).astype(BF16)
        acc = acc + _dot(shift, ext) * cw_ref[SSM_CONV - 1 - s:SSM_CONV - s, :]
    xbc = _silu(acc)

    dt = _softplus(dt_ref[0][:, :nh] + dtb_ref[...])
    da = dt * -jnp.exp(alog_ref[...])
    tril = (_iota2((q, q), 0) >= _iota2((q, q), 1))
    acum = _dot_split(da, tril.astype(F32), lhs_exact=True)
    a_end = acum[q - 1:q, :]
    chunk_decay = jnp.exp(a_end)
    acum_t = acum.T
    dt_t = dt.T
    wdec_t = (dt * jnp.exp(a_end - acum)).T

    for g in range(g_n):
        xs_g = xbc[:, g * gw:(g + 1) * gw]
        bm = xbc[:, d_inner + g * n:d_inner + (g + 1) * n]
        cm = xbc[:, d_inner + g_n * n + g * n:d_inner + g_n * n + (g + 1) * n]
        cb = _dot_nt(cm.astype(BF16), bm.astype(BF16))
        bm_t = bm.T
        xs_b = xs_g.astype(BF16)
        ys = []
        for hh in range(hg):
            h = g * hg + hh
            xs_h = xs_b[:, hh * p:(hh + 1) * p]
            st_old = state_ref[h]
            a_col = jnp.broadcast_to(acum[:, h:h + 1], (q, q))
            lmat = jnp.exp(jnp.where(tril, a_col - acum_t[h:h + 1, :], -jnp.inf))
            lhs = jnp.concatenate([cb * lmat * dt_t[h:h + 1, :], cm * jnp.exp(a_col)], axis=1)
            rhs = jnp.concatenate([xs_h, st_old.astype(BF16)], axis=0)
            ys.append(_dot(lhs.astype(BF16), rhs))
            st_new = _dot((bm_t * wdec_t[h:h + 1, :]).astype(BF16), xs_h)
            state_ref[h] = st_old * chunk_decay[:, h:h + 1] + st_new
        y = jnp.concatenate(ys, axis=1) + xs_g * dsk_ref[:, g * gw:(g + 1) * gw]
        z = zx_ref[0, :, g * gw:(g + 1) * gw].astype(F32)
        y = y * _silu(z)
        y = y * lax.rsqrt(jnp.mean(y * y, axis=-1, keepdims=True) + NORM_EPS)
        o_ref[0, :, g * gw:(g + 1) * gw] = (y * nw_ref[:, g * gw:(g + 1) * gw]).astype(o_ref.dtype)


def ssd_core(zx, dt_raw, conv_w, conv_b, dt_bias, a_log, d_skip, norm_w):
    bsz, t, width = zx.shape
    d_inner = SSM_N_GROUPS * SSM_HEADS_PER_GROUP * SSM_HEAD_DIM
    cdim = width - d_inner
    nh = dt_bias.shape[0]
    q = SSM_CHUNK
    full = lambda shape: pl.BlockSpec(shape, lambda b, c: (0,) * len(shape))
    return pl.pallas_call(
        _ssd_kernel,
        grid=(bsz, t // q),
        in_specs=[pl.BlockSpec((1, q, width), lambda b, c: (b, c, 0)),
                  pl.BlockSpec((1, q, LANES), lambda b, c: (b, c, 0)),
                  full((SSM_CONV, cdim)), full((1, cdim)), full((1, nh)), full((1, nh)),
                  full((1, d_inner)), full((1, d_inner))],
        out_specs=pl.BlockSpec((1, q, d_inner), lambda b, c: (b, c, 0)),
        out_shape=jax.ShapeDtypeStruct((bsz, t, d_inner), BF16),
        scratch_shapes=[pltpu.VMEM((q, cdim), BF16),
                        pltpu.VMEM((nh, SSM_D_STATE, SSM_HEAD_DIM), F32)],
        compiler_params=_cparams("parallel", "arbitrary"),
        name="ssd_core",
    )(zx, dt_raw, conv_w, conv_b.reshape(1, cdim), dt_bias.reshape(1, nh),
      a_log.reshape(1, nh), jnp.repeat(d_skip, SSM_HEAD_DIM).reshape(1, d_inner),
      norm_w.reshape(1, d_inner))


def mamba2_layer(x, nw, sh, sc, gate, in_w, conv_w, conv_b, dt_bias, a_log, d_skip, norm_w, out_w):
    d_inner = SSM_N_GROUPS * SSM_HEADS_PER_GROUP * SSM_HEAD_DIM
    cdim = conv_w.shape[1]
    nh = dt_bias.shape[0]
    w_main = in_w[:, :d_inner + cdim].astype(BF16)
    w_dt = jnp.pad(in_w[:, d_inner + cdim:], ((0, 0), (0, LANES - nh)))
    zx, dt_raw = norm_mod_matmul(x, nw, sh, sc, w_main, tm=PROJ_TM, tn=PROJ_TN, side_w=w_dt)
    y = ssd_core(zx, dt_raw, conv_w, conv_b, dt_bias, a_log, d_skip, norm_w)
    return matmul_gated_residual(y, out_w.astype(BF16), x, gate)


def _head_indicator(d, n):
    return jnp.where(_iota2((d, LANES), 0) // n == _iota2((d, LANES), 1), 1.0, 0.0)


def _rwkv_prep_kernel(x_ref, xp_ref, nw_ref, sh_ref, sc_ref, mix_ref, wr_ref, wk_ref, wv_ref,
                      w0_ref, w1_ref, w2_ref, a0_ref, a1_ref, a2_ref, g1_ref, g2_ref, kk_ref, ka_ref,
                      r_o, lw_o, k_o, v_o, kn_o, b_o, g_o):
    tm, d = x_ref.shape[1], x_ref.shape[2]
    nw, sh, sc = nw_ref[...], sh_ref[0], sc_ref[0]
    hm = _norm_mod(x_ref[0], nw, sh, sc)
    prev = _norm_mod(xp_ref[0], nw, sh, sc)[7:8, :]
    prev = jnp.where(pl.program_id(1) > 0, prev, 0.0)
    shifted = jnp.where(_iota2((tm, d), 0) == 0, prev, pltpu.roll(hm, 1, axis=0))
    xx = shifted - hm
    mixed = lambda j: (hm + xx * mix_ref[j:j + 1, :]).astype(BF16)
    xr, xw, xk, xv, xa, xg = [mixed(j) for j in range(6)]
    r = _dot(xr, wr_ref[...])
    k = _dot(xk, wk_ref[...])
    v = _dot(xv, wv_ref[...])
    lora = lambda u, w: _dot(u.astype(BF16), w[...])
    w_log = -_softplus(-(w0_ref[...] + lora(jnp.tanh(lora(xw, w1_ref)), w2_ref))) - 0.5
    a = _sigmoid(a0_ref[...] + lora(lora(xa, a1_ref), a2_ref))
    g = lora(_sigmoid(lora(xg, g1_ref)), g2_ref)
    kk = k * kk_ref[...]
    ind = _head_indicator(d, RWKV_HEAD_DIM)
    nrm = jnp.maximum(jnp.sqrt(_dot_split(kk * kk, ind)), 1e-12)
    kn = kk * _dot_split(1.0 / nrm, ind, transpose_rhs=True)
    r_o[0] = r.astype(r_o.dtype)
    lw_o[0] = -jnp.exp(w_log)
    k_o[0] = (k * (1.0 + (a - 1.0) * ka_ref[...])).astype(k_o.dtype)
    v_o[0] = v.astype(v_o.dtype)
    kn_o[0] = kn.astype(kn_o.dtype)
    b_o[0] = (kn * a).astype(b_o.dtype)
    g_o[0] = g.astype(g_o.dtype)


def rwkv_prep(x, nw, sh, sc, mix, rkv_w, w0, w1, w2, a0, a1, a2, g1, g2, k_k, k_a, *, tm=512):
    bsz, t, d = x.shape
    tm = min(tm, t)
    full = lambda arr: pl.BlockSpec(arr.shape, lambda b, i: (0,) * arr.ndim)
    row = pl.BlockSpec((1, tm, d), lambda b, i: (b, i, 0))
    vec = pl.BlockSpec((1, 1, d), lambda b, i: (b, 0, 0))
    bf = lambda w: w.astype(BF16)
    params = [mix, bf(rkv_w[0]), bf(rkv_w[1]), bf(rkv_w[2]), w0.reshape(1, d), bf(w1), bf(w2),
              a0.reshape(1, d), bf(a1), bf(a2), bf(g1), bf(g2), k_k.reshape(1, d), k_a.reshape(1, d)]
    prev_spec = pl.BlockSpec((1, 8, d), lambda b, i: (b, jnp.maximum(i * (tm // 8) - 1, 0), 0))
    return pl.pallas_call(
        _rwkv_prep_kernel,
        grid=(bsz, t // tm),
        in_specs=[row, prev_spec, pl.BlockSpec((1, d), lambda b, i: (0, 0)), vec, vec]
        + [full(p) for p in params],
        out_specs=[row] * 7,
        out_shape=[jax.ShapeDtypeStruct((bsz, t, d), F32 if i == 1 else BF16) for i in range(7)],
        compiler_params=_cparams("parallel", "parallel"),
        name="rwkv_prep",
    )(x, x, nw.reshape(1, d), sh, sc, *params)


def _mm(a, b, precision):
    if precision is None:
        return _dot(a.astype(BF16), b.astype(BF16))
    return _dot(a, b, precision)


def _mm_nt(a, b, precision):
    if precision is None:
        return _dot_nt(a.astype(BF16), b.astype(BF16))
    return _dot_nt(a, b, precision)


def _mm_tn(a, b, precision):
    if precision is None:
        return _dot_tn(a.astype(BF16), b.astype(BF16))
    return _dot_tn(a, b, precision)


def _unit_lower_inverse(m_list, length, sub, precision):
    ri = _iota2((length, length), 0)
    ci = _iota2((length, length), 1)
    eye = jnp.where(ri == ci, 1.0, 0.0)
    same_block = ri // sub == ci // sub
    mm = lambda a, b: _mm(a, b, precision)
    md = [jnp.where(same_block, m, 0.0) for m in m_list]
    off = [m - d for m, d in zip(m_list, md)]
    dinv = [eye + d for d in md]
    pw = md
    for _ in range(int(math.log2(sub)) - 1):
        pw = [mm(p, p) for p in pw]
        dinv = [x + mm(x, p) for x, p in zip(dinv, pw)]
    pm = [mm(x, o) for x, o in zip(dinv, off)]
    acc = [eye + p for p in pm]
    pw = pm
    for _ in range(int(math.log2(length // sub)) - 1):
        pw = [mm(p, p) for p in pw]
        acc = [x + mm(x, p) for x, p in zip(acc, pw)]
    return [mm(x, d) for x, d in zip(acc, dinv)]


def _rwkv_core_kernel(r_ref, lw_ref, k_ref, v_ref, kn_ref, b_ref, rk_ref, lnw_ref, lnb_ref, o_ref,
                      state_ref, *, inv_precision, apply_precision):
    length, d = r_ref.shape[1], r_ref.shape[2]
    n = RWKV_HEAD_DIM
    heads = range(d // n)
    mm = lambda a, b: _mm(a, b, apply_precision)

    @pl.when(pl.program_id(1) == 0)
    def _():
        state_ref[...] = jnp.zeros_like(state_ref)

    ri = _iota2((2 * length, 2 * length), 0)
    ci = _iota2((2 * length, 2 * length), 1) % length
    keep = ((ri < length) & (ri > ci)) | ((ri >= length) & (ri - length >= ci))
    eye_n = _iota2((n, n), 0) == _iota2((n, n), 1)

    lw = lw_ref[0]
    r, k, v, kn, b = [ref[0].astype(F32) for ref in (r_ref, k_ref, v_ref, kn_ref, b_ref)]
    tri = jnp.where(_iota2((length, length), 0) >= _iota2((length, length), 1), 1.0, 0.0)
    cum = _dot_split(lw, tri, lhs_exact=True)
    c_end = cum[length - 1:length, :]
    e_neg = jnp.exp(-cum)
    e_rem = jnp.exp(c_end - cum)
    ar = jnp.concatenate([-kn * jnp.exp(cum - lw), r * jnp.exp(cum)], axis=0)
    bk = jnp.concatenate([b * e_neg, k * e_neg], axis=0)
    bk_end = jnp.concatenate([b * e_rem, k * e_rem], axis=0)
    w_end = jnp.exp(c_end)
    bonus_w = r * k * rk_ref[...]
    hs = [slice(h * n, (h + 1) * n) for h in heads]
    zero = jnp.zeros((length, n), F32)

    s0 = [state_ref[h] for h in heads]
    p = [jnp.where(keep, _mm_nt(ar[:, c], bk[:, c], inv_precision), 0.0) for c in hs]
    t_inv = _unit_lower_inverse([x[:length, :length] for x in p], length, RWKV_SUB, inv_precision)
    q = [mm(ar[:, c], s) for c, s in zip(hs, s0)]
    y = [x[:length] + mm(pp[:length], jnp.concatenate([zero, v[:, c]], axis=0))
         for x, pp, c in zip(q, p, hs)]
    u = [mm(ti, yy) for ti, yy in zip(t_inv, y)]
    uv = [jnp.concatenate([uu, v[:, c]], axis=0) for uu, c in zip(u, hs)]
    o = [x[length:] + mm(pp[length:], w) for x, pp, w in zip(q, p, uv)]
    for h in heads:
        c = hs[h]
        w_col = jnp.sum(jnp.where(eye_n, w_end[:, c], 0.0), axis=1, keepdims=True)
        state_ref[h] = w_col * s0[h] + _mm_tn(bk_end[:, c], uv[h], apply_precision)
        mu = jnp.mean(o[h], axis=-1, keepdims=True)
        var = jnp.mean((o[h] - mu) ** 2, axis=-1, keepdims=True)
        yh = (o[h] - mu) * lax.rsqrt(var + RWKV_LN_EPS) * lnw_ref[:, c] + lnb_ref[:, c]
        yh = yh + jnp.sum(bonus_w[:, c], axis=-1, keepdims=True) * v[:, c]
        o_ref[0, :, c] = yh.astype(o_ref.dtype)


def rwkv_core(r, lw, k, v, kn, b, r_k, ln_w, ln_b):
    bsz, t, d = r.shape
    length = RWKV_CHUNK
    n = RWKV_HEAD_DIM
    row = pl.BlockSpec((1, length, d), lambda bb, c: (bb, c, 0))
    vec = pl.BlockSpec((1, d), lambda bb, c: (0, 0))
    return pl.pallas_call(
        functools.partial(_rwkv_core_kernel, inv_precision=RWKV_INV_PRECISION,
                          apply_precision=RWKV_APPLY_PRECISION),
        grid=(bsz, t // length),
        in_specs=[row] * 6 + [vec] * 3,
        out_specs=row,
        out_shape=jax.ShapeDtypeStruct((bsz, t, d), BF16),
        scratch_shapes=[pltpu.VMEM((d // n, n, n), F32)],
        compiler_params=_cparams("parallel", "arbitrary"),
        name="rwkv_core",
    )(r, lw, k, v, kn, b, r_k.reshape(1, d), ln_w.reshape(1, d), ln_b.reshape(1, d))


def rwkv7_layer(x, nw, sh, sc, gate, mix, rkv_w, w0, w1, w2, a0, a1, a2, g1, g2, k_k, k_a, r_k,
                ln_w, ln_b, out_w):
    r, lw, k, v, kn, b, g = rwkv_prep(x, nw, sh, sc, mix, rkv_w, w0, w1, w2, a0, a1, a2, g1, g2,
                                      k_k, k_a)
    y = rwkv_core(r, lw, k, v, kn, b, r_k, ln_w, ln_b)
    return matmul_gated_residual(y, out_w.astype(BF16), x, gate, mul=g)


def _ret_kernel(qkvg_ref, idec_ref, qdec_ref, kdec_ref, cdec_ref, o_ref, state_ref, *, d_model):
    nh = RET_N_HEADS
    dk = d_model // nh
    dv = 2 * d_model // nh
    k_scale = dk ** -0.5

    @pl.when(pl.program_id(1) == 0)
    def _():
        state_ref[...] = jnp.zeros_like(state_ref)

    for h in range(nh):
        q_b = qkvg_ref[0, :, h * dk:(h + 1) * dk]
        k_f = qkvg_ref[0, :, d_model + h * dk:d_model + (h + 1) * dk].astype(F32) * k_scale
        v_b = qkvg_ref[0, :, 2 * d_model + h * dv:2 * d_model + (h + 1) * dv]
        g_f = qkvg_ref[0, :, 4 * d_model + h * dv:4 * d_model + (h + 1) * dv].astype(F32)
        scores = _dot_nt(q_b, k_f.astype(BF16)) * idec_ref[h]
        inner = _dot(scores.astype(BF16), v_b)
        r_old = state_ref[h]
        qd = (q_b.astype(F32) * qdec_ref[:, h:h + 1]).astype(BF16)
        cross = _dot(qd, r_old.astype(BF16))
        kd = (k_f * kdec_ref[:, h:h + 1]).astype(BF16)
        state_ref[h] = r_old * cdec_ref[:, h:h + 1] + _dot_tn(kd, v_b)
        o = inner + cross
        o = o * lax.rsqrt(jnp.mean(o * o, axis=-1, keepdims=True) + NORM_EPS)
        o_ref[0, :, h * dv:(h + 1) * dv] = (_silu(g_f) * o).astype(o_ref.dtype)


def retention_core(qkvg, d_model):
    bsz, t, width = qkvg.shape
    nh, q = RET_N_HEADS, RET_CHUNK
    dk, dv = d_model // nh, 2 * d_model // nh
    log_gamma = jnp.log(1 - jnp.exp2(-5.0 - jnp.arange(nh, dtype=F32)))
    idx = jnp.arange(q)
    rel = idx[:, None] - idx[None, :]
    inner_decay = jnp.where(rel >= 0,
                            jnp.exp(jnp.maximum(rel, 0).astype(F32) * log_gamma[:, None, None]), 0.0)
    q_decay = jnp.exp((idx + 1).astype(F32)[:, None] * log_gamma)
    k_decay = jnp.exp((q - 1 - idx).astype(F32)[:, None] * log_gamma)
    chunk_decay = jnp.exp(q * log_gamma).reshape(1, nh)
    full = lambda shape: pl.BlockSpec(shape, lambda b, c: (0,) * len(shape))
    return pl.pallas_call(
        functools.partial(_ret_kernel, d_model=d_model),
        grid=(bsz, t // q),
        in_specs=[pl.BlockSpec((1, q, width), lambda b, c: (b, c, 0)),
                  full((nh, q, q)), full((q, nh)), full((q, nh)), full((1, nh))],
        out_specs=pl.BlockSpec((1, q, nh * dv), lambda b, c: (b, c, 0)),
        out_shape=jax.ShapeDtypeStruct((bsz, t, nh * dv), BF16),
        scratch_shapes=[pltpu.VMEM((nh, dk, dv), F32)],
        compiler_params=_cparams("parallel", "arbitrary"),
        name="retention_core",
    )(qkvg, inner_decay, q_decay, k_decay, chunk_decay)


def retention_layer(x, nw, sh, sc, gate, in_w, out_w):
    d = x.shape[-1]
    qkvg = norm_mod_matmul(x, nw, sh, sc, in_w.astype(BF16), tm=PROJ_TM, tn=PROJ_TN)
    y = retention_core(qkvg, d)
    return matmul_gated_residual(y, out_w.astype(BF16), x, gate)


def _att_kernel(q_ref, k_ref, v_ref, sl_ref, o_ref, lse_ref, kprev_ref, vprev_ref, *, dil, span):
    w = ATT_BLOCK
    e = ATT_HEAD_DIM
    nblk = pl.program_id(2)

    @pl.when(nblk == 0)
    def _():
        kprev_ref[...] = jnp.zeros_like(kprev_ref)
        vprev_ref[...] = jnp.zeros_like(vprev_ref)

    qi = _iota2((w, 2 * w), 0)
    ci = _iota2((w, 2 * w), 1)
    delta = w + qi - ci
    mask = (delta >= 0) & (delta <= span) & ((nblk > 0) | (ci >= w))
    dist = (dil * delta).astype(F32)
    lane = _iota2((w, LANES), 1)
    lse_all = jnp.zeros((w, LANES), F32)
    for h in range(ATT_HEADS_PER_GROUP):
        cols = slice(h * e, (h + 1) * e)
        q_b = q_ref[0, 0, :, cols]
        kk = jnp.concatenate([kprev_ref[:, cols], k_ref[0, 0, :, cols]], axis=0)
        vv = jnp.concatenate([vprev_ref[:, cols], v_ref[0, 0, :, cols]], axis=0)
        s = _dot_nt(q_b, kk) * (e ** -0.5)
        s = jnp.where(mask, s - sl_ref[:, h:h + 1] * dist, -jnp.inf)
        m = jnp.max(s, axis=-1, keepdims=True)
        p = jnp.exp(s - m)
        den = jnp.sum(p, axis=-1, keepdims=True)
        o = _dot((p / den).astype(BF16), vv)
        o_ref[0, 0, :, cols] = o.astype(o_ref.dtype)
        lse_all = jnp.where(lane == h, m + jnp.log(den), lse_all)
    lse_ref[0, 0] = lse_all
    kprev_ref[...] = k_ref[0, 0]
    vprev_ref[...] = v_ref[0, 0]


def dilated_group_core(qkv, gi, window, dil, slopes):
    bsz, _, ls, _ = qkv.shape
    w = ATT_BLOCK
    gw = ATT_HEADS_PER_GROUP * ATT_HEAD_DIM
    spec = lambda part: pl.BlockSpec((1, 1, w, gw), lambda b, r, n: (b, r, n, part))
    return pl.pallas_call(
        functools.partial(_att_kernel, dil=dil, span=window // dil),
        grid=(bsz, dil, ls // w),
        in_specs=[spec(0), spec(1), spec(2),
                  pl.BlockSpec((1, ATT_HEADS_PER_GROUP), lambda b, r, n: (0, 0))],
        out_specs=[pl.BlockSpec((1, 1, w, gw), lambda b, r, n: (b, r, n, 0)),
                   pl.BlockSpec((1, 1, w, LANES), lambda b, r, n: (b, r, n, 0))],
        out_shape=[jax.ShapeDtypeStruct((bsz, dil, ls, gw), BF16),
                   jax.ShapeDtypeStruct((bsz, dil, ls, LANES), F32)],
        scratch_shapes=[pltpu.VMEM((w, gw), BF16), pltpu.VMEM((w, gw), BF16)],
        compiler_params=_cparams("parallel", "parallel", "arbitrary"),
        name="dilated_attention_g%d" % gi,
    )(qkv, qkv, qkv, slopes.reshape(1, ATT_HEADS_PER_GROUP))


def _att_out_kernel(o0_ref, o1_ref, o2_ref, l0_ref, l1_ref, l2_ref, w_ref, x_ref, gate_ref, o_ref,
                    *scratch):
    tm = x_ref.shape[1]

    def token_major(ref, scr):
        dil = ref.shape[1]
        if dil == 1:
            return ref[0, 0].astype(F32)
        for r in range(dil):
            blk = ref[0, r].astype(F32)
            for c in range(scr.shape[0]):
                scr[c, pl.ds(r, tm // dil, stride=dil), :] = blk[:, c * LANES:(c + 1) * LANES]
        return jnp.concatenate([scr[c] for c in range(scr.shape[0])], axis=1)

    o1, l1, o2, l2 = [token_major(ref, scr) for ref, scr in
                      zip((o1_ref, l1_ref, o2_ref, l2_ref), scratch)]
    o0, l0 = token_major(o0_ref, None), token_major(l0_ref, None)
    m = jnp.maximum(jnp.maximum(l0, l1), l2)
    w0, w1, w2 = jnp.exp(l0 - m), jnp.exp(l1 - m), jnp.exp(l2 - m)
    inv = 1.0 / (w0 + w1 + w2)
    k = o0.shape[1]
    expand = jnp.where(_iota2((LANES, k), 1) // ATT_HEAD_DIM == _iota2((LANES, k), 0), 1.0, 0.0)
    y = (_dot_split(w0 * inv, expand) * o0 + _dot_split(w1 * inv, expand) * o1
         + _dot_split(w2 * inv, expand) * o2)
    o_ref[0] = x_ref[0] + gate_ref[0] * _dot(y.astype(BF16), w_ref[...])


def attention_combine_out(outs, lses, w, x, gate, *, tm=512):
    bsz, t, d = x.shape
    k = w.shape[0]
    tm = min(tm, t)
    row = pl.BlockSpec((1, tm, d), lambda b, i: (b, i, 0))
    res = lambda a: pl.BlockSpec((1, a.shape[1], tm // a.shape[1], a.shape[3]),
                                 lambda b, i: (b, 0, i, 0))
    wide, narrow = pltpu.VMEM((k // LANES, tm, LANES), F32), pltpu.VMEM((1, tm, LANES), F32)
    return pl.pallas_call(
        _att_out_kernel,
        grid=(bsz, t // tm),
        in_specs=[res(a) for a in outs] + [res(a) for a in lses] + [
            pl.BlockSpec((k, d), lambda b, i: (0, 0)), row,
            pl.BlockSpec((1, 1, d), lambda b, i: (b, 0, 0))],
        out_specs=row,
        out_shape=jax.ShapeDtypeStruct((bsz, t, d), F32),
        scratch_shapes=[wide, narrow, wide, narrow],
        compiler_params=_cparams("parallel", "parallel"),
        name="attention_combine_out",
    )(*outs, *lses, w, x, gate)


def attention_layer(x, nw, sh, sc, gate, in_w, out_w):
    n_groups = len(ATT_GROUPS)
    gw = ATT_HEADS_PER_GROUP * ATT_HEAD_DIM
    n_heads = n_groups * ATT_HEADS_PER_GROUP
    slopes = jnp.exp2(-8.0 * jnp.arange(1, n_heads + 1, dtype=F32) / n_heads)
    slopes = slopes.reshape(n_groups, ATT_HEADS_PER_GROUP)
    w_in = in_w.astype(BF16).reshape(in_w.shape[0], 3, n_groups, gw)
    outs, lses = [], []
    for gi, (window, dil) in enumerate(ATT_GROUPS):
        w_g = w_in[:, :, gi, :].reshape(in_w.shape[0], 3 * gw)
        qkv = norm_mod_matmul(x, nw, sh, sc, w_g, tm=PROJ_TM, tn=gw, dil=dil)
        o, lse = dilated_group_core(qkv, gi, window, dil, slopes[gi])
        outs.append(o)
        lses.append(lse)
    return attention_combine_out(outs, lses, out_w.astype(BF16), x, gate)


ROUTER_GROUP_LANE0 = 0
ROUTER_EXPERT_LANE0 = MOE_GROUPS


def _router_kernel(x_ref, nw_ref, sh_ref, sc_ref, wr_ref, br_ref, hf_ref, meta_ref, cnt_ref,
                   carry_ref):
    tm = x_ref.shape[1]

    @pl.when(pl.program_id(1) == 0)
    def _():
        carry_ref[...] = jnp.zeros_like(carry_ref)

    hf = _norm_mod(x_ref[0], nw_ref[...], sh_ref[0], sc_ref[0])
    hf_ref[0] = hf
    logits = _dot_3pass(hf, wr_ref[...]) + br_ref[...]
    lane = _iota2((tm, LANES), 1)
    neg = -jnp.inf
    first = lambda hit: jnp.min(jnp.where(hit, lane, LANES), axis=-1, keepdims=True)

    gl = jnp.where(lane < MOE_GROUPS, logits, neg)
    gmax = jnp.max(gl, axis=-1, keepdims=True)
    gidx = first(gl == gmax)
    g_top = 1.0 / jnp.sum(jnp.exp(gl - gmax), axis=-1, keepdims=True)

    lo = ROUTER_EXPERT_LANE0 + gidx * MOE_EXPERTS_PER_GROUP
    el = jnp.where((lane >= lo) & (lane < lo + MOE_EXPERTS_PER_GROUP), logits, neg)
    m1 = jnp.max(el, axis=-1, keepdims=True)
    i1 = first(el == m1)
    el2 = jnp.where(lane == i1, neg, el)
    m2 = jnp.max(el2, axis=-1, keepdims=True)
    i2 = first(el2 == m2)
    ex = jnp.exp(m2 - m1)
    gate1 = g_top / (1.0 + ex)
    gate2 = g_top * ex / (1.0 + ex)

    hit1 = lane == i1
    hit2 = lane == i2
    onehot = jnp.where(hit1 | hit2, 1.0, 0.0).astype(BF16)
    strict = jnp.where(_iota2((tm, tm), 0) > _iota2((tm, tm), 1), 1.0, 0.0).astype(BF16)
    before = _dot(strict, onehot) + carry_ref[0:1, :]
    rank1 = jnp.sum(jnp.where(hit1, before, 0.0), axis=-1, keepdims=True)
    rank2 = jnp.sum(jnp.where(hit2, before, 0.0), axis=-1, keepdims=True)
    carry_ref[...] = carry_ref[...] + _dot(jnp.ones((8, tm), BF16), onehot)
    cnt_ref[0] = carry_ref[...]

    e1 = (i1 - ROUTER_EXPERT_LANE0).astype(F32)
    e2 = (i2 - ROUTER_EXPERT_LANE0).astype(F32)
    meta = jnp.zeros((tm, LANES), F32)
    for j, val in enumerate((e1, e2, rank1, rank2, gate1, gate2)):
        meta = jnp.where(lane == j, val, meta)
    meta_ref[0] = meta


def moe_router(x, nw, sh, sc, group_w, group_b, expert_w, expert_b, *, tm=512):
    bsz, t, d = x.shape
    tm = min(tm, t)
    pad = LANES - MOE_GROUPS - MOE_EXPERTS
    wr = jnp.pad(jnp.concatenate([group_w, expert_w], axis=1), ((0, 0), (0, pad)))
    br = jnp.pad(jnp.concatenate([group_b, expert_b]), (0, pad)).reshape(1, LANES)
    return pl.pallas_call(
        _router_kernel,
        grid=(bsz, t // tm),
        in_specs=[pl.BlockSpec((1, tm, d), lambda b, i: (b, i, 0)),
                  pl.BlockSpec((1, d), lambda b, i: (0, 0)),
                  pl.BlockSpec((1, 1, d), lambda b, i: (b, 0, 0)),
                  pl.BlockSpec((1, 1, d), lambda b, i: (b, 0, 0)),
                  pl.BlockSpec((d, LANES), lambda b, i: (0, 0)),
                  pl.BlockSpec((1, LANES), lambda b, i: (0, 0))],
        out_specs=[pl.BlockSpec((1, tm, d), lambda b, i: (b, i, 0)),
                   pl.BlockSpec((1, tm, LANES), lambda b, i: (b, i, 0)),
                   pl.BlockSpec((1, 8, LANES), lambda b, i: (b, 0, 0))],
        out_shape=[jax.ShapeDtypeStruct((bsz, t, d), F32),
                   jax.ShapeDtypeStruct((bsz, t, LANES), F32),
                   jax.ShapeDtypeStruct((bsz, 8, LANES), F32)],
        scratch_shapes=[pltpu.VMEM((8, LANES), F32)],
        compiler_params=_cparams("parallel", "arbitrary"),
        name="moe_router",
    )(x, nw.reshape(1, d), sh, sc, wr, br)


def _expert_kernel(e1_ref, e2_ref, r1_ref, r2_ref, g1_ref, g2_ref, cnt_ref,
                   hf_hbm, x_hbm, gt_ref, w1_ref, w3_ref, w2_ref, out_hbm,
                   hf_v, acc_v, xb_v, yb_v, row_tok, row_gate, pstart, sem):
    b = pl.program_id(0)
    e = pl.program_id(1)
    n_e = pl.num_programs(1)
    t = hf_v.shape[0]
    grp = MOE_ROW_GROUP
    rows = xb_v.shape[0] * grp

    @pl.when(e == 0)
    def _():
        cp_h = pltpu.make_async_copy(hf_hbm.at[b], hf_v, sem.at[0])
        cp_x = pltpu.make_async_copy(x_hbm.at[b], acc_v, sem.at[1])
        cp_h.start()
        cp_x.start()
        xb_v[...] = jnp.zeros_like(xb_v)
        for i in range(grp):
            row_tok[2 * t + i] = 0

        def start_body(i, s):
            pstart[i] = s
            return s + cnt_ref[b * n_e + i]

        lax.fori_loop(0, n_e, start_body, 0)

        def tok_body(g, carry):
            for u in range(4):
                tok = g * 4 + u
                d1 = pstart[e1_ref[tok]] + r1_ref[tok]
                row_tok[d1] = tok
                row_gate[d1] = g1_ref[tok]
                d2 = pstart[e2_ref[tok]] + r2_ref[tok]
                row_tok[d2] = tok
                row_gate[d2] = g2_ref[tok]
            return carry

        lax.fori_loop(0, t // 4, tok_body, 0)
        cp_h.wait()
        cp_x.wait()

    cnt = cnt_ref[b * n_e + e]
    start = pstart[e]

    def block_body(j, carry):
        base = start + j * rows
        nrows = jnp.minimum(rows, cnt - j * rows)
        full_groups = nrows // grp

        def gather(g, c):
            first = base + g * grp
            for u in range(grp):
                xb_v[g, u:u + 1, :] = hf_v[pl.ds(row_tok[first + u], 1), :]
            return c

        lax.fori_loop(0, (nrows + grp - 1) // grp, gather, 0)
        xb = xb_v[...].reshape(rows, xb_v.shape[2]).astype(BF16)
        h1 = _dot(xb, w1_ref[0, 0].astype(BF16))
        h3 = _dot(xb, w3_ref[0, 0].astype(BF16))
        act = (_silu(h1) * h3).astype(BF16)
        yb_v[...] = (_dot(act, w2_ref[0, 0].astype(BF16)) * gt_ref[0]).reshape(yb_v.shape)

        def scatter(g, c):
            first = base + g * grp
            toks = [row_tok[first + u] for u in range(grp)]
            vals = [acc_v[pl.ds(toks[u], 1), :] + row_gate[first + u] * yb_v[g, u:u + 1, :]
                    for u in range(grp)]
            for u in range(grp):
                acc_v[pl.ds(toks[u], 1), :] = vals[u]
            return c

        lax.fori_loop(0, full_groups, scatter, 0)

        def scatter_tail(i, c):
            tok = row_tok[base + i]
            acc_v[pl.ds(tok, 1), :] = (acc_v[pl.ds(tok, 1), :] + row_gate[base + i]
                                       * yb_v[i // grp, pl.ds(i % grp, 1), :])
            return c

        lax.fori_loop(full_groups * grp, nrows, scatter_tail, 0)
        return carry

    lax.fori_loop(0, (cnt + rows - 1) // rows, block_body, 0)

    @pl.when(e == n_e - 1)
    def _():
        cp_o = pltpu.make_async_copy(acc_v, out_hbm.at[b], sem.at[2])
        cp_o.start()
        cp_o.wait()


def moe_experts(hf, x, gt, e1, e2, r1, r2, g1, g2, counts, layer, w1, w3, w2):
    bsz, t, d = x.shape
    _, n_e, _, f = w1.shape
    smem_tok = pl.BlockSpec((t,), lambda b, e: (b,), memory_space=pltpu.SMEM)
    return pl.pallas_call(
        _expert_kernel,
        grid=(bsz, n_e),
        in_specs=[smem_tok] * 6 + [
            pl.BlockSpec(memory_space=pltpu.SMEM),
            pl.BlockSpec(memory_space=pl.ANY),
            pl.BlockSpec(memory_space=pl.ANY),
            pl.BlockSpec((1, 1, d), lambda b, e: (b, 0, 0)),
            pl.BlockSpec((1, 1, d, f), lambda b, e: (layer, e, 0, 0)),
            pl.BlockSpec((1, 1, d, f), lambda b, e: (layer, e, 0, 0)),
            pl.BlockSpec((1, 1, f, d), lambda b, e: (layer, e, 0, 0))],
        out_specs=pl.BlockSpec(memory_space=pl.ANY),
        out_shape=jax.ShapeDtypeStruct((bsz, t, d), F32),
        scratch_shapes=[pltpu.VMEM((t, d), F32), pltpu.VMEM((t, d), F32),
                        pltpu.VMEM((MOE_ROWS // MOE_ROW_GROUP, MOE_ROW_GROUP, d), F32),
                        pltpu.VMEM((MOE_ROWS // MOE_ROW_GROUP, MOE_ROW_GROUP, d), F32),
                        pltpu.SMEM((2 * t + MOE_ROW_GROUP,), jnp.int32),
                        pltpu.SMEM((2 * t + MOE_ROW_GROUP,), F32),
                        pltpu.SMEM((n_e,), jnp.int32), pltpu.SemaphoreType.DMA((3,))],
        compiler_params=_cparams("arbitrary", "arbitrary"),
        name="moe_experts",
    )(e1, e2, r1, r2, g1, g2, counts, hf, x, gt, w1, w3, w2)


def moe_layer(x, nw, sh, sc, gate, group_w, group_b, expert_w, expert_b, layer, w1, w3, w2):
    bsz, t, d = x.shape
    hf, meta, cnt = moe_router(x, nw, sh, sc, group_w, group_b, expert_w, expert_b)
    ints = meta[:, :, :4].astype(jnp.int32).reshape(bsz * t, 4)
    gates = meta[:, :, 4:6].reshape(bsz * t, 2)
    counts = cnt[:, 0, ROUTER_EXPERT_LANE0:ROUTER_EXPERT_LANE0 + MOE_EXPERTS].astype(jnp.int32)
    return moe_experts(hf, x, gate, ints[:, 0], ints[:, 1], ints[:, 2], ints[:, 3],
                       gates[:, 0], gates[:, 1], counts.reshape(-1), layer, w1, w3, w2)


def _final_norm_kernel(x_ref, w_ref, o_ref):
    x = x_ref[0]
    o_ref[0] = x * lax.rsqrt(jnp.mean(x * x, axis=-1, keepdims=True) + NORM_EPS) * w_ref[...]


def final_rms_norm(x, w, *, tm=1024):
    bsz, t, d = x.shape
    tm = min(tm, t)
    row = pl.BlockSpec((1, tm, d), lambda b, i: (b, i, 0))
    return pl.pallas_call(
        _final_norm_kernel,
        grid=(bsz, t // tm),
        in_specs=[row, pl.BlockSpec((1, d), lambda b, i: (0, 0))],
        out_specs=row,
        out_shape=jax.ShapeDtypeStruct((bsz, t, d), F32),
        compiler_params=_cparams("parallel", "parallel"),
        name="final_rms_norm",
    )(x, w.reshape(1, d))


def kernel(x, c, ada_w, ada_b, norm_mix_w, norm_ffn_w, ssm_in_w, ssm_conv_w, ssm_conv_b, ssm_dt_bias, ssm_a_log, ssm_d, ssm_norm_w, ssm_out_w, rwkv_mix, rwkv_rkv_w, rwkv_w0, rwkv_w1, rwkv_w2, rwkv_a0, rwkv_a1, rwkv_a2, rwkv_g1, rwkv_g2, rwkv_k_k, rwkv_k_a, rwkv_r_k, rwkv_ln_w, rwkv_ln_b, rwkv_out_w, ret_in_w, ret_out_w, att_in_w, att_out_w, moe_group_w, moe_group_b, moe_expert_w, moe_expert_b, moe_w1, moe_w3, moe_w2, final_norm_w):
    depth = ada_w.shape[0]
    d = x.shape[-1]
    mod = ada_modulation(c, ada_w, ada_b)
    for i in range(depth):
        sh1, sc1, gt1, sh2, sc2, gt2 = [mod[i][:, None, j * d:(j + 1) * d] for j in range(6)]
        kind, j = i % 4, i // 4
        pre = (x, norm_mix_w[i], sh1, sc1, gt1)
        if kind == 0:
            x = mamba2_layer(*pre, ssm_in_w[j], ssm_conv_w[j], ssm_conv_b[j], ssm_dt_bias[j],
                             ssm_a_log[j], ssm_d[j], ssm_norm_w[j], ssm_out_w[j])
        elif kind == 1:
            x = rwkv7_layer(*pre, rwkv_mix[j], rwkv_rkv_w[j], rwkv_w0[j], rwkv_w1[j], rwkv_w2[j],
                            rwkv_a0[j], rwkv_a1[j], rwkv_a2[j], rwkv_g1[j], rwkv_g2[j], rwkv_k_k[j],
                            rwkv_k_a[j], rwkv_r_k[j], rwkv_ln_w[j], rwkv_ln_b[j], rwkv_out_w[j])
        elif kind == 2:
            x = retention_layer(*pre, ret_in_w[j], ret_out_w[j])
        else:
            x = attention_layer(*pre, att_in_w[j], att_out_w[j])
        x = moe_layer(x, norm_ffn_w[i], sh2, sc2, gt2, moe_group_w[i], moe_group_b[i],
                      moe_expert_w[i], moe_expert_b[i], i, moe_w1, moe_w3, moe_w2)
    return final_rms_norm(x, final_norm_w)
```

```python
import functools
import math

import jax
import jax.numpy as jnp
from jax import lax
from jax.experimental import pallas as pl
from jax.experimental.pallas import tpu as pltpu

F32 = jnp.float32
BF16 = jnp.bfloat16
HI = lax.Precision.HIGHEST

NORM_EPS = 1e-6
LANES = 128
VMEM_LIMIT = 56 * 1024 * 1024
PROJ_TM = 2048
PROJ_TN = 1024

SSM_HEAD_DIM = 64
SSM_N_GROUPS = 8
SSM_HEADS_PER_GROUP = 4
SSM_D_STATE = 128
SSM_CONV = 4
SSM_CHUNK = 128
SSM_CONV_COLS = 256
RWKV_HEAD_DIM = 64
RWKV_LN_EPS = 64e-5
RWKV_CHUNK = 128
RWKV_SUB = 16
RWKV_INV_PRECISION = None
RWKV_APPLY_PRECISION = None
RET_N_HEADS = 4
RET_CHUNK = 128
ATT_GROUPS = ((128, 1), (512, 4), (2048, 16))
ATT_HEADS_PER_GROUP = 8
ATT_HEAD_DIM = 128
ATT_BLOCK = 128
MOE_GROUPS = 4
MOE_EXPERTS_PER_GROUP = 8
MOE_EXPERTS = 32
MOE_ROWS = 256
MOE_ROW_GROUP = 8


def _cparams(*sem):
    return pltpu.CompilerParams(dimension_semantics=sem, vmem_limit_bytes=VMEM_LIMIT)


def _sigmoid(x):
    return 1.0 / (1.0 + jnp.exp(-x))


def _silu(x):
    return x * _sigmoid(x)


def _softplus(x):
    return jnp.maximum(x, 0.0) + jnp.log(1.0 + jnp.exp(-jnp.abs(x)))


def _dot(a, b, precision=None):
    return jnp.dot(a, b, preferred_element_type=F32, precision=precision)


def _dot_nt(a, b, precision=None):
    return lax.dot_general(a, b, (((1,), (1,)), ((), ())), preferred_element_type=F32,
                           precision=precision)


def _dot_tn(a, b, precision=None):
    return lax.dot_general(a, b, (((0,), (0,)), ((), ())), preferred_element_type=F32,
                           precision=precision)


def _split_bf16(a):
    hi = a.astype(BF16)
    return hi, (a - hi.astype(F32)).astype(BF16)


def _dot_split(a, m, transpose_rhs=False, lhs_exact=False):
    hi, lo = _split_bf16(a)
    mb = m.astype(BF16)
    if lhs_exact:
        return _dot(mb, hi) + _dot(mb, lo)
    dot = _dot_nt if transpose_rhs else _dot
    return dot(hi, mb) + dot(lo, mb)


def _dot_3pass(a, b):
    a_hi, a_lo = _split_bf16(a)
    b_hi, b_lo = _split_bf16(b)
    return _dot(a_hi, b_hi) + (_dot(a_lo, b_hi) + _dot(a_hi, b_lo))


def _iota2(shape, axis):
    return lax.broadcasted_iota(jnp.int32, shape, axis)


def _ada_kernel(c_ref, w_ref, b_ref, o_ref):
    cs = _silu(c_ref[...])
    o_ref[0] = _dot(cs, w_ref[0], HI) + b_ref[0]


def ada_modulation(c, ada_w, ada_b):
    depth, d, n = ada_w.shape
    bsz = c.shape[0]
    rows = -(-bsz // 8) * 8
    cp = jnp.pad(c, ((0, rows - bsz), (0, 0)))
    tn = 1536
    out = pl.pallas_call(
        _ada_kernel,
        grid=(depth, n // tn),
        in_specs=[pl.BlockSpec((rows, d), lambda l, j: (0, 0)),
                  pl.BlockSpec((1, d, tn), lambda l, j: (l, 0, j)),
                  pl.BlockSpec((1, 1, tn), lambda l, j: (l, 0, j))],
        out_specs=pl.BlockSpec((1, rows, tn), lambda l, j: (l, 0, j)),
        out_shape=jax.ShapeDtypeStruct((depth, rows, n), F32),
        compiler_params=_cparams("parallel", "parallel"),
        name="ada_modulation",
    )(cp, ada_w, ada_b.reshape(depth, 1, n))
    return out[:, :bsz]


def _norm_mod(x, nw, sh, sc):
    y = x * lax.rsqrt(jnp.mean(x * x, axis=-1, keepdims=True) + NORM_EPS) * nw
    return y * (1.0 + sc) + sh


def _nmm_kernel(*refs, dil, has_side):
    if has_side:
        x_ref, nw_ref, sh_ref, sc_ref, w_ref, ws_ref, o_ref, side_ref, h_ref, *rest = refs
    else:
        x_ref, nw_ref, sh_ref, sc_ref, w_ref, o_ref, h_ref, *rest = refs

    @pl.when(pl.program_id(2) == 0)
    def _():
        h = _norm_mod(x_ref[0], nw_ref[...], sh_ref[0], sc_ref[0]).astype(h_ref.dtype)
        h_ref[...] = h
        if has_side:
            w_hi, w_lo = _split_bf16(ws_ref[...])
            side_ref[0] = _dot(h, w_hi) + _dot(h, w_lo)

    res = _dot(h_ref[...], w_ref[...])
    if dil is None:
        o_ref[0] = res.astype(o_ref.dtype)
    else:
        acc_ref, = rest
        rows = acc_ref.shape[1] // dil
        for c in range(acc_ref.shape[0]):
            acc_ref[c] = res[:, c * LANES:(c + 1) * LANES]
        for r in range(dil):
            for c in range(acc_ref.shape[0]):
                o_ref[0, r, :, c * LANES:(c + 1) * LANES] = (
                    acc_ref[c, pl.ds(r, rows, stride=dil), :].astype(o_ref.dtype))


def norm_mod_matmul(x, nw, sh, sc, w, *, tm, tn, dil=None, side_w=None):
    bsz, t, d = x.shape
    n = w.shape[1]
    tm = min(tm, t)
    scratch = [pltpu.VMEM((tm, d), BF16)]
    in_specs = [pl.BlockSpec((1, tm, d), lambda b, i, j: (b, i, 0)),
                pl.BlockSpec((1, d), lambda b, i, j: (0, 0)),
                pl.BlockSpec((1, 1, d), lambda b, i, j: (b, 0, 0)),
                pl.BlockSpec((1, 1, d), lambda b, i, j: (b, 0, 0)),
                pl.BlockSpec((d, tn), lambda b, i, j: (0, j))]
    ins = [x, nw.reshape(1, d), sh, sc, w]
    if dil is None:
        out_specs = [pl.BlockSpec((1, tm, tn), lambda b, i, j: (b, i, j))]
        out_shape = [jax.ShapeDtypeStruct((bsz, t, n), BF16)]
    else:
        out_specs = [pl.BlockSpec((1, dil, tm // dil, tn), lambda b, i, j: (b, 0, i, j))]
        out_shape = [jax.ShapeDtypeStruct((bsz, dil, t // dil, n), BF16)]
        scratch.append(pltpu.VMEM((tn // LANES, tm, LANES), F32))
    if side_w is not None:
        in_specs.append(pl.BlockSpec((d, LANES), lambda b, i, j: (0, 0)))
        ins.append(side_w)
        out_specs.append(pl.BlockSpec((1, tm, LANES), lambda b, i, j: (b, i, 0)))
        out_shape.append(jax.ShapeDtypeStruct((bsz, t, LANES), F32))
    outs = pl.pallas_call(
        functools.partial(_nmm_kernel, dil=dil, has_side=side_w is not None),
        grid=(bsz, t // tm, n // tn),
        in_specs=in_specs,
        out_specs=out_specs,
        out_shape=out_shape,
        scratch_shapes=scratch,
        compiler_params=_cparams("parallel", "parallel", "arbitrary"),
        name="norm_mod_matmul",
    )(*ins)
    return outs if side_w is not None else outs[0]


def _mgr_kernel(*refs, has_mul):
    if has_mul:
        y_ref, g_ref, w_ref, x_ref, gate_ref, o_ref = refs
        y = (y_ref[0].astype(F32) * g_ref[0].astype(F32)).astype(BF16)
    else:
        y_ref, w_ref, x_ref, gate_ref, o_ref = refs
        y = y_ref[0].astype(BF16)
    o_ref[0] = x_ref[0] + gate_ref[0] * _dot(y, w_ref[...])


def matmul_gated_residual(y, w, x, gate, mul=None, *, tm=512):
    bsz, t, k = y.shape
    d = w.shape[1]
    tm = min(tm, t)
    ins = [y] + ([mul] if mul is not None else []) + [w, x, gate]
    row = lambda width: pl.BlockSpec((1, tm, width), lambda b, i: (b, i, 0))
    specs = [row(k)] + ([row(k)] if mul is not None else []) + [
        pl.BlockSpec((k, d), lambda b, i: (0, 0)), row(d),
        pl.BlockSpec((1, 1, d), lambda b, i: (b, 0, 0))]
    return pl.pallas_call(
        functools.partial(_mgr_kernel, has_mul=mul is not None),
        grid=(bsz, t // tm),
        in_specs=specs,
        out_specs=row(d),
        out_shape=jax.ShapeDtypeStruct((bsz, t, d), F32),
        compiler_params=_cparams("parallel", "parallel"),
        name="matmul_gated_residual",
    )(*ins)


def _ssd_kernel(zx_ref, dt_ref, cw_ref, cb_ref, dtb_ref, alog_ref, dsk_ref, nw_ref, o_ref,
                prev_ref, xbc_ref, state_ref):
    q = SSM_CHUNK
    g_n, hg, p, n = SSM_N_GROUPS, SSM_HEADS_PER_GROUP, SSM_HEAD_DIM, SSM_D_STATE
    d_inner = g_n * hg * p
    nh = g_n * hg
    gw = hg * p

    @pl.when(pl.program_id(1) == 0)
    def _():
        prev_ref[...] = jnp.zeros_like(prev_ref)
        state_ref[...] = jnp.zeros_like(state_ref)

    src = _iota2((q, 2 * q), 1) - q - _iota2((q, 2 * q), 0)
    shifts = [jnp.where(src == -s, 1.0, 0.0).astype(BF16) for s in range(1, SSM_CONV)]
    cdim = xbc_ref.shape[1]
    for c0 in range(0, cdim, SSM_CONV_COLS):
        cs = slice(c0, c0 + SSM_CONV_COLS)
        cur = zx_ref[0, :, d_inner + c0:d_inner + c0 + SSM_CONV_COLS]
        ext = jnp.concatenate([prev_ref[:, cs], cur], axis=0)
        prev_ref[:, cs] = cur
        acc = cb_ref[:, cs] + cur.astype(F32) * cw_ref[SSM_CONV - 1:SSM_CONV, cs]
        for s in range(1, SSM_CONV):
            acc = acc + _dot(shifts[s - 1], ext) * cw_ref[SSM_CONV - 1 - s:SSM_CONV - s, cs]
        xbc_ref[:, cs] = _silu(acc)

    dt = _softplus(dt_ref[0][:, :nh] + dtb_ref[...])
    da = dt * -jnp.exp(alog_ref[...])
    tril = (_iota2((q, q), 0) >= _iota2((q, q), 1))
    acum = _dot_split(da, tril.astype(F32), lhs_exact=True)
    a_end = acum[q - 1:q, :]
    chunk_decay = jnp.exp(a_end)
    adj_t = (acum - jnp.log(dt)).T
    wdec_t = (dt * jnp.exp(a_end - acum)).T

    for g in range(g_n):
        xs_g = xbc_ref[:, g * gw:(g + 1) * gw]
        bm = xbc_ref[:, d_inner + g * n:d_inner + (g + 1) * n]
        cm = xbc_ref[:, d_inner + g_n * n + g * n:d_inner + g_n * n + (g + 1) * n]
        cm_b = cm.astype(BF16)
        cb = _dot_nt(cm_b, bm.astype(BF16))
        bm_t = bm.T
        xs_b = xs_g.astype(BF16)
        st_old = state_ref[g]
        y_carry = _dot(cm_b, st_old.astype(BF16))
        ys, st_cols = [], []
        for hh in range(hg):
            h = g * hg + hh
            hs = slice(hh * p, (hh + 1) * p)
            a_col = jnp.broadcast_to(acum[:, h:h + 1], (q, q))
            m_h = cb * jnp.exp(jnp.where(tril, a_col - adj_t[h:h + 1, :], -jnp.inf))
            ys.append(_dot(m_h.astype(BF16), xs_b[:, hs]) + jnp.exp(a_col[:, :p]) * y_carry[:, hs])
            st_new = _dot((bm_t * wdec_t[h:h + 1, :]).astype(BF16), xs_b[:, hs])
            st_cols.append(st_old[:, hs] * chunk_decay[:, h:h + 1] + st_new)
        state_ref[g] = jnp.concatenate(st_cols, axis=1)
        y = jnp.concatenate(ys, axis=1) + xs_g * dsk_ref[:, g * gw:(g + 1) * gw]
        z = zx_ref[0, :, g * gw:(g + 1) * gw].astype(F32)
        y = y * _silu(z)
        y = y * lax.rsqrt(jnp.mean(y * y, axis=-1, keepdims=True) + NORM_EPS)
        o_ref[0, :, g * gw:(g + 1) * gw] = (y * nw_ref[:, g * gw:(g + 1) * gw]).astype(o_ref.dtype)


def ssd_core(zx, dt_raw, conv_w, conv_b, dt_bias, a_log, d_skip, norm_w):
    bsz, t, width = zx.shape
    d_inner = SSM_N_GROUPS * SSM_HEADS_PER_GROUP * SSM_HEAD_DIM
    cdim = width - d_inner
    nh = dt_bias.shape[0]
    q = SSM_CHUNK
    full = lambda shape: pl.BlockSpec(shape, lambda b, c: (0,) * len(shape))
    return pl.pallas_call(
        _ssd_kernel,
        grid=(bsz, t // q),
        in_specs=[pl.BlockSpec((1, q, width), lambda b, c: (b, c, 0)),
                  pl.BlockSpec((1, q, LANES), lambda b, c: (b, c, 0)),
                  full((SSM_CONV, cdim)), full((1, cdim)), full((1, nh)), full((1, nh)),
                  full((1, d_inner)), full((1, d_inner))],
        out_specs=pl.BlockSpec((1, q, d_inner), lambda b, c: (b, c, 0)),
        out_shape=jax.ShapeDtypeStruct((bsz, t, d_inner), BF16),
        scratch_shapes=[pltpu.VMEM((q, cdim), BF16), pltpu.VMEM((q, cdim), F32),
                        pltpu.VMEM((SSM_N_GROUPS, SSM_D_STATE,
                                    SSM_HEADS_PER_GROUP * SSM_HEAD_DIM), F32)],
        compiler_params=_cparams("parallel", "arbitrary"),
        name="ssd_core",
    )(zx, dt_raw, conv_w, conv_b.reshape(1, cdim), dt_bias.reshape(1, nh),
      a_log.reshape(1, nh), jnp.repeat(d_skip, SSM_HEAD_DIM).reshape(1, d_inner),
      norm_w.reshape(1, d_inner))


def mamba2_layer(x, nw, sh, sc, gate, in_w, conv_w, conv_b, dt_bias, a_log, d_skip, norm_w, out_w):
    d_inner = SSM_N_GROUPS * SSM_HEADS_PER_GROUP * SSM_HEAD_DIM
    cdim = conv_w.shape[1]
    nh = dt_bias.shape[0]
    w_main = in_w[:, :d_inner + cdim].astype(BF16)
    w_dt = jnp.pad(in_w[:, d_inner + cdim:], ((0, 0), (0, LANES - nh)))
    zx, dt_raw = norm_mod_matmul(x, nw, sh, sc, w_main, tm=PROJ_TM, tn=PROJ_TN, side_w=w_dt)
    y = ssd_core(zx, dt_raw, conv_w, conv_b, dt_bias, a_log, d_skip, norm_w)
    return matmul_gated_residual(y, out_w.astype(BF16), x, gate)


def _head_indicator(d, n):
    return jnp.where(_iota2((d, LANES), 0) // n == _iota2((d, LANES), 1), 1.0, 0.0)


def _rwkv_prep_kernel(x_ref, xp_ref, nw_ref, sh_ref, sc_ref, mix_ref, wr_ref, wk_ref, wv_ref,
                      w0_ref, w1_ref, w2_ref, a0_ref, a1_ref, a2_ref, g1_ref, g2_ref, kk_ref, ka_ref,
                      r_o, lw_o, k_o, v_o, kn_o, b_o, g_o):
    tm, d = x_ref.shape[1], x_ref.shape[2]
    nw, sh, sc = nw_ref[...], sh_ref[0], sc_ref[0]
    hm = _norm_mod(x_ref[0], nw, sh, sc)
    prev = _norm_mod(xp_ref[0], nw, sh, sc)[7:8, :]
    prev = jnp.where(pl.program_id(1) > 0, prev, 0.0)
    shifted = jnp.where(_iota2((tm, d), 0) == 0, prev, pltpu.roll(hm, 1, axis=0))
    xx = shifted - hm
    mixed = lambda j: (hm + xx * mix_ref[j:j + 1, :]).astype(BF16)
    xr, xw, xk, xv, xa, xg = [mixed(j) for j in range(6)]
    r = _dot(xr, wr_ref[...])
    k = _dot(xk, wk_ref[...])
    v = _dot(xv, wv_ref[...])
    lora = lambda u, w: _dot(u.astype(BF16), w[...])
    w_log = -_softplus(-(w0_ref[...] + lora(jnp.tanh(lora(xw, w1_ref)), w2_ref))) - 0.5
    a = _sigmoid(a0_ref[...] + lora(lora(xa, a1_ref), a2_ref))
    g = lora(_sigmoid(lora(xg, g1_ref)), g2_ref)
    kk = k * kk_ref[...]
    ind = _head_indicator(d, RWKV_HEAD_DIM)
    nrm = jnp.maximum(jnp.sqrt(_dot_split(kk * kk, ind)), 1e-12)
    kn = kk * _dot_split(1.0 / nrm, ind, transpose_rhs=True)
    r_o[0] = r.astype(r_o.dtype)
    lw_o[0] = -jnp.exp(w_log)
    k_o[0] = (k * (1.0 + (a - 1.0) * ka_ref[...])).astype(k_o.dtype)
    v_o[0] = v.astype(v_o.dtype)
    kn_o[0] = kn.astype(kn_o.dtype)
    b_o[0] = (kn * a).astype(b_o.dtype)
    g_o[0] = g.astype(g_o.dtype)


def rwkv_prep(x, nw, sh, sc, mix, rkv_w, w0, w1, w2, a0, a1, a2, g1, g2, k_k, k_a, *, tm=512):
    bsz, t, d = x.shape
    tm = min(tm, t)
    full = lambda arr: pl.BlockSpec(arr.shape, lambda b, i: (0,) * arr.ndim)
    row = pl.BlockSpec((1, tm, d), lambda b, i: (b, i, 0))
    vec = pl.BlockSpec((1, 1, d), lambda b, i: (b, 0, 0))
    bf = lambda w: w.astype(BF16)
    params = [mix, bf(rkv_w[0]), bf(rkv_w[1]), bf(rkv_w[2]), w0.reshape(1, d), bf(w1), bf(w2),
              a0.reshape(1, d), bf(a1), bf(a2), bf(g1), bf(g2), k_k.reshape(1, d), k_a.reshape(1, d)]
    prev_spec = pl.BlockSpec((1, 8, d), lambda b, i: (b, jnp.maximum(i * (tm // 8) - 1, 0), 0))
    return pl.pallas_call(
        _rwkv_prep_kernel,
        grid=(bsz, t // tm),
        in_specs=[row, prev_spec, pl.BlockSpec((1, d), lambda b, i: (0, 0)), vec, vec]
        + [full(p) for p in params],
        out_specs=[row] * 7,
        out_shape=[jax.ShapeDtypeStruct((bsz, t, d), F32 if i == 1 else BF16) for i in range(7)],
        compiler_params=_cparams("parallel", "parallel"),
        name="rwkv_prep",
    )(x, x, nw.reshape(1, d), sh, sc, *params)


def _mm(a, b, precision):
    if precision is None:
        return _dot(a.astype(BF16), b.astype(BF16))
    return _dot(a, b, precision)


def _mm_nt(a, b, precision):
    if precision is None:
        return _dot_nt(a.astype(BF16), b.astype(BF16))
    return _dot_nt(a, b, precision)


def _mm_tn(a, b, precision):
    if precision is None:
        return _dot_tn(a.astype(BF16), b.astype(BF16))
    return _dot_tn(a, b, precision)


def _unit_lower_inverse(m_list, length, sub, precision):
    ri = _iota2((length, length), 0)
    ci = _iota2((length, length), 1)
    eye = jnp.where(ri == ci, 1.0, 0.0)
    same_block = ri // sub == ci // sub
    mm = lambda a, b: _mm(a, b, precision)
    md = [jnp.where(same_block, m, 0.0) for m in m_list]
    off = [m - d for m, d in zip(m_list, md)]
    dinv = [eye + d for d in md]
    pw = md
    for _ in range(int(math.log2(sub)) - 1):
        pw = [mm(p, p) for p in pw]
        dinv = [x + mm(x, p) for x, p in zip(dinv, pw)]
    pm = [mm(x, o) for x, o in zip(dinv, off)]
    acc = [eye + p for p in pm]
    pw = pm
    for _ in range(int(math.log2(length // sub)) - 1):
        pw = [mm(p, p) for p in pw]
        acc = [x + mm(x, p) for x, p in zip(acc, pw)]
    return [mm(x, d) for x, d in zip(acc, dinv)]


def _rwkv_core_kernel(r_ref, lw_ref, k_ref, v_ref, kn_ref, b_ref, rk_ref, lnw_ref, lnb_ref, o_ref,
                      state_ref, *, inv_precision, apply_precision):
    length, d = r_ref.shape[1], r_ref.shape[2]
    n = RWKV_HEAD_DIM
    heads = range(d // n)
    mm = lambda a, b: _mm(a, b, apply_precision)

    @pl.when(pl.program_id(1) == 0)
    def _():
        state_ref[...] = jnp.zeros_like(state_ref)

    ri = _iota2((2 * length, 2 * length), 0)
    ci = _iota2((2 * length, 2 * length), 1) % length
    keep = ((ri < length) & (ri > ci)) | ((ri >= length) & (ri - length >= ci))
    eye_n = _iota2((n, n), 0) == _iota2((n, n), 1)

    lw = lw_ref[0]
    r, k, v, kn, b = [ref[0].astype(F32) for ref in (r_ref, k_ref, v_ref, kn_ref, b_ref)]
    tri = jnp.where(_iota2((length, length), 0) >= _iota2((length, length), 1), 1.0, 0.0)
    cum = _dot_split(lw, tri, lhs_exact=True)
    c_end = cum[length - 1:length, :]
    e_neg = jnp.exp(-cum)
    e_rem = jnp.exp(c_end - cum)
    ar = jnp.concatenate([-kn * jnp.exp(cum - lw), r * jnp.exp(cum)], axis=0)
    bk = jnp.concatenate([b * e_neg, k * e_neg], axis=0)
    bk_end = jnp.concatenate([b * e_rem, k * e_rem], axis=0)
    w_end = jnp.exp(c_end)
    bonus_w = r * k * rk_ref[...]
    hs = [slice(h * n, (h + 1) * n) for h in heads]
    zero = jnp.zeros((length, n), F32)

    s0 = [state_ref[h] for h in heads]
    p = [jnp.where(keep, _mm_nt(ar[:, c], bk[:, c], inv_precision), 0.0) for c in hs]
    t_inv = _unit_lower_inverse([x[:length, :length] for x in p], length, RWKV_SUB, inv_precision)
    q = [mm(ar[:, c], s) for c, s in zip(hs, s0)]
    y = [x[:length] + mm(pp[:length], jnp.concatenate([zero, v[:, c]], axis=0))
         for x, pp, c in zip(q, p, hs)]
    u = [mm(ti, yy) for ti, yy in zip(t_inv, y)]
    uv = [jnp.concatenate([uu, v[:, c]], axis=0) for uu, c in zip(u, hs)]
    o = [x[length:] + mm(pp[length:], w) for x, pp, w in zip(q, p, uv)]
    for h in heads:
        c = hs[h]
        w_col = jnp.sum(jnp.where(eye_n, w_end[:, c], 0.0), axis=1, keepdims=True)
        state_ref[h] = w_col * s0[h] + _mm_tn(bk_end[:, c], uv[h], apply_precision)
        mu = jnp.mean(o[h], axis=-1, keepdims=True)
        var = jnp.mean((o[h] - mu) ** 2, axis=-1, keepdims=True)
        yh = (o[h] - mu) * lax.rsqrt(var + RWKV_LN_EPS) * lnw_ref[:, c] + lnb_ref[:, c]
        yh = yh + jnp.sum(bonus_w[:, c], axis=-1, keepdims=True) * v[:, c]
        o_ref[0, :, c] = yh.astype(o_ref.dtype)


def rwkv_core(r, lw, k, v, kn, b, r_k, ln_w, ln_b):
    bsz, t, d = r.shape
    length = RWKV_CHUNK
    n = RWKV_HEAD_DIM
    row = pl.BlockSpec((1, length, d), lambda bb, c: (bb, c, 0))
    vec = pl.BlockSpec((1, d), lambda bb, c: (0, 0))
    return pl.pallas_call(
        functools.partial(_rwkv_core_kernel, inv_precision=RWKV_INV_PRECISION,
                          apply_precision=RWKV_APPLY_PRECISION),
        grid=(bsz, t // length),
        in_specs=[row] * 6 + [vec] * 3,
        out_specs=row,
        out_shape=jax.ShapeDtypeStruct((bsz, t, d), BF16),
        scratch_shapes=[pltpu.VMEM((d // n, n, n), F32)],
        compiler_params=_cparams("parallel", "arbitrary"),
        name="rwkv_core",
    )(r, lw, k, v, kn, b, r_k.reshape(1, d), ln_w.reshape(1, d), ln_b.reshape(1, d))


def rwkv7_layer(x, nw, sh, sc, gate, mix, rkv_w, w0, w1, w2, a0, a1, a2, g1, g2, k_k, k_a, r_k,
                ln_w, ln_b, out_w):
    r, lw, k, v, kn, b, g = rwkv_prep(x, nw, sh, sc, mix, rkv_w, w0, w1, w2, a0, a1, a2, g1, g2,
                                      k_k, k_a)
    y = rwkv_core(r, lw, k, v, kn, b, r_k, ln_w, ln_b)
    return matmul_gated_residual(y, out_w.astype(BF16), x, gate, mul=g)


def _ret_kernel(qkvg_ref, idec_ref, qdec_ref, kdec_ref, cdec_ref, o_ref, state_ref, *, d_model):
    nh = RET_N_HEADS
    dk = d_model // nh
    dv = 2 * d_model // nh
    k_scale = dk ** -0.5

    @pl.when(pl.program_id(1) == 0)
    def _():
        state_ref[...] = jnp.zeros_like(state_ref)

    for h in range(nh):
        q_b = qkvg_ref[0, :, h * dk:(h + 1) * dk]
        k_f = qkvg_ref[0, :, d_model + h * dk:d_model + (h + 1) * dk].astype(F32) * k_scale
        v_b = qkvg_ref[0, :, 2 * d_model + h * dv:2 * d_model + (h + 1) * dv]
        g_f = qkvg_ref[0, :, 4 * d_model + h * dv:4 * d_model + (h + 1) * dv].astype(F32)
        scores = _dot_nt(q_b, k_f.astype(BF16)) * idec_ref[h]
        inner = _dot(scores.astype(BF16), v_b)
        r_old = state_ref[h]
        qd = (q_b.astype(F32) * qdec_ref[:, h:h + 1]).astype(BF16)
        cross = _dot(qd, r_old.astype(BF16))
        kd = (k_f * kdec_ref[:, h:h + 1]).astype(BF16)
        state_ref[h] = r_old * cdec_ref[:, h:h + 1] + _dot_tn(kd, v_b)
        o = inner + cross
        o = o * lax.rsqrt(jnp.mean(o * o, axis=-1, keepdims=True) + NORM_EPS)
        o_ref[0, :, h * dv:(h + 1) * dv] = (_silu(g_f) * o).astype(o_ref.dtype)


def retention_core(qkvg, d_model):
    bsz, t, width = qkvg.shape
    nh, q = RET_N_HEADS, RET_CHUNK
    dk, dv = d_model // nh, 2 * d_model // nh
    log_gamma = jnp.log(1 - jnp.exp2(-5.0 - jnp.arange(nh, dtype=F32)))
    idx = jnp.arange(q)
    rel = idx[:, None] - idx[None, :]
    inner_decay = jnp.where(rel >= 0,
                            jnp.exp(jnp.maximum(rel, 0).astype(F32) * log_gamma[:, None, None]), 0.0)
    q_decay = jnp.exp((idx + 1).astype(F32)[:, None] * log_gamma)
    k_decay = jnp.exp((q - 1 - idx).astype(F32)[:, None] * log_gamma)
    chunk_decay = jnp.exp(q * log_gamma).reshape(1, nh)
    full = lambda shape: pl.BlockSpec(shape, lambda b, c: (0,) * len(shape))
    return pl.pallas_call(
        functools.partial(_ret_kernel, d_model=d_model),
        grid=(bsz, t // q),
        in_specs=[pl.BlockSpec((1, q, width), lambda b, c: (b, c, 0)),
                  full((nh, q, q)), full((q, nh)), full((q, nh)), full((1, nh))],
        out_specs=pl.BlockSpec((1, q, nh * dv), lambda b, c: (b, c, 0)),
        out_shape=jax.ShapeDtypeStruct((bsz, t, nh * dv), BF16),
        scratch_shapes=[pltpu.VMEM((nh, dk, dv), F32)],
        compiler_params=_cparams("parallel", "arbitrary"),
        name="retention_core",
    )(qkvg, inner_decay, q_decay, k_decay, chunk_decay)


def retention_layer(x, nw, sh, sc, gate, in_w, out_w):
    d = x.shape[-1]
    qkvg = norm_mod_matmul(x, nw, sh, sc, in_w.astype(BF16), tm=PROJ_TM, tn=PROJ_TN)
    y = retention_core(qkvg, d)
    return matmul_gated_residual(y, out_w.astype(BF16), x, gate)


def _att_kernel(q_ref, k_ref, v_ref, sl_ref, o_ref, lse_ref, kprev_ref, vprev_ref, *, dil, span):
    w = ATT_BLOCK
    e = ATT_HEAD_DIM
    nblk = pl.program_id(2)

    @pl.when(nblk == 0)
    def _():
        kprev_ref[...] = jnp.zeros_like(kprev_ref)
        vprev_ref[...] = jnp.zeros_like(vprev_ref)

    qi = _iota2((w, 2 * w), 0)
    ci = _iota2((w, 2 * w), 1)
    delta = w + qi - ci
    mask = (delta >= 0) & (delta <= span) & ((nblk > 0) | (ci >= w))
    dist = (dil * delta).astype(F32)
    lane = _iota2((w, LANES), 1)
    lse_all = jnp.zeros((w, LANES), F32)
    for h in range(ATT_HEADS_PER_GROUP):
        cols = slice(h * e, (h + 1) * e)
        q_b = q_ref[0, 0, :, cols]
        kk = jnp.concatenate([kprev_ref[:, cols], k_ref[0, 0, :, cols]], axis=0)
        vv = jnp.concatenate([vprev_ref[:, cols], v_ref[0, 0, :, cols]], axis=0)
        s = _dot_nt(q_b, kk) * (e ** -0.5)
        s = jnp.where(mask, s - sl_ref[:, h:h + 1] * dist, -jnp.inf)
        m = jnp.max(s, axis=-1, keepdims=True)
        p = jnp.exp(s - m)
        den = jnp.sum(p, axis=-1, keepdims=True)
        o = _dot((p / den).astype(BF16), vv)
        o_ref[0, 0, :, cols] = o.astype(o_ref.dtype)
        lse_all = jnp.where(lane == h, m + jnp.log(den), lse_all)
    lse_ref[0, 0] = lse_all
    kprev_ref[...] = k_ref[0, 0]
    vprev_ref[...] = v_ref[0, 0]


def dilated_group_core(qkv, gi, window, dil, slopes):
    bsz, _, ls, _ = qkv.shape
    w = ATT_BLOCK
    gw = ATT_HEADS_PER_GROUP * ATT_HEAD_DIM
    spec = lambda part: pl.BlockSpec((1, 1, w, gw), lambda b, r, n: (b, r, n, part))
    return pl.pallas_call(
        functools.partial(_att_kernel, dil=dil, span=window // dil),
        grid=(bsz, dil, ls // w),
        in_specs=[spec(0), spec(1), spec(2),
                  pl.BlockSpec((1, ATT_HEADS_PER_GROUP), lambda b, r, n: (0, 0))],
        out_specs=[pl.BlockSpec((1, 1, w, gw), lambda b, r, n: (b, r, n, 0)),
                   pl.BlockSpec((1, 1, w, LANES), lambda b, r, n: (b, r, n, 0))],
        out_shape=[jax.ShapeDtypeStruct((bsz, dil, ls, gw), BF16),
                   jax.ShapeDtypeStruct((bsz, dil, ls, LANES), F32)],
        scratch_shapes=[pltpu.VMEM((w, gw), BF16), pltpu.VMEM((w, gw), BF16)],
        compiler_params=_cparams("parallel", "parallel", "arbitrary"),
        name="dilated_attention_g%d" % gi,
    )(qkv, qkv, qkv, slopes.reshape(1, ATT_HEADS_PER_GROUP))


def _att_out_kernel(o0_ref, o1_ref, o2_ref, l0_ref, l1_ref, l2_ref, w_ref, x_ref, gate_ref, o_ref,
                    *scratch):
    tm = x_ref.shape[1]

    def token_major(ref, scr):
        dil = ref.shape[1]
        if dil == 1:
            return ref[0, 0].astype(F32)
        for r in range(dil):
            blk = ref[0, r].astype(F32)
            for c in range(scr.shape[0]):
                scr[c, pl.ds(r, tm // dil, stride=dil), :] = blk[:, c * LANES:(c + 1) * LANES]
        return jnp.concatenate([scr[c] for c in range(scr.shape[0])], axis=1)

    o1, l1, o2, l2 = [token_major(ref, scr) for ref, scr in
                      zip((o1_ref, l1_ref, o2_ref, l2_ref), scratch)]
    o0, l0 = token_major(o0_ref, None), token_major(l0_ref, None)
    m = jnp.maximum(jnp.maximum(l0, l1), l2)
    w0, w1, w2 = jnp.exp(l0 - m), jnp.exp(l1 - m), jnp.exp(l2 - m)
    inv = 1.0 / (w0 + w1 + w2)
    k = o0.shape[1]
    expand = jnp.where(_iota2((LANES, k), 1) // ATT_HEAD_DIM == _iota2((LANES, k), 0), 1.0, 0.0)
    y = (_dot_split(w0 * inv, expand) * o0 + _dot_split(w1 * inv, expand) * o1
         + _dot_split(w2 * inv, expand) * o2)
    o_ref[0] = x_ref[0] + gate_ref[0] * _dot(y.astype(BF16), w_ref[...])


def attention_combine_out(outs, lses, w, x, gate, *, tm=512):
    bsz, t, d = x.shape
    k = w.shape[0]
    tm = min(tm, t)
    row = pl.BlockSpec((1, tm, d), lambda b, i: (b, i, 0))
    res = lambda a: pl.BlockSpec((1, a.shape[1], tm // a.shape[1], a.shape[3]),
                                 lambda b, i: (b, 0, i, 0))
    wide, narrow = pltpu.VMEM((k // LANES, tm, LANES), F32), pltpu.VMEM((1, tm, LANES), F32)
    return pl.pallas_call(
        _att_out_kernel,
        grid=(bsz, t // tm),
        in_specs=[res(a) for a in outs] + [res(a) for a in lses] + [
            pl.BlockSpec((k, d), lambda b, i: (0, 0)), row,
            pl.BlockSpec((1, 1, d), lambda b, i: (b, 0, 0))],
        out_specs=row,
        out_shape=jax.ShapeDtypeStruct((bsz, t, d), F32),
        scratch_shapes=[wide, narrow, wide, narrow],
        compiler_params=_cparams("parallel", "parallel"),
        name="attention_combine_out",
    )(*outs, *lses, w, x, gate)


def attention_layer(x, nw, sh, sc, gate, in_w, out_w):
    n_groups = len(ATT_GROUPS)
    gw = ATT_HEADS_PER_GROUP * ATT_HEAD_DIM
    n_heads = n_groups * ATT_HEADS_PER_GROUP
    slopes = jnp.exp2(-8.0 * jnp.arange(1, n_heads + 1, dtype=F32) / n_heads)
    slopes = slopes.reshape(n_groups, ATT_HEADS_PER_GROUP)
    w_in = in_w.astype(BF16).reshape(in_w.shape[0], 3, n_groups, gw)
    outs, lses = [], []
    for gi, (window, dil) in enumerate(ATT_GROUPS):
        w_g = w_in[:, :, gi, :].reshape(in_w.shape[0], 3 * gw)
        qkv = norm_mod_matmul(x, nw, sh, sc, w_g, tm=PROJ_TM, tn=gw, dil=dil)
        o, lse = dilated_group_core(qkv, gi, window, dil, slopes[gi])
        outs.append(o)
        lses.append(lse)
    return attention_combine_out(outs, lses, out_w.astype(BF16), x, gate)


ROUTER_GROUP_LANE0 = 0
ROUTER_EXPERT_LANE0 = MOE_GROUPS
ROUTER_FIELDS = 6


def _router_kernel(x_ref, nw_ref, sh_ref, sc_ref, wr_ref, br_ref, hf_ref, meta_ref, cnt_ref,
                   carry_ref):
    tm = x_ref.shape[1]

    @pl.when(pl.program_id(1) == 0)
    def _():
        carry_ref[...] = jnp.zeros_like(carry_ref)

    hf = _norm_mod(x_ref[0], nw_ref[...], sh_ref[0], sc_ref[0])
    hf_ref[0] = hf
    logits = _dot_3pass(hf, wr_ref[...]) + br_ref[...]
    lane = _iota2((tm, LANES), 1)
    neg = -jnp.inf
    first = lambda hit: jnp.min(jnp.where(hit, lane, LANES), axis=-1, keepdims=True)

    gl = jnp.where(lane < MOE_GROUPS, logits, neg)
    gmax = jnp.max(gl, axis=-1, keepdims=True)
    gidx = first(gl == gmax)
    g_top = 1.0 / jnp.sum(jnp.exp(gl - gmax), axis=-1, keepdims=True)

    lo = ROUTER_EXPERT_LANE0 + gidx * MOE_EXPERTS_PER_GROUP
    el = jnp.where((lane >= lo) & (lane < lo + MOE_EXPERTS_PER_GROUP), logits, neg)
    m1 = jnp.max(el, axis=-1, keepdims=True)
    i1 = first(el == m1)
    el2 = jnp.where(lane == i1, neg, el)
    m2 = jnp.max(el2, axis=-1, keepdims=True)
    i2 = first(el2 == m2)
    ex = jnp.exp(m2 - m1)
    gate1 = g_top / (1.0 + ex)
    gate2 = g_top * ex / (1.0 + ex)

    hit1 = lane == i1
    hit2 = lane == i2
    onehot = jnp.where(hit1 | hit2, 1.0, 0.0).astype(BF16)
    strict = jnp.where(_iota2((tm, tm), 0) > _iota2((tm, tm), 1), 1.0, 0.0).astype(BF16)
    before = _dot(strict, onehot) + carry_ref[0:1, :]
    rank1 = jnp.sum(jnp.where(hit1, before, 0.0), axis=-1, keepdims=True)
    rank2 = jnp.sum(jnp.where(hit2, before, 0.0), axis=-1, keepdims=True)
    carry_ref[...] = carry_ref[...] + _dot(jnp.ones((8, tm), BF16), onehot)
    cnt_ref[0] = carry_ref[...]

    e1 = (i1 - ROUTER_EXPERT_LANE0).astype(F32)
    e2 = (i2 - ROUTER_EXPERT_LANE0).astype(F32)
    meta = jnp.zeros((tm, LANES), F32)
    for j, val in enumerate((e1, e2, rank1, rank2, gate1, gate2)):
        meta = jnp.where(lane == j, val, meta)
    for k in range(tm // LANES):
        fields = meta[k * LANES:(k + 1) * LANES, :].T
        for f in range(ROUTER_FIELDS):
            meta_ref[0, f, k:k + 1, :] = fields[f:f + 1, :]


def moe_router(x, nw, sh, sc, group_w, group_b, expert_w, expert_b, *, tm=8 * LANES):
    bsz, t, d = x.shape
    tm = min(tm, t)
    pad = LANES - MOE_GROUPS - MOE_EXPERTS
    wr = jnp.pad(jnp.concatenate([group_w, expert_w], axis=1), ((0, 0), (0, pad)))
    br = jnp.pad(jnp.concatenate([group_b, expert_b]), (0, pad)).reshape(1, LANES)
    return pl.pallas_call(
        _router_kernel,
        grid=(bsz, t // tm),
        in_specs=[pl.BlockSpec((1, tm, d), lambda b, i: (b, i, 0)),
                  pl.BlockSpec((1, d), lambda b, i: (0, 0)),
                  pl.BlockSpec((1, 1, d), lambda b, i: (b, 0, 0)),
                  pl.BlockSpec((1, 1, d), lambda b, i: (b, 0, 0)),
                  pl.BlockSpec((d, LANES), lambda b, i: (0, 0)),
                  pl.BlockSpec((1, LANES), lambda b, i: (0, 0))],
        out_specs=[pl.BlockSpec((1, tm, d), lambda b, i: (b, i, 0)),
                   pl.BlockSpec((1, ROUTER_FIELDS, tm // LANES, LANES), lambda b, i: (b, 0, i, 0)),
                   pl.BlockSpec((1, 8, LANES), lambda b, i: (b, 0, 0))],
        out_shape=[jax.ShapeDtypeStruct((bsz, t, d), F32),
                   jax.ShapeDtypeStruct((bsz, ROUTER_FIELDS, t // LANES, LANES), F32),
                   jax.ShapeDtypeStruct((bsz, 8, LANES), F32)],
        scratch_shapes=[pltpu.VMEM((8, LANES), F32)],
        compiler_params=_cparams("parallel", "arbitrary"),
        name="moe_router",
    )(x, nw.reshape(1, d), sh, sc, wr, br)


def _expert_kernel(e1_ref, e2_ref, r1_ref, r2_ref, g1_ref, g2_ref, cnt_ref,
                   hf_hbm, x_hbm, gt_ref, w1_ref, w3_ref, w2_ref, out_hbm,
                   hf_v, acc_v, xb_v, yb_v, row_tok, row_gate, pstart, sem):
    b = pl.program_id(0)
    e = pl.program_id(1)
    n_e = pl.num_programs(1)
    t = hf_v.shape[0]
    grp = MOE_ROW_GROUP
    rows = xb_v.shape[0] * grp

    @pl.when(e == 0)
    def _():
        cp_h = pltpu.make_async_copy(hf_hbm.at[b], hf_v, sem.at[0])
        cp_x = pltpu.make_async_copy(x_hbm.at[b], acc_v, sem.at[1])
        cp_h.start()
        cp_x.start()
        xb_v[...] = jnp.zeros_like(xb_v)
        for i in range(grp):
            row_tok[2 * t + i] = 0

        def start_body(i, s):
            pstart[i] = s
            return s + cnt_ref[b * n_e + i]

        lax.fori_loop(0, n_e, start_body, 0)

        def tok_body(g, carry):
            for u in range(4):
                tok = g * 4 + u
                d1 = pstart[e1_ref[tok]] + r1_ref[tok]
                row_tok[d1] = tok
                row_gate[d1] = g1_ref[tok]
                d2 = pstart[e2_ref[tok]] + r2_ref[tok]
                row_tok[d2] = tok
                row_gate[d2] = g2_ref[tok]
            return carry

        lax.fori_loop(0, t // 4, tok_body, 0)
        cp_h.wait()
        cp_x.wait()

    cnt = cnt_ref[b * n_e + e]
    start = pstart[e]

    def block_body(j, carry):
        base = start + j * rows
        nrows = jnp.minimum(rows, cnt - j * rows)
        full_groups = nrows // grp

        def gather(g, c):
            first = base + g * grp
            for u in range(grp):
                xb_v[g, u:u + 1, :] = hf_v[pl.ds(row_tok[first + u], 1), :]
            return c

        lax.fori_loop(0, (nrows + grp - 1) // grp, gather, 0)
        xb = xb_v[...].reshape(rows, xb_v.shape[2]).astype(BF16)
        h1 = _dot(xb, w1_ref[0, 0].astype(BF16))
        h3 = _dot(xb, w3_ref[0, 0].astype(BF16))
        act = (_silu(h1) * h3).astype(BF16)
        yb_v[...] = (_dot(act, w2_ref[0, 0].astype(BF16)) * gt_ref[0]).reshape(yb_v.shape)

        def scatter(g, c):
            first = base + g * grp
            toks = [row_tok[first + u] for u in range(grp)]
            vals = [acc_v[pl.ds(toks[u], 1), :] + row_gate[first + u] * yb_v[g, u:u + 1, :]
                    for u in range(grp)]
            for u in range(grp):
                acc_v[pl.ds(toks[u], 1), :] = vals[u]
            return c

        lax.fori_loop(0, full_groups, scatter, 0)

        def scatter_tail(i, c):
            tok = row_tok[base + i]
            acc_v[pl.ds(tok, 1), :] = (acc_v[pl.ds(tok, 1), :] + row_gate[base + i]
                                       * yb_v[i // grp, pl.ds(i % grp, 1), :])
            return c

        lax.fori_loop(full_groups * grp, nrows, scatter_tail, 0)
        return carry

    lax.fori_loop(0, (cnt + rows - 1) // rows, block_body, 0)

    @pl.when(e == n_e - 1)
    def _():
        cp_o = pltpu.make_async_copy(acc_v, out_hbm.at[b], sem.at[2])
        cp_o.start()
        cp_o.wait()


def moe_experts(hf, x, gt, e1, e2, r1, r2, g1, g2, counts, layer, w1, w3, w2):
    bsz, t, d = x.shape
    _, n_e, _, f = w1.shape
    smem_tok = pl.BlockSpec((t,), lambda b, e: (b,), memory_space=pltpu.SMEM)
    return pl.pallas_call(
        _expert_kernel,
        grid=(bsz, n_e),
        in_specs=[smem_tok] * 6 + [
            pl.BlockSpec(memory_space=pltpu.SMEM),
            pl.BlockSpec(memory_space=pl.ANY),
            pl.BlockSpec(memory_space=pl.ANY),
            pl.BlockSpec((1, 1, d), lambda b, e: (b, 0, 0)),
            pl.BlockSpec((1, 1, d, f), lambda b, e: (layer, e, 0, 0)),
            pl.BlockSpec((1, 1, d, f), lambda b, e: (layer, e, 0, 0)),
            pl.BlockSpec((1, 1, f, d), lambda b, e: (layer, e, 0, 0))],
        out_specs=pl.BlockSpec(memory_space=pl.ANY),
        out_shape=jax.ShapeDtypeStruct((bsz, t, d), F32),
        scratch_shapes=[pltpu.VMEM((t, d), F32), pltpu.VMEM((t, d), F32),
                        pltpu.VMEM((MOE_ROWS // MOE_ROW_GROUP, MOE_ROW_GROUP, d), F32),
                        pltpu.VMEM((MOE_ROWS // MOE_ROW_GROUP, MOE_ROW_GROUP, d), F32),
                        pltpu.SMEM((2 * t + MOE_ROW_GROUP,), jnp.int32),
                        pltpu.SMEM((2 * t + MOE_ROW_GROUP,), F32),
                        pltpu.SMEM((n_e,), jnp.int32), pltpu.SemaphoreType.DMA((3,))],
        compiler_params=_cparams("arbitrary", "arbitrary"),
        name="moe_experts",
    )(e1, e2, r1, r2, g1, g2, counts, hf, x, gt, w1, w3, w2)


def moe_layer(x, nw, sh, sc, gate, group_w, group_b, expert_w, expert_b, layer, w1, w3, w2):
    bsz, t, d = x.shape
    hf, meta, cnt = moe_router(x, nw, sh, sc, group_w, group_b, expert_w, expert_b)
    e1, e2, r1, r2 = [meta[:, f].astype(jnp.int32).reshape(bsz * t) for f in range(4)]
    g1, g2 = [meta[:, f].reshape(bsz * t) for f in (4, 5)]
    counts = cnt[:, 0, ROUTER_EXPERT_LANE0:ROUTER_EXPERT_LANE0 + MOE_EXPERTS].astype(jnp.int32)
    return moe_experts(hf, x, gate, e1, e2, r1, r2, g1, g2, counts.reshape(-1), layer, w1, w3, w2)


def _final_norm_kernel(x_ref, w_ref, o_ref):
    x = x_ref[0]
    o_ref[0] = x * lax.rsqrt(jnp.mean(x * x, axis=-1, keepdims=True) + NORM_EPS) * w_ref[...]


def final_rms_norm(x, w, *, tm=1024):
    bsz, t, d = x.shape
    tm = min(tm, t)
    row = pl.BlockSpec((1, tm, d), lambda b, i: (b, i, 0))
    return pl.pallas_call(
        _final_norm_kernel,
        grid=(bsz, t // tm),
        in_specs=[row, pl.BlockSpec((1, d), lambda b, i: (0, 0))],
        out_specs=row,
        out_shape=jax.ShapeDtypeStruct((bsz, t, d), F32),
        compiler_params=_cparams("parallel", "parallel"),
        name="final_rms_norm",
    )(x, w.reshape(1, d))


def kernel(x, c, ada_w, ada_b, norm_mix_w, norm_ffn_w, ssm_in_w, ssm_conv_w, ssm_conv_b, ssm_dt_bias, ssm_a_log, ssm_d, ssm_norm_w, ssm_out_w, rwkv_mix, rwkv_rkv_w, rwkv_w0, rwkv_w1, rwkv_w2, rwkv_a0, rwkv_a1, rwkv_a2, rwkv_g1, rwkv_g2, rwkv_k_k, rwkv_k_a, rwkv_r_k, rwkv_ln_w, rwkv_ln_b, rwkv_out_w, ret_in_w, ret_out_w, att_in_w, att_out_w, moe_group_w, moe_group_b, moe_expert_w, moe_expert_b, moe_w1, moe_w3, moe_w2, final_norm_w):
    depth = ada_w.shape[0]
    d = x.shape[-1]
    mod = ada_modulation(c, ada_w, ada_b)
    for i in range(depth):
        sh1, sc1, gt1, sh2, sc2, gt2 = [mod[i][:, None, j * d:(j + 1) * d] for j in range(6)]
        kind, j = i % 4, i // 4
        pre = (x, norm_mix_w[i], sh1, sc1, gt1)
        if kind == 0:
            x = mamba2_layer(*pre, ssm_in_w[j], ssm_conv_w[j], ssm_conv_b[j], ssm_dt_bias[j],
                             ssm_a_log[j], ssm_d[j], ssm_norm_w[j], ssm_out_w[j])
        elif kind == 1:
            x = rwkv7_layer(*pre, rwkv_mix[j], rwkv_rkv_w[j], rwkv_w0[j], rwkv_w1[j], rwkv_w2[j],
                            rwkv_a0[j], rwkv_a1[j], rwkv_a2[j], rwkv_g1[j], rwkv_g2[j], rwkv_k_k[j],
                            rwkv_k_a[j], rwkv_r_k[j], rwkv_ln_w[j], rwkv_ln_b[j], rwkv_out_w[j])
        elif kind == 2:
            x = retention_layer(*pre, ret_in_w[j], ret_out_w[j])
        else:
            x = attention_layer(*pre, att_in_w[j], att_out_w[j])
        x = moe_layer(x, norm_ffn_w[i], sh2, sc2, gt2, moe_group_w[i], moe_group_b[i],
                      moe_expert_w[i], moe_expert_b[i], i, moe_w1, moe_w3, moe_w2)
    return final_rms_norm(x, final_norm_w)
```

```python
import functools
import math

import jax
import jax.numpy as jnp
from jax import lax
from jax.experimental import pallas as pl
from jax.experimental.pallas import tpu as pltpu

F32 = jnp.float32
BF16 = jnp.bfloat16
HI = lax.Precision.HIGHEST

NORM_EPS = 1e-6
LANES = 128
VMEM_LIMIT = 56 * 1024 * 1024
PROJ_TM = 2048
PROJ_TN = 1024

SSM_HEAD_DIM = 64
SSM_N_GROUPS = 8
SSM_HEADS_PER_GROUP = 4
SSM_D_STATE = 128
SSM_CONV = 4
SSM_CHUNK = 128
SSM_CONV_COLS = 256
RWKV_HEAD_DIM = 64
RWKV_LN_EPS = 64e-5
RWKV_CHUNK = 128
RWKV_SUB = 16
RWKV_INV_PRECISION = None
RWKV_APPLY_PRECISION = None
RET_N_HEADS = 4
RET_CHUNK = 128
ATT_GROUPS = ((128, 1), (512, 4), (2048, 16))
ATT_HEADS_PER_GROUP = 8
ATT_HEAD_DIM = 128
ATT_BLOCK = 128
MOE_GROUPS = 4
MOE_EXPERTS_PER_GROUP = 8
MOE_EXPERTS = 32
MOE_ROWS = 256
MOE_ROW_GROUP = 8


def _cparams(*sem):
    return pltpu.CompilerParams(dimension_semantics=sem, vmem_limit_bytes=VMEM_LIMIT)


def _sigmoid(x):
    return 1.0 / (1.0 + jnp.exp(-x))


def _silu(x):
    return x * _sigmoid(x)


def _softplus(x):
    return jnp.maximum(x, 0.0) + jnp.log(1.0 + jnp.exp(-jnp.abs(x)))


def _dot(a, b, precision=None):
    return jnp.dot(a, b, preferred_element_type=F32, precision=precision)


def _dot_nt(a, b, precision=None):
    return lax.dot_general(a, b, (((1,), (1,)), ((), ())), preferred_element_type=F32,
                           precision=precision)


def _dot_tn(a, b, precision=None):
    return lax.dot_general(a, b, (((0,), (0,)), ((), ())), preferred_element_type=F32,
                           precision=precision)


def _split_bf16(a):
    hi = a.astype(BF16)
    return hi, (a - hi.astype(F32)).astype(BF16)


def _dot_split(a, m, transpose_rhs=False, lhs_exact=False):
    hi, lo = _split_bf16(a)
    mb = m.astype(BF16)
    if lhs_exact:
        return _dot(mb, hi) + _dot(mb, lo)
    dot = _dot_nt if transpose_rhs else _dot
    return dot(hi, mb) + dot(lo, mb)


def _dot_3pass(a, b):
    a_hi, a_lo = _split_bf16(a)
    b_hi, b_lo = _split_bf16(b)
    return _dot(a_hi, b_hi) + (_dot(a_lo, b_hi) + _dot(a_hi, b_lo))


def _iota2(shape, axis):
    return lax.broadcasted_iota(jnp.int32, shape, axis)


def _ada_kernel(c_ref, w_ref, b_ref, o_ref):
    cs = _silu(c_ref[...])
    o_ref[0] = _dot(cs, w_ref[0], HI) + b_ref[0]


def ada_modulation(c, ada_w, ada_b):
    depth, d, n = ada_w.shape
    bsz = c.shape[0]
    rows = -(-bsz // 8) * 8
    cp = jnp.pad(c, ((0, rows - bsz), (0, 0)))
    tn = 1536
    out = pl.pallas_call(
        _ada_kernel,
        grid=(depth, n // tn),
        in_specs=[pl.BlockSpec((rows, d), lambda l, j: (0, 0)),
                  pl.BlockSpec((1, d, tn), lambda l, j: (l, 0, j)),
                  pl.BlockSpec((1, 1, tn), lambda l, j: (l, 0, j))],
        out_specs=pl.BlockSpec((1, rows, tn), lambda l, j: (l, 0, j)),
        out_shape=jax.ShapeDtypeStruct((depth, rows, n), F32),
        compiler_params=_cparams("parallel", "parallel"),
        name="ada_modulation",
    )(cp, ada_w, ada_b.reshape(depth, 1, n))
    return out[:, :bsz]


def _norm_mod(x, nw, sh, sc):
    y = x * lax.rsqrt(jnp.mean(x * x, axis=-1, keepdims=True) + NORM_EPS) * nw
    return y * (1.0 + sc) + sh


def _nmm_kernel(*refs, dil, has_side):
    if has_side:
        x_ref, nw_ref, sh_ref, sc_ref, w_ref, ws_ref, o_ref, side_ref, h_ref, *rest = refs
    else:
        x_ref, nw_ref, sh_ref, sc_ref, w_ref, o_ref, h_ref, *rest = refs

    @pl.when(pl.program_id(2) == 0)
    def _():
        h = _norm_mod(x_ref[0], nw_ref[...], sh_ref[0], sc_ref[0]).astype(h_ref.dtype)
        h_ref[...] = h
        if has_side:
            w_hi, w_lo = _split_bf16(ws_ref[...])
            side_ref[0] = _dot(h, w_hi) + _dot(h, w_lo)

    res = _dot(h_ref[...], w_ref[...])
    if dil is None:
        o_ref[0] = res.astype(o_ref.dtype)
    else:
        acc_ref, = rest
        rows = acc_ref.shape[1] // dil
        for c in range(acc_ref.shape[0]):
            acc_ref[c] = res[:, c * LANES:(c + 1) * LANES]
        for r in range(dil):
            for c in range(acc_ref.shape[0]):
                o_ref[0, r, :, c * LANES:(c + 1) * LANES] = (
                    acc_ref[c, pl.ds(r, rows, stride=dil), :].astype(o_ref.dtype))


def norm_mod_matmul(x, nw, sh, sc, w, *, tm, tn, dil=None, side_w=None):
    bsz, t, d = x.shape
    n = w.shape[1]
    tm = min(tm, t)
    scratch = [pltpu.VMEM((tm, d), BF16)]
    in_specs = [pl.BlockSpec((1, tm, d), lambda b, i, j: (b, i, 0)),
                pl.BlockSpec((1, d), lambda b, i, j: (0, 0)),
                pl.BlockSpec((1, 1, d), lambda b, i, j: (b, 0, 0)),
                pl.BlockSpec((1, 1, d), lambda b, i, j: (b, 0, 0)),
                pl.BlockSpec((d, tn), lambda b, i, j: (0, j))]
    ins = [x, nw.reshape(1, d), sh, sc, w]
    if dil is None:
        out_specs = [pl.BlockSpec((1, tm, tn), lambda b, i, j: (b, i, j))]
        out_shape = [jax.ShapeDtypeStruct((bsz, t, n), BF16)]
    else:
        out_specs = [pl.BlockSpec((1, dil, tm // dil, tn), lambda b, i, j: (b, 0, i, j))]
        out_shape = [jax.ShapeDtypeStruct((bsz, dil, t // dil, n), BF16)]
        scratch.append(pltpu.VMEM((tn // LANES, tm, LANES), F32))
    if side_w is not None:
        in_specs.append(pl.BlockSpec((d, LANES), lambda b, i, j: (0, 0)))
        ins.append(side_w)
        out_specs.append(pl.BlockSpec((1, tm, LANES), lambda b, i, j: (b, i, 0)))
        out_shape.append(jax.ShapeDtypeStruct((bsz, t, LANES), F32))
    outs = pl.pallas_call(
        functools.partial(_nmm_kernel, dil=dil, has_side=side_w is not None),
        grid=(bsz, t // tm, n // tn),
        in_specs=in_specs,
        out_specs=out_specs,
        out_shape=out_shape,
        scratch_shapes=scratch,
        compiler_params=_cparams("parallel", "parallel", "arbitrary"),
        name="norm_mod_matmul",
    )(*ins)
    return outs if side_w is not None else outs[0]


def _mgr_kernel(*refs, has_mul):
    if has_mul:
        y_ref, g_ref, w_ref, x_ref, gate_ref, o_ref = refs
        y = (y_ref[0].astype(F32) * g_ref[0].astype(F32)).astype(BF16)
    else:
        y_ref, w_ref, x_ref, gate_ref, o_ref = refs
        y = y_ref[0].astype(BF16)
    o_ref[0] = x_ref[0] + gate_ref[0] * _dot(y, w_ref[...])


def matmul_gated_residual(y, w, x, gate, mul=None, *, tm=512):
    bsz, t, k = y.shape
    d = w.shape[1]
    tm = min(tm, t)
    ins = [y] + ([mul] if mul is not None else []) + [w, x, gate]
    row = lambda width: pl.BlockSpec((1, tm, width), lambda b, i: (b, i, 0))
    specs = [row(k)] + ([row(k)] if mul is not None else []) + [
        pl.BlockSpec((k, d), lambda b, i: (0, 0)), row(d),
        pl.BlockSpec((1, 1, d), lambda b, i: (b, 0, 0))]
    return pl.pallas_call(
        functools.partial(_mgr_kernel, has_mul=mul is not None),
        grid=(bsz, t // tm),
        in_specs=specs,
        out_specs=row(d),
        out_shape=jax.ShapeDtypeStruct((bsz, t, d), F32),
        compiler_params=_cparams("parallel", "parallel"),
        name="matmul_gated_residual",
    )(*ins)


def _ssd_kernel(zx_ref, dt_ref, cw_ref, cb_ref, dtb_ref, alog_ref, dsk_ref, nw_ref, o_ref,
                prev_ref, xbc_ref, state_ref):
    q = SSM_CHUNK
    g_n, hg, p, n = SSM_N_GROUPS, SSM_HEADS_PER_GROUP, SSM_HEAD_DIM, SSM_D_STATE
    d_inner = g_n * hg * p
    nh = g_n * hg
    gw = hg * p

    @pl.when(pl.program_id(1) == 0)
    def _():
        prev_ref[...] = jnp.zeros_like(prev_ref)
        state_ref[...] = jnp.zeros_like(state_ref)

    src = _iota2((q, 2 * q), 1) - q - _iota2((q, 2 * q), 0)
    shifts = [jnp.where(src == -s, 1.0, 0.0).astype(BF16) for s in range(1, SSM_CONV)]
    cdim = xbc_ref.shape[1]
    for c0 in range(0, cdim, SSM_CONV_COLS):
        cs = slice(c0, c0 + SSM_CONV_COLS)
        cur = zx_ref[0, :, d_inner + c0:d_inner + c0 + SSM_CONV_COLS]
        ext = jnp.concatenate([prev_ref[:, cs], cur], axis=0)
        prev_ref[:, cs] = cur
        acc = cb_ref[:, cs] + cur.astype(F32) * cw_ref[SSM_CONV - 1:SSM_CONV, cs]
        for s in range(1, SSM_CONV):
            acc = acc + _dot(shifts[s - 1], ext) * cw_ref[SSM_CONV - 1 - s:SSM_CONV - s, cs]
        xbc_ref[:, cs] = _silu(acc)

    dt = _softplus(dt_ref[0][:, :nh] + dtb_ref[...])
    da = dt * -jnp.exp(alog_ref[...])
    tril = (_iota2((q, q), 0) >= _iota2((q, q), 1))
    acum = _dot_split(da, tril.astype(F32), lhs_exact=True)
    a_end = acum[q - 1:q, :]
    chunk_decay = jnp.exp(a_end)
    adj_t = (acum - jnp.log(dt)).T
    wdec_t = (dt * jnp.exp(a_end - acum)).T

    for g in range(g_n):
        xs_g = xbc_ref[:, g * gw:(g + 1) * gw]
        bm = xbc_ref[:, d_inner + g * n:d_inner + (g + 1) * n]
        cm = xbc_ref[:, d_inner + g_n * n + g * n:d_inner + g_n * n + (g + 1) * n]
        cm_b = cm.astype(BF16)
        cb = _dot_nt(cm_b, bm.astype(BF16))
        bm_t = bm.T
        xs_b = xs_g.astype(BF16)
        st_old = state_ref[g]
        y_carry = _dot(cm_b, st_old.astype(BF16))
        ys, st_cols = [], []
        for hh in range(hg):
            h = g * hg + hh
            hs = slice(hh * p, (hh + 1) * p)
            a_col = jnp.broadcast_to(acum[:, h:h + 1], (q, q))
            m_h = cb * jnp.exp(jnp.where(tril, a_col - adj_t[h:h + 1, :], -jnp.inf))
            ys.append(_dot(m_h.astype(BF16), xs_b[:, hs]) + jnp.exp(a_col[:, :p]) * y_carry[:, hs])
            st_new = _dot((bm_t * wdec_t[h:h + 1, :]).astype(BF16), xs_b[:, hs])
            st_cols.append(st_old[:, hs] * chunk_decay[:, h:h + 1] + st_new)
        state_ref[g] = jnp.concatenate(st_cols, axis=1)
        y = jnp.concatenate(ys, axis=1) + xs_g * dsk_ref[:, g * gw:(g + 1) * gw]
        z = zx_ref[0, :, g * gw:(g + 1) * gw].astype(F32)
        y = y * _silu(z)
        y = y * lax.rsqrt(jnp.mean(y * y, axis=-1, keepdims=True) + NORM_EPS)
        o_ref[0, :, g * gw:(g + 1) * gw] = (y * nw_ref[:, g * gw:(g + 1) * gw]).astype(o_ref.dtype)


def ssd_core(zx, dt_raw, conv_w, conv_b, dt_bias, a_log, d_skip, norm_w):
    bsz, t, width = zx.shape
    d_inner = SSM_N_GROUPS * SSM_HEADS_PER_GROUP * SSM_HEAD_DIM
    cdim = width - d_inner
    nh = dt_bias.shape[0]
    q = SSM_CHUNK
    full = lambda shape: pl.BlockSpec(shape, lambda b, c: (0,) * len(shape))
    return pl.pallas_call(
        _ssd_kernel,
        grid=(bsz, t // q),
        in_specs=[pl.BlockSpec((1, q, width), lambda b, c: (b, c, 0)),
                  pl.BlockSpec((1, q, LANES), lambda b, c: (b, c, 0)),
                  full((SSM_CONV, cdim)), full((1, cdim)), full((1, nh)), full((1, nh)),
                  full((1, d_inner)), full((1, d_inner))],
        out_specs=pl.BlockSpec((1, q, d_inner), lambda b, c: (b, c, 0)),
        out_shape=jax.ShapeDtypeStruct((bsz, t, d_inner), BF16),
        scratch_shapes=[pltpu.VMEM((q, cdim), BF16), pltpu.VMEM((q, cdim), F32),
                        pltpu.VMEM((SSM_N_GROUPS, SSM_D_STATE,
                                    SSM_HEADS_PER_GROUP * SSM_HEAD_DIM), F32)],
        compiler_params=_cparams("parallel", "arbitrary"),
        name="ssd_core",
    )(zx, dt_raw, conv_w, conv_b.reshape(1, cdim), dt_bias.reshape(1, nh),
      a_log.reshape(1, nh), jnp.repeat(d_skip, SSM_HEAD_DIM).reshape(1, d_inner),
      norm_w.reshape(1, d_inner))


def mamba2_layer(x, nw, sh, sc, gate, in_w, conv_w, conv_b, dt_bias, a_log, d_skip, norm_w, out_w):
    d_inner = SSM_N_GROUPS * SSM_HEADS_PER_GROUP * SSM_HEAD_DIM
    cdim = conv_w.shape[1]
    nh = dt_bias.shape[0]
    w_main = in_w[:, :d_inner + cdim].astype(BF16)
    w_dt = jnp.pad(in_w[:, d_inner + cdim:], ((0, 0), (0, LANES - nh)))
    zx, dt_raw = norm_mod_matmul(x, nw, sh, sc, w_main, tm=PROJ_TM, tn=PROJ_TN, side_w=w_dt)
    y = ssd_core(zx, dt_raw, conv_w, conv_b, dt_bias, a_log, d_skip, norm_w)
    return matmul_gated_residual(y, out_w.astype(BF16), x, gate)


def _head_indicator(d, n):
    return jnp.where(_iota2((d, LANES), 0) // n == _iota2((d, LANES), 1), 1.0, 0.0)


def _rwkv_prep_kernel(x_ref, xp_ref, nw_ref, sh_ref, sc_ref, mix_ref, wr_ref, wk_ref, wv_ref,
                      w0_ref, w1_ref, w2_ref, a0_ref, a1_ref, a2_ref, g1_ref, g2_ref, kk_ref, ka_ref,
                      r_o, lw_o, k_o, v_o, kn_o, b_o, g_o):
    tm, d = x_ref.shape[1], x_ref.shape[2]
    nw, sh, sc = nw_ref[...], sh_ref[0], sc_ref[0]
    hm = _norm_mod(x_ref[0], nw, sh, sc)
    prev = _norm_mod(xp_ref[0], nw, sh, sc)[7:8, :]
    prev = jnp.where(pl.program_id(1) > 0, prev, 0.0)
    shifted = jnp.where(_iota2((tm, d), 0) == 0, prev, pltpu.roll(hm, 1, axis=0))
    xx = shifted - hm
    mixed = lambda j: (hm + xx * mix_ref[j:j + 1, :]).astype(BF16)
    xr, xw, xk, xv, xa, xg = [mixed(j) for j in range(6)]
    r = _dot(xr, wr_ref[...])
    k = _dot(xk, wk_ref[...])
    v = _dot(xv, wv_ref[...])
    lora = lambda u, w: _dot(u.astype(BF16), w[...])
    w_log = -_softplus(-(w0_ref[...] + lora(jnp.tanh(lora(xw, w1_ref)), w2_ref))) - 0.5
    a = _sigmoid(a0_ref[...] + lora(lora(xa, a1_ref), a2_ref))
    g = lora(_sigmoid(lora(xg, g1_ref)), g2_ref)
    kk = k * kk_ref[...]
    ind = _head_indicator(d, RWKV_HEAD_DIM)
    nrm = jnp.maximum(jnp.sqrt(_dot_split(kk * kk, ind)), 1e-12)
    kn = kk * _dot_split(1.0 / nrm, ind, transpose_rhs=True)
    r_o[0] = r.astype(r_o.dtype)
    lw_o[0] = -jnp.exp(w_log)
    k_o[0] = (k * (1.0 + (a - 1.0) * ka_ref[...])).astype(k_o.dtype)
    v_o[0] = v.astype(v_o.dtype)
    kn_o[0] = kn.astype(kn_o.dtype)
    b_o[0] = (kn * a).astype(b_o.dtype)
    g_o[0] = g.astype(g_o.dtype)


def rwkv_prep(x, nw, sh, sc, mix, rkv_w, w0, w1, w2, a0, a1, a2, g1, g2, k_k, k_a, *, tm=512):
    bsz, t, d = x.shape
    tm = min(tm, t)
    full = lambda arr: pl.BlockSpec(arr.shape, lambda b, i: (0,) * arr.ndim)
    row = pl.BlockSpec((1, tm, d), lambda b, i: (b, i, 0))
    vec = pl.BlockSpec((1, 1, d), lambda b, i: (b, 0, 0))
    bf = lambda w: w.astype(BF16)
    params = [mix, bf(rkv_w[0]), bf(rkv_w[1]), bf(rkv_w[2]), w0.reshape(1, d), bf(w1), bf(w2),
              a0.reshape(1, d), bf(a1), bf(a2), bf(g1), bf(g2), k_k.reshape(1, d), k_a.reshape(1, d)]
    prev_spec = pl.BlockSpec((1, 8, d), lambda b, i: (b, jnp.maximum(i * (tm // 8) - 1, 0), 0))
    return pl.pallas_call(
        _rwkv_prep_kernel,
        grid=(bsz, t // tm),
        in_specs=[row, prev_spec, pl.BlockSpec((1, d), lambda b, i: (0, 0)), vec, vec]
        + [full(p) for p in params],
        out_specs=[row] * 7,
        out_shape=[jax.ShapeDtypeStruct((bsz, t, d), F32 if i == 1 else BF16) for i in range(7)],
        compiler_params=_cparams("parallel", "parallel"),
        name="rwkv_prep",
    )(x, x, nw.reshape(1, d), sh, sc, *params)


def _mm(a, b, precision):
    if precision is None:
        return _dot(a.astype(BF16), b.astype(BF16))
    return _dot(a, b, precision)


def _mm_nt(a, b, precision):
    if precision is None:
        return _dot_nt(a.astype(BF16), b.astype(BF16))
    return _dot_nt(a, b, precision)


def _mm_tn(a, b, precision):
    if precision is None:
        return _dot_tn(a.astype(BF16), b.astype(BF16))
    return _dot_tn(a, b, precision)


def _unit_lower_inverse(m_list, length, sub, precision):
    ri = _iota2((length, length), 0)
    ci = _iota2((length, length), 1)
    eye = jnp.where(ri == ci, 1.0, 0.0)
    same_block = ri // sub == ci // sub
    mm = lambda a, b: _mm(a, b, precision)
    md = [jnp.where(same_block, m, 0.0) for m in m_list]
    off = [m - d for m, d in zip(m_list, md)]
    dinv = [eye + d for d in md]
    pw = md
    for _ in range(int(math.log2(sub)) - 1):
        pw = [mm(p, p) for p in pw]
        dinv = [x + mm(x, p) for x, p in zip(dinv, pw)]
    pm = [mm(x, o) for x, o in zip(dinv, off)]
    acc = [eye + p for p in pm]
    pw = pm
    for _ in range(int(math.log2(length // sub)) - 1):
        pw = [mm(p, p) for p in pw]
        acc = [x + mm(x, p) for x, p in zip(acc, pw)]
    return [mm(x, d) for x, d in zip(acc, dinv)]


def _rwkv_core_kernel(r_ref, lw_ref, k_ref, v_ref, kn_ref, b_ref, rk_ref, lnw_ref, lnb_ref, o_ref,
                      state_ref, *, inv_precision, apply_precision):
    length, d = r_ref.shape[1], r_ref.shape[2]
    n = RWKV_HEAD_DIM
    heads = range(d // n)
    mm = lambda a, b: _mm(a, b, apply_precision)

    @pl.when(pl.program_id(1) == 0)
    def _():
        state_ref[...] = jnp.zeros_like(state_ref)

    ri = _iota2((2 * length, 2 * length), 0)
    ci = _iota2((2 * length, 2 * length), 1) % length
    keep = ((ri < length) & (ri > ci)) | ((ri >= length) & (ri - length >= ci))
    eye_n = _iota2((n, n), 0) == _iota2((n, n), 1)

    lw = lw_ref[0]
    r, k, v, kn, b = [ref[0].astype(F32) for ref in (r_ref, k_ref, v_ref, kn_ref, b_ref)]
    tri = jnp.where(_iota2((length, length), 0) >= _iota2((length, length), 1), 1.0, 0.0)
    cum = _dot_split(lw, tri, lhs_exact=True)
    c_end = cum[length - 1:length, :]
    e_neg = jnp.exp(-cum)
    e_rem = jnp.exp(c_end - cum)
    ar = jnp.concatenate([-kn * jnp.exp(cum - lw), r * jnp.exp(cum)], axis=0)
    bk = jnp.concatenate([b * e_neg, k * e_neg], axis=0)
    bk_end = jnp.concatenate([b * e_rem, k * e_rem], axis=0)
    w_end = jnp.exp(c_end)
    bonus_w = r * k * rk_ref[...]
    hs = [slice(h * n, (h + 1) * n) for h in heads]
    zero = jnp.zeros((length, n), F32)

    s0 = [state_ref[h] for h in heads]
    p = [jnp.where(keep, _mm_nt(ar[:, c], bk[:, c], inv_precision), 0.0) for c in hs]
    t_inv = _unit_lower_inverse([x[:length, :length] for x in p], length, RWKV_SUB, inv_precision)
    q = [mm(ar[:, c], s) for c, s in zip(hs, s0)]
    y = [x[:length] + mm(pp[:length], jnp.concatenate([zero, v[:, c]], axis=0))
         for x, pp, c in zip(q, p, hs)]
    u = [mm(ti, yy) for ti, yy in zip(t_inv, y)]
    uv = [jnp.concatenate([uu, v[:, c]], axis=0) for uu, c in zip(u, hs)]
    o = [x[length:] + mm(pp[length:], w) for x, pp, w in zip(q, p, uv)]
    for h in heads:
        c = hs[h]
        w_col = jnp.sum(jnp.where(eye_n, w_end[:, c], 0.0), axis=1, keepdims=True)
        state_ref[h] = w_col * s0[h] + _mm_tn(bk_end[:, c], uv[h], apply_precision)
        mu = jnp.mean(o[h], axis=-1, keepdims=True)
        var = jnp.mean((o[h] - mu) ** 2, axis=-1, keepdims=True)
        yh = (o[h] - mu) * lax.rsqrt(var + RWKV_LN_EPS) * lnw_ref[:, c] + lnb_ref[:, c]
        yh = yh + jnp.sum(bonus_w[:, c], axis=-1, keepdims=True) * v[:, c]
        o_ref[0, :, c] = yh.astype(o_ref.dtype)


def rwkv_core(r, lw, k, v, kn, b, r_k, ln_w, ln_b):
    bsz, t, d = r.shape
    length = RWKV_CHUNK
    n = RWKV_HEAD_DIM
    row = pl.BlockSpec((1, length, d), lambda bb, c: (bb, c, 0))
    vec = pl.BlockSpec((1, d), lambda bb, c: (0, 0))
    return pl.pallas_call(
        functools.partial(_rwkv_core_kernel, inv_precision=RWKV_INV_PRECISION,
                          apply_precision=RWKV_APPLY_PRECISION),
        grid=(bsz, t // length),
        in_specs=[row] * 6 + [vec] * 3,
        out_specs=row,
        out_shape=jax.ShapeDtypeStruct((bsz, t, d), BF16),
        scratch_shapes=[pltpu.VMEM((d // n, n, n), F32)],
        compiler_params=_cparams("parallel", "arbitrary"),
        name="rwkv_core",
    )(r, lw, k, v, kn, b, r_k.reshape(1, d), ln_w.reshape(1, d), ln_b.reshape(1, d))


def rwkv7_layer(x, nw, sh, sc, gate, mix, rkv_w, w0, w1, w2, a0, a1, a2, g1, g2, k_k, k_a, r_k,
                ln_w, ln_b, out_w):
    r, lw, k, v, kn, b, g = rwkv_prep(x, nw, sh, sc, mix, rkv_w, w0, w1, w2, a0, a1, a2, g1, g2,
                                      k_k, k_a)
    y = rwkv_core(r, lw, k, v, kn, b, r_k, ln_w, ln_b)
    return matmul_gated_residual(y, out_w.astype(BF16), x, gate, mul=g)


def _ret_kernel(qkvg_ref, idec_ref, qdec_ref, kdec_ref, cdec_ref, o_ref, state_ref, *, d_model):
    nh = RET_N_HEADS
    dk = d_model // nh
    dv = 2 * d_model // nh
    k_scale = dk ** -0.5

    @pl.when(pl.program_id(1) == 0)
    def _():
        state_ref[...] = jnp.zeros_like(state_ref)

    for h in range(nh):
        q_b = qkvg_ref[0, :, h * dk:(h + 1) * dk]
        k_f = qkvg_ref[0, :, d_model + h * dk:d_model + (h + 1) * dk].astype(F32) * k_scale
        v_b = qkvg_ref[0, :, 2 * d_model + h * dv:2 * d_model + (h + 1) * dv]
        g_f = qkvg_ref[0, :, 4 * d_model + h * dv:4 * d_model + (h + 1) * dv].astype(F32)
        scores = _dot_nt(q_b, k_f.astype(BF16)) * idec_ref[h]
        inner = _dot(scores.astype(BF16), v_b)
        r_old = state_ref[h]
        qd = (q_b.astype(F32) * qdec_ref[:, h:h + 1]).astype(BF16)
        cross = _dot(qd, r_old.astype(BF16))
        kd = (k_f * kdec_ref[:, h:h + 1]).astype(BF16)
        state_ref[h] = r_old * cdec_ref[:, h:h + 1] + _dot_tn(kd, v_b)
        o = inner + cross
        o = o * lax.rsqrt(jnp.mean(o * o, axis=-1, keepdims=True) + NORM_EPS)
        o_ref[0, :, h * dv:(h + 1) * dv] = (_silu(g_f) * o).astype(o_ref.dtype)


def retention_core(qkvg, d_model):
    bsz, t, width = qkvg.shape
    nh, q = RET_N_HEADS, RET_CHUNK
    dk, dv = d_model // nh, 2 * d_model // nh
    log_gamma = jnp.log(1 - jnp.exp2(-5.0 - jnp.arange(nh, dtype=F32)))
    idx = jnp.arange(q)
    rel = idx[:, None] - idx[None, :]
    inner_decay = jnp.where(rel >= 0,
                            jnp.exp(jnp.maximum(rel, 0).astype(F32) * log_gamma[:, None, None]), 0.0)
    q_decay = jnp.exp((idx + 1).astype(F32)[:, None] * log_gamma)
    k_decay = jnp.exp((q - 1 - idx).astype(F32)[:, None] * log_gamma)
    chunk_decay = jnp.exp(q * log_gamma).reshape(1, nh)
    full = lambda shape: pl.BlockSpec(shape, lambda b, c: (0,) * len(shape))
    return pl.pallas_call(
        functools.partial(_ret_kernel, d_model=d_model),
        grid=(bsz, t // q),
        in_specs=[pl.BlockSpec((1, q, width), lambda b, c: (b, c, 0)),
                  full((nh, q, q)), full((q, nh)), full((q, nh)), full((1, nh))],
        out_specs=pl.BlockSpec((1, q, nh * dv), lambda b, c: (b, c, 0)),
        out_shape=jax.ShapeDtypeStruct((bsz, t, nh * dv), BF16),
        scratch_shapes=[pltpu.VMEM((nh, dk, dv), F32)],
        compiler_params=_cparams("parallel", "arbitrary"),
        name="retention_core",
    )(qkvg, inner_decay, q_decay, k_decay, chunk_decay)


def retention_layer(x, nw, sh, sc, gate, in_w, out_w):
    d = x.shape[-1]
    qkvg = norm_mod_matmul(x, nw, sh, sc, in_w.astype(BF16), tm=PROJ_TM, tn=PROJ_TN)
    y = retention_core(qkvg, d)
    return matmul_gated_residual(y, out_w.astype(BF16), x, gate)


def _att_kernel(q_ref, k_ref, v_ref, sl_ref, o_ref, lse_ref, kprev_ref, vprev_ref, *, dil, span):
    w = ATT_BLOCK
    e = ATT_HEAD_DIM
    nblk = pl.program_id(2)

    @pl.when(nblk == 0)
    def _():
        kprev_ref[...] = jnp.zeros_like(kprev_ref)
        vprev_ref[...] = jnp.zeros_like(vprev_ref)

    qi = _iota2((w, 2 * w), 0)
    ci = _iota2((w, 2 * w), 1)
    delta = w + qi - ci
    mask = (delta >= 0) & (delta <= span) & ((nblk > 0) | (ci >= w))
    dist = (dil * delta).astype(F32)
    lane = _iota2((w, LANES), 1)
    heads = range(ATT_HEADS_PER_GROUP)
    cols = [slice(h * e, (h + 1) * e) for h in heads]
    s = [_dot_nt(q_ref[0, 0, :, c], jnp.concatenate([kprev_ref[:, c], k_ref[0, 0, :, c]], axis=0))
         for c in cols]
    s = [jnp.where(mask, x * (e ** -0.5) - sl_ref[:, h:h + 1] * dist, -jnp.inf)
         for h, x in zip(heads, s)]
    m = [jnp.max(x, axis=-1, keepdims=True) for x in s]
    p = [jnp.exp(x - mx) for x, mx in zip(s, m)]
    den = [jnp.sum(x, axis=-1, keepdims=True) for x in p]
    lse_all = jnp.zeros((w, LANES), F32)
    for h in heads:
        vv = jnp.concatenate([vprev_ref[:, cols[h]], v_ref[0, 0, :, cols[h]]], axis=0)
        o = _dot((p[h] / den[h]).astype(BF16), vv)
        o_ref[0, 0, :, cols[h]] = o.astype(o_ref.dtype)
        lse_all = jnp.where(lane == h, m[h] + jnp.log(den[h]), lse_all)
    lse_ref[0, 0] = lse_all
    kprev_ref[...] = k_ref[0, 0]
    vprev_ref[...] = v_ref[0, 0]


def dilated_group_core(qkv, gi, window, dil, slopes):
    bsz, _, ls, _ = qkv.shape
    w = ATT_BLOCK
    gw = ATT_HEADS_PER_GROUP * ATT_HEAD_DIM
    spec = lambda part: pl.BlockSpec((1, 1, w, gw), lambda b, r, n: (b, r, n, part))
    return pl.pallas_call(
        functools.partial(_att_kernel, dil=dil, span=window // dil),
        grid=(bsz, dil, ls // w),
        in_specs=[spec(0), spec(1), spec(2),
                  pl.BlockSpec((1, ATT_HEADS_PER_GROUP), lambda b, r, n: (0, 0))],
        out_specs=[pl.BlockSpec((1, 1, w, gw), lambda b, r, n: (b, r, n, 0)),
                   pl.BlockSpec((1, 1, w, LANES), lambda b, r, n: (b, r, n, 0))],
        out_shape=[jax.ShapeDtypeStruct((bsz, dil, ls, gw), BF16),
                   jax.ShapeDtypeStruct((bsz, dil, ls, LANES), F32)],
        scratch_shapes=[pltpu.VMEM((w, gw), BF16), pltpu.VMEM((w, gw), BF16)],
        compiler_params=_cparams("parallel", "parallel", "arbitrary"),
        name="dilated_attention_g%d" % gi,
    )(qkv, qkv, qkv, slopes.reshape(1, ATT_HEADS_PER_GROUP))


def _att_out_kernel(o0_ref, o1_ref, o2_ref, l0_ref, l1_ref, l2_ref, w_ref, x_ref, gate_ref, o_ref,
                    *scratch):
    tm = x_ref.shape[1]

    def token_major(ref, scr):
        dil = ref.shape[1]
        if dil == 1:
            return ref[0, 0].astype(F32)
        for r in range(dil):
            blk = ref[0, r].astype(F32)
            for c in range(scr.shape[0]):
                scr[c, pl.ds(r, tm // dil, stride=dil), :] = blk[:, c * LANES:(c + 1) * LANES]
        return jnp.concatenate([scr[c] for c in range(scr.shape[0])], axis=1)

    o1, l1, o2, l2 = [token_major(ref, scr) for ref, scr in
                      zip((o1_ref, l1_ref, o2_ref, l2_ref), scratch)]
    o0, l0 = token_major(o0_ref, None), token_major(l0_ref, None)
    m = jnp.maximum(jnp.maximum(l0, l1), l2)
    w0, w1, w2 = jnp.exp(l0 - m), jnp.exp(l1 - m), jnp.exp(l2 - m)
    inv = 1.0 / (w0 + w1 + w2)
    k = o0.shape[1]
    expand = jnp.where(_iota2((LANES, k), 1) // ATT_HEAD_DIM == _iota2((LANES, k), 0), 1.0, 0.0)
    y = (_dot_split(w0 * inv, expand) * o0 + _dot_split(w1 * inv, expand) * o1
         + _dot_split(w2 * inv, expand) * o2)
    o_ref[0] = x_ref[0] + gate_ref[0] * _dot(y.astype(BF16), w_ref[...])


def attention_combine_out(outs, lses, w, x, gate, *, tm=512):
    bsz, t, d = x.shape
    k = w.shape[0]
    tm = min(tm, t)
    row = pl.BlockSpec((1, tm, d), lambda b, i: (b, i, 0))
    res = lambda a: pl.BlockSpec((1, a.shape[1], tm // a.shape[1], a.shape[3]),
                                 lambda b, i: (b, 0, i, 0))
    wide, narrow = pltpu.VMEM((k // LANES, tm, LANES), F32), pltpu.VMEM((1, tm, LANES), F32)
    return pl.pallas_call(
        _att_out_kernel,
        grid=(bsz, t // tm),
        in_specs=[res(a) for a in outs] + [res(a) for a in lses] + [
            pl.BlockSpec((k, d), lambda b, i: (0, 0)), row,
            pl.BlockSpec((1, 1, d), lambda b, i: (b, 0, 0))],
        out_specs=row,
        out_shape=jax.ShapeDtypeStruct((bsz, t, d), F32),
        scratch_shapes=[wide, narrow, wide, narrow],
        compiler_params=_cparams("parallel", "parallel"),
        name="attention_combine_out",
    )(*outs, *lses, w, x, gate)


def attention_layer(x, nw, sh, sc, gate, in_w, out_w):
    n_groups = len(ATT_GROUPS)
    gw = ATT_HEADS_PER_GROUP * ATT_HEAD_DIM
    n_heads = n_groups * ATT_HEADS_PER_GROUP
    slopes = jnp.exp2(-8.0 * jnp.arange(1, n_heads + 1, dtype=F32) / n_heads)
    slopes = slopes.reshape(n_groups, ATT_HEADS_PER_GROUP)
    w_in = in_w.astype(BF16).reshape(in_w.shape[0], 3, n_groups, gw)
    outs, lses = [], []
    for gi, (window, dil) in enumerate(ATT_GROUPS):
        w_g = w_in[:, :, gi, :].reshape(in_w.shape[0], 3 * gw)
        qkv = norm_mod_matmul(x, nw, sh, sc, w_g, tm=PROJ_TM, tn=gw, dil=dil)
        o, lse = dilated_group_core(qkv, gi, window, dil, slopes[gi])
        outs.append(o)
        lses.append(lse)
    return attention_combine_out(outs, lses, out_w.astype(BF16), x, gate)


ROUTER_GROUP_LANE0 = 0
ROUTER_EXPERT_LANE0 = MOE_GROUPS
ROUTER_FIELDS = 6


def _router_kernel(x_ref, nw_ref, sh_ref, sc_ref, wr_ref, br_ref, meta_ref, cnt_ref,
                   carry_ref):
    tm = x_ref.shape[1]

    @pl.when(pl.program_id(1) == 0)
    def _():
        carry_ref[...] = jnp.zeros_like(carry_ref)

    hf = _norm_mod(x_ref[0], nw_ref[...], sh_ref[0], sc_ref[0])
    logits = _dot_3pass(hf, wr_ref[...]) + br_ref[...]
    lane = _iota2((tm, LANES), 1)
    neg = -jnp.inf
    first = lambda hit: jnp.min(jnp.where(hit, lane, LANES), axis=-1, keepdims=True)

    gl = jnp.where(lane < MOE_GROUPS, logits, neg)
    gmax = jnp.max(gl, axis=-1, keepdims=True)
    gidx = first(gl == gmax)
    g_top = 1.0 / jnp.sum(jnp.exp(gl - gmax), axis=-1, keepdims=True)

    lo = ROUTER_EXPERT_LANE0 + gidx * MOE_EXPERTS_PER_GROUP
    el = jnp.where((lane >= lo) & (lane < lo + MOE_EXPERTS_PER_GROUP), logits, neg)
    m1 = jnp.max(el, axis=-1, keepdims=True)
    i1 = first(el == m1)
    el2 = jnp.where(lane == i1, neg, el)
    m2 = jnp.max(el2, axis=-1, keepdims=True)
    i2 = first(el2 == m2)
    ex = jnp.exp(m2 - m1)
    gate1 = g_top / (1.0 + ex)
    gate2 = g_top * ex / (1.0 + ex)

    hit1 = lane == i1
    hit2 = lane == i2
    onehot = jnp.where(hit1 | hit2, 1.0, 0.0).astype(BF16)
    strict = jnp.where(_iota2((tm, tm), 0) > _iota2((tm, tm), 1), 1.0, 0.0).astype(BF16)
    before = _dot(strict, onehot) + carry_ref[0:1, :]
    rank1 = jnp.sum(jnp.where(hit1, before, 0.0), axis=-1, keepdims=True)
    rank2 = jnp.sum(jnp.where(hit2, before, 0.0), axis=-1, keepdims=True)
    carry_ref[...] = carry_ref[...] + _dot(jnp.ones((8, tm), BF16), onehot)
    cnt_ref[0] = carry_ref[...]

    e1 = (i1 - ROUTER_EXPERT_LANE0).astype(F32)
    e2 = (i2 - ROUTER_EXPERT_LANE0).astype(F32)
    meta = jnp.zeros((tm, LANES), F32)
    for j, val in enumerate((e1, e2, rank1, rank2, gate1, gate2)):
        meta = jnp.where(lane == j, val, meta)
    for k in range(tm // LANES):
        fields = meta[k * LANES:(k + 1) * LANES, :].T
        for f in range(ROUTER_FIELDS):
            meta_ref[0, f, k:k + 1, :] = fields[f:f + 1, :]


def moe_router(x, nw, sh, sc, group_w, group_b, expert_w, expert_b, *, tm=8 * LANES):
    bsz, t, d = x.shape
    tm = min(tm, t)
    pad = LANES - MOE_GROUPS - MOE_EXPERTS
    wr = jnp.pad(jnp.concatenate([group_w, expert_w], axis=1), ((0, 0), (0, pad)))
    br = jnp.pad(jnp.concatenate([group_b, expert_b]), (0, pad)).reshape(1, LANES)
    return pl.pallas_call(
        _router_kernel,
        grid=(bsz, t // tm),
        in_specs=[pl.BlockSpec((1, tm, d), lambda b, i: (b, i, 0)),
                  pl.BlockSpec((1, d), lambda b, i: (0, 0)),
                  pl.BlockSpec((1, 1, d), lambda b, i: (b, 0, 0)),
                  pl.BlockSpec((1, 1, d), lambda b, i: (b, 0, 0)),
                  pl.BlockSpec((d, LANES), lambda b, i: (0, 0)),
                  pl.BlockSpec((1, LANES), lambda b, i: (0, 0))],
        out_specs=[pl.BlockSpec((1, ROUTER_FIELDS, tm // LANES, LANES), lambda b, i: (b, 0, i, 0)),
                   pl.BlockSpec((1, 8, LANES), lambda b, i: (b, 0, 0))],
        out_shape=[jax.ShapeDtypeStruct((bsz, ROUTER_FIELDS, t // LANES, LANES), F32),
                   jax.ShapeDtypeStruct((bsz, 8, LANES), F32)],
        scratch_shapes=[pltpu.VMEM((8, LANES), F32)],
        compiler_params=_cparams("parallel", "arbitrary"),
        name="moe_router",
    )(x, nw.reshape(1, d), sh, sc, wr, br)


def _expert_kernel(e1_ref, e2_ref, r1_ref, r2_ref, g1_ref, g2_ref, cnt_ref,
                   x_hbm, nw_ref, sh_ref, sc_ref, gt_ref, w1_ref, w3_ref, w2_ref, out_hbm,
                   hf_v, acc_v, xb_v, yb_v, row_tok, row_gate, pstart, sem):
    b = pl.program_id(0)
    e = pl.program_id(1)
    n_e = pl.num_programs(1)
    t = hf_v.shape[0]
    grp = MOE_ROW_GROUP
    rows = xb_v.shape[0] * grp

    @pl.when(e == 0)
    def _():
        cp_x = pltpu.make_async_copy(x_hbm.at[b], acc_v, sem.at[0])
        cp_x.start()
        xb_v[...] = jnp.zeros_like(xb_v)
        for i in range(grp):
            row_tok[2 * t + i] = 0

        def start_body(i, s):
            pstart[i] = s
            return s + cnt_ref[b * n_e + i]

        lax.fori_loop(0, n_e, start_body, 0)

        def tok_body(g, carry):
            for u in range(4):
                tok = g * 4 + u
                d1 = pstart[e1_ref[tok]] + r1_ref[tok]
                row_tok[d1] = tok
                row_gate[d1] = g1_ref[tok]
                d2 = pstart[e2_ref[tok]] + r2_ref[tok]
                row_tok[d2] = tok
                row_gate[d2] = g2_ref[tok]
            return carry

        lax.fori_loop(0, t // 4, tok_body, 0)
        cp_x.wait()

        def hf_body(i, carry):
            rs = pl.ds(pl.multiple_of(i * rows, rows), rows)
            hf_v[rs, :] = _norm_mod(acc_v[rs, :], nw_ref[...], sh_ref[0], sc_ref[0])
            return carry

        lax.fori_loop(0, t // rows, hf_body, 0)

    cnt = cnt_ref[b * n_e + e]
    start = pstart[e]

    def block_body(j, carry):
        base = start + j * rows
        nrows = jnp.minimum(rows, cnt - j * rows)
        full_groups = nrows // grp

        def gather(g, c):
            first = base + g * grp
            for u in range(grp):
                xb_v[g, u:u + 1, :] = hf_v[pl.ds(row_tok[first + u], 1), :]
            return c

        lax.fori_loop(0, (nrows + grp - 1) // grp, gather, 0)
        xb = xb_v[...].reshape(rows, xb_v.shape[2]).astype(BF16)
        h1 = _dot(xb, w1_ref[0, 0].astype(BF16))
        h3 = _dot(xb, w3_ref[0, 0].astype(BF16))
        act = (_silu(h1) * h3).astype(BF16)
        yb_v[...] = (_dot(act, w2_ref[0, 0].astype(BF16)) * gt_ref[0]).reshape(yb_v.shape)

        def scatter(g, c):
            first = base + g * grp
            toks = [row_tok[first + u] for u in range(grp)]
            vals = [acc_v[pl.ds(toks[u], 1), :] + row_gate[first + u] * yb_v[g, u:u + 1, :]
                    for u in range(grp)]
            for u in range(grp):
                acc_v[pl.ds(toks[u], 1), :] = vals[u]
            return c

        lax.fori_loop(0, full_groups, scatter, 0)

        def scatter_tail(i, c):
            tok = row_tok[base + i]
            acc_v[pl.ds(tok, 1), :] = (acc_v[pl.ds(tok, 1), :] + row_gate[base + i]
                                       * yb_v[i // grp, pl.ds(i % grp, 1), :])
            return c

        lax.fori_loop(full_groups * grp, nrows, scatter_tail, 0)
        return carry

    lax.fori_loop(0, (cnt + rows - 1) // rows, block_body, 0)

    @pl.when(e == n_e - 1)
    def _():
        cp_o = pltpu.make_async_copy(acc_v, out_hbm.at[b], sem.at[1])
        cp_o.start()
        cp_o.wait()


def moe_experts(x, nw, sh, sc, gt, e1, e2, r1, r2, g1, g2, counts, layer, w1, w3, w2):
    bsz, t, d = x.shape
    _, n_e, _, f = w1.shape
    smem_tok = pl.BlockSpec((t,), lambda b, e: (b,), memory_space=pltpu.SMEM)
    return pl.pallas_call(
        _expert_kernel,
        grid=(bsz, n_e),
        in_specs=[smem_tok] * 6 + [
            pl.BlockSpec(memory_space=pltpu.SMEM),
            pl.BlockSpec(memory_space=pl.ANY),
            pl.BlockSpec((1, d), lambda b, e: (0, 0)),
            pl.BlockSpec((1, 1, d), lambda b, e: (b, 0, 0)),
            pl.BlockSpec((1, 1, d), lambda b, e: (b, 0, 0)),
            pl.BlockSpec((1, 1, d), lambda b, e: (b, 0, 0)),
            pl.BlockSpec((1, 1, d, f), lambda b, e: (layer, e, 0, 0)),
            pl.BlockSpec((1, 1, d, f), lambda b, e: (layer, e, 0, 0)),
            pl.BlockSpec((1, 1, f, d), lambda b, e: (layer, e, 0, 0))],
        out_specs=pl.BlockSpec(memory_space=pl.ANY),
        out_shape=jax.ShapeDtypeStruct((bsz, t, d), F32),
        scratch_shapes=[pltpu.VMEM((t, d), F32), pltpu.VMEM((t, d), F32),
                        pltpu.VMEM((MOE_ROWS // MOE_ROW_GROUP, MOE_ROW_GROUP, d), F32),
                        pltpu.VMEM((MOE_ROWS // MOE_ROW_GROUP, MOE_ROW_GROUP, d), F32),
                        pltpu.SMEM((2 * t + MOE_ROW_GROUP,), jnp.int32),
                        pltpu.SMEM((2 * t + MOE_ROW_GROUP,), F32),
                        pltpu.SMEM((n_e,), jnp.int32), pltpu.SemaphoreType.DMA((2,))],
        compiler_params=_cparams("arbitrary", "arbitrary"),
        name="moe_experts",
    )(e1, e2, r1, r2, g1, g2, counts, x, nw.reshape(1, d), sh, sc, gt, w1, w3, w2)


def moe_layer(x, nw, sh, sc, gate, group_w, group_b, expert_w, expert_b, layer, w1, w3, w2):
    bsz, t, d = x.shape
    meta, cnt = moe_router(x, nw, sh, sc, group_w, group_b, expert_w, expert_b)
    e1, e2, r1, r2 = [meta[:, f].astype(jnp.int32).reshape(bsz * t) for f in range(4)]
    g1, g2 = [meta[:, f].reshape(bsz * t) for f in (4, 5)]
    counts = cnt[:, 0, ROUTER_EXPERT_LANE0:ROUTER_EXPERT_LANE0 + MOE_EXPERTS].astype(jnp.int32)
    return moe_experts(x, nw, sh, sc, gate, e1, e2, r1, r2, g1, g2, counts.reshape(-1), layer,
                       w1, w3, w2)


def _final_norm_kernel(x_ref, w_ref, o_ref):
    x = x_ref[0]
    o_ref[0] = x * lax.rsqrt(jnp.mean(x * x, axis=-1, keepdims=True) + NORM_EPS) * w_ref[...]


def final_rms_norm(x, w, *, tm=1024):
    bsz, t, d = x.shape
    tm = min(tm, t)
    row = pl.BlockSpec((1, tm, d), lambda b, i: (b, i, 0))
    return pl.pallas_call(
        _final_norm_kernel,
        grid=(bsz, t // tm),
        in_specs=[row, pl.BlockSpec((1, d), lambda b, i: (0, 0))],
        out_specs=row,
        out_shape=jax.ShapeDtypeStruct((bsz, t, d), F32),
        compiler_params=_cparams("parallel", "parallel"),
        name="final_rms_norm",
    )(x, w.reshape(1, d))


def kernel(x, c, ada_w, ada_b, norm_mix_w, norm_ffn_w, ssm_in_w, ssm_conv_w, ssm_conv_b, ssm_dt_bias, ssm_a_log, ssm_d, ssm_norm_w, ssm_out_w, rwkv_mix, rwkv_rkv_w, rwkv_w0, rwkv_w1, rwkv_w2, rwkv_a0, rwkv_a1, rwkv_a2, rwkv_g1, rwkv_g2, rwkv_k_k, rwkv_k_a, rwkv_r_k, rwkv_ln_w, rwkv_ln_b, rwkv_out_w, ret_in_w, ret_out_w, att_in_w, att_out_w, moe_group_w, moe_group_b, moe_expert_w, moe_expert_b, moe_w1, moe_w3, moe_w2, final_norm_w):
    depth = ada_w.shape[0]
    d = x.shape[-1]
    mod = ada_modulation(c, ada_w, ada_b)
    for i in range(depth):
        sh1, sc1, gt1, sh2, sc2, gt2 = [mod[i][:, None, j * d:(j + 1) * d] for j in range(6)]
        kind, j = i % 4, i // 4
        pre = (x, norm_mix_w[i], sh1, sc1, gt1)
        if kind == 0:
            x = mamba2_layer(*pre, ssm_in_w[j], ssm_conv_w[j], ssm_conv_b[j], ssm_dt_bias[j],
                             ssm_a_log[j], ssm_d[j], ssm_norm_w[j], ssm_out_w[j])
        elif kind == 1:
            x = rwkv7_layer(*pre, rwkv_mix[j], rwkv_rkv_w[j], rwkv_w0[j], rwkv_w1[j], rwkv_w2[j],
                            rwkv_a0[j], rwkv_a1[j], rwkv_a2[j], rwkv_g1[j], rwkv_g2[j], rwkv_k_k[j],
                            rwkv_k_a[j], rwkv_r_k[j], rwkv_ln_w[j], rwkv_ln_b[j], rwkv_out_w[j])
        elif kind == 2:
            x = retention_layer(*pre, ret_in_w[j], ret_out_w[j])
        else:
            x = attention_layer(*pre, att_in_w[j], att_out_w[j])
        x = moe_layer(x, norm_ffn_w[i], sh2, sc2, gt2, moe_group_w[i], moe_group_b[i],
                      moe_expert_w[i], moe_expert_b[i], i, moe_w1, moe_w3, moe_w2)
    return final_rms_norm(x, final_norm_w)
```

```python
import functools
import math

import jax
import jax.numpy as jnp
from jax import lax
from jax.experimental import pallas as pl
from jax.experimental.pallas import tpu as pltpu

F32 = jnp.float32
BF16 = jnp.bfloat16
HI = lax.Precision.HIGHEST

NORM_EPS = 1e-6
LANES = 128
VMEM_LIMIT = 56 * 1024 * 1024
PROJ_TM = 2048
PROJ_TN = 1024

SSM_HEAD_DIM = 64
SSM_N_GROUPS = 8
SSM_HEADS_PER_GROUP = 4
SSM_D_STATE = 128
SSM_CONV = 4
SSM_CHUNK = 128
SSM_CONV_COLS = 256
RWKV_HEAD_DIM = 64
RWKV_LN_EPS = 64e-5
RWKV_CHUNK = 128
RWKV_SUB = 16
RWKV_INV_PRECISION = None
RWKV_APPLY_PRECISION = None
RET_N_HEADS = 4
RET_CHUNK = 128
ATT_GROUPS = ((128, 1), (512, 4), (2048, 16))
ATT_HEADS_PER_GROUP = 8
ATT_HEAD_DIM = 128
ATT_BLOCK = 128
MOE_GROUPS = 4
MOE_EXPERTS_PER_GROUP = 8
MOE_EXPERTS = 32
MOE_ROWS = 320
MOE_ROW_GROUP = 8


def _cparams(*sem):
    return pltpu.CompilerParams(dimension_semantics=sem, vmem_limit_bytes=VMEM_LIMIT)


def _sigmoid(x):
    return 1.0 / (1.0 + jnp.exp(-x))


def _silu(x):
    return x * _sigmoid(x)


def _softplus(x):
    return jnp.maximum(x, 0.0) + jnp.log(1.0 + jnp.exp(-jnp.abs(x)))


def _dot(a, b, precision=None):
    return jnp.dot(a, b, preferred_element_type=F32, precision=precision)


def _dot_nt(a, b, precision=None):
    return lax.dot_general(a, b, (((1,), (1,)), ((), ())), preferred_element_type=F32,
                           precision=precision)


def _dot_tn(a, b, precision=None):
    return lax.dot_general(a, b, (((0,), (0,)), ((), ())), preferred_element_type=F32,
                           precision=precision)


def _split_bf16(a):
    hi = a.astype(BF16)
    return hi, (a - hi.astype(F32)).astype(BF16)


def _dot_split(a, m, transpose_rhs=False, lhs_exact=False):
    hi, lo = _split_bf16(a)
    mb = m.astype(BF16)
    if lhs_exact:
        return _dot(mb, hi) + _dot(mb, lo)
    dot = _dot_nt if transpose_rhs else _dot
    return dot(hi, mb) + dot(lo, mb)


def _dot_3pass(a, b):
    a_hi, a_lo = _split_bf16(a)
    b_hi, b_lo = _split_bf16(b)
    return _dot(a_hi, b_hi) + (_dot(a_lo, b_hi) + _dot(a_hi, b_lo))


def _iota2(shape, axis):
    return lax.broadcasted_iota(jnp.int32, shape, axis)


def _ada_kernel(c_ref, w_ref, b_ref, o_ref):
    cs = _silu(c_ref[...])
    o_ref[0] = _dot(cs, w_ref[0], HI) + b_ref[0]


def ada_modulation(c, ada_w, ada_b):
    depth, d, n = ada_w.shape
    bsz = c.shape[0]
    rows = -(-bsz // 8) * 8
    cp = jnp.pad(c, ((0, rows - bsz), (0, 0)))
    tn = 1536
    out = pl.pallas_call(
        _ada_kernel,
        grid=(depth, n // tn),
        in_specs=[pl.BlockSpec((rows, d), lambda l, j: (0, 0)),
                  pl.BlockSpec((1, d, tn), lambda l, j: (l, 0, j)),
                  pl.BlockSpec((1, 1, tn), lambda l, j: (l, 0, j))],
        out_specs=pl.BlockSpec((1, rows, tn), lambda l, j: (l, 0, j)),
        out_shape=jax.ShapeDtypeStruct((depth, rows, n), F32),
        compiler_params=_cparams("parallel", "parallel"),
        name="ada_modulation",
    )(cp, ada_w, ada_b.reshape(depth, 1, n))
    return out[:, :bsz]


def _norm_mod(x, nw, sh, sc):
    y = x * lax.rsqrt(jnp.mean(x * x, axis=-1, keepdims=True) + NORM_EPS) * nw
    return y * (1.0 + sc) + sh


def _nmm_kernel(*refs, dil, has_side):
    if has_side:
        x_ref, nw_ref, sh_ref, sc_ref, w_ref, ws_ref, o_ref, side_ref, h_ref, *rest = refs
    else:
        x_ref, nw_ref, sh_ref, sc_ref, w_ref, o_ref, h_ref, *rest = refs

    @pl.when(pl.program_id(2) == 0)
    def _():
        h = _norm_mod(x_ref[0], nw_ref[...], sh_ref[0], sc_ref[0]).astype(h_ref.dtype)
        h_ref[...] = h
        if has_side:
            w_hi, w_lo = _split_bf16(ws_ref[...])
            side_ref[0] = _dot(h, w_hi) + _dot(h, w_lo)

    res = _dot(h_ref[...], w_ref[...])
    if dil is None:
        o_ref[0] = res.astype(o_ref.dtype)
    else:
        acc_ref, = rest
        rows = acc_ref.shape[1] // dil
        for c in range(acc_ref.shape[0]):
            acc_ref[c] = res[:, c * LANES:(c + 1) * LANES]
        for r in range(dil):
            for c in range(acc_ref.shape[0]):
                o_ref[0, r, :, c * LANES:(c + 1) * LANES] = (
                    acc_ref[c, pl.ds(r, rows, stride=dil), :].astype(o_ref.dtype))


def norm_mod_matmul(x, nw, sh, sc, w, *, tm, tn, dil=None, side_w=None):
    bsz, t, d = x.shape
    n = w.shape[1]
    tm = min(tm, t)
    scratch = [pltpu.VMEM((tm, d), BF16)]
    in_specs = [pl.BlockSpec((1, tm, d), lambda b, i, j: (b, i, 0)),
                pl.BlockSpec((1, d), lambda b, i, j: (0, 0)),
                pl.BlockSpec((1, 1, d), lambda b, i, j: (b, 0, 0)),
                pl.BlockSpec((1, 1, d), lambda b, i, j: (b, 0, 0)),
                pl.BlockSpec((d, tn), lambda b, i, j: (0, j))]
    ins = [x, nw.reshape(1, d), sh, sc, w]
    if dil is None:
        out_specs = [pl.BlockSpec((1, tm, tn), lambda b, i, j: (b, i, j))]
        out_shape = [jax.ShapeDtypeStruct((bsz, t, n), BF16)]
    else:
        out_specs = [pl.BlockSpec((1, dil, tm // dil, tn), lambda b, i, j: (b, 0, i, j))]
        out_shape = [jax.ShapeDtypeStruct((bsz, dil, t // dil, n), BF16)]
        scratch.append(pltpu.VMEM((tn // LANES, tm, LANES), F32))
    if side_w is not None:
        in_specs.append(pl.BlockSpec((d, LANES), lambda b, i, j: (0, 0)))
        ins.append(side_w)
        out_specs.append(pl.BlockSpec((1, tm, LANES), lambda b, i, j: (b, i, 0)))
        out_shape.append(jax.ShapeDtypeStruct((bsz, t, LANES), F32))
    outs = pl.pallas_call(
        functools.partial(_nmm_kernel, dil=dil, has_side=side_w is not None),
        grid=(bsz, t // tm, n // tn),
        in_specs=in_specs,
        out_specs=out_specs,
        out_shape=out_shape,
        scratch_shapes=scratch,
        compiler_params=_cparams("parallel", "parallel", "arbitrary"),
        name="norm_mod_matmul",
    )(*ins)
    return outs if side_w is not None else outs[0]


def _mgr_kernel(*refs, has_mul):
    if has_mul:
        y_ref, g_ref, w_ref, x_ref, gate_ref, o_ref = refs
        y = (y_ref[0].astype(F32) * g_ref[0].astype(F32)).astype(BF16)
    else:
        y_ref, w_ref, x_ref, gate_ref, o_ref = refs
        y = y_ref[0].astype(BF16)
    o_ref[0] = x_ref[0] + gate_ref[0] * _dot(y, w_ref[...])


def matmul_gated_residual(y, w, x, gate, mul=None, *, tm=512):
    bsz, t, k = y.shape
    d = w.shape[1]
    tm = min(tm, t)
    ins = [y] + ([mul] if mul is not None else []) + [w, x, gate]
    row = lambda width: pl.BlockSpec((1, tm, width), lambda b, i: (b, i, 0))
    specs = [row(k)] + ([row(k)] if mul is not None else []) + [
        pl.BlockSpec((k, d), lambda b, i: (0, 0)), row(d),
        pl.BlockSpec((1, 1, d), lambda b, i: (b, 0, 0))]
    return pl.pallas_call(
        functools.partial(_mgr_kernel, has_mul=mul is not None),
        grid=(bsz, t // tm),
        in_specs=specs,
        out_specs=row(d),
        out_shape=jax.ShapeDtypeStruct((bsz, t, d), F32),
        compiler_params=_cparams("parallel", "parallel"),
        name="matmul_gated_residual",
    )(*ins)


def _ssd_kernel(zx_ref, dt_ref, cw_ref, cb_ref, dtb_ref, alog_ref, dsk_ref, nw_ref, o_ref,
                prev_ref, xbc_ref, state_ref):
    q = SSM_CHUNK
    g_n, hg, p, n = SSM_N_GROUPS, SSM_HEADS_PER_GROUP, SSM_HEAD_DIM, SSM_D_STATE
    d_inner = g_n * hg * p
    nh = g_n * hg
    gw = hg * p

    @pl.when(pl.program_id(1) == 0)
    def _():
        prev_ref[...] = jnp.zeros_like(prev_ref)
        state_ref[...] = jnp.zeros_like(state_ref)

    src = _iota2((q, 2 * q), 1) - q - _iota2((q, 2 * q), 0)
    shifts = [jnp.where(src == -s, 1.0, 0.0).astype(BF16) for s in range(1, SSM_CONV)]
    cdim = xbc_ref.shape[1]
    for c0 in range(0, cdim, SSM_CONV_COLS):
        cs = slice(c0, c0 + SSM_CONV_COLS)
        cur = zx_ref[0, :, d_inner + c0:d_inner + c0 + SSM_CONV_COLS]
        ext = jnp.concatenate([prev_ref[:, cs], cur], axis=0)
        prev_ref[:, cs] = cur
        acc = cb_ref[:, cs] + cur.astype(F32) * cw_ref[SSM_CONV - 1:SSM_CONV, cs]
        for s in range(1, SSM_CONV):
            acc = acc + _dot(shifts[s - 1], ext) * cw_ref[SSM_CONV - 1 - s:SSM_CONV - s, cs]
        xbc_ref[:, cs] = _silu(acc)

    dt = _softplus(dt_ref[0][:, :nh] + dtb_ref[...])
    da = dt * -jnp.exp(alog_ref[...])
    tril = (_iota2((q, q), 0) >= _iota2((q, q), 1))
    acum = _dot_split(da, tril.astype(F32), lhs_exact=True)
    a_end = acum[q - 1:q, :]
    chunk_decay = jnp.exp(a_end)
    adj_t = (acum - jnp.log(dt)).T
    wdec_t = (dt * jnp.exp(a_end - acum)).T

    for g in range(g_n):
        xs_g = xbc_ref[:, g * gw:(g + 1) * gw]
        bm = xbc_ref[:, d_inner + g * n:d_inner + (g + 1) * n]
        cm = xbc_ref[:, d_inner + g_n * n + g * n:d_inner + g_n * n + (g + 1) * n]
        cm_b = cm.astype(BF16)
        cb = _dot_nt(cm_b, bm.astype(BF16))
        bm_t = bm.T
        xs_b = xs_g.astype(BF16)
        st_old = state_ref[g]
        y_carry = _dot(cm_b, st_old.astype(BF16))
        ys, st_cols = [], []
        for hh in range(hg):
            h = g * hg + hh
            hs = slice(hh * p, (hh + 1) * p)
            a_col = jnp.broadcast_to(acum[:, h:h + 1], (q, q))
            m_h = cb * jnp.exp(jnp.where(tril, a_col - adj_t[h:h + 1, :], -jnp.inf))
            ys.append(_dot(m_h.astype(BF16), xs_b[:, hs]) + jnp.exp(a_col[:, :p]) * y_carry[:, hs])
            st_new = _dot((bm_t * wdec_t[h:h + 1, :]).astype(BF16), xs_b[:, hs])
            st_cols.append(st_old[:, hs] * chunk_decay[:, h:h + 1] + st_new)
        state_ref[g] = jnp.concatenate(st_cols, axis=1)
        y = jnp.concatenate(ys, axis=1) + xs_g * dsk_ref[:, g * gw:(g + 1) * gw]
        z = zx_ref[0, :, g * gw:(g + 1) * gw].astype(F32)
        y = y * _silu(z)
        y = y * lax.rsqrt(jnp.mean(y * y, axis=-1, keepdims=True) + NORM_EPS)
        o_ref[0, :, g * gw:(g + 1) * gw] = (y * nw_ref[:, g * gw:(g + 1) * gw]).astype(o_ref.dtype)


def ssd_core(zx, dt_raw, conv_w, conv_b, dt_bias, a_log, d_skip, norm_w):
    bsz, t, width = zx.shape
    d_inner = SSM_N_GROUPS * SSM_HEADS_PER_GROUP * SSM_HEAD_DIM
    cdim = width - d_inner
    nh = dt_bias.shape[0]
    q = SSM_CHUNK
    full = lambda shape: pl.BlockSpec(shape, lambda b, c: (0,) * len(shape))
    return pl.pallas_call(
        _ssd_kernel,
        grid=(bsz, t // q),
        in_specs=[pl.BlockSpec((1, q, width), lambda b, c: (b, c, 0)),
                  pl.BlockSpec((1, q, LANES), lambda b, c: (b, c, 0)),
                  full((SSM_CONV, cdim)), full((1, cdim)), full((1, nh)), full((1, nh)),
                  full((1, d_inner)), full((1, d_inner))],
        out_specs=pl.BlockSpec((1, q, d_inner), lambda b, c: (b, c, 0)),
        out_shape=jax.ShapeDtypeStruct((bsz, t, d_inner), BF16),
        scratch_shapes=[pltpu.VMEM((q, cdim), BF16), pltpu.VMEM((q, cdim), F32),
                        pltpu.VMEM((SSM_N_GROUPS, SSM_D_STATE,
                                    SSM_HEADS_PER_GROUP * SSM_HEAD_DIM), F32)],
        compiler_params=_cparams("parallel", "arbitrary"),
        name="ssd_core",
    )(zx, dt_raw, conv_w, conv_b.reshape(1, cdim), dt_bias.reshape(1, nh),
      a_log.reshape(1, nh), jnp.repeat(d_skip, SSM_HEAD_DIM).reshape(1, d_inner),
      norm_w.reshape(1, d_inner))


def mamba2_layer(x, nw, sh, sc, gate, in_w, conv_w, conv_b, dt_bias, a_log, d_skip, norm_w, out_w):
    d_inner = SSM_N_GROUPS * SSM_HEADS_PER_GROUP * SSM_HEAD_DIM
    cdim = conv_w.shape[1]
    nh = dt_bias.shape[0]
    w_main = in_w[:, :d_inner + cdim].astype(BF16)
    w_dt = jnp.pad(in_w[:, d_inner + cdim:], ((0, 0), (0, LANES - nh)))
    zx, dt_raw = norm_mod_matmul(x, nw, sh, sc, w_main, tm=PROJ_TM, tn=PROJ_TN, side_w=w_dt)
    y = ssd_core(zx, dt_raw, conv_w, conv_b, dt_bias, a_log, d_skip, norm_w)
    return matmul_gated_residual(y, out_w.astype(BF16), x, gate)


def _head_indicator(d, n):
    return jnp.where(_iota2((d, LANES), 0) // n == _iota2((d, LANES), 1), 1.0, 0.0)


def _rwkv_prep_kernel(x_ref, xp_ref, nw_ref, sh_ref, sc_ref, mix_ref, wr_ref, wk_ref, wv_ref,
                      w0_ref, w1_ref, w2_ref, a0_ref, a1_ref, a2_ref, g1_ref, g2_ref, kk_ref, ka_ref,
                      r_o, lw_o, k_o, v_o, kn_o, b_o, g_o):
    tm, d = x_ref.shape[1], x_ref.shape[2]
    nw, sh, sc = nw_ref[...], sh_ref[0], sc_ref[0]
    hm = _norm_mod(x_ref[0], nw, sh, sc)
    prev = _norm_mod(xp_ref[0], nw, sh, sc)[7:8, :]
    prev = jnp.where(pl.program_id(1) > 0, prev, 0.0)
    shifted = jnp.where(_iota2((tm, d), 0) == 0, prev, pltpu.roll(hm, 1, axis=0))
    xx = shifted - hm
    mixed = lambda j: (hm + xx * mix_ref[j:j + 1, :]).astype(BF16)
    xr, xw, xk, xv, xa, xg = [mixed(j) for j in range(6)]
    r = _dot(xr, wr_ref[...])
    k = _dot(xk, wk_ref[...])
    v = _dot(xv, wv_ref[...])
    lora = lambda u, w: _dot(u.astype(BF16), w[...])
    w_log = -_softplus(-(w0_ref[...] + lora(jnp.tanh(lora(xw, w1_ref)), w2_ref))) - 0.5
    a = _sigmoid(a0_ref[...] + lora(lora(xa, a1_ref), a2_ref))
    g = lora(_sigmoid(lora(xg, g1_ref)), g2_ref)
    kk = k * kk_ref[...]
    ind = _head_indicator(d, RWKV_HEAD_DIM)
    nrm = jnp.maximum(jnp.sqrt(_dot_split(kk * kk, ind)), 1e-12)
    kn = kk * _dot_split(1.0 / nrm, ind, transpose_rhs=True)
    r_o[0] = r.astype(r_o.dtype)
    lw_o[0] = -jnp.exp(w_log)
    k_o[0] = (k * (1.0 + (a - 1.0) * ka_ref[...])).astype(k_o.dtype)
    v_o[0] = v.astype(v_o.dtype)
    kn_o[0] = kn.astype(kn_o.dtype)
    b_o[0] = (kn * a).astype(b_o.dtype)
    g_o[0] = g.astype(g_o.dtype)


def rwkv_prep(x, nw, sh, sc, mix, rkv_w, w0, w1, w2, a0, a1, a2, g1, g2, k_k, k_a, *, tm=512):
    bsz, t, d = x.shape
    tm = min(tm, t)
    full = lambda arr: pl.BlockSpec(arr.shape, lambda b, i: (0,) * arr.ndim)
    row = pl.BlockSpec((1, tm, d), lambda b, i: (b, i, 0))
    vec = pl.BlockSpec((1, 1, d), lambda b, i: (b, 0, 0))
    bf = lambda w: w.astype(BF16)
    params = [mix, bf(rkv_w[0]), bf(rkv_w[1]), bf(rkv_w[2]), w0.reshape(1, d), bf(w1), bf(w2),
              a0.reshape(1, d), bf(a1), bf(a2), bf(g1), bf(g2), k_k.reshape(1, d), k_a.reshape(1, d)]
    prev_spec = pl.BlockSpec((1, 8, d), lambda b, i: (b, jnp.maximum(i * (tm // 8) - 1, 0), 0))
    return pl.pallas_call(
        _rwkv_prep_kernel,
        grid=(bsz, t // tm),
        in_specs=[row, prev_spec, pl.BlockSpec((1, d), lambda b, i: (0, 0)), vec, vec]
        + [full(p) for p in params],
        out_specs=[row] * 7,
        out_shape=[jax.ShapeDtypeStruct((bsz, t, d), F32 if i == 1 else BF16) for i in range(7)],
        compiler_params=_cparams("parallel", "parallel"),
        name="rwkv_prep",
    )(x, x, nw.reshape(1, d), sh, sc, *params)


def _mm(a, b, precision):
    if precision is None:
        return _dot(a.astype(BF16), b.astype(BF16))
    return _dot(a, b, precision)


def _mm_nt(a, b, precision):
    if precision is None:
        return _dot_nt(a.astype(BF16), b.astype(BF16))
    return _dot_nt(a, b, precision)


def _mm_tn(a, b, precision):
    if precision is None:
        return _dot_tn(a.astype(BF16), b.astype(BF16))
    return _dot_tn(a, b, precision)


def _unit_lower_inverse(m_list, length, sub, precision):
    ri = _iota2((length, length), 0)
    ci = _iota2((length, length), 1)
    eye = jnp.where(ri == ci, 1.0, 0.0)
    same_block = ri // sub == ci // sub
    mm = lambda a, b: _mm(a, b, precision)
    md = [jnp.where(same_block, m, 0.0) for m in m_list]
    off = [m - d for m, d in zip(m_list, md)]
    dinv = [eye + d for d in md]
    pw = md
    for _ in range(int(math.log2(sub)) - 1):
        pw = [mm(p, p) for p in pw]
        dinv = [x + mm(x, p) for x, p in zip(dinv, pw)]
    pm = [mm(x, o) for x, o in zip(dinv, off)]
    acc = [eye + p for p in pm]
    pw = pm
    for _ in range(int(math.log2(length // sub)) - 1):
        pw = [mm(p, p) for p in pw]
        acc = [x + mm(x, p) for x, p in zip(acc, pw)]
    return [mm(x, d) for x, d in zip(acc, dinv)]


def _rwkv_core_kernel(r_ref, lw_ref, k_ref, v_ref, kn_ref, b_ref, rk_ref, lnw_ref, lnb_ref, o_ref,
                      state_ref, *, inv_precision, apply_precision):
    length, d = r_ref.shape[1], r_ref.shape[2]
    n = RWKV_HEAD_DIM
    heads = range(d // n)
    mm = lambda a, b: _mm(a, b, apply_precision)

    @pl.when(pl.program_id(1) == 0)
    def _():
        state_ref[...] = jnp.zeros_like(state_ref)

    ri = _iota2((2 * length, 2 * length), 0)
    ci = _iota2((2 * length, 2 * length), 1) % length
    keep = ((ri < length) & (ri > ci)) | ((ri >= length) & (ri - length >= ci))
    eye_n = _iota2((n, n), 0) == _iota2((n, n), 1)

    lw = lw_ref[0]
    r, k, v, kn, b = [ref[0].astype(F32) for ref in (r_ref, k_ref, v_ref, kn_ref, b_ref)]
    tri = jnp.where(_iota2((length, length), 0) >= _iota2((length, length), 1), 1.0, 0.0)
    cum = _dot_split(lw, tri, lhs_exact=True)
    c_end = cum[length - 1:length, :]
    e_neg = jnp.exp(-cum)
    e_rem = jnp.exp(c_end - cum)
    ar = jnp.concatenate([-kn * jnp.exp(cum - lw), r * jnp.exp(cum)], axis=0)
    bk = jnp.concatenate([b * e_neg, k * e_neg], axis=0)
    bk_end = jnp.concatenate([b * e_rem, k * e_rem], axis=0)
    w_end = jnp.exp(c_end)
    bonus_w = r * k * rk_ref[...]
    hs = [slice(h * n, (h + 1) * n) for h in heads]
    zero = jnp.zeros((length, n), F32)

    s0 = [state_ref[h] for h in heads]
    p = [jnp.where(keep, _mm_nt(ar[:, c], bk[:, c], inv_precision), 0.0) for c in hs]
    t_inv = _unit_lower_inverse([x[:length, :length] for x in p], length, RWKV_SUB, inv_precision)
    q = [mm(ar[:, c], s) for c, s in zip(hs, s0)]
    y = [x[:length] + mm(pp[:length], jnp.concatenate([zero, v[:, c]], axis=0))
         for x, pp, c in zip(q, p, hs)]
    u = [mm(ti, yy) for ti, yy in zip(t_inv, y)]
    uv = [jnp.concatenate([uu, v[:, c]], axis=0) for uu, c in zip(u, hs)]
    o = [x[length:] + mm(pp[length:], w) for x, pp, w in zip(q, p, uv)]
    for h in heads:
        c = hs[h]
        w_col = jnp.sum(jnp.where(eye_n, w_end[:, c], 0.0), axis=1, keepdims=True)
        state_ref[h] = w_col * s0[h] + _mm_tn(bk_end[:, c], uv[h], apply_precision)
        mu = jnp.mean(o[h], axis=-1, keepdims=True)
        var = jnp.mean((o[h] - mu) ** 2, axis=-1, keepdims=True)
        yh = (o[h] - mu) * lax.rsqrt(var + RWKV_LN_EPS) * lnw_ref[:, c] + lnb_ref[:, c]
        yh = yh + jnp.sum(bonus_w[:, c], axis=-1, keepdims=True) * v[:, c]
        o_ref[0, :, c] = yh.astype(o_ref.dtype)


def rwkv_core(r, lw, k, v, kn, b, r_k, ln_w, ln_b):
    bsz, t, d = r.shape
    length = RWKV_CHUNK
    n = RWKV_HEAD_DIM
    row = pl.BlockSpec((1, length, d), lambda bb, c: (bb, c, 0))
    vec = pl.BlockSpec((1, d), lambda bb, c: (0, 0))
    return pl.pallas_call(
        functools.partial(_rwkv_core_kernel, inv_precision=RWKV_INV_PRECISION,
                          apply_precision=RWKV_APPLY_PRECISION),
        grid=(bsz, t // length),
        in_specs=[row] * 6 + [vec] * 3,
        out_specs=row,
        out_shape=jax.ShapeDtypeStruct((bsz, t, d), BF16),
        scratch_shapes=[pltpu.VMEM((d // n, n, n), F32)],
        compiler_params=_cparams("parallel", "arbitrary"),
        name="rwkv_core",
    )(r, lw, k, v, kn, b, r_k.reshape(1, d), ln_w.reshape(1, d), ln_b.reshape(1, d))


def rwkv7_layer(x, nw, sh, sc, gate, mix, rkv_w, w0, w1, w2, a0, a1, a2, g1, g2, k_k, k_a, r_k,
                ln_w, ln_b, out_w):
    r, lw, k, v, kn, b, g = rwkv_prep(x, nw, sh, sc, mix, rkv_w, w0, w1, w2, a0, a1, a2, g1, g2,
                                      k_k, k_a)
    y = rwkv_core(r, lw, k, v, kn, b, r_k, ln_w, ln_b)
    return matmul_gated_residual(y, out_w.astype(BF16), x, gate, mul=g)


def _ret_kernel(qkvg_ref, idec_ref, qdec_ref, kdec_ref, cdec_ref, o_ref, state_ref, *, d_model):
    nh = RET_N_HEADS
    dk = d_model // nh
    dv = 2 * d_model // nh
    k_scale = dk ** -0.5

    @pl.when(pl.program_id(1) == 0)
    def _():
        state_ref[...] = jnp.zeros_like(state_ref)

    for h in range(nh):
        q_b = qkvg_ref[0, :, h * dk:(h + 1) * dk]
        k_f = qkvg_ref[0, :, d_model + h * dk:d_model + (h + 1) * dk].astype(F32) * k_scale
        v_b = qkvg_ref[0, :, 2 * d_model + h * dv:2 * d_model + (h + 1) * dv]
        g_f = qkvg_ref[0, :, 4 * d_model + h * dv:4 * d_model + (h + 1) * dv].astype(F32)
        scores = _dot_nt(q_b, k_f.astype(BF16)) * idec_ref[h]
        inner = _dot(scores.astype(BF16), v_b)
        r_old = state_ref[h]
        qd = (q_b.astype(F32) * qdec_ref[:, h:h + 1]).astype(BF16)
        cross = _dot(qd, r_old.astype(BF16))
        kd = (k_f * kdec_ref[:, h:h + 1]).astype(BF16)
        state_ref[h] = r_old * cdec_ref[:, h:h + 1] + _dot_tn(kd, v_b)
        o = inner + cross
        o = o * lax.rsqrt(jnp.mean(o * o, axis=-1, keepdims=True) + NORM_EPS)
        o_ref[0, :, h * dv:(h + 1) * dv] = (_silu(g_f) * o).astype(o_ref.dtype)


def retention_core(qkvg, d_model):
    bsz, t, width = qkvg.shape
    nh, q = RET_N_HEADS, RET_CHUNK
    dk, dv = d_model // nh, 2 * d_model // nh
    log_gamma = jnp.log(1 - jnp.exp2(-5.0 - jnp.arange(nh, dtype=F32)))
    idx = jnp.arange(q)
    rel = idx[:, None] - idx[None, :]
    inner_decay = jnp.where(rel >= 0,
                            jnp.exp(jnp.maximum(rel, 0).astype(F32) * log_gamma[:, None, None]), 0.0)
    q_decay = jnp.exp((idx + 1).astype(F32)[:, None] * log_gamma)
    k_decay = jnp.exp((q - 1 - idx).astype(F32)[:, None] * log_gamma)
    chunk_decay = jnp.exp(q * log_gamma).reshape(1, nh)
    full = lambda shape: pl.BlockSpec(shape, lambda b, c: (0,) * len(shape))
    return pl.pallas_call(
        functools.partial(_ret_kernel, d_model=d_model),
        grid=(bsz, t // q),
        in_specs=[pl.BlockSpec((1, q, width), lambda b, c: (b, c, 0)),
                  full((nh, q, q)), full((q, nh)), full((q, nh)), full((1, nh))],
        out_specs=pl.BlockSpec((1, q, nh * dv), lambda b, c: (b, c, 0)),
        out_shape=jax.ShapeDtypeStruct((bsz, t, nh * dv), BF16),
        scratch_shapes=[pltpu.VMEM((nh, dk, dv), F32)],
        compiler_params=_cparams("parallel", "arbitrary"),
        name="retention_core",
    )(qkvg, inner_decay, q_decay, k_decay, chunk_decay)


def retention_layer(x, nw, sh, sc, gate, in_w, out_w):
    d = x.shape[-1]
    qkvg = norm_mod_matmul(x, nw, sh, sc, in_w.astype(BF16), tm=PROJ_TM, tn=PROJ_TN)
    y = retention_core(qkvg, d)
    return matmul_gated_residual(y, out_w.astype(BF16), x, gate)


def _att_kernel(q_ref, k_ref, v_ref, sl_ref, o_ref, lse_ref, kprev_ref, vprev_ref, *, dil, span):
    w = ATT_BLOCK
    e = ATT_HEAD_DIM
    nblk = pl.program_id(2)

    @pl.when(nblk == 0)
    def _():
        kprev_ref[...] = jnp.zeros_like(kprev_ref)
        vprev_ref[...] = jnp.zeros_like(vprev_ref)

    qi = _iota2((w, 2 * w), 0)
    ci = _iota2((w, 2 * w), 1)
    delta = w + qi - ci
    mask = (delta >= 0) & (delta <= span) & ((nblk > 0) | (ci >= w))
    dist = (dil * delta).astype(F32)
    lane = _iota2((w, LANES), 1)
    heads = range(ATT_HEADS_PER_GROUP)
    cols = [slice(h * e, (h + 1) * e) for h in heads]
    s = [_dot_nt(q_ref[0, 0, :, c], jnp.concatenate([kprev_ref[:, c], k_ref[0, 0, :, c]], axis=0))
         for c in cols]
    s = [jnp.where(mask, x * (e ** -0.5) - sl_ref[:, h:h + 1] * dist, -jnp.inf)
         for h, x in zip(heads, s)]
    m = [jnp.max(x, axis=-1, keepdims=True) for x in s]
    p = [jnp.exp(x - mx) for x, mx in zip(s, m)]
    den = [jnp.sum(x, axis=-1, keepdims=True) for x in p]
    lse_all = jnp.zeros((w, LANES), F32)
    for h in heads:
        vv = jnp.concatenate([vprev_ref[:, cols[h]], v_ref[0, 0, :, cols[h]]], axis=0)
        o = _dot((p[h] / den[h]).astype(BF16), vv)
        o_ref[0, 0, :, cols[h]] = o.astype(o_ref.dtype)
        lse_all = jnp.where(lane == h, m[h] + jnp.log(den[h]), lse_all)
    lse_ref[0, 0] = lse_all
    kprev_ref[...] = k_ref[0, 0]
    vprev_ref[...] = v_ref[0, 0]


def dilated_group_core(qkv, gi, window, dil, slopes):
    bsz, _, ls, _ = qkv.shape
    w = ATT_BLOCK
    gw = ATT_HEADS_PER_GROUP * ATT_HEAD_DIM
    spec = lambda part: pl.BlockSpec((1, 1, w, gw), lambda b, r, n: (b, r, n, part))
    return pl.pallas_call(
        functools.partial(_att_kernel, dil=dil, span=window // dil),
        grid=(bsz, dil, ls // w),
        in_specs=[spec(0), spec(1), spec(2),
                  pl.BlockSpec((1, ATT_HEADS_PER_GROUP), lambda b, r, n: (0, 0))],
        out_specs=[pl.BlockSpec((1, 1, w, gw), lambda b, r, n: (b, r, n, 0)),
                   pl.BlockSpec((1, 1, w, LANES), lambda b, r, n: (b, r, n, 0))],
        out_shape=[jax.ShapeDtypeStruct((bsz, dil, ls, gw), BF16),
                   jax.ShapeDtypeStruct((bsz, dil, ls, LANES), F32)],
        scratch_shapes=[pltpu.VMEM((w, gw), BF16), pltpu.VMEM((w, gw), BF16)],
        compiler_params=_cparams("parallel", "parallel", "arbitrary"),
        name="dilated_attention_g%d" % gi,
    )(qkv, qkv, qkv, slopes.reshape(1, ATT_HEADS_PER_GROUP))


def _att_out_kernel(o0_ref, o1_ref, o2_ref, l0_ref, l1_ref, l2_ref, w_ref, x_ref, gate_ref, o_ref,
                    *scratch):
    tm = x_ref.shape[1]

    def token_major(ref, scr):
        dil = ref.shape[1]
        if dil == 1:
            return ref[0, 0].astype(F32)
        for r in range(dil):
            blk = ref[0, r].astype(F32)
            for c in range(scr.shape[0]):
                scr[c, pl.ds(r, tm // dil, stride=dil), :] = blk[:, c * LANES:(c + 1) * LANES]
        return jnp.concatenate([scr[c] for c in range(scr.shape[0])], axis=1)

    o1, l1, o2, l2 = [token_major(ref, scr) for ref, scr in
                      zip((o1_ref, l1_ref, o2_ref, l2_ref), scratch)]
    o0, l0 = token_major(o0_ref, None), token_major(l0_ref, None)
    m = jnp.maximum(jnp.maximum(l0, l1), l2)
    w0, w1, w2 = jnp.exp(l0 - m), jnp.exp(l1 - m), jnp.exp(l2 - m)
    inv = 1.0 / (w0 + w1 + w2)
    k = o0.shape[1]
    expand = jnp.where(_iota2((LANES, k), 1) // ATT_HEAD_DIM == _iota2((LANES, k), 0), 1.0, 0.0)
    y = (_dot_split(w0 * inv, expand) * o0 + _dot_split(w1 * inv, expand) * o1
         + _dot_split(w2 * inv, expand) * o2)
    o_ref[0] = x_ref[0] + gate_ref[0] * _dot(y.astype(BF16), w_ref[...])


def attention_combine_out(outs, lses, w, x, gate, *, tm=512):
    bsz, t, d = x.shape
    k = w.shape[0]
    tm = min(tm, t)
    row = pl.BlockSpec((1, tm, d), lambda b, i: (b, i, 0))
    res = lambda a: pl.BlockSpec((1, a.shape[1], tm // a.shape[1], a.shape[3]),
                                 lambda b, i: (b, 0, i, 0))
    wide, narrow = pltpu.VMEM((k // LANES, tm, LANES), F32), pltpu.VMEM((1, tm, LANES), F32)
    return pl.pallas_call(
        _att_out_kernel,
        grid=(bsz, t // tm),
        in_specs=[res(a) for a in outs] + [res(a) for a in lses] + [
            pl.BlockSpec((k, d), lambda b, i: (0, 0)), row,
            pl.BlockSpec((1, 1, d), lambda b, i: (b, 0, 0))],
        out_specs=row,
        out_shape=jax.ShapeDtypeStruct((bsz, t, d), F32),
        scratch_shapes=[wide, narrow, wide, narrow],
        compiler_params=_cparams("parallel", "parallel"),
        name="attention_combine_out",
    )(*outs, *lses, w, x, gate)


def attention_layer(x, nw, sh, sc, gate, in_w, out_w):
    n_groups = len(ATT_GROUPS)
    gw = ATT_HEADS_PER_GROUP * ATT_HEAD_DIM
    n_heads = n_groups * ATT_HEADS_PER_GROUP
    slopes = jnp.exp2(-8.0 * jnp.arange(1, n_heads + 1, dtype=F32) / n_heads)
    slopes = slopes.reshape(n_groups, ATT_HEADS_PER_GROUP)
    w_in = in_w.astype(BF16).reshape(in_w.shape[0], 3, n_groups, gw)
    outs, lses = [], []
    for gi, (window, dil) in enumerate(ATT_GROUPS):
        w_g = w_in[:, :, gi, :].reshape(in_w.shape[0], 3 * gw)
        qkv = norm_mod_matmul(x, nw, sh, sc, w_g, tm=PROJ_TM, tn=gw, dil=dil)
        o, lse = dilated_group_core(qkv, gi, window, dil, slopes[gi])
        outs.append(o)
        lses.append(lse)
    return attention_combine_out(outs, lses, out_w.astype(BF16), x, gate)


ROUTER_GROUP_LANE0 = 0
ROUTER_EXPERT_LANE0 = MOE_GROUPS
ROUTER_FIELDS = 6


def _router_kernel(x_ref, nw_ref, sh_ref, sc_ref, wr_ref, br_ref, hf_ref, meta_ref, cnt_ref,
                   carry_ref):
    tm = x_ref.shape[1]

    @pl.when(pl.program_id(1) == 0)
    def _():
        carry_ref[...] = jnp.zeros_like(carry_ref)

    hf = _norm_mod(x_ref[0], nw_ref[...], sh_ref[0], sc_ref[0])
    hf_ref[0] = hf
    logits = _dot_3pass(hf, wr_ref[...]) + br_ref[...]
    lane = _iota2((tm, LANES), 1)
    neg = -jnp.inf
    first = lambda hit: jnp.min(jnp.where(hit, lane, LANES), axis=-1, keepdims=True)

    gl = jnp.where(lane < MOE_GROUPS, logits, neg)
    gmax = jnp.max(gl, axis=-1, keepdims=True)
    gidx = first(gl == gmax)
    g_top = 1.0 / jnp.sum(jnp.exp(gl - gmax), axis=-1, keepdims=True)

    lo = ROUTER_EXPERT_LANE0 + gidx * MOE_EXPERTS_PER_GROUP
    el = jnp.where((lane >= lo) & (lane < lo + MOE_EXPERTS_PER_GROUP), logits, neg)
    m1 = jnp.max(el, axis=-1, keepdims=True)
    i1 = first(el == m1)
    el2 = jnp.where(lane == i1, neg, el)
    m2 = jnp.max(el2, axis=-1, keepdims=True)
    i2 = first(el2 == m2)
    ex = jnp.exp(m2 - m1)
    gate1 = g_top / (1.0 + ex)
    gate2 = g_top * ex / (1.0 + ex)

    hit1 = lane == i1
    hit2 = lane == i2
    onehot = jnp.where(hit1 | hit2, 1.0, 0.0).astype(BF16)
    strict = jnp.where(_iota2((tm, tm), 0) > _iota2((tm, tm), 1), 1.0, 0.0).astype(BF16)
    before = _dot(strict, onehot) + carry_ref[0:1, :]
    rank1 = jnp.sum(jnp.where(hit1, before, 0.0), axis=-1, keepdims=True)
    rank2 = jnp.sum(jnp.where(hit2, before, 0.0), axis=-1, keepdims=True)
    carry_ref[...] = carry_ref[...] + _dot(jnp.ones((8, tm), BF16), onehot)
    cnt_ref[0] = carry_ref[...]

    e1 = (i1 - ROUTER_EXPERT_LANE0).astype(F32)
    e2 = (i2 - ROUTER_EXPERT_LANE0).astype(F32)
    meta = jnp.zeros((tm, LANES), F32)
    for j, val in enumerate((e1, e2, rank1, rank2, gate1, gate2)):
        meta = jnp.where(lane == j, val, meta)
    for k in range(tm // LANES):
        fields = meta[k * LANES:(k + 1) * LANES, :].T
        for f in range(ROUTER_FIELDS):
            meta_ref[0, f, k:k + 1, :] = fields[f:f + 1, :]


def moe_router(x, nw, sh, sc, group_w, group_b, expert_w, expert_b, *, tm=8 * LANES):
    bsz, t, d = x.shape
    tm = min(tm, t)
    pad = LANES - MOE_GROUPS - MOE_EXPERTS
    wr = jnp.pad(jnp.concatenate([group_w, expert_w], axis=1), ((0, 0), (0, pad)))
    br = jnp.pad(jnp.concatenate([group_b, expert_b]), (0, pad)).reshape(1, LANES)
    return pl.pallas_call(
        _router_kernel,
        grid=(bsz, t // tm),
        in_specs=[pl.BlockSpec((1, tm, d), lambda b, i: (b, i, 0)),
                  pl.BlockSpec((1, d), lambda b, i: (0, 0)),
                  pl.BlockSpec((1, 1, d), lambda b, i: (b, 0, 0)),
                  pl.BlockSpec((1, 1, d), lambda b, i: (b, 0, 0)),
                  pl.BlockSpec((d, LANES), lambda b, i: (0, 0)),
                  pl.BlockSpec((1, LANES), lambda b, i: (0, 0))],
        out_specs=[pl.BlockSpec((1, tm, d), lambda b, i: (b, i, 0)),
                   pl.BlockSpec((1, ROUTER_FIELDS, tm // LANES, LANES), lambda b, i: (b, 0, i, 0)),
                   pl.BlockSpec((1, 8, LANES), lambda b, i: (b, 0, 0))],
        out_shape=[jax.ShapeDtypeStruct((bsz, t, d), F32),
                   jax.ShapeDtypeStruct((bsz, ROUTER_FIELDS, t // LANES, LANES), F32),
                   jax.ShapeDtypeStruct((bsz, 8, LANES), F32)],
        scratch_shapes=[pltpu.VMEM((8, LANES), F32)],
        compiler_params=_cparams("parallel", "arbitrary"),
        name="moe_router",
    )(x, nw.reshape(1, d), sh, sc, wr, br)


def _dest_kernel(meta_ref, cnt_ref, d_ref):
    upper = jnp.where(_iota2((LANES, LANES), 0) < _iota2((LANES, LANES), 1), 1.0, 0.0)
    first = _dot_split(cnt_ref[0], upper)
    for k in range(2):
        expert = meta_ref[0, k]
        off = jnp.zeros_like(expert)
        for x in range(MOE_EXPERTS):
            lane = ROUTER_EXPERT_LANE0 + x
            off = jnp.where(expert == x, first[0:1, lane:lane + 1], off)
        d_ref[0, k] = (meta_ref[0, 2 + k] + off).astype(jnp.int32)


def moe_dest(meta, cnt):
    bsz, nf, rows, _ = meta.shape
    return pl.pallas_call(
        _dest_kernel,
        grid=(bsz,),
        in_specs=[pl.BlockSpec((1, nf, rows, LANES), lambda b: (b, 0, 0, 0)),
                  pl.BlockSpec((1, 8, LANES), lambda b: (b, 0, 0))],
        out_specs=pl.BlockSpec((1, 2, rows, LANES), lambda b: (b, 0, 0, 0)),
        out_shape=jax.ShapeDtypeStruct((bsz, 2, rows, LANES), jnp.int32),
        compiler_params=_cparams("parallel"),
        name="moe_dest",
    )(meta, cnt)


def _expert_kernel(d1_ref, d2_ref, g1_ref, g2_ref, cnt_ref,
                   hf_hbm, x_hbm, gt_ref, w1_ref, w3_ref, w2_ref, out_hbm,
                   hf_v, acc_v, xb_v, yb_v, row_tok, row_gate, pstart, sem):
    b = pl.program_id(0)
    e = pl.program_id(1)
    n_e = pl.num_programs(1)
    t = hf_v.shape[0]
    grp = MOE_ROW_GROUP
    rows = xb_v.shape[0] * grp

    @pl.when(e == 0)
    def _():
        cp_h = pltpu.make_async_copy(hf_hbm.at[b], hf_v, sem.at[0])
        cp_x = pltpu.make_async_copy(x_hbm.at[b], acc_v, sem.at[1])
        cp_h.start()
        cp_x.start()
        xb_v[...] = jnp.zeros_like(xb_v)
        for i in range(grp):
            row_tok[2 * t + i] = 0

        def start_body(i, s):
            pstart[i] = s
            return s + cnt_ref[b * n_e + i]

        lax.fori_loop(0, n_e, start_body, 0)

        def tok_body(g, carry):
            for u in range(4):
                tok = g * 4 + u
                row_tok[d1_ref[tok]] = tok
                row_gate[d1_ref[tok]] = g1_ref[tok]
                row_tok[d2_ref[tok]] = tok
                row_gate[d2_ref[tok]] = g2_ref[tok]
            return carry

        lax.fori_loop(0, t // 4, tok_body, 0)
        cp_h.wait()
        cp_x.wait()

    cnt = cnt_ref[b * n_e + e]
    start = pstart[e]

    def block_body(j, carry):
        base = start + j * rows
        nrows = jnp.minimum(rows, cnt - j * rows)
        full_groups = nrows // grp

        def gather(g, c):
            first = base + g * grp
            for u in range(grp):
                xb_v[g, u:u + 1, :] = hf_v[pl.ds(row_tok[first + u], 1), :]
            return c

        lax.fori_loop(0, (nrows + grp - 1) // grp, gather, 0)
        xb = xb_v[...].reshape(rows, xb_v.shape[2]).astype(BF16)
        h1 = _dot(xb, w1_ref[0, 0].astype(BF16))
        h3 = _dot(xb, w3_ref[0, 0].astype(BF16))
        act = (_silu(h1) * h3).astype(BF16)
        yb_v[...] = (_dot(act, w2_ref[0, 0].astype(BF16)) * gt_ref[0]).reshape(yb_v.shape)

        def scatter(g, c):
            first = base + g * grp
            toks = [row_tok[first + u] for u in range(grp)]
            vals = [acc_v[pl.ds(toks[u], 1), :] + row_gate[first + u] * yb_v[g, u:u + 1, :]
                    for u in range(grp)]
            for u in range(grp):
                acc_v[pl.ds(toks[u], 1), :] = vals[u]
            return c

        lax.fori_loop(0, full_groups, scatter, 0)

        def scatter_tail(i, c):
            tok = row_tok[base + i]
            acc_v[pl.ds(tok, 1), :] = (acc_v[pl.ds(tok, 1), :] + row_gate[base + i]
                                       * yb_v[i // grp, pl.ds(i % grp, 1), :])
            return c

        lax.fori_loop(full_groups * grp, nrows, scatter_tail, 0)
        return carry

    lax.fori_loop(0, (cnt + rows - 1) // rows, block_body, 0)

    @pl.when(e == n_e - 1)
    def _():
        cp_o = pltpu.make_async_copy(acc_v, out_hbm.at[b], sem.at[2])
        cp_o.start()
        cp_o.wait()


def moe_experts(hf, x, gt, d1, d2, g1, g2, counts, layer, w1, w3, w2):
    bsz, t, d = x.shape
    _, n_e, _, f = w1.shape
    smem_tok = pl.BlockSpec((t,), lambda b, e: (b,), memory_space=pltpu.SMEM)
    return pl.pallas_call(
        _expert_kernel,
        grid=(bsz, n_e),
        in_specs=[smem_tok] * 4 + [
            pl.BlockSpec(memory_space=pltpu.SMEM),
            pl.BlockSpec(memory_space=pl.ANY),
            pl.BlockSpec(memory_space=pl.ANY),
            pl.BlockSpec((1, 1, d), lambda b, e: (b, 0, 0)),
            pl.BlockSpec((1, 1, d, f), lambda b, e: (layer, e, 0, 0)),
            pl.BlockSpec((1, 1, d, f), lambda b, e: (layer, e, 0, 0)),
            pl.BlockSpec((1, 1, f, d), lambda b, e: (layer, e, 0, 0))],
        out_specs=pl.BlockSpec(memory_space=pl.ANY),
        out_shape=jax.ShapeDtypeStruct((bsz, t, d), F32),
        scratch_shapes=[pltpu.VMEM((t, d), F32), pltpu.VMEM((t, d), F32),
                        pltpu.VMEM((MOE_ROWS // MOE_ROW_GROUP, MOE_ROW_GROUP, d), F32),
                        pltpu.VMEM((MOE_ROWS // MOE_ROW_GROUP, MOE_ROW_GROUP, d), F32),
                        pltpu.SMEM((2 * t + MOE_ROW_GROUP,), jnp.int32),
                        pltpu.SMEM((2 * t + MOE_ROW_GROUP,), F32),
                        pltpu.SMEM((n_e,), jnp.int32), pltpu.SemaphoreType.DMA((3,))],
        compiler_params=_cparams("arbitrary", "arbitrary"),
        name="moe_experts",
    )(d1, d2, g1, g2, counts, hf, x, gt, w1, w3, w2)


def moe_layer(x, nw, sh, sc, gate, group_w, group_b, expert_w, expert_b, layer, w1, w3, w2):
    bsz, t, d = x.shape
    hf, meta, cnt = moe_router(x, nw, sh, sc, group_w, group_b, expert_w, expert_b)
    dest = moe_dest(meta, cnt)
    d1, d2 = [dest[:, k].reshape(bsz * t) for k in range(2)]
    g1, g2 = [meta[:, f].reshape(bsz * t) for f in (4, 5)]
    counts = cnt[:, 0, ROUTER_EXPERT_LANE0:ROUTER_EXPERT_LANE0 + MOE_EXPERTS].astype(jnp.int32)
    return moe_experts(hf, x, gate, d1, d2, g1, g2, counts.reshape(-1), layer, w1, w3, w2)


def _final_norm_kernel(x_ref, w_ref, o_ref):
    x = x_ref[0]
    o_ref[0] = x * lax.rsqrt(jnp.mean(x * x, axis=-1, keepdims=True) + NORM_EPS) * w_ref[...]


def final_rms_norm(x, w, *, tm=1024):
    bsz, t, d = x.shape
    tm = min(tm, t)
    row = pl.BlockSpec((1, tm, d), lambda b, i: (b, i, 0))
    return pl.pallas_call(
        _final_norm_kernel,
        grid=(bsz, t // tm),
        in_specs=[row, pl.BlockSpec((1, d), lambda b, i: (0, 0))],
        out_specs=row,
        out_shape=jax.ShapeDtypeStruct((bsz, t, d), F32),
        compiler_params=_cparams("parallel", "parallel"),
        name="final_rms_norm",
    )(x, w.reshape(1, d))


def kernel(x, c, ada_w, ada_b, norm_mix_w, norm_ffn_w, ssm_in_w, ssm_conv_w, ssm_conv_b, ssm_dt_bias, ssm_a_log, ssm_d, ssm_norm_w, ssm_out_w, rwkv_mix, rwkv_rkv_w, rwkv_w0, rwkv_w1, rwkv_w2, rwkv_a0, rwkv_a1, rwkv_a2, rwkv_g1, rwkv_g2, rwkv_k_k, rwkv_k_a, rwkv_r_k, rwkv_ln_w, rwkv_ln_b, rwkv_out_w, ret_in_w, ret_out_w, att_in_w, att_out_w, moe_group_w, moe_group_b, moe_expert_w, moe_expert_b, moe_w1, moe_w3, moe_w2, final_norm_w):
    depth = ada_w.shape[0]
    d = x.shape[-1]
    mod = ada_modulation(c, ada_w, ada_b)
    for i in range(depth):
        sh1, sc1, gt1, sh2, sc2, gt2 = [mod[i][:, None, j * d:(j + 1) * d] for j in range(6)]
        kind, j = i % 4, i // 4
        pre = (x, norm_mix_w[i], sh1, sc1, gt1)
        if kind == 0:
            x = mamba2_layer(*pre, ssm_in_w[j], ssm_conv_w[j], ssm_conv_b[j], ssm_dt_bias[j],
                             ssm_a_log[j], ssm_d[j], ssm_norm_w[j], ssm_out_w[j])
        elif kind == 1:
            x = rwkv7_layer(*pre, rwkv_mix[j], rwkv_rkv_w[j], rwkv_w0[j], rwkv_w1[j], rwkv_w2[j],
                            rwkv_a0[j], rwkv_a1[j], rwkv_a2[j], rwkv_g1[j], rwkv_g2[j], rwkv_k_k[j],
                            rwkv_k_a[j], rwkv_r_k[j], rwkv_ln_w[j], rwkv_ln_b[j], rwkv_out_w[j])
        elif kind == 2:
            x = retention_layer(*pre, ret_in_w[j], ret_out_w[j])
        else:
            x = attention_layer(*pre, att_in_w[j], att_out_w[j])
        x = moe_layer(x, norm_ffn_w[i], sh2, sc2, gt2, moe_group_w[i], moe_group_b[i],
                      moe_expert_w[i], moe_expert_b[i], i, moe_w1, moe_w3, moe_w2)
    return final_rms_norm(x, final_norm_w)
```

```python
import functools
import math

import jax
import jax.numpy as jnp
from jax import lax
from jax.experimental import pallas as pl
from jax.experimental.pallas import tpu as pltpu

F32 = jnp.float32
BF16 = jnp.bfloat16
HI = lax.Precision.HIGHEST

NORM_EPS = 1e-6
LANES = 128
VMEM_LIMIT = 56 * 1024 * 1024
PROJ_TM = 2048
PROJ_TN = 1024

SSM_HEAD_DIM = 64
SSM_N_GROUPS = 8
SSM_HEADS_PER_GROUP = 4
SSM_D_STATE = 128
SSM_CONV = 4
SSM_CHUNK = 128
SSM_CONV_COLS = 256
RWKV_HEAD_DIM = 64
RWKV_LN_EPS = 64e-5
RWKV_CHUNK = 128
RWKV_SUB = 16
RWKV_INV_PRECISION = None
RWKV_APPLY_PRECISION = None
RET_N_HEADS = 4
RET_CHUNK = 128
ATT_GROUPS = ((128, 1), (512, 4), (2048, 16))
ATT_HEADS_PER_GROUP = 8
ATT_HEAD_DIM = 128
ATT_BLOCK = 128
MOE_GROUPS = 4
MOE_EXPERTS_PER_GROUP = 8
MOE_EXPERTS = 32
MOE_ROWS = 320
MOE_ROW_GROUP = 8


def _cparams(*sem):
    return pltpu.CompilerParams(dimension_semantics=sem, vmem_limit_bytes=VMEM_LIMIT)


def _sigmoid(x):
    return 1.0 / (1.0 + jnp.exp(-x))


def _silu(x):
    return x * _sigmoid(x)


def _softplus(x):
    return jnp.maximum(x, 0.0) + jnp.log(1.0 + jnp.exp(-jnp.abs(x)))


def _dot(a, b, precision=None):
    return jnp.dot(a, b, preferred_element_type=F32, precision=precision)


def _dot_nt(a, b, precision=None):
    return lax.dot_general(a, b, (((1,), (1,)), ((), ())), preferred_element_type=F32,
                           precision=precision)


def _dot_tn(a, b, precision=None):
    return lax.dot_general(a, b, (((0,), (0,)), ((), ())), preferred_element_type=F32,
                           precision=precision)


def _split_bf16(a):
    hi = a.astype(BF16)
    return hi, (a - hi.astype(F32)).astype(BF16)


def _dot_split(a, m, transpose_rhs=False, lhs_exact=False):
    hi, lo = _split_bf16(a)
    mb = m.astype(BF16)
    if lhs_exact:
        return _dot(mb, hi) + _dot(mb, lo)
    dot = _dot_nt if transpose_rhs else _dot
    return dot(hi, mb) + dot(lo, mb)


def _dot_3pass(a, b):
    a_hi, a_lo = _split_bf16(a)
    b_hi, b_lo = _split_bf16(b)
    return _dot(a_hi, b_hi) + (_dot(a_lo, b_hi) + _dot(a_hi, b_lo))


def _iota2(shape, axis):
    return lax.broadcasted_iota(jnp.int32, shape, axis)


def _ada_kernel(c_ref, w_ref, b_ref, o_ref):
    cs = _silu(c_ref[...])
    o_ref[0] = _dot(cs, w_ref[0], HI) + b_ref[0]


def ada_modulation(c, ada_w, ada_b):
    depth, d, n = ada_w.shape
    bsz = c.shape[0]
    rows = -(-bsz // 8) * 8
    cp = jnp.pad(c, ((0, rows - bsz), (0, 0)))
    tn = 1536
    out = pl.pallas_call(
        _ada_kernel,
        grid=(depth, n // tn),
        in_specs=[pl.BlockSpec((rows, d), lambda l, j: (0, 0)),
                  pl.BlockSpec((1, d, tn), lambda l, j: (l, 0, j)),
                  pl.BlockSpec((1, 1, tn), lambda l, j: (l, 0, j))],
        out_specs=pl.BlockSpec((1, rows, tn), lambda l, j: (l, 0, j)),
        out_shape=jax.ShapeDtypeStruct((depth, rows, n), F32),
        compiler_params=_cparams("parallel", "parallel"),
        name="ada_modulation",
    )(cp, ada_w, ada_b.reshape(depth, 1, n))
    return out[:, :bsz]


def _norm_mod(x, nw, sh, sc):
    y = x * lax.rsqrt(jnp.mean(x * x, axis=-1, keepdims=True) + NORM_EPS) * nw
    return y * (1.0 + sc) + sh


def _nmm_kernel(*refs, dil, has_side):
    if has_side:
        x_ref, nw_ref, sh_ref, sc_ref, w_ref, ws_ref, o_ref, side_ref, h_ref, *rest = refs
    else:
        x_ref, nw_ref, sh_ref, sc_ref, w_ref, o_ref, h_ref, *rest = refs

    @pl.when(pl.program_id(2) == 0)
    def _():
        h = _norm_mod(x_ref[0], nw_ref[...], sh_ref[0], sc_ref[0]).astype(h_ref.dtype)
        h_ref[...] = h
        if has_side:
            w_hi, w_lo = _split_bf16(ws_ref[...])
            side_ref[0] = _dot(h, w_hi) + _dot(h, w_lo)

    res = _dot(h_ref[...], w_ref[...])
    if dil is None:
        o_ref[0] = res.astype(o_ref.dtype)
    else:
        acc_ref, = rest
        rows = acc_ref.shape[1] // dil
        for c in range(acc_ref.shape[0]):
            acc_ref[c] = res[:, c * LANES:(c + 1) * LANES]
        for r in range(dil):
            for c in range(acc_ref.shape[0]):
                o_ref[0, r, :, c * LANES:(c + 1) * LANES] = (
                    acc_ref[c, pl.ds(r, rows, stride=dil), :].astype(o_ref.dtype))


def norm_mod_matmul(x, nw, sh, sc, w, *, tm, tn, dil=None, side_w=None):
    bsz, t, d = x.shape
    n = w.shape[1]
    tm = min(tm, t)
    scratch = [pltpu.VMEM((tm, d), BF16)]
    in_specs = [pl.BlockSpec((1, tm, d), lambda b, i, j: (b, i, 0)),
                pl.BlockSpec((1, d), lambda b, i, j: (0, 0)),
                pl.BlockSpec((1, 1, d), lambda b, i, j: (b, 0, 0)),
                pl.BlockSpec((1, 1, d), lambda b, i, j: (b, 0, 0)),
                pl.BlockSpec((d, tn), lambda b, i, j: (0, j))]
    ins = [x, nw.reshape(1, d), sh, sc, w]
    if dil is None:
        out_specs = [pl.BlockSpec((1, tm, tn), lambda b, i, j: (b, i, j))]
        out_shape = [jax.ShapeDtypeStruct((bsz, t, n), BF16)]
    else:
        out_specs = [pl.BlockSpec((1, dil, tm // dil, tn), lambda b, i, j: (b, 0, i, j))]
        out_shape = [jax.ShapeDtypeStruct((bsz, dil, t // dil, n), BF16)]
        scratch.append(pltpu.VMEM((tn // LANES, tm, LANES), F32))
    if side_w is not None:
        in_specs.append(pl.BlockSpec((d, LANES), lambda b, i, j: (0, 0)))
        ins.append(side_w)
        out_specs.append(pl.BlockSpec((1, tm, LANES), lambda b, i, j: (b, i, 0)))
        out_shape.append(jax.ShapeDtypeStruct((bsz, t, LANES), F32))
    outs = pl.pallas_call(
        functools.partial(_nmm_kernel, dil=dil, has_side=side_w is not None),
        grid=(bsz, t // tm, n // tn),
        in_specs=in_specs,
        out_specs=out_specs,
        out_shape=out_shape,
        scratch_shapes=scratch,
        compiler_params=_cparams("parallel", "parallel", "arbitrary"),
        name="norm_mod_matmul",
    )(*ins)
    return outs if side_w is not None else outs[0]


def _mgr_kernel(*refs, has_mul):
    if has_mul:
        y_ref, g_ref, w_ref, x_ref, gate_ref, o_ref = refs
        y = (y_ref[0].astype(F32) * g_ref[0].astype(F32)).astype(BF16)
    else:
        y_ref, w_ref, x_ref, gate_ref, o_ref = refs
        y = y_ref[0].astype(BF16)
    o_ref[0] = x_ref[0] + gate_ref[0] * _dot(y, w_ref[...])


def matmul_gated_residual(y, w, x, gate, mul=None, *, tm=512):
    bsz, t, k = y.shape
    d = w.shape[1]
    tm = min(tm, t)
    ins = [y] + ([mul] if mul is not None else []) + [w, x, gate]
    row = lambda width: pl.BlockSpec((1, tm, width), lambda b, i: (b, i, 0))
    specs = [row(k)] + ([row(k)] if mul is not None else []) + [
        pl.BlockSpec((k, d), lambda b, i: (0, 0)), row(d),
        pl.BlockSpec((1, 1, d), lambda b, i: (b, 0, 0))]
    return pl.pallas_call(
        functools.partial(_mgr_kernel, has_mul=mul is not None),
        grid=(bsz, t // tm),
        in_specs=specs,
        out_specs=row(d),
        out_shape=jax.ShapeDtypeStruct((bsz, t, d), F32),
        compiler_params=_cparams("parallel", "parallel"),
        name="matmul_gated_residual",
    )(*ins)


def _ssd_kernel(zx_ref, dt_ref, cw_ref, cb_ref, dtb_ref, alog_ref, dsk_ref, nw_ref, o_ref,
                prev_ref, xbc_ref, state_ref):
    q = SSM_CHUNK
    g_n, hg, p, n = SSM_N_GROUPS, SSM_HEADS_PER_GROUP, SSM_HEAD_DIM, SSM_D_STATE
    d_inner = g_n * hg * p
    nh = g_n * hg
    gw = hg * p

    @pl.when(pl.program_id(1) == 0)
    def _():
        prev_ref[...] = jnp.zeros_like(prev_ref)
        state_ref[...] = jnp.zeros_like(state_ref)

    dt = _softplus(dt_ref[0][:, :nh] + dtb_ref[...])
    da = dt * -jnp.exp(alog_ref[...])
    tril = (_iota2((q, q), 0) >= _iota2((q, q), 1))
    acum = _dot_split(da, tril.astype(F32), lhs_exact=True)
    a_end = acum[q - 1:q, :]
    chunk_decay = jnp.exp(a_end)
    adj_t = (acum - jnp.log(dt)).T
    wdec_t = (dt * jnp.exp(a_end - acum)).T

    src = _iota2((q, 2 * q), 1) - q - _iota2((q, 2 * q), 0)
    shifts = [jnp.where(src == -s, 1.0, 0.0).astype(BF16) for s in range(1, SSM_CONV)]
    cdim = xbc_ref.shape[1]
    for c0 in range(0, cdim, SSM_CONV_COLS):
        cs = slice(c0, c0 + SSM_CONV_COLS)
        cur = zx_ref[0, :, d_inner + c0:d_inner + c0 + SSM_CONV_COLS]
        ext = jnp.concatenate([prev_ref[:, cs], cur], axis=0)
        prev_ref[:, cs] = cur
        acc = cb_ref[:, cs] + cur.astype(F32) * cw_ref[SSM_CONV - 1:SSM_CONV, cs]
        for s in range(1, SSM_CONV):
            acc = acc + _dot(shifts[s - 1], ext) * cw_ref[SSM_CONV - 1 - s:SSM_CONV - s, cs]
        xbc_ref[:, cs] = _silu(acc)

    for g in range(g_n):
        xs_g = xbc_ref[:, g * gw:(g + 1) * gw]
        bm = xbc_ref[:, d_inner + g * n:d_inner + (g + 1) * n]
        cm = xbc_ref[:, d_inner + g_n * n + g * n:d_inner + g_n * n + (g + 1) * n]
        cm_b = cm.astype(BF16)
        cb = _dot_nt(cm_b, bm.astype(BF16))
        bm_t = bm.T
        xs_b = xs_g.astype(BF16)
        st_old = state_ref[g]
        y_carry = _dot(cm_b, st_old.astype(BF16))
        hd = [g * hg + hh for hh in range(hg)]
        hs = [slice(hh * p, (hh + 1) * p) for hh in range(hg)]
        a_col = [jnp.broadcast_to(acum[:, h:h + 1], (q, q)) for h in hd]
        m_h = [cb * jnp.exp(jnp.where(tril, ac - adj_t[h:h + 1, :], -jnp.inf))
               for h, ac in zip(hd, a_col)]
        st_new = [_dot((bm_t * wdec_t[h:h + 1, :]).astype(BF16), xs_b[:, c])
                  for h, c in zip(hd, hs)]
        ys = [_dot(m.astype(BF16), xs_b[:, c]) + jnp.exp(ac[:, :p]) * y_carry[:, c]
              for m, ac, c in zip(m_h, a_col, hs)]
        state_ref[g] = jnp.concatenate(
            [st_old[:, c] * chunk_decay[:, h:h + 1] + sn for h, c, sn in zip(hd, hs, st_new)], axis=1)
        y = jnp.concatenate(ys, axis=1) + xs_g * dsk_ref[:, g * gw:(g + 1) * gw]
        z = zx_ref[0, :, g * gw:(g + 1) * gw].astype(F32)
        y = y * _silu(z)
        y = y * lax.rsqrt(jnp.mean(y * y, axis=-1, keepdims=True) + NORM_EPS)
        o_ref[0, :, g * gw:(g + 1) * gw] = (y * nw_ref[:, g * gw:(g + 1) * gw]).astype(o_ref.dtype)


def ssd_core(zx, dt_raw, conv_w, conv_b, dt_bias, a_log, d_skip, norm_w):
    bsz, t, width = zx.shape
    d_inner = SSM_N_GROUPS * SSM_HEADS_PER_GROUP * SSM_HEAD_DIM
    cdim = width - d_inner
    nh = dt_bias.shape[0]
    q = SSM_CHUNK
    full = lambda shape: pl.BlockSpec(shape, lambda b, c: (0,) * len(shape))
    return pl.pallas_call(
        _ssd_kernel,
        grid=(bsz, t // q),
        in_specs=[pl.BlockSpec((1, q, width), lambda b, c: (b, c, 0)),
                  pl.BlockSpec((1, q, LANES), lambda b, c: (b, c, 0)),
                  full((SSM_CONV, cdim)), full((1, cdim)), full((1, nh)), full((1, nh)),
                  full((1, d_inner)), full((1, d_inner))],
        out_specs=pl.BlockSpec((1, q, d_inner), lambda b, c: (b, c, 0)),
        out_shape=jax.ShapeDtypeStruct((bsz, t, d_inner), BF16),
        scratch_shapes=[pltpu.VMEM((q, cdim), BF16), pltpu.VMEM((q, cdim), F32),
                        pltpu.VMEM((SSM_N_GROUPS, SSM_D_STATE,
                                    SSM_HEADS_PER_GROUP * SSM_HEAD_DIM), F32)],
        compiler_params=_cparams("parallel", "arbitrary"),
        name="ssd_core",
    )(zx, dt_raw, conv_w, conv_b.reshape(1, cdim), dt_bias.reshape(1, nh),
      a_log.reshape(1, nh), jnp.repeat(d_skip, SSM_HEAD_DIM).reshape(1, d_inner),
      norm_w.reshape(1, d_inner))


def mamba2_layer(x, nw, sh, sc, gate, in_w, conv_w, conv_b, dt_bias, a_log, d_skip, norm_w, out_w):
    d_inner = SSM_N_GROUPS * SSM_HEADS_PER_GROUP * SSM_HEAD_DIM
    cdim = conv_w.shape[1]
    nh = dt_bias.shape[0]
    w_main = in_w[:, :d_inner + cdim].astype(BF16)
    w_dt = jnp.pad(in_w[:, d_inner + cdim:], ((0, 0), (0, LANES - nh)))
    zx, dt_raw = norm_mod_matmul(x, nw, sh, sc, w_main, tm=PROJ_TM, tn=PROJ_TN, side_w=w_dt)
    y = ssd_core(zx, dt_raw, conv_w, conv_b, dt_bias, a_log, d_skip, norm_w)
    return matmul_gated_residual(y, out_w.astype(BF16), x, gate)


def _head_indicator(d, n):
    return jnp.where(_iota2((d, LANES), 0) // n == _iota2((d, LANES), 1), 1.0, 0.0)


def _rwkv_prep_kernel(x_ref, xp_ref, nw_ref, sh_ref, sc_ref, mix_ref, wr_ref, wk_ref, wv_ref,
                      w0_ref, w1_ref, w2_ref, a0_ref, a1_ref, a2_ref, g1_ref, g2_ref, kk_ref, ka_ref,
                      r_o, lw_o, k_o, v_o, kn_o, b_o, g_o):
    tm, d = x_ref.shape[1], x_ref.shape[2]
    nw, sh, sc = nw_ref[...], sh_ref[0], sc_ref[0]
    hm = _norm_mod(x_ref[0], nw, sh, sc)
    prev = _norm_mod(xp_ref[0], nw, sh, sc)[7:8, :]
    prev = jnp.where(pl.program_id(1) > 0, prev, 0.0)
    shifted = jnp.where(_iota2((tm, d), 0) == 0, prev, pltpu.roll(hm, 1, axis=0))
    xx = shifted - hm
    mixed = lambda j: (hm + xx * mix_ref[j:j + 1, :]).astype(BF16)
    xr, xw, xk, xv, xa, xg = [mixed(j) for j in range(6)]
    r = _dot(xr, wr_ref[...])
    k = _dot(xk, wk_ref[...])
    v = _dot(xv, wv_ref[...])
    lora = lambda u, w: _dot(u.astype(BF16), w[...])
    w_log = -_softplus(-(w0_ref[...] + lora(jnp.tanh(lora(xw, w1_ref)), w2_ref))) - 0.5
    a = _sigmoid(a0_ref[...] + lora(lora(xa, a1_ref), a2_ref))
    g = lora(_sigmoid(lora(xg, g1_ref)), g2_ref)
    kk = k * kk_ref[...]
    ind = _head_indicator(d, RWKV_HEAD_DIM)
    nrm = jnp.maximum(jnp.sqrt(_dot_split(kk * kk, ind)), 1e-12)
    kn = kk * _dot_split(1.0 / nrm, ind, transpose_rhs=True)
    r_o[0] = r.astype(r_o.dtype)
    lw_o[0] = -jnp.exp(w_log)
    k_o[0] = (k * (1.0 + (a - 1.0) * ka_ref[...])).astype(k_o.dtype)
    v_o[0] = v.astype(v_o.dtype)
    kn_o[0] = kn.astype(kn_o.dtype)
    b_o[0] = (kn * a).astype(b_o.dtype)
    g_o[0] = g.astype(g_o.dtype)


def rwkv_prep(x, nw, sh, sc, mix, rkv_w, w0, w1, w2, a0, a1, a2, g1, g2, k_k, k_a, *, tm=512):
    bsz, t, d = x.shape
    tm = min(tm, t)
    full = lambda arr: pl.BlockSpec(arr.shape, lambda b, i: (0,) * arr.ndim)
    row = pl.BlockSpec((1, tm, d), lambda b, i: (b, i, 0))
    vec = pl.BlockSpec((1, 1, d), lambda b, i: (b, 0, 0))
    bf = lambda w: w.astype(BF16)
    params = [mix, bf(rkv_w[0]), bf(rkv_w[1]), bf(rkv_w[2]), w0.reshape(1, d), bf(w1), bf(w2),
              a0.reshape(1, d), bf(a1), bf(a2), bf(g1), bf(g2), k_k.reshape(1, d), k_a.reshape(1, d)]
    prev_spec = pl.BlockSpec((1, 8, d), lambda b, i: (b, jnp.maximum(i * (tm // 8) - 1, 0), 0))
    return pl.pallas_call(
        _rwkv_prep_kernel,
        grid=(bsz, t // tm),
        in_specs=[row, prev_spec, pl.BlockSpec((1, d), lambda b, i: (0, 0)), vec, vec]
        + [full(p) for p in params],
        out_specs=[row] * 7,
        out_shape=[jax.ShapeDtypeStruct((bsz, t, d), F32 if i == 1 else BF16) for i in range(7)],
        compiler_params=_cparams("parallel", "parallel"),
        name="rwkv_prep",
    )(x, x, nw.reshape(1, d), sh, sc, *params)


def _mm(a, b, precision):
    if precision is None:
        return _dot(a.astype(BF16), b.astype(BF16))
    return _dot(a, b, precision)


def _mm_nt(a, b, precision):
    if precision is None:
        return _dot_nt(a.astype(BF16), b.astype(BF16))
    return _dot_nt(a, b, precision)


def _mm_tn(a, b, precision):
    if precision is None:
        return _dot_tn(a.astype(BF16), b.astype(BF16))
    return _dot_tn(a, b, precision)


def _unit_lower_inverse(m_list, length, sub, precision):
    ri = _iota2((length, length), 0)
    ci = _iota2((length, length), 1)
    eye = jnp.where(ri == ci, 1.0, 0.0)
    same_block = ri // sub == ci // sub
    mm = lambda a, b: _mm(a, b, precision)
    md = [jnp.where(same_block, m, 0.0) for m in m_list]
    off = [m - d for m, d in zip(m_list, md)]
    dinv = [eye + d for d in md]
    pw = md
    for _ in range(int(math.log2(sub)) - 1):
        pw = [mm(p, p) for p in pw]
        dinv = [x + mm(x, p) for x, p in zip(dinv, pw)]
    pm = [mm(x, o) for x, o in zip(dinv, off)]
    acc = [eye + p for p in pm]
    pw = pm
    for _ in range(int(math.log2(length // sub)) - 1):
        pw = [mm(p, p) for p in pw]
        acc = [x + mm(x, p) for x, p in zip(acc, pw)]
    return [mm(x, d) for x, d in zip(acc, dinv)]


def _rwkv_core_kernel(r_ref, lw_ref, k_ref, v_ref, kn_ref, b_ref, rk_ref, lnw_ref, lnb_ref, o_ref,
                      state_ref, *, inv_precision, apply_precision):
    length, d = r_ref.shape[1], r_ref.shape[2]
    n = RWKV_HEAD_DIM
    heads = range(d // n)
    mm = lambda a, b: _mm(a, b, apply_precision)

    @pl.when(pl.program_id(1) == 0)
    def _():
        state_ref[...] = jnp.zeros_like(state_ref)

    ri = _iota2((2 * length, 2 * length), 0)
    ci = _iota2((2 * length, 2 * length), 1) % length
    keep = ((ri < length) & (ri > ci)) | ((ri >= length) & (ri - length >= ci))
    eye_n = _iota2((n, n), 0) == _iota2((n, n), 1)

    lw = lw_ref[0]
    r, k, v, kn, b = [ref[0].astype(F32) for ref in (r_ref, k_ref, v_ref, kn_ref, b_ref)]
    tri = jnp.where(_iota2((length, length), 0) >= _iota2((length, length), 1), 1.0, 0.0)
    cum = _dot_split(lw, tri, lhs_exact=True)
    c_end = cum[length - 1:length, :]
    e_neg = jnp.exp(-cum)
    e_rem = jnp.exp(c_end - cum)
    ar = jnp.concatenate([-kn * jnp.exp(cum - lw), r * jnp.exp(cum)], axis=0)
    bk = jnp.concatenate([b * e_neg, k * e_neg], axis=0)
    bk_end = jnp.concatenate([b * e_rem, k * e_rem], axis=0)
    w_end = jnp.exp(c_end)
    bonus_w = r * k * rk_ref[...]
    hs = [slice(h * n, (h + 1) * n) for h in heads]

    s0 = [state_ref[h] for h in heads]
    p = [jnp.where(keep, _mm_nt(ar[:, c], bk[:, c], inv_precision), 0.0) for c in hs]
    t_inv = _unit_lower_inverse([x[:length, :length] for x in p], length, RWKV_SUB, inv_precision)
    q = [mm(ar[:, c], s) for c, s in zip(hs, s0)]
    y = [x[:length] + mm(pp[:length, length:], v[:, c]) for x, pp, c in zip(q, p, hs)]
    u = [mm(ti, yy) for ti, yy in zip(t_inv, y)]
    uv = [jnp.concatenate([uu, v[:, c]], axis=0) for uu, c in zip(u, hs)]
    o = [x[length:] + mm(pp[length:], w) for x, pp, w in zip(q, p, uv)]
    upd = [_mm_tn(bk_end[:, c], w, apply_precision) for c, w in zip(hs, uv)]
    w_col = [jnp.sum(jnp.where(eye_n, w_end[:, c], 0.0), axis=1, keepdims=True) for c in hs]
    for h in heads:
        state_ref[h] = w_col[h] * s0[h] + upd[h]
    ind = _head_indicator(d, n)
    head_sum = lambda x: _dot_split(_dot_split(x, ind), ind, transpose_rhs=True)
    o_all = jnp.concatenate(o, axis=1)
    cen = o_all - head_sum(o_all) * (1.0 / n)
    var = head_sum(cen * cen) * (1.0 / n)
    y = cen * lax.rsqrt(var + RWKV_LN_EPS) * lnw_ref[...] + lnb_ref[...]
    o_ref[0] = (y + head_sum(bonus_w) * v).astype(o_ref.dtype)


def rwkv_core(r, lw, k, v, kn, b, r_k, ln_w, ln_b):
    bsz, t, d = r.shape
    length = RWKV_CHUNK
    n = RWKV_HEAD_DIM
    row = pl.BlockSpec((1, length, d), lambda bb, c: (bb, c, 0))
    vec = pl.BlockSpec((1, d), lambda bb, c: (0, 0))
    return pl.pallas_call(
        functools.partial(_rwkv_core_kernel, inv_precision=RWKV_INV_PRECISION,
                          apply_precision=RWKV_APPLY_PRECISION),
        grid=(bsz, t // length),
        in_specs=[row] * 6 + [vec] * 3,
        out_specs=row,
        out_shape=jax.ShapeDtypeStruct((bsz, t, d), BF16),
        scratch_shapes=[pltpu.VMEM((d // n, n, n), F32)],
        compiler_params=_cparams("parallel", "arbitrary"),
        name="rwkv_core",
    )(r, lw, k, v, kn, b, r_k.reshape(1, d), ln_w.reshape(1, d), ln_b.reshape(1, d))


def rwkv7_layer(x, nw, sh, sc, gate, mix, rkv_w, w0, w1, w2, a0, a1, a2, g1, g2, k_k, k_a, r_k,
                ln_w, ln_b, out_w):
    r, lw, k, v, kn, b, g = rwkv_prep(x, nw, sh, sc, mix, rkv_w, w0, w1, w2, a0, a1, a2, g1, g2,
                                      k_k, k_a)
    y = rwkv_core(r, lw, k, v, kn, b, r_k, ln_w, ln_b)
    return matmul_gated_residual(y, out_w.astype(BF16), x, gate, mul=g)


def _ret_kernel(qkvg_ref, idec_ref, qdec_ref, kdec_ref, cdec_ref, o_ref, state_ref, *, d_model):
    nh = RET_N_HEADS
    dk = d_model // nh
    dv = 2 * d_model // nh
    k_scale = dk ** -0.5

    @pl.when(pl.program_id(1) == 0)
    def _():
        state_ref[...] = jnp.zeros_like(state_ref)

    for h in range(nh):
        q_b = qkvg_ref[0, :, h * dk:(h + 1) * dk]
        k_f = qkvg_ref[0, :, d_model + h * dk:d_model + (h + 1) * dk].astype(F32) * k_scale
        v_b = qkvg_ref[0, :, 2 * d_model + h * dv:2 * d_model + (h + 1) * dv]
        g_f = qkvg_ref[0, :, 4 * d_model + h * dv:4 * d_model + (h + 1) * dv].astype(F32)
        scores = _dot_nt(q_b, k_f.astype(BF16)) * idec_ref[h]
        inner = _dot(scores.astype(BF16), v_b)
        r_old = state_ref[h]
        qd = (q_b.astype(F32) * qdec_ref[:, h:h + 1]).astype(BF16)
        cross = _dot(qd, r_old.astype(BF16))
        kd = (k_f * kdec_ref[:, h:h + 1]).astype(BF16)
        state_ref[h] = r_old * cdec_ref[:, h:h + 1] + _dot_tn(kd, v_b)
        o = inner + cross
        o = o * lax.rsqrt(jnp.mean(o * o, axis=-1, keepdims=True) + NORM_EPS)
        o_ref[0, :, h * dv:(h + 1) * dv] = (_silu(g_f) * o).astype(o_ref.dtype)


def retention_core(qkvg, d_model):
    bsz, t, width = qkvg.shape
    nh, q = RET_N_HEADS, RET_CHUNK
    dk, dv = d_model // nh, 2 * d_model // nh
    log_gamma = jnp.log(1 - jnp.exp2(-5.0 - jnp.arange(nh, dtype=F32)))
    idx = jnp.arange(q)
    rel = idx[:, None] - idx[None, :]
    inner_decay = jnp.where(rel >= 0,
                            jnp.exp(jnp.maximum(rel, 0).astype(F32) * log_gamma[:, None, None]), 0.0)
    q_decay = jnp.exp((idx + 1).astype(F32)[:, None] * log_gamma)
    k_decay = jnp.exp((q - 1 - idx).astype(F32)[:, None] * log_gamma)
    chunk_decay = jnp.exp(q * log_gamma).reshape(1, nh)
    full = lambda shape: pl.BlockSpec(shape, lambda b, c: (0,) * len(shape))
    return pl.pallas_call(
        functools.partial(_ret_kernel, d_model=d_model),
        grid=(bsz, t // q),
        in_specs=[pl.BlockSpec((1, q, width), lambda b, c: (b, c, 0)),
                  full((nh, q, q)), full((q, nh)), full((q, nh)), full((1, nh))],
        out_specs=pl.BlockSpec((1, q, nh * dv), lambda b, c: (b, c, 0)),
        out_shape=jax.ShapeDtypeStruct((bsz, t, nh * dv), BF16),
        scratch_shapes=[pltpu.VMEM((nh, dk, dv), F32)],
        compiler_params=_cparams("parallel", "arbitrary"),
        name="retention_core",
    )(qkvg, inner_decay, q_decay, k_decay, chunk_decay)


def retention_layer(x, nw, sh, sc, gate, in_w, out_w):
    d = x.shape[-1]
    qkvg = norm_mod_matmul(x, nw, sh, sc, in_w.astype(BF16), tm=PROJ_TM, tn=PROJ_TN)
    y = retention_core(qkvg, d)
    return matmul_gated_residual(y, out_w.astype(BF16), x, gate)


def _att_kernel(q_ref, k_ref, v_ref, sl_ref, o_ref, lse_ref, kprev_ref, vprev_ref, *, dil, span):
    w = ATT_BLOCK
    e = ATT_HEAD_DIM
    nblk = pl.program_id(2)

    @pl.when(nblk == 0)
    def _():
        kprev_ref[...] = jnp.zeros_like(kprev_ref)
        vprev_ref[...] = jnp.zeros_like(vprev_ref)

    qi = _iota2((w, 2 * w), 0)
    ci = _iota2((w, 2 * w), 1)
    delta = w + qi - ci
    mask = (delta >= 0) & (delta <= span) & ((nblk > 0) | (ci >= w))
    dist = (dil * delta).astype(F32)
    lane = _iota2((w, LANES), 1)
    heads = range(ATT_HEADS_PER_GROUP)
    cols = [slice(h * e, (h + 1) * e) for h in heads]
    s = [_dot_nt(q_ref[0, 0, :, c], jnp.concatenate([kprev_ref[:, c], k_ref[0, 0, :, c]], axis=0))
         for c in cols]
    s = [jnp.where(mask, x * (e ** -0.5) - sl_ref[:, h:h + 1] * dist, -jnp.inf)
         for h, x in zip(heads, s)]
    m = [jnp.max(x, axis=-1, keepdims=True) for x in s]
    p = [jnp.exp(x - mx) for x, mx in zip(s, m)]
    den = [jnp.sum(x, axis=-1, keepdims=True) for x in p]
    lse_all = jnp.zeros((w, LANES), F32)
    for h in heads:
        vv = jnp.concatenate([vprev_ref[:, cols[h]], v_ref[0, 0, :, cols[h]]], axis=0)
        o = _dot((p[h] / den[h]).astype(BF16), vv)
        o_ref[0, 0, :, cols[h]] = o.astype(o_ref.dtype)
        lse_all = jnp.where(lane == h, m[h] + jnp.log(den[h]), lse_all)
    lse_ref[0, 0] = lse_all
    kprev_ref[...] = k_ref[0, 0]
    vprev_ref[...] = v_ref[0, 0]


def dilated_group_core(qkv, gi, window, dil, slopes):
    bsz, _, ls, _ = qkv.shape
    w = ATT_BLOCK
    gw = ATT_HEADS_PER_GROUP * ATT_HEAD_DIM
    spec = lambda part: pl.BlockSpec((1, 1, w, gw), lambda b, r, n: (b, r, n, part))
    return pl.pallas_call(
        functools.partial(_att_kernel, dil=dil, span=window // dil),
        grid=(bsz, dil, ls // w),
        in_specs=[spec(0), spec(1), spec(2),
                  pl.BlockSpec((1, ATT_HEADS_PER_GROUP), lambda b, r, n: (0, 0))],
        out_specs=[pl.BlockSpec((1, 1, w, gw), lambda b, r, n: (b, r, n, 0)),
                   pl.BlockSpec((1, 1, w, LANES), lambda b, r, n: (b, r, n, 0))],
        out_shape=[jax.ShapeDtypeStruct((bsz, dil, ls, gw), BF16),
                   jax.ShapeDtypeStruct((bsz, dil, ls, LANES), F32)],
        scratch_shapes=[pltpu.VMEM((w, gw), BF16), pltpu.VMEM((w, gw), BF16)],
        compiler_params=_cparams("parallel", "parallel", "arbitrary"),
        name="dilated_attention_g%d" % gi,
    )(qkv, qkv, qkv, slopes.reshape(1, ATT_HEADS_PER_GROUP))


def _att_out_kernel(o0_ref, o1_ref, o2_ref, l0_ref, l1_ref, l2_ref, w_ref, x_ref, gate_ref, o_ref,
                    *scratch):
    tm = x_ref.shape[1]

    def token_major(ref, scr):
        dil = ref.shape[1]
        if dil == 1:
            return ref[0, 0].astype(F32)
        for r in range(dil):
            blk = ref[0, r].astype(F32)
            for c in range(scr.shape[0]):
                scr[c, pl.ds(r, tm // dil, stride=dil), :] = blk[:, c * LANES:(c + 1) * LANES]
        return jnp.concatenate([scr[c] for c in range(scr.shape[0])], axis=1)

    o1, l1, o2, l2 = [token_major(ref, scr) for ref, scr in
                      zip((o1_ref, l1_ref, o2_ref, l2_ref), scratch)]
    o0, l0 = token_major(o0_ref, None), token_major(l0_ref, None)
    m = jnp.maximum(jnp.maximum(l0, l1), l2)
    w0, w1, w2 = jnp.exp(l0 - m), jnp.exp(l1 - m), jnp.exp(l2 - m)
    inv = 1.0 / (w0 + w1 + w2)
    k = o0.shape[1]
    expand = jnp.where(_iota2((LANES, k), 1) // ATT_HEAD_DIM == _iota2((LANES, k), 0), 1.0, 0.0)
    y = (_dot_split(w0 * inv, expand) * o0 + _dot_split(w1 * inv, expand) * o1
         + _dot_split(w2 * inv, expand) * o2)
    o_ref[0] = x_ref[0] + gate_ref[0] * _dot(y.astype(BF16), w_ref[...])


def attention_combine_out(outs, lses, w, x, gate, *, tm=512):
    bsz, t, d = x.shape
    k = w.shape[0]
    tm = min(tm, t)
    row = pl.BlockSpec((1, tm, d), lambda b, i: (b, i, 0))
    res = lambda a: pl.BlockSpec((1, a.shape[1], tm // a.shape[1], a.shape[3]),
                                 lambda b, i: (b, 0, i, 0))
    wide, narrow = pltpu.VMEM((k // LANES, tm, LANES), F32), pltpu.VMEM((1, tm, LANES), F32)
    return pl.pallas_call(
        _att_out_kernel,
        grid=(bsz, t // tm),
        in_specs=[res(a) for a in outs] + [res(a) for a in lses] + [
            pl.BlockSpec((k, d), lambda b, i: (0, 0)), row,
            pl.BlockSpec((1, 1, d), lambda b, i: (b, 0, 0))],
        out_specs=row,
        out_shape=jax.ShapeDtypeStruct((bsz, t, d), F32),
        scratch_shapes=[wide, narrow, wide, narrow],
        compiler_params=_cparams("parallel", "parallel"),
        name="attention_combine_out",
    )(*outs, *lses, w, x, gate)


def attention_layer(x, nw, sh, sc, gate, in_w, out_w):
    n_groups = len(ATT_GROUPS)
    gw = ATT_HEADS_PER_GROUP * ATT_HEAD_DIM
    n_heads = n_groups * ATT_HEADS_PER_GROUP
    slopes = jnp.exp2(-8.0 * jnp.arange(1, n_heads + 1, dtype=F32) / n_heads)
    slopes = slopes.reshape(n_groups, ATT_HEADS_PER_GROUP)
    w_in = in_w.astype(BF16).reshape(in_w.shape[0], 3, n_groups, gw)
    outs, lses = [], []
    for gi, (window, dil) in enumerate(ATT_GROUPS):
        w_g = w_in[:, :, gi, :].reshape(in_w.shape[0], 3 * gw)
        qkv = norm_mod_matmul(x, nw, sh, sc, w_g, tm=PROJ_TM, tn=gw, dil=dil)
        o, lse = dilated_group_core(qkv, gi, window, dil, slopes[gi])
        outs.append(o)
        lses.append(lse)
    return attention_combine_out(outs, lses, out_w.astype(BF16), x, gate)


ROUTER_GROUP_LANE0 = 0
ROUTER_EXPERT_LANE0 = MOE_GROUPS
ROUTER_FIELDS = 6


def _router_kernel(x_ref, nw_ref, sh_ref, sc_ref, wr_ref, br_ref, hf_ref, meta_ref, cnt_ref,
                   carry_ref):
    tm = x_ref.shape[1]

    @pl.when(pl.program_id(1) == 0)
    def _():
        carry_ref[...] = jnp.zeros_like(carry_ref)

    hf = _norm_mod(x_ref[0], nw_ref[...], sh_ref[0], sc_ref[0])
    hf_ref[0] = hf
    logits = _dot_3pass(hf, wr_ref[...]) + br_ref[...]
    lane = _iota2((tm, LANES), 1)
    neg = -jnp.inf
    first = lambda hit: jnp.min(jnp.where(hit, lane, LANES), axis=-1, keepdims=True)

    gl = jnp.where(lane < MOE_GROUPS, logits, neg)
    gmax = jnp.max(gl, axis=-1, keepdims=True)
    gidx = first(gl == gmax)
    g_top = 1.0 / jnp.sum(jnp.exp(gl - gmax), axis=-1, keepdims=True)

    lo = ROUTER_EXPERT_LANE0 + gidx * MOE_EXPERTS_PER_GROUP
    el = jnp.where((lane >= lo) & (lane < lo + MOE_EXPERTS_PER_GROUP), logits, neg)
    m1 = jnp.max(el, axis=-1, keepdims=True)
    i1 = first(el == m1)
    el2 = jnp.where(lane == i1, neg, el)
    m2 = jnp.max(el2, axis=-1, keepdims=True)
    i2 = first(el2 == m2)
    ex = jnp.exp(m2 - m1)
    gate1 = g_top / (1.0 + ex)
    gate2 = g_top * ex / (1.0 + ex)

    hit1 = lane == i1
    hit2 = lane == i2
    onehot = jnp.where(hit1 | hit2, 1.0, 0.0).astype(BF16)
    strict = jnp.where(_iota2((tm, tm), 0) > _iota2((tm, tm), 1), 1.0, 0.0).astype(BF16)
    before = _dot(strict, onehot) + carry_ref[0:1, :]
    rank1 = jnp.sum(jnp.where(hit1, before, 0.0), axis=-1, keepdims=True)
    rank2 = jnp.sum(jnp.where(hit2, before, 0.0), axis=-1, keepdims=True)
    carry_ref[...] = carry_ref[...] + _dot(jnp.ones((8, tm), BF16), onehot)
    cnt_ref[0] = carry_ref[...]

    e1 = (i1 - ROUTER_EXPERT_LANE0).astype(F32)
    e2 = (i2 - ROUTER_EXPERT_LANE0).astype(F32)
    meta = jnp.zeros((tm, LANES), F32)
    for j, val in enumerate((e1, e2, rank1, rank2, gate1, gate2)):
        meta = jnp.where(lane == j, val, meta)
    for k in range(tm // LANES):
        fields = meta[k * LANES:(k + 1) * LANES, :].T
        for f in range(ROUTER_FIELDS):
            meta_ref[0, f, k:k + 1, :] = fields[f:f + 1, :]


def moe_router(x, nw, sh, sc, group_w, group_b, expert_w, expert_b, *, tm=8 * LANES):
    bsz, t, d = x.shape
    tm = min(tm, t)
    pad = LANES - MOE_GROUPS - MOE_EXPERTS
    wr = jnp.pad(jnp.concatenate([group_w, expert_w], axis=1), ((0, 0), (0, pad)))
    br = jnp.pad(jnp.concatenate([group_b, expert_b]), (0, pad)).reshape(1, LANES)
    return pl.pallas_call(
        _router_kernel,
        grid=(bsz, t // tm),
        in_specs=[pl.BlockSpec((1, tm, d), lambda b, i: (b, i, 0)),
                  pl.BlockSpec((1, d), lambda b, i: (0, 0)),
                  pl.BlockSpec((1, 1, d), lambda b, i: (b, 0, 0)),
                  pl.BlockSpec((1, 1, d), lambda b, i: (b, 0, 0)),
                  pl.BlockSpec((d, LANES), lambda b, i: (0, 0)),
                  pl.BlockSpec((1, LANES), lambda b, i: (0, 0))],
        out_specs=[pl.BlockSpec((1, tm, d), lambda b, i: (b, i, 0)),
                   pl.BlockSpec((1, ROUTER_FIELDS, tm // LANES, LANES), lambda b, i: (b, 0, i, 0)),
                   pl.BlockSpec((1, 8, LANES), lambda b, i: (b, 0, 0))],
        out_shape=[jax.ShapeDtypeStruct((bsz, t, d), F32),
                   jax.ShapeDtypeStruct((bsz, ROUTER_FIELDS, t // LANES, LANES), F32),
                   jax.ShapeDtypeStruct((bsz, 8, LANES), F32)],
        scratch_shapes=[pltpu.VMEM((8, LANES), F32)],
        compiler_params=_cparams("parallel", "arbitrary"),
        name="moe_router",
    )(x, nw.reshape(1, d), sh, sc, wr, br)


def _dest_kernel(meta_ref, cnt_ref, d_ref):
    upper = jnp.where(_iota2((LANES, LANES), 0) < _iota2((LANES, LANES), 1), 1.0, 0.0)
    first = _dot_split(cnt_ref[0], upper)
    for k in range(2):
        expert = meta_ref[0, k]
        off = jnp.zeros_like(expert)
        for x in range(MOE_EXPERTS):
            lane = ROUTER_EXPERT_LANE0 + x
            off = jnp.where(expert == x, first[0:1, lane:lane + 1], off)
        d_ref[0, k] = (meta_ref[0, 2 + k] + off).astype(jnp.int32)


def moe_dest(meta, cnt):
    bsz, nf, rows, _ = meta.shape
    return pl.pallas_call(
        _dest_kernel,
        grid=(bsz,),
        in_specs=[pl.BlockSpec((1, nf, rows, LANES), lambda b: (b, 0, 0, 0)),
                  pl.BlockSpec((1, 8, LANES), lambda b: (b, 0, 0))],
        out_specs=pl.BlockSpec((1, 2, rows, LANES), lambda b: (b, 0, 0, 0)),
        out_shape=jax.ShapeDtypeStruct((bsz, 2, rows, LANES), jnp.int32),
        compiler_params=_cparams("parallel"),
        name="moe_dest",
    )(meta, cnt)


def _expert_kernel(d1_ref, d2_ref, g1_ref, g2_ref, cnt_ref,
                   hf_hbm, x_hbm, gt_ref, w1_ref, w3_ref, w2_ref, *rest, final_norm):
    if final_norm:
        fw_ref, *rest = rest
    out_hbm, hf_v, acc_v, xb_v, yb_v, row_tok, row_gate, pstart, sem = rest
    b = pl.program_id(0)
    e = pl.program_id(1)
    n_e = pl.num_programs(1)
    t = hf_v.shape[0]
    grp = MOE_ROW_GROUP
    rows = xb_v.shape[0] * grp

    @pl.when(e == 0)
    def _():
        cp_h = pltpu.make_async_copy(hf_hbm.at[b], hf_v, sem.at[0])
        cp_x = pltpu.make_async_copy(x_hbm.at[b], acc_v, sem.at[1])
        cp_h.start()
        cp_x.start()
        xb_v[...] = jnp.zeros_like(xb_v)
        for i in range(grp):
            row_tok[2 * t + i] = 0

        def start_body(i, s):
            pstart[i] = s
            return s + cnt_ref[b * n_e + i]

        lax.fori_loop(0, n_e, start_body, 0)

        def tok_body(g, carry):
            for u in range(4):
                tok = g * 4 + u
                row_tok[d1_ref[tok]] = tok
                row_gate[d1_ref[tok]] = g1_ref[tok]
                row_tok[d2_ref[tok]] = tok
                row_gate[d2_ref[tok]] = g2_ref[tok]
            return carry

        lax.fori_loop(0, t // 4, tok_body, 0)
        cp_h.wait()
        cp_x.wait()

    cnt = cnt_ref[b * n_e + e]
    start = pstart[e]

    def block_body(j, carry):
        base = start + j * rows
        nrows = jnp.minimum(rows, cnt - j * rows)
        full_groups = nrows // grp

        def gather(g, c):
            first = base + g * grp
            for u in range(grp):
                xb_v[g, u:u + 1, :] = hf_v[pl.ds(row_tok[first + u], 1), :]
            return c

        lax.fori_loop(0, (nrows + grp - 1) // grp, gather, 0)
        xb = xb_v[...].reshape(rows, xb_v.shape[2]).astype(BF16)
        h1 = _dot(xb, w1_ref[0, 0].astype(BF16))
        h3 = _dot(xb, w3_ref[0, 0].astype(BF16))
        act = (_silu(h1) * h3).astype(BF16)
        yb_v[...] = (_dot(act, w2_ref[0, 0].astype(BF16)) * gt_ref[0]).reshape(yb_v.shape)

        def scatter(g, c):
            first = base + g * grp
            toks = [row_tok[first + u] for u in range(grp)]
            vals = [acc_v[pl.ds(toks[u], 1), :] + row_gate[first + u] * yb_v[g, u:u + 1, :]
                    for u in range(grp)]
            for u in range(grp):
                acc_v[pl.ds(toks[u], 1), :] = vals[u]
            return c

        lax.fori_loop(0, full_groups, scatter, 0)

        def scatter_tail(i, c):
            tok = row_tok[base + i]
            acc_v[pl.ds(tok, 1), :] = (acc_v[pl.ds(tok, 1), :] + row_gate[base + i]
                                       * yb_v[i // grp, pl.ds(i % grp, 1), :])
            return c

        lax.fori_loop(full_groups * grp, nrows, scatter_tail, 0)
        return carry

    lax.fori_loop(0, (cnt + rows - 1) // rows, block_body, 0)

    @pl.when(e == n_e - 1)
    def _():
        if final_norm:
            chunk = math.gcd(t, 256)

            def norm_body(i, carry):
                rs = pl.ds(pl.multiple_of(i * chunk, chunk), chunk)
                xr = acc_v[rs, :]
                acc_v[rs, :] = (xr * lax.rsqrt(jnp.mean(xr * xr, axis=-1, keepdims=True) + NORM_EPS)
                                * fw_ref[...])
                return carry

            lax.fori_loop(0, t // chunk, norm_body, 0)
        cp_o = pltpu.make_async_copy(acc_v, out_hbm.at[b], sem.at[2])
        cp_o.start()
        cp_o.wait()


def moe_experts(hf, x, gt, d1, d2, g1, g2, counts, layer, w1, w3, w2, final_w=None):
    bsz, t, d = x.shape
    _, n_e, _, f = w1.shape
    smem_tok = pl.BlockSpec((t,), lambda b, e: (b,), memory_space=pltpu.SMEM)
    extra_specs, extra = [], []
    if final_w is not None:
        extra_specs, extra = [pl.BlockSpec((1, d), lambda b, e: (0, 0))], [final_w.reshape(1, d)]
    return pl.pallas_call(
        functools.partial(_expert_kernel, final_norm=final_w is not None),
        grid=(bsz, n_e),
        in_specs=[smem_tok] * 4 + [
            pl.BlockSpec(memory_space=pltpu.SMEM),
            pl.BlockSpec(memory_space=pl.ANY),
            pl.BlockSpec(memory_space=pl.ANY),
            pl.BlockSpec((1, 1, d), lambda b, e: (b, 0, 0)),
            pl.BlockSpec((1, 1, d, f), lambda b, e: (layer, e, 0, 0)),
            pl.BlockSpec((1, 1, d, f), lambda b, e: (layer, e, 0, 0)),
            pl.BlockSpec((1, 1, f, d), lambda b, e: (layer, e, 0, 0))] + extra_specs,
        out_specs=pl.BlockSpec(memory_space=pl.ANY),
        out_shape=jax.ShapeDtypeStruct((bsz, t, d), F32),
        scratch_shapes=[pltpu.VMEM((t, d), F32), pltpu.VMEM((t, d), F32),
                        pltpu.VMEM((MOE_ROWS // MOE_ROW_GROUP, MOE_ROW_GROUP, d), F32),
                        pltpu.VMEM((MOE_ROWS // MOE_ROW_GROUP, MOE_ROW_GROUP, d), F32),
                        pltpu.SMEM((2 * t + MOE_ROW_GROUP,), jnp.int32),
                        pltpu.SMEM((2 * t + MOE_ROW_GROUP,), F32),
                        pltpu.SMEM((n_e,), jnp.int32), pltpu.SemaphoreType.DMA((3,))],
        compiler_params=_cparams("arbitrary", "arbitrary"),
        name="moe_experts",
    )(d1, d2, g1, g2, counts, hf, x, gt, w1, w3, w2, *extra)


def moe_layer(x, nw, sh, sc, gate, group_w, group_b, expert_w, expert_b, layer, w1, w3, w2,
              final_w=None):
    bsz, t, d = x.shape
    hf, meta, cnt = moe_router(x, nw, sh, sc, group_w, group_b, expert_w, expert_b)
    dest = moe_dest(meta, cnt)
    d1, d2 = [dest[:, k].reshape(bsz * t) for k in range(2)]
    g1, g2 = [meta[:, f].reshape(bsz * t) for f in (4, 5)]
    counts = cnt[:, 0, ROUTER_EXPERT_LANE0:ROUTER_EXPERT_LANE0 + MOE_EXPERTS].astype(jnp.int32)
    return moe_experts(hf, x, gate, d1, d2, g1, g2, counts.reshape(-1), layer, w1, w3, w2, final_w)


def kernel(x, c, ada_w, ada_b, norm_mix_w, norm_ffn_w, ssm_in_w, ssm_conv_w, ssm_conv_b, ssm_dt_bias, ssm_a_log, ssm_d, ssm_norm_w, ssm_out_w, rwkv_mix, rwkv_rkv_w, rwkv_w0, rwkv_w1, rwkv_w2, rwkv_a0, rwkv_a1, rwkv_a2, rwkv_g1, rwkv_g2, rwkv_k_k, rwkv_k_a, rwkv_r_k, rwkv_ln_w, rwkv_ln_b, rwkv_out_w, ret_in_w, ret_out_w, att_in_w, att_out_w, moe_group_w, moe_group_b, moe_expert_w, moe_expert_b, moe_w1, moe_w3, moe_w2, final_norm_w):
    depth = ada_w.shape[0]
    d = x.shape[-1]
    mod = ada_modulation(c, ada_w, ada_b)
    for i in range(depth):
        sh1, sc1, gt1, sh2, sc2, gt2 = [mod[i][:, None, j * d:(j + 1) * d] for j in range(6)]
        kind, j = i % 4, i // 4
        pre = (x, norm_mix_w[i], sh1, sc1, gt1)
        if kind == 0:
            x = mamba2_layer(*pre, ssm_in_w[j], ssm_conv_w[j], ssm_conv_b[j], ssm_dt_bias[j],
                             ssm_a_log[j], ssm_d[j], ssm_norm_w[j], ssm_out_w[j])
        elif kind == 1:
            x = rwkv7_layer(*pre, rwkv_mix[j], rwkv_rkv_w[j], rwkv_w0[j], rwkv_w1[j], rwkv_w2[j],
                            rwkv_a0[j], rwkv_a1[j], rwkv_a2[j], rwkv_g1[j], rwkv_g2[j], rwkv_k_k[j],
                            rwkv_k_a[j], rwkv_r_k[j], rwkv_ln_w[j], rwkv_ln_b[j], rwkv_out_w[j])
        elif kind == 2:
            x = retention_layer(*pre, ret_in_w[j], ret_out_w[j])
        else:
            x = attention_layer(*pre, att_in_w[j], att_out_w[j])
        x = moe_layer(x, norm_ffn_w[i], sh2, sc2, gt2, moe_group_w[i], moe_group_b[i],
                      moe_expert_w[i], moe_expert_b[i], i, moe_w1, moe_w3, moe_w2,
                      final_w=final_norm_w if i == depth - 1 else None)
    return x
```

```python
import functools
import math

import jax
import jax.numpy as jnp
from jax import lax
from jax.experimental import pallas as pl
from jax.experimental.pallas import tpu as pltpu

F32 = jnp.float32
BF16 = jnp.bfloat16
HI = lax.Precision.HIGHEST

NORM_EPS = 1e-6
LANES = 128
VMEM_LIMIT = 56 * 1024 * 1024
PROJ_TM = 2048
PROJ_TN = 1024
DESTRIDE_STEP = 4

SSM_HEAD_DIM = 64
SSM_N_GROUPS = 8
SSM_HEADS_PER_GROUP = 4
SSM_D_STATE = 128
SSM_CONV = 4
SSM_CHUNK = 128
SSM_CONV_COLS = 256
RWKV_HEAD_DIM = 64
RWKV_LN_EPS = 64e-5
RWKV_CHUNK = 128
RWKV_SUB = 16
RWKV_INV_PRECISION = None
RWKV_APPLY_PRECISION = None
RET_N_HEADS = 4
RET_CHUNK = 128
ATT_GROUPS = ((128, 1), (512, 4), (2048, 16))
ATT_HEADS_PER_GROUP = 8
ATT_HEAD_DIM = 128
ATT_BLOCK = 128
MOE_GROUPS = 4
MOE_EXPERTS_PER_GROUP = 8
MOE_EXPERTS = 32
MOE_ROWS = 320
MOE_ROW_GROUP = 8


def _cparams(*sem):
    return pltpu.CompilerParams(dimension_semantics=sem, vmem_limit_bytes=VMEM_LIMIT)


def _sigmoid(x):
    return 1.0 / (1.0 + jnp.exp(-x))


def _silu(x):
    return x * _sigmoid(x)


def _softplus(x):
    return jnp.maximum(x, 0.0) + jnp.log(1.0 + jnp.exp(-jnp.abs(x)))


def _dot(a, b, precision=None):
    return jnp.dot(a, b, preferred_element_type=F32, precision=precision)


def _dot_nt(a, b, precision=None):
    return lax.dot_general(a, b, (((1,), (1,)), ((), ())), preferred_element_type=F32,
                           precision=precision)


def _dot_tn(a, b, precision=None):
    return lax.dot_general(a, b, (((0,), (0,)), ((), ())), preferred_element_type=F32,
                           precision=precision)


def _split_bf16(a):
    hi = a.astype(BF16)
    return hi, (a - hi.astype(F32)).astype(BF16)


def _dot_split(a, m, transpose_rhs=False, lhs_exact=False):
    hi, lo = _split_bf16(a)
    mb = m.astype(BF16)
    if lhs_exact:
        return _dot(mb, hi) + _dot(mb, lo)
    dot = _dot_nt if transpose_rhs else _dot
    return dot(hi, mb) + dot(lo, mb)


def _dot_3pass(a, b):
    a_hi, a_lo = _split_bf16(a)
    b_hi, b_lo = _split_bf16(b)
    return _dot(a_hi, b_hi) + (_dot(a_lo, b_hi) + _dot(a_hi, b_lo))


def _iota2(shape, axis):
    return lax.broadcasted_iota(jnp.int32, shape, axis)


def _ada_kernel(c_ref, w_ref, b_ref, o_ref):
    cs = _silu(c_ref[...])
    o_ref[0] = _dot(cs, w_ref[0], HI) + b_ref[0]


def ada_modulation(c, ada_w, ada_b):
    depth, d, n = ada_w.shape
    bsz = c.shape[0]
    rows = -(-bsz // 8) * 8
    cp = jnp.pad(c, ((0, rows - bsz), (0, 0)))
    tn = 1536
    out = pl.pallas_call(
        _ada_kernel,
        grid=(depth, n // tn),
        in_specs=[pl.BlockSpec((rows, d), lambda l, j: (0, 0)),
                  pl.BlockSpec((1, d, tn), lambda l, j: (l, 0, j)),
                  pl.BlockSpec((1, 1, tn), lambda l, j: (l, 0, j))],
        out_specs=pl.BlockSpec((1, rows, tn), lambda l, j: (l, 0, j)),
        out_shape=jax.ShapeDtypeStruct((depth, rows, n), F32),
        compiler_params=_cparams("parallel", "parallel"),
        name="ada_modulation",
    )(cp, ada_w, ada_b.reshape(depth, 1, n))
    return out[:, :bsz]


def _norm_mod(x, nw, sh, sc):
    y = x * lax.rsqrt(jnp.mean(x * x, axis=-1, keepdims=True) + NORM_EPS) * nw
    return y * (1.0 + sc) + sh


def _nmm_kernel(*refs, dil, has_side):
    if has_side:
        x_ref, nw_ref, sh_ref, sc_ref, w_ref, ws_ref, o_ref, side_ref, h_ref, *rest = refs
    else:
        x_ref, nw_ref, sh_ref, sc_ref, w_ref, o_ref, h_ref, *rest = refs

    @pl.when(pl.program_id(2) == 0)
    def _():
        h = _norm_mod(x_ref[0], nw_ref[...], sh_ref[0], sc_ref[0]).astype(h_ref.dtype)
        h_ref[...] = h
        if has_side:
            w_hi, w_lo = _split_bf16(ws_ref[...])
            side_ref[0] = _dot(h, w_hi) + _dot(h, w_lo)

    res = _dot(h_ref[...], w_ref[...])
    if dil is None:
        o_ref[0] = res.astype(o_ref.dtype)
    else:
        acc_ref, *tmp = rest
        tm = acc_ref.shape[1]
        rows = tm // dil
        for c in range(acc_ref.shape[0]):
            acc_ref[c] = res[:, c * LANES:(c + 1) * LANES]
        f1 = min(dil, DESTRIDE_STEP)
        f2 = dil // f1
        for c in range(acc_ref.shape[0]):
            cols = slice(c * LANES, (c + 1) * LANES)
            if f2 == 1:
                for r in range(dil):
                    o_ref[0, r, :, cols] = acc_ref[c, pl.ds(r, rows, stride=dil), :].astype(o_ref.dtype)
            else:
                tmp_ref, = tmp
                for b in range(f1):
                    tmp_ref[c, b * (tm // f1):(b + 1) * (tm // f1), :] = (
                        acc_ref[c, pl.ds(b, tm // f1, stride=f1), :])
                for a in range(f2):
                    for b in range(f1):
                        o_ref[0, a * f1 + b, :, cols] = tmp_ref[
                            c, pl.ds(b * (tm // f1) + a, rows, stride=f2), :].astype(o_ref.dtype)


def norm_mod_matmul(x, nw, sh, sc, w, *, tm, tn, dil=None, side_w=None):
    bsz, t, d = x.shape
    n = w.shape[1]
    tm = min(tm, t)
    scratch = [pltpu.VMEM((tm, d), BF16)]
    in_specs = [pl.BlockSpec((1, tm, d), lambda b, i, j: (b, i, 0)),
                pl.BlockSpec((1, d), lambda b, i, j: (0, 0)),
                pl.BlockSpec((1, 1, d), lambda b, i, j: (b, 0, 0)),
                pl.BlockSpec((1, 1, d), lambda b, i, j: (b, 0, 0)),
                pl.BlockSpec((d, tn), lambda b, i, j: (0, j))]
    ins = [x, nw.reshape(1, d), sh, sc, w]
    if dil is None:
        out_specs = [pl.BlockSpec((1, tm, tn), lambda b, i, j: (b, i, j))]
        out_shape = [jax.ShapeDtypeStruct((bsz, t, n), BF16)]
    else:
        out_specs = [pl.BlockSpec((1, dil, tm // dil, tn), lambda b, i, j: (b, 0, i, j))]
        out_shape = [jax.ShapeDtypeStruct((bsz, dil, t // dil, n), BF16)]
        scratch += [pltpu.VMEM((tn // LANES, tm, LANES), F32)] * (2 if dil > DESTRIDE_STEP else 1)
    if side_w is not None:
        in_specs.append(pl.BlockSpec((d, LANES), lambda b, i, j: (0, 0)))
        ins.append(side_w)
        out_specs.append(pl.BlockSpec((1, tm, LANES), lambda b, i, j: (b, i, 0)))
        out_shape.append(jax.ShapeDtypeStruct((bsz, t, LANES), F32))
    outs = pl.pallas_call(
        functools.partial(_nmm_kernel, dil=dil, has_side=side_w is not None),
        grid=(bsz, t // tm, n // tn),
        in_specs=in_specs,
        out_specs=out_specs,
        out_shape=out_shape,
        scratch_shapes=scratch,
        compiler_params=_cparams("parallel", "parallel", "arbitrary"),
        name="norm_mod_matmul",
    )(*ins)
    return outs if side_w is not None else outs[0]


def _mgr_kernel(*refs, has_mul):
    if has_mul:
        y_ref, g_ref, w_ref, x_ref, gate_ref, o_ref = refs
        y = (y_ref[0].astype(F32) * g_ref[0].astype(F32)).astype(BF16)
    else:
        y_ref, w_ref, x_ref, gate_ref, o_ref = refs
        y = y_ref[0].astype(BF16)
    o_ref[0] = x_ref[0] + gate_ref[0] * _dot(y, w_ref[...])


def matmul_gated_residual(y, w, x, gate, mul=None, *, tm=512):
    bsz, t, k = y.shape
    d = w.shape[1]
    tm = min(tm, t)
    ins = [y] + ([mul] if mul is not None else []) + [w, x, gate]
    row = lambda width: pl.BlockSpec((1, tm, width), lambda b, i: (b, i, 0))
    specs = [row(k)] + ([row(k)] if mul is not None else []) + [
        pl.BlockSpec((k, d), lambda b, i: (0, 0)), row(d),
        pl.BlockSpec((1, 1, d), lambda b, i: (b, 0, 0))]
    return pl.pallas_call(
        functools.partial(_mgr_kernel, has_mul=mul is not None),
        grid=(bsz, t // tm),
        in_specs=specs,
        out_specs=row(d),
        out_shape=jax.ShapeDtypeStruct((bsz, t, d), F32),
        compiler_params=_cparams("parallel", "parallel"),
        name="matmul_gated_residual",
    )(*ins)


def _ssd_kernel(zx_ref, dt_ref, cw_ref, cb_ref, dtb_ref, alog_ref, dsk_ref, nw_ref, o_ref,
                prev_ref, xbc_ref, state_ref):
    q = SSM_CHUNK
    g_n, hg, p, n = SSM_N_GROUPS, SSM_HEADS_PER_GROUP, SSM_HEAD_DIM, SSM_D_STATE
    d_inner = g_n * hg * p
    nh = g_n * hg
    gw = hg * p

    @pl.when(pl.program_id(1) == 0)
    def _():
        prev_ref[...] = jnp.zeros_like(prev_ref)
        state_ref[...] = jnp.zeros_like(state_ref)

    dt = _softplus(dt_ref[0][:, :nh] + dtb_ref[...])
    da = dt * -jnp.exp(alog_ref[...])
    tril = (_iota2((q, q), 0) >= _iota2((q, q), 1))
    acum = _dot_split(da, tril.astype(F32), lhs_exact=True)
    a_end = acum[q - 1:q, :]
    chunk_decay = jnp.exp(a_end)
    adj_t = (acum - jnp.log(dt)).T
    wdec_t = (dt * jnp.exp(a_end - acum)).T

    src = _iota2((q, 2 * q), 1) - q - _iota2((q, 2 * q), 0)
    shifts = [jnp.where(src == -s, 1.0, 0.0).astype(BF16) for s in range(1, SSM_CONV)]
    cdim = xbc_ref.shape[1]
    for c0 in range(0, cdim, SSM_CONV_COLS):
        cs = slice(c0, c0 + SSM_CONV_COLS)
        cur = zx_ref[0, :, d_inner + c0:d_inner + c0 + SSM_CONV_COLS]
        ext = jnp.concatenate([prev_ref[:, cs], cur], axis=0)
        prev_ref[:, cs] = cur
        acc = cb_ref[:, cs] + cur.astype(F32) * cw_ref[SSM_CONV - 1:SSM_CONV, cs]
        for s in range(1, SSM_CONV):
            acc = acc + _dot(shifts[s - 1], ext) * cw_ref[SSM_CONV - 1 - s:SSM_CONV - s, cs]
        xbc_ref[:, cs] = _silu(acc)

    for g in range(g_n):
        xs_g = xbc_ref[:, g * gw:(g + 1) * gw]
        bm = xbc_ref[:, d_inner + g * n:d_inner + (g + 1) * n]
        cm = xbc_ref[:, d_inner + g_n * n + g * n:d_inner + g_n * n + (g + 1) * n]
        cm_b = cm.astype(BF16)
        cb = _dot_nt(cm_b, bm.astype(BF16))
        bm_t = bm.T
        xs_b = xs_g.astype(BF16)
        st_old = state_ref[g]
        y_carry = _dot(cm_b, st_old.astype(BF16))
        hd = [g * hg + hh for hh in range(hg)]
        hs = [slice(hh * p, (hh + 1) * p) for hh in range(hg)]
        a_col = [jnp.broadcast_to(acum[:, h:h + 1], (q, q)) for h in hd]
        m_h = [cb * jnp.exp(jnp.where(tril, ac - adj_t[h:h + 1, :], -jnp.inf))
               for h, ac in zip(hd, a_col)]
        st_new = [_dot((bm_t * wdec_t[h:h + 1, :]).astype(BF16), xs_b[:, c])
                  for h, c in zip(hd, hs)]
        ys = [_dot(m.astype(BF16), xs_b[:, c]) + jnp.exp(ac[:, :p]) * y_carry[:, c]
              for m, ac, c in zip(m_h, a_col, hs)]
        state_ref[g] = jnp.concatenate(
            [st_old[:, c] * chunk_decay[:, h:h + 1] + sn for h, c, sn in zip(hd, hs, st_new)], axis=1)
        y = jnp.concatenate(ys, axis=1) + xs_g * dsk_ref[:, g * gw:(g + 1) * gw]
        z = zx_ref[0, :, g * gw:(g + 1) * gw].astype(F32)
        y = y * _silu(z)
        y = y * lax.rsqrt(jnp.mean(y * y, axis=-1, keepdims=True) + NORM_EPS)
        o_ref[0, :, g * gw:(g + 1) * gw] = (y * nw_ref[:, g * gw:(g + 1) * gw]).astype(o_ref.dtype)


def ssd_core(zx, dt_raw, conv_w, conv_b, dt_bias, a_log, d_skip, norm_w):
    bsz, t, width = zx.shape
    d_inner = SSM_N_GROUPS * SSM_HEADS_PER_GROUP * SSM_HEAD_DIM
    cdim = width - d_inner
    nh = dt_bias.shape[0]
    q = SSM_CHUNK
    full = lambda shape: pl.BlockSpec(shape, lambda b, c: (0,) * len(shape))
    return pl.pallas_call(
        _ssd_kernel,
        grid=(bsz, t // q),
        in_specs=[pl.BlockSpec((1, q, width), lambda b, c: (b, c, 0)),
                  pl.BlockSpec((1, q, LANES), lambda b, c: (b, c, 0)),
                  full((SSM_CONV, cdim)), full((1, cdim)), full((1, nh)), full((1, nh)),
                  full((1, d_inner)), full((1, d_inner))],
        out_specs=pl.BlockSpec((1, q, d_inner), lambda b, c: (b, c, 0)),
        out_shape=jax.ShapeDtypeStruct((bsz, t, d_inner), BF16),
        scratch_shapes=[pltpu.VMEM((q, cdim), BF16), pltpu.VMEM((q, cdim), F32),
                        pltpu.VMEM((SSM_N_GROUPS, SSM_D_STATE,
                                    SSM_HEADS_PER_GROUP * SSM_HEAD_DIM), F32)],
        compiler_params=_cparams("parallel", "arbitrary"),
        name="ssd_core",
    )(zx, dt_raw, conv_w, conv_b.reshape(1, cdim), dt_bias.reshape(1, nh),
      a_log.reshape(1, nh), jnp.repeat(d_skip, SSM_HEAD_DIM).reshape(1, d_inner),
      norm_w.reshape(1, d_inner))


def mamba2_layer(x, nw, sh, sc, gate, in_w, conv_w, conv_b, dt_bias, a_log, d_skip, norm_w, out_w):
    d_inner = SSM_N_GROUPS * SSM_HEADS_PER_GROUP * SSM_HEAD_DIM
    cdim = conv_w.shape[1]
    nh = dt_bias.shape[0]
    w_main = in_w[:, :d_inner + cdim].astype(BF16)
    w_dt = jnp.pad(in_w[:, d_inner + cdim:], ((0, 0), (0, LANES - nh)))
    zx, dt_raw = norm_mod_matmul(x, nw, sh, sc, w_main, tm=PROJ_TM, tn=PROJ_TN, side_w=w_dt)
    y = ssd_core(zx, dt_raw, conv_w, conv_b, dt_bias, a_log, d_skip, norm_w)
    return matmul_gated_residual(y, out_w.astype(BF16), x, gate)


def _head_indicator(d, n):
    return jnp.where(_iota2((d, LANES), 0) // n == _iota2((d, LANES), 1), 1.0, 0.0)


def _rwkv_prep_kernel(x_ref, xp_ref, nw_ref, sh_ref, sc_ref, mix_ref, wr_ref, wk_ref, wv_ref,
                      w0_ref, w1_ref, w2_ref, a0_ref, a1_ref, a2_ref, g1_ref, g2_ref, kk_ref, ka_ref,
                      r_o, lw_o, k_o, v_o, kn_o, b_o, g_o):
    tm, d = x_ref.shape[1], x_ref.shape[2]
    nw, sh, sc = nw_ref[...], sh_ref[0], sc_ref[0]
    hm = _norm_mod(x_ref[0], nw, sh, sc)
    prev = _norm_mod(xp_ref[0], nw, sh, sc)[7:8, :]
    prev = jnp.where(pl.program_id(1) > 0, prev, 0.0)
    shifted = jnp.where(_iota2((tm, d), 0) == 0, prev, pltpu.roll(hm, 1, axis=0))
    xx = shifted - hm
    mixed = lambda j: (hm + xx * mix_ref[j:j + 1, :]).astype(BF16)
    xr, xw, xk, xv, xa, xg = [mixed(j) for j in range(6)]
    r = _dot(xr, wr_ref[...])
    k = _dot(xk, wk_ref[...])
    v = _dot(xv, wv_ref[...])
    lora = lambda u, w: _dot(u.astype(BF16), w[...])
    w_log = -_softplus(-(w0_ref[...] + lora(jnp.tanh(lora(xw, w1_ref)), w2_ref))) - 0.5
    a = _sigmoid(a0_ref[...] + lora(lora(xa, a1_ref), a2_ref))
    g = lora(_sigmoid(lora(xg, g1_ref)), g2_ref)
    kk = k * kk_ref[...]
    ind = _head_indicator(d, RWKV_HEAD_DIM)
    nrm = jnp.maximum(jnp.sqrt(_dot_split(kk * kk, ind)), 1e-12)
    kn = kk * _dot_split(1.0 / nrm, ind, transpose_rhs=True)
    r_o[0] = r.astype(r_o.dtype)
    lw_o[0] = -jnp.exp(w_log)
    k_o[0] = (k * (1.0 + (a - 1.0) * ka_ref[...])).astype(k_o.dtype)
    v_o[0] = v.astype(v_o.dtype)
    kn_o[0] = kn.astype(kn_o.dtype)
    b_o[0] = (kn * a).astype(b_o.dtype)
    g_o[0] = g.astype(g_o.dtype)


def rwkv_prep(x, nw, sh, sc, mix, rkv_w, w0, w1, w2, a0, a1, a2, g1, g2, k_k, k_a, *, tm=512):
    bsz, t, d = x.shape
    tm = min(tm, t)
    full = lambda arr: pl.BlockSpec(arr.shape, lambda b, i: (0,) * arr.ndim)
    row = pl.BlockSpec((1, tm, d), lambda b, i: (b, i, 0))
    vec = pl.BlockSpec((1, 1, d), lambda b, i: (b, 0, 0))
    bf = lambda w: w.astype(BF16)
    params = [mix, bf(rkv_w[0]), bf(rkv_w[1]), bf(rkv_w[2]), w0.reshape(1, d), bf(w1), bf(w2),
              a0.reshape(1, d), bf(a1), bf(a2), bf(g1), bf(g2), k_k.reshape(1, d), k_a.reshape(1, d)]
    prev_spec = pl.BlockSpec((1, 8, d), lambda b, i: (b, jnp.maximum(i * (tm // 8) - 1, 0), 0))
    return pl.pallas_call(
        _rwkv_prep_kernel,
        grid=(bsz, t // tm),
        in_specs=[row, prev_spec, pl.BlockSpec((1, d), lambda b, i: (0, 0)), vec, vec]
        + [full(p) for p in params],
        out_specs=[row] * 7,
        out_shape=[jax.ShapeDtypeStruct((bsz, t, d), F32 if i == 1 else BF16) for i in range(7)],
        compiler_params=_cparams("parallel", "parallel"),
        name="rwkv_prep",
    )(x, x, nw.reshape(1, d), sh, sc, *params)


def _mm(a, b, precision):
    if precision is None:
        return _dot(a.astype(BF16), b.astype(BF16))
    return _dot(a, b, precision)


def _mm_nt(a, b, precision):
    if precision is None:
        return _dot_nt(a.astype(BF16), b.astype(BF16))
    return _dot_nt(a, b, precision)


def _mm_tn(a, b, precision):
    if precision is None:
        return _dot_tn(a.astype(BF16), b.astype(BF16))
    return _dot_tn(a, b, precision)


def _unit_lower_inverse(m_list, length, sub, precision):
    ri = _iota2((length, length), 0)
    ci = _iota2((length, length), 1)
    eye = jnp.where(ri == ci, 1.0, 0.0)
    same_block = ri // sub == ci // sub
    mm = lambda a, b: _mm(a, b, precision)
    md = [jnp.where(same_block, m, 0.0) for m in m_list]
    off = [m - d for m, d in zip(m_list, md)]
    dinv = [eye + d for d in md]
    pw = md
    for _ in range(int(math.log2(sub)) - 1):
        pw = [mm(p, p) for p in pw]
        dinv = [x + mm(x, p) for x, p in zip(dinv, pw)]
    pm = [mm(x, o) for x, o in zip(dinv, off)]
    acc = [eye + p for p in pm]
    pw = pm
    for _ in range(int(math.log2(length // sub)) - 1):
        pw = [mm(p, p) for p in pw]
        acc = [x + mm(x, p) for x, p in zip(acc, pw)]
    return [mm(x, d) for x, d in zip(acc, dinv)]


def _rwkv_core_kernel(r_ref, lw_ref, k_ref, v_ref, kn_ref, b_ref, rk_ref, lnw_ref, lnb_ref, o_ref,
                      state_ref, *, inv_precision, apply_precision):
    length, d = r_ref.shape[1], r_ref.shape[2]
    n = RWKV_HEAD_DIM
    heads = range(d // n)
    mm = lambda a, b: _mm(a, b, apply_precision)

    @pl.when(pl.program_id(1) == 0)
    def _():
        state_ref[...] = jnp.zeros_like(state_ref)

    ri = _iota2((2 * length, 2 * length), 0)
    ci = _iota2((2 * length, 2 * length), 1) % length
    keep = ((ri < length) & (ri > ci)) | ((ri >= length) & (ri - length >= ci))
    eye_n = _iota2((n, n), 0) == _iota2((n, n), 1)

    lw = lw_ref[0]
    r, k, v, kn, b = [ref[0].astype(F32) for ref in (r_ref, k_ref, v_ref, kn_ref, b_ref)]
    tri = jnp.where(_iota2((length, length), 0) >= _iota2((length, length), 1), 1.0, 0.0)
    cum = _dot_split(lw, tri, lhs_exact=True)
    c_end = cum[length - 1:length, :]
    e_neg = jnp.exp(-cum)
    e_rem = jnp.exp(c_end - cum)
    ar = jnp.concatenate([-kn * jnp.exp(cum - lw), r * jnp.exp(cum)], axis=0)
    bk = jnp.concatenate([b * e_neg, k * e_neg], axis=0)
    bk_end = jnp.concatenate([b * e_rem, k * e_rem], axis=0)
    w_end = jnp.exp(c_end)
    bonus_w = r * k * rk_ref[...]
    hs = [slice(h * n, (h + 1) * n) for h in heads]

    s0 = [state_ref[h] for h in heads]
    p = [jnp.where(keep, _mm_nt(ar[:, c], bk[:, c], inv_precision), 0.0) for c in hs]
    t_inv = _unit_lower_inverse([x[:length, :length] for x in p], length, RWKV_SUB, inv_precision)
    q = [mm(ar[:, c], s) for c, s in zip(hs, s0)]
    y = [x[:length] + mm(pp[:length, length:], v[:, c]) for x, pp, c in zip(q, p, hs)]
    u = [mm(ti, yy) for ti, yy in zip(t_inv, y)]
    uv = [jnp.concatenate([uu, v[:, c]], axis=0) for uu, c in zip(u, hs)]
    o = [x[length:] + mm(pp[length:], w) for x, pp, w in zip(q, p, uv)]
    upd = [_mm_tn(bk_end[:, c], w, apply_precision) for c, w in zip(hs, uv)]
    w_col = [jnp.sum(jnp.where(eye_n, w_end[:, c], 0.0), axis=1, keepdims=True) for c in hs]
    for h in heads:
        state_ref[h] = w_col[h] * s0[h] + upd[h]
    ind = _head_indicator(d, n)
    head_sum = lambda x: _dot_split(_dot_split(x, ind), ind, transpose_rhs=True)
    o_all = jnp.concatenate(o, axis=1)
    cen = o_all - head_sum(o_all) * (1.0 / n)
    var = head_sum(cen * cen) * (1.0 / n)
    y = cen * lax.rsqrt(var + RWKV_LN_EPS) * lnw_ref[...] + lnb_ref[...]
    o_ref[0] = (y + head_sum(bonus_w) * v).astype(o_ref.dtype)


def rwkv_core(r, lw, k, v, kn, b, r_k, ln_w, ln_b):
    bsz, t, d = r.shape
    length = RWKV_CHUNK
    n = RWKV_HEAD_DIM
    row = pl.BlockSpec((1, length, d), lambda bb, c: (bb, c, 0))
    vec = pl.BlockSpec((1, d), lambda bb, c: (0, 0))
    return pl.pallas_call(
        functools.partial(_rwkv_core_kernel, inv_precision=RWKV_INV_PRECISION,
                          apply_precision=RWKV_APPLY_PRECISION),
        grid=(bsz, t // length),
        in_specs=[row] * 6 + [vec] * 3,
        out_specs=row,
        out_shape=jax.ShapeDtypeStruct((bsz, t, d), BF16),
        scratch_shapes=[pltpu.VMEM((d // n, n, n), F32)],
        compiler_params=_cparams("parallel", "arbitrary"),
        name="rwkv_core",
    )(r, lw, k, v, kn, b, r_k.reshape(1, d), ln_w.reshape(1, d), ln_b.reshape(1, d))


def rwkv7_layer(x, nw, sh, sc, gate, mix, rkv_w, w0, w1, w2, a0, a1, a2, g1, g2, k_k, k_a, r_k,
                ln_w, ln_b, out_w):
    r, lw, k, v, kn, b, g = rwkv_prep(x, nw, sh, sc, mix, rkv_w, w0, w1, w2, a0, a1, a2, g1, g2,
                                      k_k, k_a)
    y = rwkv_core(r, lw, k, v, kn, b, r_k, ln_w, ln_b)
    return matmul_gated_residual(y, out_w.astype(BF16), x, gate, mul=g)


def _ret_kernel(qkvg_ref, idec_ref, qdec_ref, kdec_ref, cdec_ref, o_ref, state_ref, *, d_model):
    nh = RET_N_HEADS
    dk = d_model // nh
    dv = 2 * d_model // nh
    k_scale = dk ** -0.5

    @pl.when(pl.program_id(1) == 0)
    def _():
        state_ref[...] = jnp.zeros_like(state_ref)

    for h in range(nh):
        q_b = qkvg_ref[0, :, h * dk:(h + 1) * dk]
        k_f = qkvg_ref[0, :, d_model + h * dk:d_model + (h + 1) * dk].astype(F32) * k_scale
        v_b = qkvg_ref[0, :, 2 * d_model + h * dv:2 * d_model + (h + 1) * dv]
        g_f = qkvg_ref[0, :, 4 * d_model + h * dv:4 * d_model + (h + 1) * dv].astype(F32)
        scores = _dot_nt(q_b, k_f.astype(BF16)) * idec_ref[h]
        inner = _dot(scores.astype(BF16), v_b)
        r_old = state_ref[h]
        qd = (q_b.astype(F32) * qdec_ref[:, h:h + 1]).astype(BF16)
        cross = _dot(qd, r_old.astype(BF16))
        kd = (k_f * kdec_ref[:, h:h + 1]).astype(BF16)
        state_ref[h] = r_old * cdec_ref[:, h:h + 1] + _dot_tn(kd, v_b)
        o = inner + cross
        o = o * lax.rsqrt(jnp.mean(o * o, axis=-1, keepdims=True) + NORM_EPS)
        o_ref[0, :, h * dv:(h + 1) * dv] = (_silu(g_f) * o).astype(o_ref.dtype)


def retention_core(qkvg, d_model):
    bsz, t, width = qkvg.shape
    nh, q = RET_N_HEADS, RET_CHUNK
    dk, dv = d_model // nh, 2 * d_model // nh
    log_gamma = jnp.log(1 - jnp.exp2(-5.0 - jnp.arange(nh, dtype=F32)))
    idx = jnp.arange(q)
    rel = idx[:, None] - idx[None, :]
    inner_decay = jnp.where(rel >= 0,
                            jnp.exp(jnp.maximum(rel, 0).astype(F32) * log_gamma[:, None, None]), 0.0)
    q_decay = jnp.exp((idx + 1).astype(F32)[:, None] * log_gamma)
    k_decay = jnp.exp((q - 1 - idx).astype(F32)[:, None] * log_gamma)
    chunk_decay = jnp.exp(q * log_gamma).reshape(1, nh)
    full = lambda shape: pl.BlockSpec(shape, lambda b, c: (0,) * len(shape))
    return pl.pallas_call(
        functools.partial(_ret_kernel, d_model=d_model),
        grid=(bsz, t // q),
        in_specs=[pl.BlockSpec((1, q, width), lambda b, c: (b, c, 0)),
                  full((nh, q, q)), full((q, nh)), full((q, nh)), full((1, nh))],
        out_specs=pl.BlockSpec((1, q, nh * dv), lambda b, c: (b, c, 0)),
        out_shape=jax.ShapeDtypeStruct((bsz, t, nh * dv), BF16),
        scratch_shapes=[pltpu.VMEM((nh, dk, dv), F32)],
        compiler_params=_cparams("parallel", "arbitrary"),
        name="retention_core",
    )(qkvg, inner_decay, q_decay, k_decay, chunk_decay)


def retention_layer(x, nw, sh, sc, gate, in_w, out_w):
    d = x.shape[-1]
    qkvg = norm_mod_matmul(x, nw, sh, sc, in_w.astype(BF16), tm=PROJ_TM, tn=PROJ_TN)
    y = retention_core(qkvg, d)
    return matmul_gated_residual(y, out_w.astype(BF16), x, gate)


def _att_kernel(q_ref, k_ref, v_ref, sl_ref, o_ref, lse_ref, kprev_ref, vprev_ref, *, dil, span):
    w = ATT_BLOCK
    e = ATT_HEAD_DIM
    nblk = pl.program_id(2)

    @pl.when(nblk == 0)
    def _():
        kprev_ref[...] = jnp.zeros_like(kprev_ref)
        vprev_ref[...] = jnp.zeros_like(vprev_ref)

    qi = _iota2((w, 2 * w), 0)
    ci = _iota2((w, 2 * w), 1)
    delta = w + qi - ci
    mask = (delta >= 0) & (delta <= span) & ((nblk > 0) | (ci >= w))
    dist = (dil * delta).astype(F32)
    lane = _iota2((w, LANES), 1)
    heads = range(ATT_HEADS_PER_GROUP)
    cols = [slice(h * e, (h + 1) * e) for h in heads]
    s = [_dot_nt(q_ref[0, 0, :, c], jnp.concatenate([kprev_ref[:, c], k_ref[0, 0, :, c]], axis=0))
         for c in cols]
    s = [jnp.where(mask, x * (e ** -0.5) - sl_ref[:, h:h + 1] * dist, -jnp.inf)
         for h, x in zip(heads, s)]
    m = [jnp.max(x, axis=-1, keepdims=True) for x in s]
    p = [jnp.exp(x - mx) for x, mx in zip(s, m)]
    den = [jnp.sum(x, axis=-1, keepdims=True) for x in p]
    lse_all = jnp.zeros((w, LANES), F32)
    for h in heads:
        vv = jnp.concatenate([vprev_ref[:, cols[h]], v_ref[0, 0, :, cols[h]]], axis=0)
        o = _dot((p[h] / den[h]).astype(BF16), vv)
        o_ref[0, 0, :, cols[h]] = o.astype(o_ref.dtype)
        lse_all = jnp.where(lane == h, m[h] + jnp.log(den[h]), lse_all)
    lse_ref[0, 0] = lse_all
    kprev_ref[...] = k_ref[0, 0]
    vprev_ref[...] = v_ref[0, 0]


def dilated_group_core(qkv, gi, window, dil, slopes):
    bsz, _, ls, _ = qkv.shape
    w = ATT_BLOCK
    gw = ATT_HEADS_PER_GROUP * ATT_HEAD_DIM
    spec = lambda part: pl.BlockSpec((1, 1, w, gw), lambda b, r, n: (b, r, n, part))
    return pl.pallas_call(
        functools.partial(_att_kernel, dil=dil, span=window // dil),
        grid=(bsz, dil, ls // w),
        in_specs=[spec(0), spec(1), spec(2),
                  pl.BlockSpec((1, ATT_HEADS_PER_GROUP), lambda b, r, n: (0, 0))],
        out_specs=[pl.BlockSpec((1, 1, w, gw), lambda b, r, n: (b, r, n, 0)),
                   pl.BlockSpec((1, 1, w, LANES), lambda b, r, n: (b, r, n, 0))],
        out_shape=[jax.ShapeDtypeStruct((bsz, dil, ls, gw), BF16),
                   jax.ShapeDtypeStruct((bsz, dil, ls, LANES), F32)],
        scratch_shapes=[pltpu.VMEM((w, gw), BF16), pltpu.VMEM((w, gw), BF16)],
        compiler_params=_cparams("parallel", "parallel", "arbitrary"),
        name="dilated_attention_g%d" % gi,
    )(qkv, qkv, qkv, slopes.reshape(1, ATT_HEADS_PER_GROUP))


def _att_out_kernel(o0_ref, o1_ref, o2_ref, l0_ref, l1_ref, l2_ref, w_ref, x_ref, gate_ref, o_ref,
                    *scratch):
    tm = x_ref.shape[1]

    def token_major(ref, scr):
        dil = ref.shape[1]
        if dil == 1:
            return ref[0, 0].astype(F32)
        f1 = min(dil, DESTRIDE_STEP)
        f2 = dil // f1
        n1 = tm // f1
        for c in range(scr.shape[0]):
            cols = slice(c * LANES, (c + 1) * LANES)
            if f2 == 1:
                for r in range(dil):
                    scr[c, pl.ds(r, tm // dil, stride=dil), :] = ref[0, r, :, cols].astype(F32)
            else:
                for b in range(f1):
                    for a in range(f2):
                        tmp_ref[c, pl.ds(b * n1 + a, tm // dil, stride=f2), :] = (
                            ref[0, a * f1 + b, :, cols].astype(F32))
                for b in range(f1):
                    scr[c, pl.ds(b, n1, stride=f1), :] = tmp_ref[c, b * n1:(b + 1) * n1, :]
        return jnp.concatenate([scr[c] for c in range(scr.shape[0])], axis=1)

    *scratch, tmp_ref = scratch
    o1, l1, o2, l2 = [token_major(ref, scr) for ref, scr in
                      zip((o1_ref, l1_ref, o2_ref, l2_ref), scratch)]
    o0, l0 = token_major(o0_ref, None), token_major(l0_ref, None)
    m = jnp.maximum(jnp.maximum(l0, l1), l2)
    w0, w1, w2 = jnp.exp(l0 - m), jnp.exp(l1 - m), jnp.exp(l2 - m)
    inv = 1.0 / (w0 + w1 + w2)
    k = o0.shape[1]
    expand = jnp.where(_iota2((LANES, k), 1) // ATT_HEAD_DIM == _iota2((LANES, k), 0), 1.0, 0.0)
    y = (_dot_split(w0 * inv, expand) * o0 + _dot_split(w1 * inv, expand) * o1
         + _dot_split(w2 * inv, expand) * o2)
    o_ref[0] = x_ref[0] + gate_ref[0] * _dot(y.astype(BF16), w_ref[...])


def attention_combine_out(outs, lses, w, x, gate, *, tm=512):
    bsz, t, d = x.shape
    k = w.shape[0]
    tm = min(tm, t)
    row = pl.BlockSpec((1, tm, d), lambda b, i: (b, i, 0))
    res = lambda a: pl.BlockSpec((1, a.shape[1], tm // a.shape[1], a.shape[3]),
                                 lambda b, i: (b, 0, i, 0))
    wide, narrow = pltpu.VMEM((k // LANES, tm, LANES), F32), pltpu.VMEM((1, tm, LANES), F32)
    return pl.pallas_call(
        _att_out_kernel,
        grid=(bsz, t // tm),
        in_specs=[res(a) for a in outs] + [res(a) for a in lses] + [
            pl.BlockSpec((k, d), lambda b, i: (0, 0)), row,
            pl.BlockSpec((1, 1, d), lambda b, i: (b, 0, 0))],
        out_specs=row,
        out_shape=jax.ShapeDtypeStruct((bsz, t, d), F32),
        scratch_shapes=[wide, narrow, wide, narrow, wide],
        compiler_params=_cparams("parallel", "parallel"),
        name="attention_combine_out",
    )(*outs, *lses, w, x, gate)


def attention_layer(x, nw, sh, sc, gate, in_w, out_w):
    n_groups = len(ATT_GROUPS)
    gw = ATT_HEADS_PER_GROUP * ATT_HEAD_DIM
    n_heads = n_groups * ATT_HEADS_PER_GROUP
    slopes = jnp.exp2(-8.0 * jnp.arange(1, n_heads + 1, dtype=F32) / n_heads)
    slopes = slopes.reshape(n_groups, ATT_HEADS_PER_GROUP)
    w_in = in_w.astype(BF16).reshape(in_w.shape[0], 3, n_groups, gw)
    outs, lses = [], []
    for gi, (window, dil) in enumerate(ATT_GROUPS):
        w_g = w_in[:, :, gi, :].reshape(in_w.shape[0], 3 * gw)
        qkv = norm_mod_matmul(x, nw, sh, sc, w_g, tm=PROJ_TM, tn=gw, dil=dil)
        o, lse = dilated_group_core(qkv, gi, window, dil, slopes[gi])
        outs.append(o)
        lses.append(lse)
    return attention_combine_out(outs, lses, out_w.astype(BF16), x, gate)


ROUTER_GROUP_LANE0 = 0
ROUTER_EXPERT_LANE0 = MOE_GROUPS
ROUTER_FIELDS = 6
ROUTER_PREFIX_ROWS = 256


def _router_kernel(x_ref, nw_ref, sh_ref, sc_ref, wr_ref, br_ref, hf_ref, meta_ref, cnt_ref,
                   carry_ref):
    tm = x_ref.shape[1]

    @pl.when(pl.program_id(1) == 0)
    def _():
        carry_ref[...] = jnp.zeros_like(carry_ref)

    hf = _norm_mod(x_ref[0], nw_ref[...], sh_ref[0], sc_ref[0])
    hf_ref[0] = hf
    logits = _dot_3pass(hf, wr_ref[...]) + br_ref[...]
    lane = _iota2((tm, LANES), 1)
    neg = -jnp.inf
    first = lambda hit: jnp.min(jnp.where(hit, lane, LANES), axis=-1, keepdims=True)

    gl = jnp.where(lane < MOE_GROUPS, logits, neg)
    gmax = jnp.max(gl, axis=-1, keepdims=True)
    gidx = first(gl == gmax)
    g_top = 1.0 / jnp.sum(jnp.exp(gl - gmax), axis=-1, keepdims=True)

    lo = ROUTER_EXPERT_LANE0 + gidx * MOE_EXPERTS_PER_GROUP
    el = jnp.where((lane >= lo) & (lane < lo + MOE_EXPERTS_PER_GROUP), logits, neg)
    m1 = jnp.max(el, axis=-1, keepdims=True)
    i1 = first(el == m1)
    el2 = jnp.where(lane == i1, neg, el)
    m2 = jnp.max(el2, axis=-1, keepdims=True)
    i2 = first(el2 == m2)
    ex = jnp.exp(m2 - m1)
    gate1 = g_top / (1.0 + ex)
    gate2 = g_top * ex / (1.0 + ex)

    hit1 = lane == i1
    hit2 = lane == i2
    onehot = jnp.where(hit1 | hit2, 1.0, 0.0).astype(BF16)
    sub = min(tm, ROUTER_PREFIX_ROWS)
    strict = jnp.where(_iota2((sub, sub), 0) > _iota2((sub, sub), 1), 1.0, 0.0).astype(BF16)
    ones = jnp.ones((8, sub), BF16)
    running = carry_ref[...]
    parts = []
    for s0 in range(0, tm, sub):
        block = onehot[s0:s0 + sub]
        parts.append(_dot(strict, block) + running[0:1, :])
        running = running + _dot(ones, block)
    before = jnp.concatenate(parts, axis=0)
    rank1 = jnp.sum(jnp.where(hit1, before, 0.0), axis=-1, keepdims=True)
    rank2 = jnp.sum(jnp.where(hit2, before, 0.0), axis=-1, keepdims=True)
    carry_ref[...] = running
    cnt_ref[0] = running

    e1 = (i1 - ROUTER_EXPERT_LANE0).astype(F32)
    e2 = (i2 - ROUTER_EXPERT_LANE0).astype(F32)
    meta = jnp.zeros((tm, LANES), F32)
    for j, val in enumerate((e1, e2, rank1, rank2, gate1, gate2)):
        meta = jnp.where(lane == j, val, meta)
    for k in range(tm // LANES):
        fields = meta[k * LANES:(k + 1) * LANES, :].T
        for f in range(ROUTER_FIELDS):
            meta_ref[0, f, k:k + 1, :] = fields[f:f + 1, :]


def moe_router(x, nw, sh, sc, group_w, group_b, expert_w, expert_b, *, tm=8 * LANES):
    bsz, t, d = x.shape
    tm = min(tm, t)
    pad = LANES - MOE_GROUPS - MOE_EXPERTS
    wr = jnp.pad(jnp.concatenate([group_w, expert_w], axis=1), ((0, 0), (0, pad)))
    br = jnp.pad(jnp.concatenate([group_b, expert_b]), (0, pad)).reshape(1, LANES)
    return pl.pallas_call(
        _router_kernel,
        grid=(bsz, t // tm),
        in_specs=[pl.BlockSpec((1, tm, d), lambda b, i: (b, i, 0)),
                  pl.BlockSpec((1, d), lambda b, i: (0, 0)),
                  pl.BlockSpec((1, 1, d), lambda b, i: (b, 0, 0)),
                  pl.BlockSpec((1, 1, d), lambda b, i: (b, 0, 0)),
                  pl.BlockSpec((d, LANES), lambda b, i: (0, 0)),
                  pl.BlockSpec((1, LANES), lambda b, i: (0, 0))],
        out_specs=[pl.BlockSpec((1, tm, d), lambda b, i: (b, i, 0)),
                   pl.BlockSpec((1, ROUTER_FIELDS, tm // LANES, LANES), lambda b, i: (b, 0, i, 0)),
                   pl.BlockSpec((1, 8, LANES), lambda b, i: (b, 0, 0))],
        out_shape=[jax.ShapeDtypeStruct((bsz, t, d), F32),
                   jax.ShapeDtypeStruct((bsz, ROUTER_FIELDS, t // LANES, LANES), F32),
                   jax.ShapeDtypeStruct((bsz, 8, LANES), F32)],
        scratch_shapes=[pltpu.VMEM((8, LANES), F32)],
        compiler_params=_cparams("parallel", "arbitrary"),
        name="moe_router",
    )(x, nw.reshape(1, d), sh, sc, wr, br)


def _dest_kernel(meta_ref, cnt_ref, d_ref):
    upper = jnp.where(_iota2((LANES, LANES), 0) < _iota2((LANES, LANES), 1), 1.0, 0.0)
    first = _dot_split(cnt_ref[0], upper)
    for k in range(2):
        expert = meta_ref[0, k]
        off = jnp.zeros_like(expert)
        for x in range(MOE_EXPERTS):
            lane = ROUTER_EXPERT_LANE0 + x
            off = jnp.where(expert == x, first[0:1, lane:lane + 1], off)
        d_ref[0, k] = (meta_ref[0, 2 + k] + off).astype(jnp.int32)


def moe_dest(meta, cnt):
    bsz, nf, rows, _ = meta.shape
    return pl.pallas_call(
        _dest_kernel,
        grid=(bsz,),
        in_specs=[pl.BlockSpec((1, nf, rows, LANES), lambda b: (b, 0, 0, 0)),
                  pl.BlockSpec((1, 8, LANES), lambda b: (b, 0, 0))],
        out_specs=pl.BlockSpec((1, 2, rows, LANES), lambda b: (b, 0, 0, 0)),
        out_shape=jax.ShapeDtypeStruct((bsz, 2, rows, LANES), jnp.int32),
        compiler_params=_cparams("parallel"),
        name="moe_dest",
    )(meta, cnt)


def _expert_kernel(d1_ref, d2_ref, g1_ref, g2_ref, cnt_ref,
                   hf_hbm, x_hbm, gt_ref, w1_ref, w3_ref, w2_ref, *rest, final_norm):
    if final_norm:
        fw_ref, *rest = rest
    out_hbm, hf_v, acc_v, xb_v, yb_v, row_tok, row_gate, pstart, sem = rest
    b = pl.program_id(0)
    e = pl.program_id(1)
    n_e = pl.num_programs(1)
    t = hf_v.shape[0]
    grp = MOE_ROW_GROUP
    rows = xb_v.shape[0] * grp

    @pl.when(e == 0)
    def _():
        cp_h = pltpu.make_async_copy(hf_hbm.at[b], hf_v, sem.at[0])
        cp_x = pltpu.make_async_copy(x_hbm.at[b], acc_v, sem.at[1])
        cp_h.start()
        cp_x.start()
        xb_v[...] = jnp.zeros_like(xb_v)
        for i in range(grp):
            row_tok[2 * t + i] = 0

        def start_body(i, s):
            pstart[i] = s
            return s + cnt_ref[b * n_e + i]

        lax.fori_loop(0, n_e, start_body, 0)

        def tok_body(g, carry):
            for u in range(4):
                tok = g * 4 + u
                row_tok[d1_ref[tok]] = tok
                row_gate[d1_ref[tok]] = g1_ref[tok]
                row_tok[d2_ref[tok]] = tok
                row_gate[d2_ref[tok]] = g2_ref[tok]
            return carry

        lax.fori_loop(0, t // 4, tok_body, 0)
        cp_h.wait()
        cp_x.wait()

    cnt = cnt_ref[b * n_e + e]
    start = pstart[e]

    def block_body(j, carry):
        base = start + j * rows
        nrows = jnp.minimum(rows, cnt - j * rows)
        full_groups = nrows // grp

        def gather(g, c):
            first = base + g * grp
            for u in range(grp):
                xb_v[g, u:u + 1, :] = hf_v[pl.ds(row_tok[first + u], 1), :]
            return c

        lax.fori_loop(0, (nrows + grp - 1) // grp, gather, 0)
        xb = xb_v[...].reshape(rows, xb_v.shape[2]).astype(BF16)
        h1 = _dot(xb, w1_ref[0, 0].astype(BF16))
        h3 = _dot(xb, w3_ref[0, 0].astype(BF16))
        act = (_silu(h1) * h3).astype(BF16)
        yb_v[...] = (_dot(act, w2_ref[0, 0].astype(BF16)) * gt_ref[0]).reshape(yb_v.shape)

        def scatter(g, c):
            first = base + g * grp
            toks = [row_tok[first + u] for u in range(grp)]
            vals = [acc_v[pl.ds(toks[u], 1), :] + row_gate[first + u] * yb_v[g, u:u + 1, :]
                    for u in range(grp)]
            for u in range(grp):
                acc_v[pl.ds(toks[u], 1), :] = vals[u]
            return c

        lax.fori_loop(0, full_groups, scatter, 0)

        def scatter_tail(i, c):
            tok = row_tok[base + i]
            acc_v[pl.ds(tok, 1), :] = (acc_v[pl.ds(tok, 1), :] + row_gate[base + i]
                                       * yb_v[i // grp, pl.ds(i % grp, 1), :])
            return c

        lax.fori_loop(full_groups * grp, nrows, scatter_tail, 0)
        return carry

    lax.fori_loop(0, (cnt + rows - 1) // rows, block_body, 0)

    @pl.when(e == n_e - 1)
    def _():
        if final_norm:
            chunk = math.gcd(t, 256)

            def norm_body(i, carry):
                rs = pl.ds(pl.multiple_of(i * chunk, chunk), chunk)
                xr = acc_v[rs, :]
                acc_v[rs, :] = (xr * lax.rsqrt(jnp.mean(xr * xr, axis=-1, keepdims=True) + NORM_EPS)
                                * fw_ref[...])
                return carry

            lax.fori_loop(0, t // chunk, norm_body, 0)
        cp_o = pltpu.make_async_copy(acc_v, out_hbm.at[b], sem.at[2])
        cp_o.start()
        cp_o.wait()


def moe_experts(hf, x, gt, d1, d2, g1, g2, counts, layer, w1, w3, w2, final_w=None):
    bsz, t, d = x.shape
    _, n_e, _, f = w1.shape
    smem_tok = pl.BlockSpec((t,), lambda b, e: (b,), memory_space=pltpu.SMEM)
    extra_specs, extra = [], []
    if final_w is not None:
        extra_specs, extra = [pl.BlockSpec((1, d), lambda b, e: (0, 0))], [final_w.reshape(1, d)]
    return pl.pallas_call(
        functools.partial(_expert_kernel, final_norm=final_w is not None),
        grid=(bsz, n_e),
        in_specs=[smem_tok] * 4 + [
            pl.BlockSpec(memory_space=pltpu.SMEM),
            pl.BlockSpec(memory_space=pl.ANY),
            pl.BlockSpec(memory_space=pl.ANY),
            pl.BlockSpec((1, 1, d), lambda b, e: (b, 0, 0)),
            pl.BlockSpec((1, 1, d, f), lambda b, e: (layer, e, 0, 0)),
            pl.BlockSpec((1, 1, d, f), lambda b, e: (layer, e, 0, 0)),
            pl.BlockSpec((1, 1, f, d), lambda b, e: (layer, e, 0, 0))] + extra_specs,
        out_specs=pl.BlockSpec(memory_space=pl.ANY),
        out_shape=jax.ShapeDtypeStruct((bsz, t, d), F32),
        scratch_shapes=[pltpu.VMEM((t, d), F32), pltpu.VMEM((t, d), F32),
                        pltpu.VMEM((MOE_ROWS // MOE_ROW_GROUP, MOE_ROW_GROUP, d), F32),
                        pltpu.VMEM((MOE_ROWS // MOE_ROW_GROUP, MOE_ROW_GROUP, d), F32),
                        pltpu.SMEM((2 * t + MOE_ROW_GROUP,), jnp.int32),
                        pltpu.SMEM((2 * t + MOE_ROW_GROUP,), F32),
                        pltpu.SMEM((n_e,), jnp.int32), pltpu.SemaphoreType.DMA((3,))],
        compiler_params=_cparams("arbitrary", "arbitrary"),
        name="moe_experts",
    )(d1, d2, g1, g2, counts, hf, x, gt, w1, w3, w2, *extra)


def moe_layer(x, nw, sh, sc, gate, group_w, group_b, expert_w, expert_b, layer, w1, w3, w2,
              final_w=None):
    bsz, t, d = x.shape
    hf, meta, cnt = moe_router(x, nw, sh, sc, group_w, group_b, expert_w, expert_b)
    dest = moe_dest(meta, cnt)
    d1, d2 = [dest[:, k].reshape(bsz * t) for k in range(2)]
    g1, g2 = [meta[:, f].reshape(bsz * t) for f in (4, 5)]
    counts = cnt[:, 0, ROUTER_EXPERT_LANE0:ROUTER_EXPERT_LANE0 + MOE_EXPERTS].astype(jnp.int32)
    return moe_experts(hf, x, gate, d1, d2, g1, g2, counts.reshape(-1), layer, w1, w3, w2, final_w)


def kernel(x, c, ada_w, ada_b, norm_mix_w, norm_ffn_w, ssm_in_w, ssm_conv_w, ssm_conv_b, ssm_dt_bias, ssm_a_log, ssm_d, ssm_norm_w, ssm_out_w, rwkv_mix, rwkv_rkv_w, rwkv_w0, rwkv_w1, rwkv_w2, rwkv_a0, rwkv_a1, rwkv_a2, rwkv_g1, rwkv_g2, rwkv_k_k, rwkv_k_a, rwkv_r_k, rwkv_ln_w, rwkv_ln_b, rwkv_out_w, ret_in_w, ret_out_w, att_in_w, att_out_w, moe_group_w, moe_group_b, moe_expert_w, moe_expert_b, moe_w1, moe_w3, moe_w2, final_norm_w):
    depth = ada_w.shape[0]
    d = x.shape[-1]
    mod = ada_modulation(c, ada_w, ada_b)
    for i in range(depth):
        sh1, sc1, gt1, sh2, sc2, gt2 = [mod[i][:, None, j * d:(j + 1) * d] for j in range(6)]
        kind, j = i % 4, i // 4
        pre = (x, norm_mix_w[i], sh1, sc1, gt1)
        if kind == 0:
            x = mamba2_layer(*pre, ssm_in_w[j], ssm_conv_w[j], ssm_conv_b[j], ssm_dt_bias[j],
                             ssm_a_log[j], ssm_d[j], ssm_norm_w[j], ssm_out_w[j])
        elif kind == 1:
            x = rwkv7_layer(*pre, rwkv_mix[j], rwkv_rkv_w[j], rwkv_w0[j], rwkv_w1[j], rwkv_w2[j],
                            rwkv_a0[j], rwkv_a1[j], rwkv_a2[j], rwkv_g1[j], rwkv_g2[j], rwkv_k_k[j],
                            rwkv_k_a[j], rwkv_r_k[j], rwkv_ln_w[j], rwkv_ln_b[j], rwkv_out_w[j])
        elif kind == 2:
            x = retention_layer(*pre, ret_in_w[j], ret_out_w[j])
        else:
            x = attention_layer(*pre, att_in_w[j], att_out_w[j])
        x = moe_layer(x, norm_ffn_w[i], sh2, sc2, gt2, moe_group_w[i], moe_group_b[i],
                      moe_expert_w[i], moe_expert_b[i], i, moe_w1, moe_w3, moe_w2,
                      final_w=final_norm_w if i == depth - 1 else None)
    return x
```

```python
import functools
import math

import jax
import jax.numpy as jnp
from jax import lax
from jax.experimental import pallas as pl
from jax.experimental.pallas import tpu as pltpu

F32 = jnp.float32
BF16 = jnp.bfloat16
HI = lax.Precision.HIGHEST

NORM_EPS = 1e-6
LANES = 128
VMEM_LIMIT = 56 * 1024 * 1024
PROJ_TM = 2048
PROJ_TN = 1024
DESTRIDE_STEP = 4

SSM_HEAD_DIM = 64
SSM_N_GROUPS = 8
SSM_HEADS_PER_GROUP = 4
SSM_D_STATE = 128
SSM_CONV = 4
SSM_CHUNK = 128
SSM_CONV_COLS = 256
RWKV_HEAD_DIM = 64
RWKV_LN_EPS = 64e-5
RWKV_CHUNK = 128
RWKV_SUB = 16
RWKV_INV_PRECISION = None
RWKV_APPLY_PRECISION = None
RET_N_HEADS = 4
RET_CHUNK = 128
ATT_GROUPS = ((128, 1), (512, 4), (2048, 16))
ATT_HEADS_PER_GROUP = 8
ATT_HEAD_DIM = 128
ATT_BLOCK = 128
ATT_STEP_BLOCKS = 2
MOE_GROUPS = 4
MOE_EXPERTS_PER_GROUP = 8
MOE_EXPERTS = 32
MOE_ROWS = 320
MOE_ROW_GROUP = 8


def _cparams(*sem):
    return pltpu.CompilerParams(dimension_semantics=sem, vmem_limit_bytes=VMEM_LIMIT)


def _sigmoid(x):
    return 1.0 / (1.0 + jnp.exp(-x))


def _silu(x):
    return x * _sigmoid(x)


def _softplus(x):
    return jnp.maximum(x, 0.0) + jnp.log(1.0 + jnp.exp(-jnp.abs(x)))


def _dot(a, b, precision=None):
    return jnp.dot(a, b, preferred_element_type=F32, precision=precision)


def _dot_nt(a, b, precision=None):
    return lax.dot_general(a, b, (((1,), (1,)), ((), ())), preferred_element_type=F32,
                           precision=precision)


def _dot_tn(a, b, precision=None):
    return lax.dot_general(a, b, (((0,), (0,)), ((), ())), preferred_element_type=F32,
                           precision=precision)


def _split_bf16(a):
    hi = a.astype(BF16)
    return hi, (a - hi.astype(F32)).astype(BF16)


def _dot_split(a, m, transpose_rhs=False, lhs_exact=False):
    hi, lo = _split_bf16(a)
    mb = m.astype(BF16)
    if lhs_exact:
        return _dot(mb, hi) + _dot(mb, lo)
    dot = _dot_nt if transpose_rhs else _dot
    return dot(hi, mb) + dot(lo, mb)


def _dot_3pass(a, b):
    a_hi, a_lo = _split_bf16(a)
    b_hi, b_lo = _split_bf16(b)
    return _dot(a_hi, b_hi) + (_dot(a_lo, b_hi) + _dot(a_hi, b_lo))


def _iota2(shape, axis):
    return lax.broadcasted_iota(jnp.int32, shape, axis)


def _ada_kernel(c_ref, w_ref, b_ref, o_ref):
    cs = _silu(c_ref[...])
    o_ref[0] = _dot(cs, w_ref[0], HI) + b_ref[0]


def ada_modulation(c, ada_w, ada_b):
    depth, d, n = ada_w.shape
    bsz = c.shape[0]
    rows = -(-bsz // 8) * 8
    cp = jnp.pad(c, ((0, rows - bsz), (0, 0)))
    tn = 3072
    out = pl.pallas_call(
        _ada_kernel,
        grid=(depth, n // tn),
        in_specs=[pl.BlockSpec((rows, d), lambda l, j: (0, 0)),
                  pl.BlockSpec((1, d, tn), lambda l, j: (l, 0, j)),
                  pl.BlockSpec((1, 1, tn), lambda l, j: (l, 0, j))],
        out_specs=pl.BlockSpec((1, rows, tn), lambda l, j: (l, 0, j)),
        out_shape=jax.ShapeDtypeStruct((depth, rows, n), F32),
        compiler_params=_cparams("parallel", "parallel"),
        name="ada_modulation",
    )(cp, ada_w, ada_b.reshape(depth, 1, n))
    return out[:, :bsz]


def _norm_mod(x, nw, sh, sc):
    y = x * lax.rsqrt(jnp.mean(x * x, axis=-1, keepdims=True) + NORM_EPS) * nw
    return y * (1.0 + sc) + sh


def _nmm_kernel(*refs, dil, has_side):
    if has_side:
        x_ref, nw_ref, sh_ref, sc_ref, w_ref, ws_ref, o_ref, side_ref, h_ref, *rest = refs
    else:
        x_ref, nw_ref, sh_ref, sc_ref, w_ref, o_ref, h_ref, *rest = refs

    @pl.when(pl.program_id(2) == 0)
    def _():
        h = _norm_mod(x_ref[0], nw_ref[...], sh_ref[0], sc_ref[0]).astype(h_ref.dtype)
        h_ref[...] = h
        if has_side:
            w_hi, w_lo = _split_bf16(ws_ref[...])
            side_ref[0] = _dot(h, w_hi) + _dot(h, w_lo)

    res = _dot(h_ref[...], w_ref[...])
    if dil is None:
        o_ref[0] = res.astype(o_ref.dtype)
    else:
        acc_ref, *tmp = rest
        tm = acc_ref.shape[1]
        rows = tm // dil
        for c in range(acc_ref.shape[0]):
            acc_ref[c] = res[:, c * LANES:(c + 1) * LANES]
        f1 = min(dil, DESTRIDE_STEP)
        f2 = dil // f1
        for c in range(acc_ref.shape[0]):
            cols = slice(c * LANES, (c + 1) * LANES)
            if f2 == 1:
                for r in range(dil):
                    o_ref[0, r, :, cols] = acc_ref[c, pl.ds(r, rows, stride=dil), :].astype(o_ref.dtype)
            else:
                tmp_ref, = tmp
                for b in range(f1):
                    tmp_ref[c, b * (tm // f1):(b + 1) * (tm // f1), :] = (
                        acc_ref[c, pl.ds(b, tm // f1, stride=f1), :])
                for a in range(f2):
                    for b in range(f1):
                        o_ref[0, a * f1 + b, :, cols] = tmp_ref[
                            c, pl.ds(b * (tm // f1) + a, rows, stride=f2), :].astype(o_ref.dtype)


def norm_mod_matmul(x, nw, sh, sc, w, *, tm, tn, dil=None, side_w=None, n_out=None,
                    w_tile=lambda j: j):
    bsz, t, d = x.shape
    n = w.shape[1] if n_out is None else n_out
    tm = min(tm, t)
    scratch = [pltpu.VMEM((tm, d), BF16)]
    in_specs = [pl.BlockSpec((1, tm, d), lambda b, i, j: (b, i, 0)),
                pl.BlockSpec((1, d), lambda b, i, j: (0, 0)),
                pl.BlockSpec((1, 1, d), lambda b, i, j: (b, 0, 0)),
                pl.BlockSpec((1, 1, d), lambda b, i, j: (b, 0, 0)),
                pl.BlockSpec((d, tn), lambda b, i, j: (0, w_tile(j)))]
    ins = [x, nw.reshape(1, d), sh, sc, w]
    if dil is None:
        out_specs = [pl.BlockSpec((1, tm, tn), lambda b, i, j: (b, i, j))]
        out_shape = [jax.ShapeDtypeStruct((bsz, t, n), BF16)]
    else:
        out_specs = [pl.BlockSpec((1, dil, tm // dil, tn), lambda b, i, j: (b, 0, i, j))]
        out_shape = [jax.ShapeDtypeStruct((bsz, dil, t // dil, n), BF16)]
        scratch += [pltpu.VMEM((tn // LANES, tm, LANES), F32)] * (2 if dil > DESTRIDE_STEP else 1)
    if side_w is not None:
        in_specs.append(pl.BlockSpec((d, LANES), lambda b, i, j: (0, 0)))
        ins.append(side_w)
        out_specs.append(pl.BlockSpec((1, tm, LANES), lambda b, i, j: (b, i, 0)))
        out_shape.append(jax.ShapeDtypeStruct((bsz, t, LANES), F32))
    outs = pl.pallas_call(
        functools.partial(_nmm_kernel, dil=dil, has_side=side_w is not None),
        grid=(bsz, t // tm, n // tn),
        in_specs=in_specs,
        out_specs=out_specs,
        out_shape=out_shape,
        scratch_shapes=scratch,
        compiler_params=_cparams("parallel", "parallel", "arbitrary"),
        name="norm_mod_matmul",
    )(*ins)
    return outs if side_w is not None else outs[0]


def _mgr_kernel(*refs, has_mul):
    if has_mul:
        y_ref, g_ref, w_ref, x_ref, gate_ref, o_ref = refs
        y = (y_ref[0].astype(F32) * g_ref[0].astype(F32)).astype(BF16)
    else:
        y_ref, w_ref, x_ref, gate_ref, o_ref = refs
        y = y_ref[0].astype(BF16)
    o_ref[0] = x_ref[0] + gate_ref[0] * _dot(y, w_ref[...])


def matmul_gated_residual(y, w, x, gate, mul=None, *, tm=512):
    bsz, t, k = y.shape
    d = w.shape[1]
    tm = min(tm, t)
    ins = [y] + ([mul] if mul is not None else []) + [w, x, gate]
    row = lambda width: pl.BlockSpec((1, tm, width), lambda b, i: (b, i, 0))
    specs = [row(k)] + ([row(k)] if mul is not None else []) + [
        pl.BlockSpec((k, d), lambda b, i: (0, 0)), row(d),
        pl.BlockSpec((1, 1, d), lambda b, i: (b, 0, 0))]
    return pl.pallas_call(
        functools.partial(_mgr_kernel, has_mul=mul is not None),
        grid=(bsz, t // tm),
        in_specs=specs,
        out_specs=row(d),
        out_shape=jax.ShapeDtypeStruct((bsz, t, d), F32),
        compiler_params=_cparams("parallel", "parallel"),
        name="matmul_gated_residual",
    )(*ins)


def _ssd_kernel(zx_ref, dt_ref, cw_ref, cb_ref, dtb_ref, alog_ref, dsk_ref, nw_ref, o_ref,
                prev_ref, xbc_ref, state_ref):
    q = SSM_CHUNK
    g_n, hg, p, n = SSM_N_GROUPS, SSM_HEADS_PER_GROUP, SSM_HEAD_DIM, SSM_D_STATE
    d_inner = g_n * hg * p
    nh = g_n * hg
    gw = hg * p

    @pl.when(pl.program_id(1) == 0)
    def _():
        prev_ref[...] = jnp.zeros_like(prev_ref)
        state_ref[...] = jnp.zeros_like(state_ref)

    dt = _softplus(dt_ref[0][:, :nh] + dtb_ref[...])
    da = dt * -jnp.exp(alog_ref[...])
    tril = (_iota2((q, q), 0) >= _iota2((q, q), 1))
    acum = _dot_split(da, tril.astype(F32), lhs_exact=True)
    a_end = acum[q - 1:q, :]
    chunk_decay = jnp.exp(a_end)
    adj_t = (acum - jnp.log(dt)).T
    wdec_t = (dt * jnp.exp(a_end - acum)).T

    src = _iota2((q, 2 * q), 1) - q - _iota2((q, 2 * q), 0)
    shifts = [jnp.where(src == -s, 1.0, 0.0).astype(BF16) for s in range(1, SSM_CONV)]
    cdim = xbc_ref.shape[1]
    for c0 in range(0, cdim, SSM_CONV_COLS):
        cs = slice(c0, c0 + SSM_CONV_COLS)
        cur = zx_ref[0, :, d_inner + c0:d_inner + c0 + SSM_CONV_COLS]
        ext = jnp.concatenate([prev_ref[:, cs], cur], axis=0)
        prev_ref[:, cs] = cur
        acc = cb_ref[:, cs] + cur.astype(F32) * cw_ref[SSM_CONV - 1:SSM_CONV, cs]
        for s in range(1, SSM_CONV):
            acc = acc + _dot(shifts[s - 1], ext) * cw_ref[SSM_CONV - 1 - s:SSM_CONV - s, cs]
        xbc_ref[:, cs] = _silu(acc)

    for g in range(g_n):
        xs_g = xbc_ref[:, g * gw:(g + 1) * gw]
        bm = xbc_ref[:, d_inner + g * n:d_inner + (g + 1) * n]
        cm = xbc_ref[:, d_inner + g_n * n + g * n:d_inner + g_n * n + (g + 1) * n]
        cm_b = cm.astype(BF16)
        cb = _dot_nt(cm_b, bm.astype(BF16))
        bm_t = bm.T
        xs_b = xs_g.astype(BF16)
        st_old = state_ref[g]
        y_carry = _dot(cm_b, st_old.astype(BF16))
        hd = [g * hg + hh for hh in range(hg)]
        hs = [slice(hh * p, (hh + 1) * p) for hh in range(hg)]
        a_col = [jnp.broadcast_to(acum[:, h:h + 1], (q, q)) for h in hd]
        m_h = [cb * jnp.exp(jnp.where(tril, ac - adj_t[h:h + 1, :], -jnp.inf))
               for h, ac in zip(hd, a_col)]
        st_new = [_dot((bm_t * wdec_t[h:h + 1, :]).astype(BF16), xs_b[:, c])
                  for h, c in zip(hd, hs)]
        ys = [_dot(m.astype(BF16), xs_b[:, c]) + jnp.exp(ac[:, :p]) * y_carry[:, c]
              for m, ac, c in zip(m_h, a_col, hs)]
        state_ref[g] = jnp.concatenate(
            [st_old[:, c] * chunk_decay[:, h:h + 1] + sn for h, c, sn in zip(hd, hs, st_new)], axis=1)
        y = jnp.concatenate(ys, axis=1) + xs_g * dsk_ref[:, g * gw:(g + 1) * gw]
        z = zx_ref[0, :, g * gw:(g + 1) * gw].astype(F32)
        y = y * _silu(z)
        y = y * lax.rsqrt(jnp.mean(y * y, axis=-1, keepdims=True) + NORM_EPS)
        o_ref[0, :, g * gw:(g + 1) * gw] = (y * nw_ref[:, g * gw:(g + 1) * gw]).astype(o_ref.dtype)


def ssd_core(zx, dt_raw, conv_w, conv_b, dt_bias, a_log, d_skip, norm_w):
    bsz, t, width = zx.shape
    d_inner = SSM_N_GROUPS * SSM_HEADS_PER_GROUP * SSM_HEAD_DIM
    cdim = width - d_inner
    nh = dt_bias.shape[0]
    q = SSM_CHUNK
    full = lambda shape: pl.BlockSpec(shape, lambda b, c: (0,) * len(shape))
    return pl.pallas_call(
        _ssd_kernel,
        grid=(bsz, t // q),
        in_specs=[pl.BlockSpec((1, q, width), lambda b, c: (b, c, 0)),
                  pl.BlockSpec((1, q, LANES), lambda b, c: (b, c, 0)),
                  full((SSM_CONV, cdim)), full((1, cdim)), full((1, nh)), full((1, nh)),
                  full((1, d_inner)), full((1, d_inner))],
        out_specs=pl.BlockSpec((1, q, d_inner), lambda b, c: (b, c, 0)),
        out_shape=jax.ShapeDtypeStruct((bsz, t, d_inner), BF16),
        scratch_shapes=[pltpu.VMEM((q, cdim), BF16), pltpu.VMEM((q, cdim), F32),
                        pltpu.VMEM((SSM_N_GROUPS, SSM_D_STATE,
                                    SSM_HEADS_PER_GROUP * SSM_HEAD_DIM), F32)],
        compiler_params=_cparams("parallel", "arbitrary"),
        name="ssd_core",
    )(zx, dt_raw, conv_w, conv_b.reshape(1, cdim), dt_bias.reshape(1, nh),
      a_log.reshape(1, nh), jnp.repeat(d_skip, SSM_HEAD_DIM).reshape(1, d_inner),
      norm_w.reshape(1, d_inner))


def mamba2_layer(x, nw, sh, sc, gate, in_w, conv_w, conv_b, dt_bias, a_log, d_skip, norm_w, out_w):
    d_inner = SSM_N_GROUPS * SSM_HEADS_PER_GROUP * SSM_HEAD_DIM
    cdim = conv_w.shape[1]
    nh = dt_bias.shape[0]
    w_dt = jnp.pad(in_w[:, d_inner + cdim:], ((0, 0), (0, LANES - nh)))
    zx, dt_raw = norm_mod_matmul(x, nw, sh, sc, in_w.astype(BF16), tm=PROJ_TM, tn=PROJ_TN,
                                 side_w=w_dt, n_out=d_inner + cdim)
    y = ssd_core(zx, dt_raw, conv_w, conv_b, dt_bias, a_log, d_skip, norm_w)
    return matmul_gated_residual(y, out_w.astype(BF16), x, gate)


def _head_indicator(d, n):
    return jnp.where(_iota2((d, LANES), 0) // n == _iota2((d, LANES), 1), 1.0, 0.0)


def _rwkv_prep_kernel(x_ref, xp_ref, nw_ref, sh_ref, sc_ref, mix_ref, wr_ref, wk_ref, wv_ref,
                      w0_ref, w1_ref, w2_ref, a0_ref, a1_ref, a2_ref, g1_ref, g2_ref, kk_ref, ka_ref,
                      r_o, lw_o, k_o, v_o, kn_o, b_o, g_o):
    tm, d = x_ref.shape[1], x_ref.shape[2]
    nw, sh, sc = nw_ref[...], sh_ref[0], sc_ref[0]
    hm = _norm_mod(x_ref[0], nw, sh, sc)
    prev = _norm_mod(xp_ref[0], nw, sh, sc)[7:8, :]
    prev = jnp.where(pl.program_id(1) > 0, prev, 0.0)
    shifted = jnp.where(_iota2((tm, d), 0) == 0, prev, pltpu.roll(hm, 1, axis=0))
    xx = shifted - hm
    mixed = lambda j: (hm + xx * mix_ref[j:j + 1, :]).astype(BF16)
    xr, xw, xk, xv, xa, xg = [mixed(j) for j in range(6)]
    r = _dot(xr, wr_ref[...])
    k = _dot(xk, wk_ref[...])
    v = _dot(xv, wv_ref[...])
    lora = lambda u, w: _dot(u.astype(BF16), w[...])
    w_log = -_softplus(-(w0_ref[...] + lora(jnp.tanh(lora(xw, w1_ref)), w2_ref))) - 0.5
    a = _sigmoid(a0_ref[...] + lora(lora(xa, a1_ref), a2_ref))
    g = lora(_sigmoid(lora(xg, g1_ref)), g2_ref)
    kk = k * kk_ref[...]
    ind = _head_indicator(d, RWKV_HEAD_DIM)
    nrm = jnp.maximum(jnp.sqrt(_dot_split(kk * kk, ind)), 1e-12)
    kn = kk * _dot_split(1.0 / nrm, ind, transpose_rhs=True)
    r_o[0] = r.astype(r_o.dtype)
    lw_o[0] = -jnp.exp(w_log)
    k_o[0] = (k * (1.0 + (a - 1.0) * ka_ref[...])).astype(k_o.dtype)
    v_o[0] = v.astype(v_o.dtype)
    kn_o[0] = kn.astype(kn_o.dtype)
    b_o[0] = (kn * a).astype(b_o.dtype)
    g_o[0] = g.astype(g_o.dtype)


def rwkv_prep(x, nw, sh, sc, mix, rkv_w, w0, w1, w2, a0, a1, a2, g1, g2, k_k, k_a, *, tm=512):
    bsz, t, d = x.shape
    tm = min(tm, t)
    full = lambda arr: pl.BlockSpec(arr.shape, lambda b, i: (0,) * arr.ndim)
    row = pl.BlockSpec((1, tm, d), lambda b, i: (b, i, 0))
    vec = pl.BlockSpec((1, 1, d), lambda b, i: (b, 0, 0))
    bf = lambda w: w.astype(BF16)
    params = [mix, bf(rkv_w[0]), bf(rkv_w[1]), bf(rkv_w[2]), w0.reshape(1, d), bf(w1), bf(w2),
              a0.reshape(1, d), bf(a1), bf(a2), bf(g1), bf(g2), k_k.reshape(1, d), k_a.reshape(1, d)]
    prev_spec = pl.BlockSpec((1, 8, d), lambda b, i: (b, jnp.maximum(i * (tm // 8) - 1, 0), 0))
    return pl.pallas_call(
        _rwkv_prep_kernel,
        grid=(bsz, t // tm),
        in_specs=[row, prev_spec, pl.BlockSpec((1, d), lambda b, i: (0, 0)), vec, vec]
        + [full(p) for p in params],
        out_specs=[row] * 7,
        out_shape=[jax.ShapeDtypeStruct((bsz, t, d), F32 if i == 1 else BF16) for i in range(7)],
        compiler_params=_cparams("parallel", "parallel"),
        name="rwkv_prep",
    )(x, x, nw.reshape(1, d), sh, sc, *params)


def _mm(a, b, precision):
    if precision is None:
        return _dot(a.astype(BF16), b.astype(BF16))
    return _dot(a, b, precision)


def _mm_nt(a, b, precision):
    if precision is None:
        return _dot_nt(a.astype(BF16), b.astype(BF16))
    return _dot_nt(a, b, precision)


def _mm_tn(a, b, precision):
    if precision is None:
        return _dot_tn(a.astype(BF16), b.astype(BF16))
    return _dot_tn(a, b, precision)


def _unit_lower_inverse(m_list, length, sub, precision):
    ri = _iota2((length, length), 0)
    ci = _iota2((length, length), 1)
    eye = jnp.where(ri == ci, 1.0, 0.0)
    same_block = ri // sub == ci // sub
    mm = lambda a, b: _mm(a, b, precision)
    md = [jnp.where(same_block, m, 0.0) for m in m_list]
    off = [m - d for m, d in zip(m_list, md)]
    dinv = [eye + d for d in md]
    pw = md
    for _ in range(int(math.log2(sub)) - 1):
        pw = [mm(p, p) for p in pw]
        dinv = [x + mm(x, p) for x, p in zip(dinv, pw)]
    pm = [mm(x, o) for x, o in zip(dinv, off)]
    acc = [eye + p for p in pm]
    pw = pm
    for _ in range(int(math.log2(length // sub)) - 1):
        pw = [mm(p, p) for p in pw]
        acc = [x + mm(x, p) for x, p in zip(acc, pw)]
    return [mm(x, d) for x, d in zip(acc, dinv)]


def _rwkv_core_kernel(r_ref, lw_ref, k_ref, v_ref, kn_ref, b_ref, rk_ref, lnw_ref, lnb_ref, o_ref,
                      state_ref, *, inv_precision, apply_precision):
    length, d = r_ref.shape[1], r_ref.shape[2]
    n = RWKV_HEAD_DIM
    heads = range(d // n)
    mm = lambda a, b: _mm(a, b, apply_precision)

    @pl.when(pl.program_id(1) == 0)
    def _():
        state_ref[...] = jnp.zeros_like(state_ref)

    ri = _iota2((2 * length, 2 * length), 0)
    ci = _iota2((2 * length, 2 * length), 1) % length
    keep = ((ri < length) & (ri > ci)) | ((ri >= length) & (ri - length >= ci))
    eye_n = _iota2((n, n), 0) == _iota2((n, n), 1)

    lw = lw_ref[0]
    r, k, v, kn, b = [ref[0].astype(F32) for ref in (r_ref, k_ref, v_ref, kn_ref, b_ref)]
    tri = jnp.where(_iota2((length, length), 0) >= _iota2((length, length), 1), 1.0, 0.0)
    cum = _dot_split(lw, tri, lhs_exact=True)
    c_end = cum[length - 1:length, :]
    e_neg = jnp.exp(-cum)
    e_rem = jnp.exp(c_end - cum)
    ar = jnp.concatenate([-kn * jnp.exp(cum - lw), r * jnp.exp(cum)], axis=0)
    bk = jnp.concatenate([b * e_neg, k * e_neg], axis=0)
    bk_end = jnp.concatenate([b * e_rem, k * e_rem], axis=0)
    w_end = jnp.exp(c_end)
    bonus_w = r * k * rk_ref[...]
    hs = [slice(h * n, (h + 1) * n) for h in heads]

    s0 = [state_ref[h] for h in heads]
    p = [jnp.where(keep, _mm_nt(ar[:, c], bk[:, c], inv_precision), 0.0) for c in hs]
    t_inv = _unit_lower_inverse([x[:length, :length] for x in p], length, RWKV_SUB, inv_precision)
    q = [mm(ar[:, c], s) for c, s in zip(hs, s0)]
    y = [x[:length] + mm(pp[:length, length:], v[:, c]) for x, pp, c in zip(q, p, hs)]
    u = [mm(ti, yy) for ti, yy in zip(t_inv, y)]
    uv = [jnp.concatenate([uu, v[:, c]], axis=0) for uu, c in zip(u, hs)]
    o = [x[length:] + mm(pp[length:], w) for x, pp, w in zip(q, p, uv)]
    upd = [_mm_tn(bk_end[:, c], w, apply_precision) for c, w in zip(hs, uv)]
    w_col = [jnp.sum(jnp.where(eye_n, w_end[:, c], 0.0), axis=1, keepdims=True) for c in hs]
    for h in heads:
        state_ref[h] = w_col[h] * s0[h] + upd[h]
    ind = _head_indicator(d, n)
    head_sum = lambda x: _dot_split(_dot_split(x, ind), ind, transpose_rhs=True)
    o_all = jnp.concatenate(o, axis=1)
    cen = o_all - head_sum(o_all) * (1.0 / n)
    var = head_sum(cen * cen) * (1.0 / n)
    y = cen * lax.rsqrt(var + RWKV_LN_EPS) * lnw_ref[...] + lnb_ref[...]
    o_ref[0] = (y + head_sum(bonus_w) * v).astype(o_ref.dtype)


def rwkv_core(r, lw, k, v, kn, b, r_k, ln_w, ln_b):
    bsz, t, d = r.shape
    length = RWKV_CHUNK
    n = RWKV_HEAD_DIM
    row = pl.BlockSpec((1, length, d), lambda bb, c: (bb, c, 0))
    vec = pl.BlockSpec((1, d), lambda bb, c: (0, 0))
    return pl.pallas_call(
        functools.partial(_rwkv_core_kernel, inv_precision=RWKV_INV_PRECISION,
                          apply_precision=RWKV_APPLY_PRECISION),
        grid=(bsz, t // length),
        in_specs=[row] * 6 + [vec] * 3,
        out_specs=row,
        out_shape=jax.ShapeDtypeStruct((bsz, t, d), BF16),
        scratch_shapes=[pltpu.VMEM((d // n, n, n), F32)],
        compiler_params=_cparams("parallel", "arbitrary"),
        name="rwkv_core",
    )(r, lw, k, v, kn, b, r_k.reshape(1, d), ln_w.reshape(1, d), ln_b.reshape(1, d))


def rwkv7_layer(x, nw, sh, sc, gate, mix, rkv_w, w0, w1, w2, a0, a1, a2, g1, g2, k_k, k_a, r_k,
                ln_w, ln_b, out_w):
    r, lw, k, v, kn, b, g = rwkv_prep(x, nw, sh, sc, mix, rkv_w, w0, w1, w2, a0, a1, a2, g1, g2,
                                      k_k, k_a)
    y = rwkv_core(r, lw, k, v, kn, b, r_k, ln_w, ln_b)
    return matmul_gated_residual(y, out_w.astype(BF16), x, gate, mul=g)


def _ret_kernel(qkvg_ref, idec_ref, qdec_ref, kdec_ref, cdec_ref, o_ref, state_ref, *, d_model):
    nh = RET_N_HEADS
    dk = d_model // nh
    dv = 2 * d_model // nh
    k_scale = dk ** -0.5

    @pl.when(pl.program_id(1) == 0)
    def _():
        state_ref[...] = jnp.zeros_like(state_ref)

    for h in range(nh):
        q_b = qkvg_ref[0, :, h * dk:(h + 1) * dk]
        k_f = qkvg_ref[0, :, d_model + h * dk:d_model + (h + 1) * dk].astype(F32) * k_scale
        v_b = qkvg_ref[0, :, 2 * d_model + h * dv:2 * d_model + (h + 1) * dv]
        g_f = qkvg_ref[0, :, 4 * d_model + h * dv:4 * d_model + (h + 1) * dv].astype(F32)
        scores = _dot_nt(q_b, k_f.astype(BF16)) * idec_ref[h]
        inner = _dot(scores.astype(BF16), v_b)
        r_old = state_ref[h]
        qd = (q_b.astype(F32) * qdec_ref[:, h:h + 1]).astype(BF16)
        cross = _dot(qd, r_old.astype(BF16))
        kd = (k_f * kdec_ref[:, h:h + 1]).astype(BF16)
        state_ref[h] = r_old * cdec_ref[:, h:h + 1] + _dot_tn(kd, v_b)
        o = inner + cross
        o = o * lax.rsqrt(jnp.mean(o * o, axis=-1, keepdims=True) + NORM_EPS)
        o_ref[0, :, h * dv:(h + 1) * dv] = (_silu(g_f) * o).astype(o_ref.dtype)


def retention_core(qkvg, d_model):
    bsz, t, width = qkvg.shape
    nh, q = RET_N_HEADS, RET_CHUNK
    dk, dv = d_model // nh, 2 * d_model // nh
    log_gamma = jnp.log(1 - jnp.exp2(-5.0 - jnp.arange(nh, dtype=F32)))
    idx = jnp.arange(q)
    rel = idx[:, None] - idx[None, :]
    inner_decay = jnp.where(rel >= 0,
                            jnp.exp(jnp.maximum(rel, 0).astype(F32) * log_gamma[:, None, None]), 0.0)
    q_decay = jnp.exp((idx + 1).astype(F32)[:, None] * log_gamma)
    k_decay = jnp.exp((q - 1 - idx).astype(F32)[:, None] * log_gamma)
    chunk_decay = jnp.exp(q * log_gamma).reshape(1, nh)
    full = lambda shape: pl.BlockSpec(shape, lambda b, c: (0,) * len(shape))
    return pl.pallas_call(
        functools.partial(_ret_kernel, d_model=d_model),
        grid=(bsz, t // q),
        in_specs=[pl.BlockSpec((1, q, width), lambda b, c: (b, c, 0)),
                  full((nh, q, q)), full((q, nh)), full((q, nh)), full((1, nh))],
        out_specs=pl.BlockSpec((1, q, nh * dv), lambda b, c: (b, c, 0)),
        out_shape=jax.ShapeDtypeStruct((bsz, t, nh * dv), BF16),
        scratch_shapes=[pltpu.VMEM((nh, dk, dv), F32)],
        compiler_params=_cparams("parallel", "arbitrary"),
        name="retention_core",
    )(qkvg, inner_decay, q_decay, k_decay, chunk_decay)


def retention_layer(x, nw, sh, sc, gate, in_w, out_w):
    d = x.shape[-1]
    qkvg = norm_mod_matmul(x, nw, sh, sc, in_w.astype(BF16), tm=PROJ_TM, tn=PROJ_TN)
    y = retention_core(qkvg, d)
    return matmul_gated_residual(y, out_w.astype(BF16), x, gate)


def _att_kernel(q_ref, k_ref, v_ref, sl_ref, o_ref, lse_ref, kprev_ref, vprev_ref, *, dil, span):
    w = ATT_BLOCK
    e = ATT_HEAD_DIM
    nblk = pl.program_id(2)

    @pl.when(nblk == 0)
    def _():
        kprev_ref[...] = jnp.zeros_like(kprev_ref)
        vprev_ref[...] = jnp.zeros_like(vprev_ref)

    n_sub = q_ref.shape[2] // w
    qi = _iota2((w, 2 * w), 0)
    ci = _iota2((w, 2 * w), 1)
    delta = w + qi - ci
    band = (delta >= 0) & (delta <= span)
    first_mask = band & ((nblk > 0) | (ci >= w))
    dist = (dil * delta).astype(F32)
    lane = _iota2((w, LANES), 1)
    heads = range(ATT_HEADS_PER_GROUP)
    cols = [slice(h * e, (h + 1) * e) for h in heads]

    def keys(ref, prev_ref, sub, c):
        before = prev_ref[:, c] if sub == 0 else ref[0, 0, (sub - 1) * w:sub * w, c]
        return jnp.concatenate([before, ref[0, 0, sub * w:(sub + 1) * w, c]], axis=0)

    units = [(sub, h) for sub in range(n_sub) for h in heads]
    s = [_dot_nt(q_ref[0, 0, sub * w:(sub + 1) * w, cols[h]], keys(k_ref, kprev_ref, sub, cols[h]))
         for sub, h in units]
    s = [jnp.where(first_mask if sub == 0 else band,
                   x * (e ** -0.5) - sl_ref[:, h:h + 1] * dist, -jnp.inf)
         for (sub, h), x in zip(units, s)]
    m = [jnp.max(x, axis=-1, keepdims=True) for x in s]
    p = [jnp.exp(x - mx) for x, mx in zip(s, m)]
    den = [jnp.sum(x, axis=-1, keepdims=True) for x in p]
    lse_all = [jnp.zeros((w, LANES), F32) for _ in range(n_sub)]
    for u, (sub, h) in enumerate(units):
        o = _dot((p[u] / den[u]).astype(BF16), keys(v_ref, vprev_ref, sub, cols[h]))
        o_ref[0, 0, sub * w:(sub + 1) * w, cols[h]] = o.astype(o_ref.dtype)
        lse_all[sub] = jnp.where(lane == h, m[u] + jnp.log(den[u]), lse_all[sub])
    for sub in range(n_sub):
        lse_ref[0, 0, sub * w:(sub + 1) * w, :] = lse_all[sub]
    kprev_ref[...] = k_ref[0, 0, (n_sub - 1) * w:, :]
    vprev_ref[...] = v_ref[0, 0, (n_sub - 1) * w:, :]


def dilated_group_core(qkv, gi, window, dil, slopes):
    bsz, _, ls, _ = qkv.shape
    w = ATT_BLOCK
    gw = ATT_HEADS_PER_GROUP * ATT_HEAD_DIM
    rows = min(ls, ATT_STEP_BLOCKS * w)
    spec = lambda part: pl.BlockSpec((1, 1, rows, gw), lambda b, r, n: (b, r, n, part))
    return pl.pallas_call(
        functools.partial(_att_kernel, dil=dil, span=window // dil),
        grid=(bsz, dil, ls // rows),
        in_specs=[spec(0), spec(1), spec(2),
                  pl.BlockSpec((1, ATT_HEADS_PER_GROUP), lambda b, r, n: (0, 0))],
        out_specs=[pl.BlockSpec((1, 1, rows, gw), lambda b, r, n: (b, r, n, 0)),
                   pl.BlockSpec((1, 1, rows, LANES), lambda b, r, n: (b, r, n, 0))],
        out_shape=[jax.ShapeDtypeStruct((bsz, dil, ls, gw), BF16),
                   jax.ShapeDtypeStruct((bsz, dil, ls, LANES), F32)],
        scratch_shapes=[pltpu.VMEM((w, gw), BF16), pltpu.VMEM((w, gw), BF16)],
        compiler_params=_cparams("parallel", "parallel", "arbitrary"),
        name="dilated_attention_g%d" % gi,
    )(qkv, qkv, qkv, slopes.reshape(1, ATT_HEADS_PER_GROUP))


def _att_out_kernel(o0_ref, o1_ref, o2_ref, l0_ref, l1_ref, l2_ref, w_ref, x_ref, gate_ref, o_ref,
                    *scratch):
    tm = x_ref.shape[1]

    def token_major(ref, scr):
        dil = ref.shape[1]
        if dil == 1:
            return ref[0, 0].astype(F32)
        f1 = min(dil, DESTRIDE_STEP)
        f2 = dil // f1
        n1 = tm // f1
        for c in range(scr.shape[0]):
            cols = slice(c * LANES, (c + 1) * LANES)
            if f2 == 1:
                for r in range(dil):
                    scr[c, pl.ds(r, tm // dil, stride=dil), :] = ref[0, r, :, cols].astype(F32)
            else:
                for b in range(f1):
                    for a in range(f2):
                        tmp_ref[c, pl.ds(b * n1 + a, tm // dil, stride=f2), :] = (
                            ref[0, a * f1 + b, :, cols].astype(F32))
                for b in range(f1):
                    scr[c, pl.ds(b, n1, stride=f1), :] = tmp_ref[c, b * n1:(b + 1) * n1, :]
        return jnp.concatenate([scr[c] for c in range(scr.shape[0])], axis=1)

    *scratch, tmp_ref = scratch
    o1, l1, o2, l2 = [token_major(ref, scr) for ref, scr in
                      zip((o1_ref, l1_ref, o2_ref, l2_ref), scratch)]
    o0, l0 = token_major(o0_ref, None), token_major(l0_ref, None)
    m = jnp.maximum(jnp.maximum(l0, l1), l2)
    w0, w1, w2 = jnp.exp(l0 - m), jnp.exp(l1 - m), jnp.exp(l2 - m)
    inv = 1.0 / (w0 + w1 + w2)
    k = o0.shape[1]
    expand = jnp.where(_iota2((LANES, k), 1) // ATT_HEAD_DIM == _iota2((LANES, k), 0), 1.0, 0.0)
    y = (_dot_split(w0 * inv, expand) * o0 + _dot_split(w1 * inv, expand) * o1
         + _dot_split(w2 * inv, expand) * o2)
    o_ref[0] = x_ref[0] + gate_ref[0] * _dot(y.astype(BF16), w_ref[...])


def attention_combine_out(outs, lses, w, x, gate, *, tm=512):
    bsz, t, d = x.shape
    k = w.shape[0]
    tm = min(tm, t)
    row = pl.BlockSpec((1, tm, d), lambda b, i: (b, i, 0))
    res = lambda a: pl.BlockSpec((1, a.shape[1], tm // a.shape[1], a.shape[3]),
                                 lambda b, i: (b, 0, i, 0))
    wide, narrow = pltpu.VMEM((k // LANES, tm, LANES), F32), pltpu.VMEM((1, tm, LANES), F32)
    return pl.pallas_call(
        _att_out_kernel,
        grid=(bsz, t // tm),
        in_specs=[res(a) for a in outs] + [res(a) for a in lses] + [
            pl.BlockSpec((k, d), lambda b, i: (0, 0)), row,
            pl.BlockSpec((1, 1, d), lambda b, i: (b, 0, 0))],
        out_specs=row,
        out_shape=jax.ShapeDtypeStruct((bsz, t, d), F32),
        scratch_shapes=[wide, narrow, wide, narrow, wide],
        compiler_params=_cparams("parallel", "parallel"),
        name="attention_combine_out",
    )(*outs, *lses, w, x, gate)


def attention_layer(x, nw, sh, sc, gate, in_w, out_w):
    n_groups = len(ATT_GROUPS)
    gw = ATT_HEADS_PER_GROUP * ATT_HEAD_DIM
    n_heads = n_groups * ATT_HEADS_PER_GROUP
    slopes = jnp.exp2(-8.0 * jnp.arange(1, n_heads + 1, dtype=F32) / n_heads)
    slopes = slopes.reshape(n_groups, ATT_HEADS_PER_GROUP)
    w_in = in_w.astype(BF16)
    outs, lses = [], []
    for gi, (window, dil) in enumerate(ATT_GROUPS):
        qkv = norm_mod_matmul(x, nw, sh, sc, w_in, tm=PROJ_TM, tn=gw, dil=dil, n_out=3 * gw,
                              w_tile=lambda j, gi=gi: j * n_groups + gi)
        o, lse = dilated_group_core(qkv, gi, window, dil, slopes[gi])
        outs.append(o)
        lses.append(lse)
    return attention_combine_out(outs, lses, out_w.astype(BF16), x, gate)


ROUTER_GROUP_LANE0 = 0
ROUTER_EXPERT_LANE0 = MOE_GROUPS
ROUTER_FIELDS = 6
ROUTER_PREFIX_ROWS = 256


def _router_kernel(x_ref, nw_ref, sh_ref, sc_ref, wr_ref, br_ref, hf_ref, meta_ref, cnt_ref,
                   carry_ref):
    tm = x_ref.shape[1]

    @pl.when(pl.program_id(1) == 0)
    def _():
        carry_ref[...] = jnp.zeros_like(carry_ref)

    hf = _norm_mod(x_ref[0], nw_ref[...], sh_ref[0], sc_ref[0])
    hf_ref[0] = hf
    logits = _dot_3pass(hf, wr_ref[...]) + br_ref[...]
    lane = _iota2((tm, LANES), 1)
    neg = -jnp.inf
    first = lambda hit: jnp.min(jnp.where(hit, lane, LANES), axis=-1, keepdims=True)

    gl = jnp.where(lane < MOE_GROUPS, logits, neg)
    gmax = jnp.max(gl, axis=-1, keepdims=True)
    gidx = first(gl == gmax)
    g_top = 1.0 / jnp.sum(jnp.exp(gl - gmax), axis=-1, keepdims=True)

    lo = ROUTER_EXPERT_LANE0 + gidx * MOE_EXPERTS_PER_GROUP
    el = jnp.where((lane >= lo) & (lane < lo + MOE_EXPERTS_PER_GROUP), logits, neg)
    m1 = jnp.max(el, axis=-1, keepdims=True)
    i1 = first(el == m1)
    el2 = jnp.where(lane == i1, neg, el)
    m2 = jnp.max(el2, axis=-1, keepdims=True)
    i2 = first(el2 == m2)
    ex = jnp.exp(m2 - m1)
    gate1 = g_top / (1.0 + ex)
    gate2 = g_top * ex / (1.0 + ex)

    hit1 = lane == i1
    hit2 = lane == i2
    onehot = jnp.where(hit1 | hit2, 1.0, 0.0).astype(BF16)
    sub = min(tm, ROUTER_PREFIX_ROWS)
    strict = jnp.where(_iota2((sub, sub), 0) > _iota2((sub, sub), 1), 1.0, 0.0).astype(BF16)
    ones = jnp.ones((8, sub), BF16)
    running = carry_ref[...]
    parts = []
    for s0 in range(0, tm, sub):
        block = onehot[s0:s0 + sub]
        parts.append(_dot(strict, block) + running[0:1, :])
        running = running + _dot(ones, block)
    before = jnp.concatenate(parts, axis=0)
    rank1 = jnp.sum(jnp.where(hit1, before, 0.0), axis=-1, keepdims=True)
    rank2 = jnp.sum(jnp.where(hit2, before, 0.0), axis=-1, keepdims=True)
    carry_ref[...] = running
    cnt_ref[0] = running

    e1 = (i1 - ROUTER_EXPERT_LANE0).astype(F32)
    e2 = (i2 - ROUTER_EXPERT_LANE0).astype(F32)
    meta = jnp.zeros((tm, LANES), F32)
    for j, val in enumerate((e1, e2, rank1, rank2, gate1, gate2)):
        meta = jnp.where(lane == j, val, meta)
    for k in range(tm // LANES):
        fields = meta[k * LANES:(k + 1) * LANES, :].T
        for f in range(ROUTER_FIELDS):
            meta_ref[0, f, k:k + 1, :] = fields[f:f + 1, :]


def moe_router(x, nw, sh, sc, group_w, group_b, expert_w, expert_b, *, tm=8 * LANES):
    bsz, t, d = x.shape
    tm = min(tm, t)
    pad = LANES - MOE_GROUPS - MOE_EXPERTS
    wr = jnp.pad(jnp.concatenate([group_w, expert_w], axis=1), ((0, 0), (0, pad)))
    br = jnp.pad(jnp.concatenate([group_b, expert_b]), (0, pad)).reshape(1, LANES)
    return pl.pallas_call(
        _router_kernel,
        grid=(bsz, t // tm),
        in_specs=[pl.BlockSpec((1, tm, d), lambda b, i: (b, i, 0)),
                  pl.BlockSpec((1, d), lambda b, i: (0, 0)),
                  pl.BlockSpec((1, 1, d), lambda b, i: (b, 0, 0)),
                  pl.BlockSpec((1, 1, d), lambda b, i: (b, 0, 0)),
                  pl.BlockSpec((d, LANES), lambda b, i: (0, 0)),
                  pl.BlockSpec((1, LANES), lambda b, i: (0, 0))],
        out_specs=[pl.BlockSpec((1, tm, d), lambda b, i: (b, i, 0)),
                   pl.BlockSpec((1, ROUTER_FIELDS, tm // LANES, LANES), lambda b, i: (b, 0, i, 0)),
                   pl.BlockSpec((1, 8, LANES), lambda b, i: (b, 0, 0))],
        out_shape=[jax.ShapeDtypeStruct((bsz, t, d), F32),
                   jax.ShapeDtypeStruct((bsz, ROUTER_FIELDS, t // LANES, LANES), F32),
                   jax.ShapeDtypeStruct((bsz, 8, LANES), F32)],
        scratch_shapes=[pltpu.VMEM((8, LANES), F32)],
        compiler_params=_cparams("parallel", "arbitrary"),
        name="moe_router",
    )(x, nw.reshape(1, d), sh, sc, wr, br)


def _dest_kernel(meta_ref, cnt_ref, d_ref):
    upper = jnp.where(_iota2((LANES, LANES), 0) < _iota2((LANES, LANES), 1), 1.0, 0.0)
    first = _dot_split(cnt_ref[0], upper)
    for k in range(2):
        expert = meta_ref[0, k]
        off = jnp.zeros_like(expert)
        for x in range(MOE_EXPERTS):
            lane = ROUTER_EXPERT_LANE0 + x
            off = jnp.where(expert == x, first[0:1, lane:lane + 1], off)
        d_ref[0, k] = (meta_ref[0, 2 + k] + off).astype(jnp.int32)


def moe_dest(meta, cnt):
    bsz, nf, rows, _ = meta.shape
    return pl.pallas_call(
        _dest_kernel,
        grid=(bsz,),
        in_specs=[pl.BlockSpec((1, nf, rows, LANES), lambda b: (b, 0, 0, 0)),
                  pl.BlockSpec((1, 8, LANES), lambda b: (b, 0, 0))],
        out_specs=pl.BlockSpec((1, 2, rows, LANES), lambda b: (b, 0, 0, 0)),
        out_shape=jax.ShapeDtypeStruct((bsz, 2, rows, LANES), jnp.int32),
        compiler_params=_cparams("parallel"),
        name="moe_dest",
    )(meta, cnt)


def _expert_kernel(d1_ref, d2_ref, g1_ref, g2_ref, cnt_ref,
                   hf_hbm, x_hbm, gt_ref, w1_ref, w3_ref, w2_ref, *rest, final_norm):
    if final_norm:
        fw_ref, *rest = rest
    out_hbm, hf_v, acc_v, xb_v, yb_v, row_tok, row_gate, pstart, sem = rest
    b = pl.program_id(0)
    e = pl.program_id(1)
    n_e = pl.num_programs(1)
    t = hf_v.shape[0]
    grp = MOE_ROW_GROUP
    rows = xb_v.shape[0] * grp

    @pl.when(e == 0)
    def _():
        cp_h = pltpu.make_async_copy(hf_hbm.at[b], hf_v, sem.at[0])
        cp_x = pltpu.make_async_copy(x_hbm.at[b], acc_v, sem.at[1])
        cp_h.start()
        cp_x.start()
        xb_v[...] = jnp.zeros_like(xb_v)
        for i in range(grp):
            row_tok[2 * t + i] = 0

        def start_body(i, s):
            pstart[i] = s
            return s + cnt_ref[b * n_e + i]

        lax.fori_loop(0, n_e, start_body, 0)

        def tok_body(g, carry):
            for u in range(4):
                tok = g * 4 + u
                row_tok[d1_ref[tok]] = tok
                row_gate[d1_ref[tok]] = g1_ref[tok]
                row_tok[d2_ref[tok]] = tok
                row_gate[d2_ref[tok]] = g2_ref[tok]
            return carry

        lax.fori_loop(0, t // 4, tok_body, 0)
        cp_h.wait()
        cp_x.wait()

    cnt = cnt_ref[b * n_e + e]
    start = pstart[e]

    def block_body(j, carry):
        base = start + j * rows
        nrows = jnp.minimum(rows, cnt - j * rows)
        full_groups = nrows // grp

        def gather(g, c):
            first = base + g * grp
            for u in range(grp):
                xb_v[g, u:u + 1, :] = hf_v[pl.ds(row_tok[first + u], 1), :]
            return c

        lax.fori_loop(0, (nrows + grp - 1) // grp, gather, 0)
        xb = xb_v[...].reshape(rows, xb_v.shape[2]).astype(BF16)
        h1 = _dot(xb, w1_ref[0, 0].astype(BF16))
        h3 = _dot(xb, w3_ref[0, 0].astype(BF16))
        act = (_silu(h1) * h3).astype(BF16)
        yb_v[...] = (_dot(act, w2_ref[0, 0].astype(BF16)) * gt_ref[0]).reshape(yb_v.shape)

        def scatter(g, c):
            first = base + g * grp
            toks = [row_tok[first + u] for u in range(grp)]
            vals = [acc_v[pl.ds(toks[u], 1), :] + row_gate[first + u] * yb_v[g, u:u + 1, :]
                    for u in range(grp)]
            for u in range(grp):
                acc_v[pl.ds(toks[u], 1), :] = vals[u]
            return c

        lax.fori_loop(0, full_groups, scatter, 0)

        def scatter_tail(i, c):
            tok = row_tok[base + i]
            acc_v[pl.ds(tok, 1), :] = (acc_v[pl.ds(tok, 1), :] + row_gate[base + i]
                                       * yb_v[i // grp, pl.ds(i % grp, 1), :])
            return c

        lax.fori_loop(full_groups * grp, nrows, scatter_tail, 0)
        return carry

    lax.fori_loop(0, (cnt + rows - 1) // rows, block_body, 0)

    @pl.when(e == n_e - 1)
    def _():
        if final_norm:
            chunk = math.gcd(t, 256)

            def norm_body(i, carry):
                rs = pl.ds(pl.multiple_of(i * chunk, chunk), chunk)
                xr = acc_v[rs, :]
                acc_v[rs, :] = (xr * lax.rsqrt(jnp.mean(xr * xr, axis=-1, keepdims=True) + NORM_EPS)
                                * fw_ref[...])
                return carry

            lax.fori_loop(0, t // chunk, norm_body, 0)
        cp_o = pltpu.make_async_copy(acc_v, out_hbm.at[b], sem.at[2])
        cp_o.start()
        cp_o.wait()


def moe_experts(hf, x, gt, d1, d2, g1, g2, counts, layer, w1, w3, w2, final_w=None):
    bsz, t, d = x.shape
    _, n_e, _, f = w1.shape
    smem_tok = pl.BlockSpec((t,), lambda b, e: (b,), memory_space=pltpu.SMEM)
    extra_specs, extra = [], []
    if final_w is not None:
        extra_specs, extra = [pl.BlockSpec((1, d), lambda b, e: (0, 0))], [final_w.reshape(1, d)]
    return pl.pallas_call(
        functools.partial(_expert_kernel, final_norm=final_w is not None),
        grid=(bsz, n_e),
        in_specs=[smem_tok] * 4 + [
            pl.BlockSpec(memory_space=pltpu.SMEM),
            pl.BlockSpec(memory_space=pl.ANY),
            pl.BlockSpec(memory_space=pl.ANY),
            pl.BlockSpec((1, 1, d), lambda b, e: (b, 0, 0)),
            pl.BlockSpec((1, 1, d, f), lambda b, e: (layer, e, 0, 0)),
            pl.BlockSpec((1, 1, d, f), lambda b, e: (layer, e, 0, 0)),
            pl.BlockSpec((1, 1, f, d), lambda b, e: (layer, e, 0, 0))] + extra_specs,
        out_specs=pl.BlockSpec(memory_space=pl.ANY),
        out_shape=jax.ShapeDtypeStruct((bsz, t, d), F32),
        scratch_shapes=[pltpu.VMEM((t, d), F32), pltpu.VMEM((t, d), F32),
                        pltpu.VMEM((MOE_ROWS // MOE_ROW_GROUP, MOE_ROW_GROUP, d), F32),
                        pltpu.VMEM((MOE_ROWS // MOE_ROW_GROUP, MOE_ROW_GROUP, d), F32),
                        pltpu.SMEM((2 * t + MOE_ROW_GROUP,), jnp.int32),
                        pltpu.SMEM((2 * t + MOE_ROW_GROUP,), F32),
                        pltpu.SMEM((n_e,), jnp.int32), pltpu.SemaphoreType.DMA((3,))],
        compiler_params=_cparams("arbitrary", "arbitrary"),
        name="moe_experts",
    )(d1, d2, g1, g2, counts, hf, x, gt, w1, w3, w2, *extra)


def moe_layer(x, nw, sh, sc, gate, group_w, group_b, expert_w, expert_b, layer, w1, w3, w2,
              final_w=None):
    bsz, t, d = x.shape
    hf, meta, cnt = moe_router(x, nw, sh, sc, group_w, group_b, expert_w, expert_b)
    dest = moe_dest(meta, cnt)
    d1, d2 = [dest[:, k].reshape(bsz * t) for k in range(2)]
    g1, g2 = [meta[:, f].reshape(bsz * t) for f in (4, 5)]
    counts = cnt[:, 0, ROUTER_EXPERT_LANE0:ROUTER_EXPERT_LANE0 + MOE_EXPERTS].astype(jnp.int32)
    return moe_experts(hf, x, gate, d1, d2, g1, g2, counts.reshape(-1), layer, w1, w3, w2, final_w)


def kernel(x, c, ada_w, ada_b, norm_mix_w, norm_ffn_w, ssm_in_w, ssm_conv_w, ssm_conv_b, ssm_dt_bias, ssm_a_log, ssm_d, ssm_norm_w, ssm_out_w, rwkv_mix, rwkv_rkv_w, rwkv_w0, rwkv_w1, rwkv_w2, rwkv_a0, rwkv_a1, rwkv_a2, rwkv_g1, rwkv_g2, rwkv_k_k, rwkv_k_a, rwkv_r_k, rwkv_ln_w, rwkv_ln_b, rwkv_out_w, ret_in_w, ret_out_w, att_in_w, att_out_w, moe_group_w, moe_group_b, moe_expert_w, moe_expert_b, moe_w1, moe_w3, moe_w2, final_norm_w):
    depth = ada_w.shape[0]
    d = x.shape[-1]
    mod = ada_modulation(c, ada_w, ada_b)
    for i in range(depth):
        sh1, sc1, gt1, sh2, sc2, gt2 = [mod[i][:, None, j * d:(j + 1) * d] for j in range(6)]
        kind, j = i % 4, i // 4
        pre = (x, norm_mix_w[i], sh1, sc1, gt1)
        if kind == 0:
            x = mamba2_layer(*pre, ssm_in_w[j], ssm_conv_w[j], ssm_conv_b[j], ssm_dt_bias[j],
                             ssm_a_log[j], ssm_d[j], ssm_norm_w[j], ssm_out_w[j])
        elif kind == 1:
            x = rwkv7_layer(*pre, rwkv_mix[j], rwkv_rkv_w[j], rwkv_w0[j], rwkv_w1[j], rwkv_w2[j],
                            rwkv_a0[j], rwkv_a1[j], rwkv_a2[j], rwkv_g1[j], rwkv_g2[j], rwkv_k_k[j],
                            rwkv_k_a[j], rwkv_r_k[j], rwkv_ln_w[j], rwkv_ln_b[j], rwkv_out_w[j])
        elif kind == 2:
            x = retention_layer(*pre, ret_in_w[j], ret_out_w[j])
        else:
            x = attention_layer(*pre, att_in_w[j], att_out_w[j])
        x = moe_layer(x, norm_ffn_w[i], sh2, sc2, gt2, moe_group_w[i], moe_group_b[i],
                      moe_expert_w[i], moe_expert_b[i], i, moe_w1, moe_w3, moe_w2,
                      final_w=final_norm_w if i == depth - 1 else None)
    return x
```

```python
import functools
import math

import jax
import jax.numpy as jnp
from jax import lax
from jax.experimental import pallas as pl
from jax.experimental.pallas import tpu as pltpu

F32 = jnp.float32
BF16 = jnp.bfloat16
HI = lax.Precision.HIGHEST

NORM_EPS = 1e-6
LANES = 128
VMEM_LIMIT = 56 * 1024 * 1024
PROJ_TM = 2048
PROJ_TN = 1024
DESTRIDE_STEP = 4

SSM_HEAD_DIM = 64
SSM_N_GROUPS = 8
SSM_HEADS_PER_GROUP = 4
SSM_D_STATE = 128
SSM_CONV = 4
SSM_CHUNK = 128
SSM_CONV_COLS = 256
RWKV_HEAD_DIM = 64
RWKV_LN_EPS = 64e-5
RWKV_CHUNK = 128
RWKV_SUB = 16
RWKV_INV_PRECISION = None
RWKV_APPLY_PRECISION = None
RET_N_HEADS = 4
RET_CHUNK = 128
ATT_GROUPS = ((128, 1), (512, 4), (2048, 16))
ATT_HEADS_PER_GROUP = 8
ATT_HEAD_DIM = 128
ATT_BLOCK = 128
ATT_STEP_BLOCKS = 4
MOE_GROUPS = 4
MOE_EXPERTS_PER_GROUP = 8
MOE_EXPERTS = 32
MOE_ROWS = 320
MOE_ROW_GROUP = 8


def _cparams(*sem):
    return pltpu.CompilerParams(dimension_semantics=sem, vmem_limit_bytes=VMEM_LIMIT)


def _sigmoid(x):
    return 1.0 / (1.0 + jnp.exp(-x))


def _silu(x):
    return x * _sigmoid(x)


def _softplus(x):
    return jnp.maximum(x, 0.0) + jnp.log(1.0 + jnp.exp(-jnp.abs(x)))


def _dot(a, b, precision=None):
    return jnp.dot(a, b, preferred_element_type=F32, precision=precision)


def _dot_nt(a, b, precision=None):
    return lax.dot_general(a, b, (((1,), (1,)), ((), ())), preferred_element_type=F32,
                           precision=precision)


def _dot_tn(a, b, precision=None):
    return lax.dot_general(a, b, (((0,), (0,)), ((), ())), preferred_element_type=F32,
                           precision=precision)


def _split_bf16(a):
    hi = a.astype(BF16)
    return hi, (a - hi.astype(F32)).astype(BF16)


def _dot_split(a, m, transpose_rhs=False, lhs_exact=False):
    hi, lo = _split_bf16(a)
    mb = m.astype(BF16)
    if lhs_exact:
        return _dot(mb, hi) + _dot(mb, lo)
    dot = _dot_nt if transpose_rhs else _dot
    return dot(hi, mb) + dot(lo, mb)


def _dot_3pass(a, b):
    a_hi, a_lo = _split_bf16(a)
    b_hi, b_lo = _split_bf16(b)
    return _dot(a_hi, b_hi) + (_dot(a_lo, b_hi) + _dot(a_hi, b_lo))


def _iota2(shape, axis):
    return lax.broadcasted_iota(jnp.int32, shape, axis)


def _ada_kernel(c_ref, w_ref, b_ref, o_ref):
    cs = _silu(c_ref[...])
    o_ref[0] = _dot(cs, w_ref[0], HI) + b_ref[0]


def ada_modulation(c, ada_w, ada_b):
    depth, d, n = ada_w.shape
    bsz = c.shape[0]
    rows = -(-bsz // 8) * 8
    cp = jnp.pad(c, ((0, rows - bsz), (0, 0)))
    tn = 3072
    out = pl.pallas_call(
        _ada_kernel,
        grid=(depth, n // tn),
        in_specs=[pl.BlockSpec((rows, d), lambda l, j: (0, 0)),
                  pl.BlockSpec((1, d, tn), lambda l, j: (l, 0, j)),
                  pl.BlockSpec((1, 1, tn), lambda l, j: (l, 0, j))],
        out_specs=pl.BlockSpec((1, rows, tn), lambda l, j: (l, 0, j)),
        out_shape=jax.ShapeDtypeStruct((depth, rows, n), F32),
        compiler_params=_cparams("parallel", "parallel"),
        name="ada_modulation",
    )(cp, ada_w, ada_b.reshape(depth, 1, n))
    return out[:, :bsz]


def _norm_mod(x, nw, sh, sc):
    y = x * lax.rsqrt(jnp.mean(x * x, axis=-1, keepdims=True) + NORM_EPS) * nw
    return y * (1.0 + sc) + sh


def _nmm_kernel(*refs, dil, has_side):
    if has_side:
        x_ref, nw_ref, sh_ref, sc_ref, w_ref, ws_ref, o_ref, side_ref, h_ref, *rest = refs
    else:
        x_ref, nw_ref, sh_ref, sc_ref, w_ref, o_ref, h_ref, *rest = refs

    @pl.when(pl.program_id(2) == 0)
    def _():
        h = _norm_mod(x_ref[0], nw_ref[...], sh_ref[0], sc_ref[0]).astype(h_ref.dtype)
        h_ref[...] = h
        if has_side:
            w_hi, w_lo = _split_bf16(ws_ref[...])
            side_ref[0] = _dot(h, w_hi) + _dot(h, w_lo)

    res = _dot(h_ref[...], w_ref[...])
    if dil is None:
        o_ref[0] = res.astype(o_ref.dtype)
    else:
        acc_ref, *tmp = rest
        tm = acc_ref.shape[1]
        rows = tm // dil
        for c in range(acc_ref.shape[0]):
            acc_ref[c] = res[:, c * LANES:(c + 1) * LANES]
        f1 = min(dil, DESTRIDE_STEP)
        f2 = dil // f1
        for c in range(acc_ref.shape[0]):
            cols = slice(c * LANES, (c + 1) * LANES)
            if f2 == 1:
                for r in range(dil):
                    o_ref[0, r, :, cols] = acc_ref[c, pl.ds(r, rows, stride=dil), :].astype(o_ref.dtype)
            else:
                tmp_ref, = tmp
                for b in range(f1):
                    tmp_ref[c, b * (tm // f1):(b + 1) * (tm // f1), :] = (
                        acc_ref[c, pl.ds(b, tm // f1, stride=f1), :])
                for a in range(f2):
                    for b in range(f1):
                        o_ref[0, a * f1 + b, :, cols] = tmp_ref[
                            c, pl.ds(b * (tm // f1) + a, rows, stride=f2), :].astype(o_ref.dtype)


def norm_mod_matmul(x, nw, sh, sc, w, *, tm, tn, dil=None, side_w=None, n_out=None,
                    w_tile=lambda j: j):
    bsz, t, d = x.shape
    n = w.shape[1] if n_out is None else n_out
    tm = min(tm, t)
    scratch = [pltpu.VMEM((tm, d), BF16)]
    in_specs = [pl.BlockSpec((1, tm, d), lambda b, i, j: (b, i, 0)),
                pl.BlockSpec((1, d), lambda b, i, j: (0, 0)),
                pl.BlockSpec((1, 1, d), lambda b, i, j: (b, 0, 0)),
                pl.BlockSpec((1, 1, d), lambda b, i, j: (b, 0, 0)),
                pl.BlockSpec((d, tn), lambda b, i, j: (0, w_tile(j)))]
    ins = [x, nw.reshape(1, d), sh, sc, w]
    if dil is None:
        out_specs = [pl.BlockSpec((1, tm, tn), lambda b, i, j: (b, i, j))]
        out_shape = [jax.ShapeDtypeStruct((bsz, t, n), BF16)]
    else:
        out_specs = [pl.BlockSpec((1, dil, tm // dil, tn), lambda b, i, j: (b, 0, i, j))]
        out_shape = [jax.ShapeDtypeStruct((bsz, dil, t // dil, n), BF16)]
        scratch += [pltpu.VMEM((tn // LANES, tm, LANES), F32)] * (2 if dil > DESTRIDE_STEP else 1)
    if side_w is not None:
        in_specs.append(pl.BlockSpec((d, LANES), lambda b, i, j: (0, 0)))
        ins.append(side_w)
        out_specs.append(pl.BlockSpec((1, tm, LANES), lambda b, i, j: (b, i, 0)))
        out_shape.append(jax.ShapeDtypeStruct((bsz, t, LANES), F32))
    outs = pl.pallas_call(
        functools.partial(_nmm_kernel, dil=dil, has_side=side_w is not None),
        grid=(bsz, t // tm, n // tn),
        in_specs=in_specs,
        out_specs=out_specs,
        out_shape=out_shape,
        scratch_shapes=scratch,
        compiler_params=_cparams("parallel", "parallel", "arbitrary"),
        name="norm_mod_matmul",
    )(*ins)
    return outs if side_w is not None else outs[0]


def _mgr_kernel(*refs, has_mul):
    if has_mul:
        y_ref, g_ref, w_ref, x_ref, gate_ref, o_ref = refs
        y = (y_ref[0].astype(F32) * g_ref[0].astype(F32)).astype(BF16)
    else:
        y_ref, w_ref, x_ref, gate_ref, o_ref = refs
        y = y_ref[0].astype(BF16)
    o_ref[0] = x_ref[0] + gate_ref[0] * _dot(y, w_ref[...])


def matmul_gated_residual(y, w, x, gate, mul=None, *, tm=512):
    bsz, t, k = y.shape
    d = w.shape[1]
    tm = min(tm, t)
    ins = [y] + ([mul] if mul is not None else []) + [w, x, gate]
    row = lambda width: pl.BlockSpec((1, tm, width), lambda b, i: (b, i, 0))
    specs = [row(k)] + ([row(k)] if mul is not None else []) + [
        pl.BlockSpec((k, d), lambda b, i: (0, 0)), row(d),
        pl.BlockSpec((1, 1, d), lambda b, i: (b, 0, 0))]
    return pl.pallas_call(
        functools.partial(_mgr_kernel, has_mul=mul is not None),
        grid=(bsz, t // tm),
        in_specs=specs,
        out_specs=row(d),
        out_shape=jax.ShapeDtypeStruct((bsz, t, d), F32),
        compiler_params=_cparams("parallel", "parallel"),
        name="matmul_gated_residual",
    )(*ins)


def _ssd_kernel(zx_ref, dt_ref, cw_ref, cb_ref, dtb_ref, alog_ref, dsk_ref, nw_ref, o_ref,
                prev_ref, xbc_ref, state_ref):
    q = SSM_CHUNK
    g_n, hg, p, n = SSM_N_GROUPS, SSM_HEADS_PER_GROUP, SSM_HEAD_DIM, SSM_D_STATE
    d_inner = g_n * hg * p
    nh = g_n * hg
    gw = hg * p

    @pl.when(pl.program_id(1) == 0)
    def _():
        prev_ref[...] = jnp.zeros_like(prev_ref)
        state_ref[...] = jnp.zeros_like(state_ref)

    dt = _softplus(dt_ref[0][:, :nh] + dtb_ref[...])
    da = dt * -jnp.exp(alog_ref[...])
    tril = (_iota2((q, q), 0) >= _iota2((q, q), 1))
    acum = _dot_split(da, tril.astype(F32), lhs_exact=True)
    a_end = acum[q - 1:q, :]
    chunk_decay = jnp.exp(a_end)
    adj_t = (acum - jnp.log(dt)).T
    wdec_t = (dt * jnp.exp(a_end - acum)).T

    src = _iota2((q, 2 * q), 1) - q - _iota2((q, 2 * q), 0)
    shifts = [jnp.where(src == -s, 1.0, 0.0).astype(BF16) for s in range(1, SSM_CONV)]
    cdim = xbc_ref.shape[1]
    for c0 in range(0, cdim, SSM_CONV_COLS):
        cs = slice(c0, c0 + SSM_CONV_COLS)
        cur = zx_ref[0, :, d_inner + c0:d_inner + c0 + SSM_CONV_COLS]
        ext = jnp.concatenate([prev_ref[:, cs], cur], axis=0)
        prev_ref[:, cs] = cur
        acc = cb_ref[:, cs] + cur.astype(F32) * cw_ref[SSM_CONV - 1:SSM_CONV, cs]
        for s in range(1, SSM_CONV):
            acc = acc + _dot(shifts[s - 1], ext) * cw_ref[SSM_CONV - 1 - s:SSM_CONV - s, cs]
        xbc_ref[:, cs] = _silu(acc)

    for g in range(g_n):
        xs_g = xbc_ref[:, g * gw:(g + 1) * gw]
        bm = xbc_ref[:, d_inner + g * n:d_inner + (g + 1) * n]
        cm = xbc_ref[:, d_inner + g_n * n + g * n:d_inner + g_n * n + (g + 1) * n]
        cm_b = cm.astype(BF16)
        cb = _dot_nt(cm_b, bm.astype(BF16))
        bm_t = bm.T
        xs_b = xs_g.astype(BF16)
        st_old = state_ref[g]
        y_carry = _dot(cm_b, st_old.astype(BF16))
        hd = [g * hg + hh for hh in range(hg)]
        hs = [slice(hh * p, (hh + 1) * p) for hh in range(hg)]
        a_col = [jnp.broadcast_to(acum[:, h:h + 1], (q, q)) for h in hd]
        m_h = [cb * jnp.exp(jnp.where(tril, ac - adj_t[h:h + 1, :], -jnp.inf))
               for h, ac in zip(hd, a_col)]
        st_new = [_dot((bm_t * wdec_t[h:h + 1, :]).astype(BF16), xs_b[:, c])
                  for h, c in zip(hd, hs)]
        ys = [_dot(m.astype(BF16), xs_b[:, c]) + jnp.exp(ac[:, :p]) * y_carry[:, c]
              for m, ac, c in zip(m_h, a_col, hs)]
        state_ref[g] = jnp.concatenate(
            [st_old[:, c] * chunk_decay[:, h:h + 1] + sn for h, c, sn in zip(hd, hs, st_new)], axis=1)
        y = jnp.concatenate(ys, axis=1) + xs_g * dsk_ref[:, g * gw:(g + 1) * gw]
        z = zx_ref[0, :, g * gw:(g + 1) * gw].astype(F32)
        y = y * _silu(z)
        y = y * lax.rsqrt(jnp.mean(y * y, axis=-1, keepdims=True) + NORM_EPS)
        o_ref[0, :, g * gw:(g + 1) * gw] = (y * nw_ref[:, g * gw:(g + 1) * gw]).astype(o_ref.dtype)


def ssd_core(zx, dt_raw, conv_w, conv_b, dt_bias, a_log, d_skip, norm_w):
    bsz, t, width = zx.shape
    d_inner = SSM_N_GROUPS * SSM_HEADS_PER_GROUP * SSM_HEAD_DIM
    cdim = width - d_inner
    nh = dt_bias.shape[0]
    q = SSM_CHUNK
    full = lambda shape: pl.BlockSpec(shape, lambda b, c: (0,) * len(shape))
    return pl.pallas_call(
        _ssd_kernel,
        grid=(bsz, t // q),
        in_specs=[pl.BlockSpec((1, q, width), lambda b, c: (b, c, 0)),
                  pl.BlockSpec((1, q, LANES), lambda b, c: (b, c, 0)),
                  full((SSM_CONV, cdim)), full((1, cdim)), full((1, nh)), full((1, nh)),
                  full((1, d_inner)), full((1, d_inner))],
        out_specs=pl.BlockSpec((1, q, d_inner), lambda b, c: (b, c, 0)),
        out_shape=jax.ShapeDtypeStruct((bsz, t, d_inner), BF16),
        scratch_shapes=[pltpu.VMEM((q, cdim), BF16), pltpu.VMEM((q, cdim), F32),
                        pltpu.VMEM((SSM_N_GROUPS, SSM_D_STATE,
                                    SSM_HEADS_PER_GROUP * SSM_HEAD_DIM), F32)],
        compiler_params=_cparams("parallel", "arbitrary"),
        name="ssd_core",
    )(zx, dt_raw, conv_w, conv_b.reshape(1, cdim), dt_bias.reshape(1, nh),
      a_log.reshape(1, nh), jnp.repeat(d_skip, SSM_HEAD_DIM).reshape(1, d_inner),
      norm_w.reshape(1, d_inner))


def mamba2_layer(x, nw, sh, sc, gate, in_w, conv_w, conv_b, dt_bias, a_log, d_skip, norm_w, out_w):
    d_inner = SSM_N_GROUPS * SSM_HEADS_PER_GROUP * SSM_HEAD_DIM
    cdim = conv_w.shape[1]
    nh = dt_bias.shape[0]
    w_dt = jnp.pad(in_w[:, d_inner + cdim:], ((0, 0), (0, LANES - nh)))
    zx, dt_raw = norm_mod_matmul(x, nw, sh, sc, in_w.astype(BF16), tm=PROJ_TM, tn=PROJ_TN,
                                 side_w=w_dt, n_out=d_inner + cdim)
    y = ssd_core(zx, dt_raw, conv_w, conv_b, dt_bias, a_log, d_skip, norm_w)
    return matmul_gated_residual(y, out_w.astype(BF16), x, gate)


def _head_indicator(d, n):
    return jnp.where(_iota2((d, LANES), 0) // n == _iota2((d, LANES), 1), 1.0, 0.0)


def _rwkv_prep_kernel(x_ref, xp_ref, nw_ref, sh_ref, sc_ref, mix_ref, wr_ref, wk_ref, wv_ref,
                      w0_ref, w1_ref, w2_ref, a0_ref, a1_ref, a2_ref, g1_ref, g2_ref, kk_ref, ka_ref,
                      r_o, lw_o, k_o, v_o, kn_o, b_o, g_o):
    tm, d = x_ref.shape[1], x_ref.shape[2]
    nw, sh, sc = nw_ref[...], sh_ref[0], sc_ref[0]
    hm = _norm_mod(x_ref[0], nw, sh, sc)
    prev = _norm_mod(xp_ref[0], nw, sh, sc)[7:8, :]
    prev = jnp.where(pl.program_id(1) > 0, prev, 0.0)
    shifted = jnp.where(_iota2((tm, d), 0) == 0, prev, pltpu.roll(hm, 1, axis=0))
    xx = shifted - hm
    mixed = lambda j: (hm + xx * mix_ref[j:j + 1, :]).astype(BF16)
    xr, xw, xk, xv, xa, xg = [mixed(j) for j in range(6)]
    r = _dot(xr, wr_ref[...])
    k = _dot(xk, wk_ref[...])
    v = _dot(xv, wv_ref[...])
    lora = lambda u, w: _dot(u.astype(BF16), w[...])
    w_log = -_softplus(-(w0_ref[...] + lora(jnp.tanh(lora(xw, w1_ref)), w2_ref))) - 0.5
    a = _sigmoid(a0_ref[...] + lora(lora(xa, a1_ref), a2_ref))
    g = lora(_sigmoid(lora(xg, g1_ref)), g2_ref)
    kk = k * kk_ref[...]
    ind = _head_indicator(d, RWKV_HEAD_DIM)
    nrm = jnp.maximum(jnp.sqrt(_dot_split(kk * kk, ind)), 1e-12)
    kn = kk * _dot_split(1.0 / nrm, ind, transpose_rhs=True)
    r_o[0] = r.astype(r_o.dtype)
    lw_o[0] = -jnp.exp(w_log)
    k_o[0] = (k * (1.0 + (a - 1.0) * ka_ref[...])).astype(k_o.dtype)
    v_o[0] = v.astype(v_o.dtype)
    kn_o[0] = kn.astype(kn_o.dtype)
    b_o[0] = (kn * a).astype(b_o.dtype)
    g_o[0] = g.astype(g_o.dtype)


def rwkv_prep(x, nw, sh, sc, mix, rkv_w, w0, w1, w2, a0, a1, a2, g1, g2, k_k, k_a, *, tm=512):
    bsz, t, d = x.shape
    tm = min(tm, t)
    full = lambda arr: pl.BlockSpec(arr.shape, lambda b, i: (0,) * arr.ndim)
    row = pl.BlockSpec((1, tm, d), lambda b, i: (b, i, 0))
    vec = pl.BlockSpec((1, 1, d), lambda b, i: (b, 0, 0))
    bf = lambda w: w.astype(BF16)
    params = [mix, bf(rkv_w[0]), bf(rkv_w[1]), bf(rkv_w[2]), w0.reshape(1, d), bf(w1), bf(w2),
              a0.reshape(1, d), bf(a1), bf(a2), bf(g1), bf(g2), k_k.reshape(1, d), k_a.reshape(1, d)]
    prev_spec = pl.BlockSpec((1, 8, d), lambda b, i: (b, jnp.maximum(i * (tm // 8) - 1, 0), 0))
    return pl.pallas_call(
        _rwkv_prep_kernel,
        grid=(bsz, t // tm),
        in_specs=[row, prev_spec, pl.BlockSpec((1, d), lambda b, i: (0, 0)), vec, vec]
        + [full(p) for p in params],
        out_specs=[row] * 7,
        out_shape=[jax.ShapeDtypeStruct((bsz, t, d), F32 if i == 1 else BF16) for i in range(7)],
        compiler_params=_cparams("parallel", "parallel"),
        name="rwkv_prep",
    )(x, x, nw.reshape(1, d), sh, sc, *params)


def _mm(a, b, precision):
    if precision is None:
        return _dot(a.astype(BF16), b.astype(BF16))
    return _dot(a, b, precision)


def _mm_nt(a, b, precision):
    if precision is None:
        return _dot_nt(a.astype(BF16), b.astype(BF16))
    return _dot_nt(a, b, precision)


def _mm_tn(a, b, precision):
    if precision is None:
        return _dot_tn(a.astype(BF16), b.astype(BF16))
    return _dot_tn(a, b, precision)


def _unit_lower_inverse(m_list, length, sub, precision):
    ri = _iota2((length, length), 0)
    ci = _iota2((length, length), 1)
    eye = jnp.where(ri == ci, 1.0, 0.0)
    same_block = ri // sub == ci // sub
    mm = lambda a, b: _mm(a, b, precision)
    md = [jnp.where(same_block, m, 0.0) for m in m_list]
    off = [m - d for m, d in zip(m_list, md)]
    dinv = [eye + d for d in md]
    pw = md
    for _ in range(int(math.log2(sub)) - 1):
        pw = [mm(p, p) for p in pw]
        dinv = [x + mm(x, p) for x, p in zip(dinv, pw)]
    pm = [mm(x, o) for x, o in zip(dinv, off)]
    acc = [eye + p for p in pm]
    pw = pm
    for _ in range(int(math.log2(length // sub)) - 1):
        pw = [mm(p, p) for p in pw]
        acc = [x + mm(x, p) for x, p in zip(acc, pw)]
    return [mm(x, d) for x, d in zip(acc, dinv)]


def _rwkv_core_kernel(r_ref, lw_ref, k_ref, v_ref, kn_ref, b_ref, rk_ref, lnw_ref, lnb_ref, o_ref,
                      state_ref, *, inv_precision, apply_precision):
    length, d = r_ref.shape[1], r_ref.shape[2]
    n = RWKV_HEAD_DIM
    heads = range(d // n)
    mm = lambda a, b: _mm(a, b, apply_precision)

    @pl.when(pl.program_id(1) == 0)
    def _():
        state_ref[...] = jnp.zeros_like(state_ref)

    ri = _iota2((2 * length, 2 * length), 0)
    ci = _iota2((2 * length, 2 * length), 1) % length
    keep = ((ri < length) & (ri > ci)) | ((ri >= length) & (ri - length >= ci))
    eye_n = _iota2((n, n), 0) == _iota2((n, n), 1)

    lw = lw_ref[0]
    r, k, v, kn, b = [ref[0].astype(F32) for ref in (r_ref, k_ref, v_ref, kn_ref, b_ref)]
    tri = jnp.where(_iota2((length, length), 0) >= _iota2((length, length), 1), 1.0, 0.0)
    cum = _dot_split(lw, tri, lhs_exact=True)
    c_end = cum[length - 1:length, :]
    e_neg = jnp.exp(-cum)
    e_rem = jnp.exp(c_end - cum)
    ar = jnp.concatenate([-kn * jnp.exp(cum - lw), r * jnp.exp(cum)], axis=0)
    bk = jnp.concatenate([b * e_neg, k * e_neg], axis=0)
    bk_end = jnp.concatenate([b * e_rem, k * e_rem], axis=0)
    w_end = jnp.exp(c_end)
    bonus_w = r * k * rk_ref[...]
    hs = [slice(h * n, (h + 1) * n) for h in heads]

    s0 = [state_ref[h] for h in heads]
    p = [jnp.where(keep, _mm_nt(ar[:, c], bk[:, c], inv_precision), 0.0) for c in hs]
    t_inv = _unit_lower_inverse([x[:length, :length] for x in p], length, RWKV_SUB, inv_precision)
    q = [mm(ar[:, c], s) for c, s in zip(hs, s0)]
    y = [x[:length] + mm(pp[:length, length:], v[:, c]) for x, pp, c in zip(q, p, hs)]
    u = [mm(ti, yy) for ti, yy in zip(t_inv, y)]
    uv = [jnp.concatenate([uu, v[:, c]], axis=0) for uu, c in zip(u, hs)]
    o = [x[length:] + mm(pp[length:], w) for x, pp, w in zip(q, p, uv)]
    upd = [_mm_tn(bk_end[:, c], w, apply_precision) for c, w in zip(hs, uv)]
    w_col = [jnp.sum(jnp.where(eye_n, w_end[:, c], 0.0), axis=1, keepdims=True) for c in hs]
    for h in heads:
        state_ref[h] = w_col[h] * s0[h] + upd[h]
    ind = _head_indicator(d, n)
    head_sum = lambda x: _dot_split(_dot_split(x, ind), ind, transpose_rhs=True)
    o_all = jnp.concatenate(o, axis=1)
    cen = o_all - head_sum(o_all) * (1.0 / n)
    var = head_sum(cen * cen) * (1.0 / n)
    y = cen * lax.rsqrt(var + RWKV_LN_EPS) * lnw_ref[...] + lnb_ref[...]
    o_ref[0] = (y + head_sum(bonus_w) * v).astype(o_ref.dtype)


def rwkv_core(r, lw, k, v, kn, b, r_k, ln_w, ln_b):
    bsz, t, d = r.shape
    length = RWKV_CHUNK
    n = RWKV_HEAD_DIM
    row = pl.BlockSpec((1, length, d), lambda bb, c: (bb, c, 0))
    vec = pl.BlockSpec((1, d), lambda bb, c: (0, 0))
    return pl.pallas_call(
        functools.partial(_rwkv_core_kernel, inv_precision=RWKV_INV_PRECISION,
                          apply_precision=RWKV_APPLY_PRECISION),
        grid=(bsz, t // length),
        in_specs=[row] * 6 + [vec] * 3,
        out_specs=row,
        out_shape=jax.ShapeDtypeStruct((bsz, t, d), BF16),
        scratch_shapes=[pltpu.VMEM((d // n, n, n), F32)],
        compiler_params=_cparams("parallel", "arbitrary"),
        name="rwkv_core",
    )(r, lw, k, v, kn, b, r_k.reshape(1, d), ln_w.reshape(1, d), ln_b.reshape(1, d))


def rwkv7_layer(x, nw, sh, sc, gate, mix, rkv_w, w0, w1, w2, a0, a1, a2, g1, g2, k_k, k_a, r_k,
                ln_w, ln_b, out_w):
    r, lw, k, v, kn, b, g = rwkv_prep(x, nw, sh, sc, mix, rkv_w, w0, w1, w2, a0, a1, a2, g1, g2,
                                      k_k, k_a)
    y = rwkv_core(r, lw, k, v, kn, b, r_k, ln_w, ln_b)
    return matmul_gated_residual(y, out_w.astype(BF16), x, gate, mul=g)


def _ret_kernel(qkvg_ref, idec_ref, qdec_ref, kdec_ref, cdec_ref, o_ref, state_ref, *, d_model):
    nh = RET_N_HEADS
    dk = d_model // nh
    dv = 2 * d_model // nh
    k_scale = dk ** -0.5

    @pl.when(pl.program_id(1) == 0)
    def _():
        state_ref[...] = jnp.zeros_like(state_ref)

    for h in range(nh):
        q_b = qkvg_ref[0, :, h * dk:(h + 1) * dk]
        k_f = qkvg_ref[0, :, d_model + h * dk:d_model + (h + 1) * dk].astype(F32) * k_scale
        v_b = qkvg_ref[0, :, 2 * d_model + h * dv:2 * d_model + (h + 1) * dv]
        g_f = qkvg_ref[0, :, 4 * d_model + h * dv:4 * d_model + (h + 1) * dv].astype(F32)
        scores = _dot_nt(q_b, k_f.astype(BF16)) * idec_ref[h]
        inner = _dot(scores.astype(BF16), v_b)
        r_old = state_ref[h]
        qd = (q_b.astype(F32) * qdec_ref[:, h:h + 1]).astype(BF16)
        cross = _dot(qd, r_old.astype(BF16))
        kd = (k_f * kdec_ref[:, h:h + 1]).astype(BF16)
        state_ref[h] = r_old * cdec_ref[:, h:h + 1] + _dot_tn(kd, v_b)
        o = inner + cross
        o = o * lax.rsqrt(jnp.mean(o * o, axis=-1, keepdims=True) + NORM_EPS)
        o_ref[0, :, h * dv:(h + 1) * dv] = (_silu(g_f) * o).astype(o_ref.dtype)


def retention_core(qkvg, d_model):
    bsz, t, width = qkvg.shape
    nh, q = RET_N_HEADS, RET_CHUNK
    dk, dv = d_model // nh, 2 * d_model // nh
    log_gamma = jnp.log(1 - jnp.exp2(-5.0 - jnp.arange(nh, dtype=F32)))
    idx = jnp.arange(q)
    rel = idx[:, None] - idx[None, :]
    inner_decay = jnp.where(rel >= 0,
                            jnp.exp(jnp.maximum(rel, 0).astype(F32) * log_gamma[:, None, None]), 0.0)
    q_decay = jnp.exp((idx + 1).astype(F32)[:, None] * log_gamma)
    k_decay = jnp.exp((q - 1 - idx).astype(F32)[:, None] * log_gamma)
    chunk_decay = jnp.exp(q * log_gamma).reshape(1, nh)
    full = lambda shape: pl.BlockSpec(shape, lambda b, c: (0,) * len(shape))
    return pl.pallas_call(
        functools.partial(_ret_kernel, d_model=d_model),
        grid=(bsz, t // q),
        in_specs=[pl.BlockSpec((1, q, width), lambda b, c: (b, c, 0)),
                  full((nh, q, q)), full((q, nh)), full((q, nh)), full((1, nh))],
        out_specs=pl.BlockSpec((1, q, nh * dv), lambda b, c: (b, c, 0)),
        out_shape=jax.ShapeDtypeStruct((bsz, t, nh * dv), BF16),
        scratch_shapes=[pltpu.VMEM((nh, dk, dv), F32)],
        compiler_params=_cparams("parallel", "arbitrary"),
        name="retention_core",
    )(qkvg, inner_decay, q_decay, k_decay, chunk_decay)


def retention_layer(x, nw, sh, sc, gate, in_w, out_w):
    d = x.shape[-1]
    qkvg = norm_mod_matmul(x, nw, sh, sc, in_w.astype(BF16), tm=PROJ_TM, tn=PROJ_TN)
    y = retention_core(qkvg, d)
    return matmul_gated_residual(y, out_w.astype(BF16), x, gate)


def _att_kernel(q_ref, k_ref, v_ref, sl_ref, o_ref, lse_ref, kprev_ref, vprev_ref, *, dil, span):
    w = ATT_BLOCK
    e = ATT_HEAD_DIM
    nblk = pl.program_id(2)

    @pl.when(nblk == 0)
    def _():
        kprev_ref[...] = jnp.zeros_like(kprev_ref)
        vprev_ref[...] = jnp.zeros_like(vprev_ref)

    n_sub = q_ref.shape[2] // w
    qi = _iota2((w, 2 * w), 0)
    ci = _iota2((w, 2 * w), 1)
    delta = w + qi - ci
    band = (delta >= 0) & (delta <= span)
    first_mask = band & ((nblk > 0) | (ci >= w))
    dist = (dil * delta).astype(F32)
    lane = _iota2((w, LANES), 1)
    heads = range(ATT_HEADS_PER_GROUP)
    cols = [slice(h * e, (h + 1) * e) for h in heads]

    def keys(ref, prev_ref, sub, c):
        before = prev_ref[:, c] if sub == 0 else ref[0, 0, (sub - 1) * w:sub * w, c]
        return jnp.concatenate([before, ref[0, 0, sub * w:(sub + 1) * w, c]], axis=0)

    units = [(sub, h) for sub in range(n_sub) for h in heads]
    s = [_dot_nt(q_ref[0, 0, sub * w:(sub + 1) * w, cols[h]], keys(k_ref, kprev_ref, sub, cols[h]))
         for sub, h in units]
    s = [jnp.where(first_mask if sub == 0 else band,
                   x * (e ** -0.5) - sl_ref[:, h:h + 1] * dist, -jnp.inf)
         for (sub, h), x in zip(units, s)]
    m = [jnp.max(x, axis=-1, keepdims=True) for x in s]
    p = [jnp.exp(x - mx) for x, mx in zip(s, m)]
    den = [jnp.sum(x, axis=-1, keepdims=True) for x in p]
    lse_all = [jnp.zeros((w, LANES), F32) for _ in range(n_sub)]
    for u, (sub, h) in enumerate(units):
        o = _dot((p[u] / den[u]).astype(BF16), keys(v_ref, vprev_ref, sub, cols[h]))
        o_ref[0, 0, sub * w:(sub + 1) * w, cols[h]] = o.astype(o_ref.dtype)
        lse_all[sub] = jnp.where(lane == h, m[u] + jnp.log(den[u]), lse_all[sub])
    for sub in range(n_sub):
        lse_ref[0, 0, sub * w:(sub + 1) * w, :] = lse_all[sub]
    kprev_ref[...] = k_ref[0, 0, (n_sub - 1) * w:, :]
    vprev_ref[...] = v_ref[0, 0, (n_sub - 1) * w:, :]


def dilated_group_core(qkv, gi, window, dil, slopes):
    bsz, _, ls, _ = qkv.shape
    w = ATT_BLOCK
    gw = ATT_HEADS_PER_GROUP * ATT_HEAD_DIM
    rows = min(ls, ATT_STEP_BLOCKS * w)
    spec = lambda part: pl.BlockSpec((1, 1, rows, gw), lambda b, r, n: (b, r, n, part))
    return pl.pallas_call(
        functools.partial(_att_kernel, dil=dil, span=window // dil),
        grid=(bsz, dil, ls // rows),
        in_specs=[spec(0), spec(1), spec(2),
                  pl.BlockSpec((1, ATT_HEADS_PER_GROUP), lambda b, r, n: (0, 0))],
        out_specs=[pl.BlockSpec((1, 1, rows, gw), lambda b, r, n: (b, r, n, 0)),
                   pl.BlockSpec((1, 1, rows, LANES), lambda b, r, n: (b, r, n, 0))],
        out_shape=[jax.ShapeDtypeStruct((bsz, dil, ls, gw), BF16),
                   jax.ShapeDtypeStruct((bsz, dil, ls, LANES), F32)],
        scratch_shapes=[pltpu.VMEM((w, gw), BF16), pltpu.VMEM((w, gw), BF16)],
        compiler_params=_cparams("parallel", "parallel", "arbitrary"),
        name="dilated_attention_g%d" % gi,
    )(qkv, qkv, qkv, slopes.reshape(1, ATT_HEADS_PER_GROUP))


def _att_out_kernel(o0_ref, o1_ref, o2_ref, l0_ref, l1_ref, l2_ref, w_ref, x_ref, gate_ref, o_ref,
                    *scratch):
    tm = x_ref.shape[1]

    def token_major(ref, scr):
        dil = ref.shape[1]
        if dil == 1:
            return ref[0, 0].astype(F32)
        f1 = min(dil, DESTRIDE_STEP)
        f2 = dil // f1
        n1 = tm // f1
        for c in range(scr.shape[0]):
            cols = slice(c * LANES, (c + 1) * LANES)
            if f2 == 1:
                for r in range(dil):
                    scr[c, pl.ds(r, tm // dil, stride=dil), :] = ref[0, r, :, cols].astype(F32)
            else:
                for b in range(f1):
                    for a in range(f2):
                        tmp_ref[c, pl.ds(b * n1 + a, tm // dil, stride=f2), :] = (
                            ref[0, a * f1 + b, :, cols].astype(F32))
                for b in range(f1):
                    scr[c, pl.ds(b, n1, stride=f1), :] = tmp_ref[c, b * n1:(b + 1) * n1, :]
        return jnp.concatenate([scr[c] for c in range(scr.shape[0])], axis=1)

    *scratch, tmp_ref = scratch
    o1, l1, o2, l2 = [token_major(ref, scr) for ref, scr in
                      zip((o1_ref, l1_ref, o2_ref, l2_ref), scratch)]
    o0, l0 = token_major(o0_ref, None), token_major(l0_ref, None)
    m = jnp.maximum(jnp.maximum(l0, l1), l2)
    w0, w1, w2 = jnp.exp(l0 - m), jnp.exp(l1 - m), jnp.exp(l2 - m)
    inv = 1.0 / (w0 + w1 + w2)
    k = o0.shape[1]
    expand = jnp.where(_iota2((LANES, k), 1) // ATT_HEAD_DIM == _iota2((LANES, k), 0), 1.0, 0.0)
    y = (_dot_split(w0 * inv, expand) * o0 + _dot_split(w1 * inv, expand) * o1
         + _dot_split(w2 * inv, expand) * o2)
    o_ref[0] = x_ref[0] + gate_ref[0] * _dot(y.astype(BF16), w_ref[...])


def attention_combine_out(outs, lses, w, x, gate, *, tm=512):
    bsz, t, d = x.shape
    k = w.shape[0]
    tm = min(tm, t)
    row = pl.BlockSpec((1, tm, d), lambda b, i: (b, i, 0))
    res = lambda a: pl.BlockSpec((1, a.shape[1], tm // a.shape[1], a.shape[3]),
                                 lambda b, i: (b, 0, i, 0))
    wide, narrow = pltpu.VMEM((k // LANES, tm, LANES), F32), pltpu.VMEM((1, tm, LANES), F32)
    return pl.pallas_call(
        _att_out_kernel,
        grid=(bsz, t // tm),
        in_specs=[res(a) for a in outs] + [res(a) for a in lses] + [
            pl.BlockSpec((k, d), lambda b, i: (0, 0)), row,
            pl.BlockSpec((1, 1, d), lambda b, i: (b, 0, 0))],
        out_specs=row,
        out_shape=jax.ShapeDtypeStruct((bsz, t, d), F32),
        scratch_shapes=[wide, narrow, wide, narrow, wide],
        compiler_params=_cparams("parallel", "parallel"),
        name="attention_combine_out",
    )(*outs, *lses, w, x, gate)


def attention_layer(x, nw, sh, sc, gate, in_w, out_w):
    n_groups = len(ATT_GROUPS)
    gw = ATT_HEADS_PER_GROUP * ATT_HEAD_DIM
    n_heads = n_groups * ATT_HEADS_PER_GROUP
    slopes = jnp.exp2(-8.0 * jnp.arange(1, n_heads + 1, dtype=F32) / n_heads)
    slopes = slopes.reshape(n_groups, ATT_HEADS_PER_GROUP)
    w_in = in_w.astype(BF16)
    outs, lses = [], []
    for gi, (window, dil) in enumerate(ATT_GROUPS):
        qkv = norm_mod_matmul(x, nw, sh, sc, w_in, tm=PROJ_TM, tn=gw, dil=dil, n_out=3 * gw,
                              w_tile=lambda j, gi=gi: j * n_groups + gi)
        o, lse = dilated_group_core(qkv, gi, window, dil, slopes[gi])
        outs.append(o)
        lses.append(lse)
    return attention_combine_out(outs, lses, out_w.astype(BF16), x, gate)


ROUTER_GROUP_LANE0 = 0
ROUTER_EXPERT_LANE0 = MOE_GROUPS
ROUTER_FIELDS = 6
ROUTER_PREFIX_ROWS = 256


def _router_kernel(x_ref, nw_ref, sh_ref, sc_ref, wr_ref, br_ref, hf_ref, meta_ref, cnt_ref,
                   carry_ref):
    tm = x_ref.shape[1]

    @pl.when(pl.program_id(1) == 0)
    def _():
        carry_ref[...] = jnp.zeros_like(carry_ref)

    hf = _norm_mod(x_ref[0], nw_ref[...], sh_ref[0], sc_ref[0])
    hf_ref[0] = hf
    logits = _dot_3pass(hf, wr_ref[...]) + br_ref[...]
    lane = _iota2((tm, LANES), 1)
    neg = -jnp.inf
    first = lambda hit: jnp.min(jnp.where(hit, lane, LANES), axis=-1, keepdims=True)

    gl = jnp.where(lane < MOE_GROUPS, logits, neg)
    gmax = jnp.max(gl, axis=-1, keepdims=True)
    gidx = first(gl == gmax)
    g_top = 1.0 / jnp.sum(jnp.exp(gl - gmax), axis=-1, keepdims=True)

    lo = ROUTER_EXPERT_LANE0 + gidx * MOE_EXPERTS_PER_GROUP
    el = jnp.where((lane >= lo) & (lane < lo + MOE_EXPERTS_PER_GROUP), logits, neg)
    m1 = jnp.max(el, axis=-1, keepdims=True)
    i1 = first(el == m1)
    el2 = jnp.where(lane == i1, neg, el)
    m2 = jnp.max(el2, axis=-1, keepdims=True)
    i2 = first(el2 == m2)
    ex = jnp.exp(m2 - m1)
    gate1 = g_top / (1.0 + ex)
    gate2 = g_top * ex / (1.0 + ex)

    hit1 = lane == i1
    hit2 = lane == i2
    onehot = jnp.where(hit1 | hit2, 1.0, 0.0).astype(BF16)
    sub = min(tm, ROUTER_PREFIX_ROWS)
    strict = jnp.where(_iota2((sub, sub), 0) > _iota2((sub, sub), 1), 1.0, 0.0).astype(BF16)
    ones = jnp.ones((8, sub), BF16)
    running = carry_ref[...]
    parts = []
    for s0 in range(0, tm, sub):
        block = onehot[s0:s0 + sub]
        parts.append(_dot(strict, block) + running[0:1, :])
        running = running + _dot(ones, block)
    before = jnp.concatenate(parts, axis=0)
    rank1 = jnp.sum(jnp.where(hit1, before, 0.0), axis=-1, keepdims=True)
    rank2 = jnp.sum(jnp.where(hit2, before, 0.0), axis=-1, keepdims=True)
    carry_ref[...] = running
    cnt_ref[0] = running

    e1 = (i1 - ROUTER_EXPERT_LANE0).astype(F32)
    e2 = (i2 - ROUTER_EXPERT_LANE0).astype(F32)
    meta = jnp.zeros((tm, LANES), F32)
    for j, val in enumerate((e1, e2, rank1, rank2, gate1, gate2)):
        meta = jnp.where(lane == j, val, meta)
    for k in range(tm // LANES):
        fields = meta[k * LANES:(k + 1) * LANES, :].T
        for f in range(ROUTER_FIELDS):
            meta_ref[0, f, k:k + 1, :] = fields[f:f + 1, :]


def moe_router(x, nw, sh, sc, group_w, group_b, expert_w, expert_b, *, tm=8 * LANES):
    bsz, t, d = x.shape
    tm = min(tm, t)
    pad = LANES - MOE_GROUPS - MOE_EXPERTS
    wr = jnp.pad(jnp.concatenate([group_w, expert_w], axis=1), ((0, 0), (0, pad)))
    br = jnp.pad(jnp.concatenate([group_b, expert_b]), (0, pad)).reshape(1, LANES)
    return pl.pallas_call(
        _router_kernel,
        grid=(bsz, t // tm),
        in_specs=[pl.BlockSpec((1, tm, d), lambda b, i: (b, i, 0)),
                  pl.BlockSpec((1, d), lambda b, i: (0, 0)),
                  pl.BlockSpec((1, 1, d), lambda b, i: (b, 0, 0)),
                  pl.BlockSpec((1, 1, d), lambda b, i: (b, 0, 0)),
                  pl.BlockSpec((d, LANES), lambda b, i: (0, 0)),
                  pl.BlockSpec((1, LANES), lambda b, i: (0, 0))],
        out_specs=[pl.BlockSpec((1, tm, d), lambda b, i: (b, i, 0)),
                   pl.BlockSpec((1, ROUTER_FIELDS, tm // LANES, LANES), lambda b, i: (b, 0, i, 0)),
                   pl.BlockSpec((1, 8, LANES), lambda b, i: (b, 0, 0))],
        out_shape=[jax.ShapeDtypeStruct((bsz, t, d), F32),
                   jax.ShapeDtypeStruct((bsz, ROUTER_FIELDS, t // LANES, LANES), F32),
                   jax.ShapeDtypeStruct((bsz, 8, LANES), F32)],
        scratch_shapes=[pltpu.VMEM((8, LANES), F32)],
        compiler_params=_cparams("parallel", "arbitrary"),
        name="moe_router",
    )(x, nw.reshape(1, d), sh, sc, wr, br)


def _dest_kernel(meta_ref, cnt_ref, d_ref):
    upper = jnp.where(_iota2((LANES, LANES), 0) < _iota2((LANES, LANES), 1), 1.0, 0.0)
    first = _dot_split(cnt_ref[0], upper)
    for k in range(2):
        expert = meta_ref[0, k]
        off = jnp.zeros_like(expert)
        for x in range(MOE_EXPERTS):
            lane = ROUTER_EXPERT_LANE0 + x
            off = jnp.where(expert == x, first[0:1, lane:lane + 1], off)
        d_ref[0, k] = (meta_ref[0, 2 + k] + off).astype(jnp.int32)


def moe_dest(meta, cnt):
    bsz, nf, rows, _ = meta.shape
    return pl.pallas_call(
        _dest_kernel,
        grid=(bsz,),
        in_specs=[pl.BlockSpec((1, nf, rows, LANES), lambda b: (b, 0, 0, 0)),
                  pl.BlockSpec((1, 8, LANES), lambda b: (b, 0, 0))],
        out_specs=pl.BlockSpec((1, 2, rows, LANES), lambda b: (b, 0, 0, 0)),
        out_shape=jax.ShapeDtypeStruct((bsz, 2, rows, LANES), jnp.int32),
        compiler_params=_cparams("parallel"),
        name="moe_dest",
    )(meta, cnt)


def _expert_kernel(d1_ref, d2_ref, g1_ref, g2_ref, cnt_ref,
                   hf_hbm, x_hbm, gt_ref, w1_ref, w3_ref, w2_ref, *rest, final_norm):
    if final_norm:
        fw_ref, *rest = rest
    out_hbm, hf_v, acc_v, xb_v, yb_v, row_tok, row_gate, pstart, sem = rest
    b = pl.program_id(0)
    e = pl.program_id(1)
    n_e = pl.num_programs(1)
    t = hf_v.shape[0]
    grp = MOE_ROW_GROUP
    rows = xb_v.shape[0] * grp

    @pl.when(e == 0)
    def _():
        cp_h = pltpu.make_async_copy(hf_hbm.at[b], hf_v, sem.at[0])
        cp_x = pltpu.make_async_copy(x_hbm.at[b], acc_v, sem.at[1])
        cp_h.start()
        cp_x.start()
        xb_v[...] = jnp.zeros_like(xb_v)
        for i in range(grp):
            row_tok[2 * t + i] = 0

        def start_body(i, s):
            pstart[i] = s
            return s + cnt_ref[b * n_e + i]

        lax.fori_loop(0, n_e, start_body, 0)

        def tok_body(g, carry):
            for u in range(4):
                tok = g * 4 + u
                row_tok[d1_ref[tok]] = tok
                row_gate[d1_ref[tok]] = g1_ref[tok]
                row_tok[d2_ref[tok]] = tok
                row_gate[d2_ref[tok]] = g2_ref[tok]
            return carry

        lax.fori_loop(0, t // 4, tok_body, 0)
        cp_h.wait()
        cp_x.wait()

    cnt = cnt_ref[b * n_e + e]
    start = pstart[e]

    def block_body(j, carry):
        base = start + j * rows
        nrows = jnp.minimum(rows, cnt - j * rows)
        full_groups = nrows // grp

        def gather(g, c):
            first = base + g * grp
            for u in range(grp):
                xb_v[g, u:u + 1, :] = hf_v[pl.ds(row_tok[first + u], 1), :]
            return c

        lax.fori_loop(0, (nrows + grp - 1) // grp, gather, 0)
        xb = xb_v[...].reshape(rows, xb_v.shape[2]).astype(BF16)
        h1 = _dot(xb, w1_ref[0, 0].astype(BF16))
        h3 = _dot(xb, w3_ref[0, 0].astype(BF16))
        act = (_silu(h1) * h3).astype(BF16)
        yb_v[...] = (_dot(act, w2_ref[0, 0].astype(BF16)) * gt_ref[0]).reshape(yb_v.shape)

        def scatter(g, c):
            first = base + g * grp
            toks = [row_tok[first + u] for u in range(grp)]
            vals = [acc_v[pl.ds(toks[u], 1), :] + row_gate[first + u] * yb_v[g, u:u + 1, :]
                    for u in range(grp)]
            for u in range(grp):
                acc_v[pl.ds(toks[u], 1), :] = vals[u]
            return c

        lax.fori_loop(0, full_groups, scatter, 0)

        def scatter_tail(i, c):
            tok = row_tok[base + i]
            acc_v[pl.ds(tok, 1), :] = (acc_v[pl.ds(tok, 1), :] + row_gate[base + i]
                                       * yb_v[i // grp, pl.ds(i % grp, 1), :])
            return c

        lax.fori_loop(full_groups * grp, nrows, scatter_tail, 0)
        return carry

    lax.fori_loop(0, (cnt + rows - 1) // rows, block_body, 0)

    @pl.when(e == n_e - 1)
    def _():
        if final_norm:
            chunk = math.gcd(t, 256)

            def norm_body(i, carry):
                rs = pl.ds(pl.multiple_of(i * chunk, chunk), chunk)
                xr = acc_v[rs, :]
                acc_v[rs, :] = (xr * lax.rsqrt(jnp.mean(xr * xr, axis=-1, keepdims=True) + NORM_EPS)
                                * fw_ref[...])
                return carry

            lax.fori_loop(0, t // chunk, norm_body, 0)
        cp_o = pltpu.make_async_copy(acc_v, out_hbm.at[b], sem.at[2])
        cp_o.start()
        cp_o.wait()


def moe_experts(hf, x, gt, d1, d2, g1, g2, counts, layer, w1, w3, w2, final_w=None):
    bsz, t, d = x.shape
    _, n_e, _, f = w1.shape
    smem_tok = pl.BlockSpec((t,), lambda b, e: (b,), memory_space=pltpu.SMEM)
    extra_specs, extra = [], []
    if final_w is not None:
        extra_specs, extra = [pl.BlockSpec((1, d), lambda b, e: (0, 0))], [final_w.reshape(1, d)]
    return pl.pallas_call(
        functools.partial(_expert_kernel, final_norm=final_w is not None),
        grid=(bsz, n_e),
        in_specs=[smem_tok] * 4 + [
            pl.BlockSpec(memory_space=pltpu.SMEM),
            pl.BlockSpec(memory_space=pl.ANY),
            pl.BlockSpec(memory_space=pl.ANY),
            pl.BlockSpec((1, 1, d), lambda b, e: (b, 0, 0)),
            pl.BlockSpec((1, 1, d, f), lambda b, e: (layer, e, 0, 0)),
            pl.BlockSpec((1, 1, d, f), lambda b, e: (layer, e, 0, 0)),
            pl.BlockSpec((1, 1, f, d), lambda b, e: (layer, e, 0, 0))] + extra_specs,
        out_specs=pl.BlockSpec(memory_space=pl.ANY),
        out_shape=jax.ShapeDtypeStruct((bsz, t, d), F32),
        scratch_shapes=[pltpu.VMEM((t, d), F32), pltpu.VMEM((t, d), F32),
                        pltpu.VMEM((MOE_ROWS // MOE_ROW_GROUP, MOE_ROW_GROUP, d), F32),
                        pltpu.VMEM((MOE_ROWS // MOE_ROW_GROUP, MOE_ROW_GROUP, d), F32),
                        pltpu.SMEM((2 * t + MOE_ROW_GROUP,), jnp.int32),
                        pltpu.SMEM((2 * t + MOE_ROW_GROUP,), F32),
                        pltpu.SMEM((n_e,), jnp.int32), pltpu.SemaphoreType.DMA((3,))],
        compiler_params=_cparams("arbitrary", "arbitrary"),
        name="moe_experts",
    )(d1, d2, g1, g2, counts, hf, x, gt, w1, w3, w2, *extra)


def moe_layer(x, nw, sh, sc, gate, group_w, group_b, expert_w, expert_b, layer, w1, w3, w2,
              final_w=None):
    bsz, t, d = x.shape
    hf, meta, cnt = moe_router(x, nw, sh, sc, group_w, group_b, expert_w, expert_b)
    dest = moe_dest(meta, cnt)
    d1, d2 = [dest[:, k].reshape(bsz * t) for k in range(2)]
    g1, g2 = [meta[:, f].reshape(bsz * t) for f in (4, 5)]
    counts = cnt[:, 0, ROUTER_EXPERT_LANE0:ROUTER_EXPERT_LANE0 + MOE_EXPERTS].astype(jnp.int32)
    return moe_experts(hf, x, gate, d1, d2, g1, g2, counts.reshape(-1), layer, w1, w3, w2, final_w)


def kernel(x, c, ada_w, ada_b, norm_mix_w, norm_ffn_w, ssm_in_w, ssm_conv_w, ssm_conv_b, ssm_dt_bias, ssm_a_log, ssm_d, ssm_norm_w, ssm_out_w, rwkv_mix, rwkv_rkv_w, rwkv_w0, rwkv_w1, rwkv_w2, rwkv_a0, rwkv_a1, rwkv_a2, rwkv_g1, rwkv_g2, rwkv_k_k, rwkv_k_a, rwkv_r_k, rwkv_ln_w, rwkv_ln_b, rwkv_out_w, ret_in_w, ret_out_w, att_in_w, att_out_w, moe_group_w, moe_group_b, moe_expert_w, moe_expert_b, moe_w1, moe_w3, moe_w2, final_norm_w):
    depth = ada_w.shape[0]
    d = x.shape[-1]
    mod = ada_modulation(c, ada_w, ada_b)
    for i in range(depth):
        sh1, sc1, gt1, sh2, sc2, gt2 = [mod[i][:, None, j * d:(j + 1) * d] for j in range(6)]
        kind, j = i % 4, i // 4
        pre = (x, norm_mix_w[i], sh1, sc1, gt1)
        if kind == 0:
            x = mamba2_layer(*pre, ssm_in_w[j], ssm_conv_w[j], ssm_conv_b[j], ssm_dt_bias[j],
                             ssm_a_log[j], ssm_d[j], ssm_norm_w[j], ssm_out_w[j])
        elif kind == 1:
            x = rwkv7_layer(*pre, rwkv_mix[j], rwkv_rkv_w[j], rwkv_w0[j], rwkv_w1[j], rwkv_w2[j],
                            rwkv_a0[j], rwkv_a1[j], rwkv_a2[j], rwkv_g1[j], rwkv_g2[j], rwkv_k_k[j],
                            rwkv_k_a[j], rwkv_r_k[j], rwkv_ln_w[j], rwkv_ln_b[j], rwkv_out_w[j])
        elif kind == 2:
            x = retention_layer(*pre, ret_in_w[j], ret_out_w[j])
        else:
            x = attention_layer(*pre, att_in_w[j], att_out_w[j])
        x = moe_layer(x, norm_ffn_w[i], sh2, sc2, gt2, moe_group_w[i], moe_group_b[i],
                      moe_expert_w[i], moe_expert_b[i], i, moe_w1, moe_w3, moe_w2,
                      final_w=final_norm_w if i == depth - 1 else None)
    return x
```
